```python
import math
import jax, jax.numpy as jnp
from jax import lax
import numpy as np

D_MODEL = 1024
BATCH = 8
SEQ = 16384
DEPTH = 1

HEAD_DIM = 64
HEADS_PER_GROUP = 8
DILATED_GROUPS = ((128, 1), (512, 4), (2048, 16))
N_GROUPS = len(DILATED_GROUPS)
N_ATTN_HEADS = N_GROUPS * HEADS_PER_GROUP
ATTN_W = N_ATTN_HEADS * HEAD_DIM
ATTN_OUT_W = HEADS_PER_GROUP * HEAD_DIM
BLOCK = 128
REL_BUCKETS = 32
REL_MAX_DISTANCE = 2048
CONV_CHANNELS = D_MODEL
CONV_WIDTH = 31
FFN_HIDDEN = -(-8 * D_MODEL // (3 * 256)) * 256
IN_W = 3 * ATTN_W + 2 * CONV_CHANNELS + 2 * D_MODEL
RMS_EPS = 1e-6
LN_EPS = 1e-5
NEG_INF = -1e30

kernel_name = "hybrid_dilated_attn_conformer_conv_gated_block"


def rms_norm(x, g):
    xf = x.astype(jnp.float32)
    y = xf * lax.rsqrt(jnp.mean(xf * xf, axis=-1, keepdims=True) + RMS_EPS)
    return (y * g.astype(jnp.float32)).astype(x.dtype)


def layer_norm(x, g, b):
    xf = x.astype(jnp.float32)
    mu = jnp.mean(xf, axis=-1, keepdims=True)
    xc = xf - mu
    y = xc * lax.rsqrt(jnp.mean(xc * xc, axis=-1, keepdims=True) + LN_EPS)
    return (y * g.astype(jnp.float32) + b.astype(jnp.float32)).astype(x.dtype)


def rel_bucket(dist):
    max_exact = REL_BUCKETS // 2
    d = jnp.maximum(dist, 0)
    df = jnp.maximum(d, 1).astype(jnp.float32)
    large = max_exact + (jnp.log(df / max_exact) / math.log(REL_MAX_DISTANCE / max_exact)
                         * (REL_BUCKETS - max_exact)).astype(jnp.int32)
    large = jnp.minimum(large, REL_BUCKETS - 1)
    return jnp.where(d < max_exact, d, large)


def dilated_group_attention(q, k, v, bias_tab, window, dilation):
    B, S, H, Dh = q.shape
    span = window // dilation
    L = S // dilation
    nb = -(-L // BLOCK)
    Lp = nb * BLOCK
    n_prev = -(-span // BLOCK)
    kb_len = (n_prev + 1) * BLOCK

    def to_sub(t, front):
        t = t.reshape(B, L, dilation, H, Dh).transpose(0, 2, 1, 3, 4)
        return jnp.pad(t, ((0, 0), (0, 0), (front, Lp - L), (0, 0), (0, 0)))

    qb = to_sub(q, 0).reshape(B, dilation, nb, BLOCK, H, Dh)

    def band(t):
        tb = to_sub(t, n_prev * BLOCK).reshape(B, dilation, nb + n_prev, BLOCK, H, Dh)
        return jnp.concatenate([tb[:, :, j:j + nb] for j in range(n_prev + 1)], axis=3)

    kb, vb = band(k), band(v)
    a = jnp.arange(BLOCK, dtype=jnp.int32)[:, None]
    c = jnp.arange(kb_len, dtype=jnp.int32)[None, :]
    offset = a - c + n_prev * BLOCK
    bias = bias_tab[rel_bucket(offset * dilation)].astype(jnp.float32).transpose(2, 0, 1)
    kj = (jnp.arange(nb, dtype=jnp.int32)[:, None, None] - n_prev) * BLOCK + c[None]
    valid = (offset >= 0) & (offset <= span) & (kj >= 0)

    s = jnp.einsum('brnqhd,brnkhd->brnhqk', qb, kb).astype(jnp.float32) * (Dh ** -0.5) + bias
    s = jnp.where(valid[:, None], s, NEG_INF)
    lse = jax.nn.logsumexp(s, axis=-1)
    p = jnp.exp(s - lse[..., None])
    o = jnp.einsum('brnhqk,brnkhd->brnqhd', p, vb.astype(jnp.float32))
    o = o.reshape(B, dilation, Lp, H, Dh)[:, :, :L].transpose(0, 2, 1, 3, 4).reshape(B, S, H, Dh)
    lse = lse.transpose(0, 1, 2, 4, 3).reshape(B, dilation, Lp, H)[:, :, :L]
    lse = lse.transpose(0, 2, 1, 3).reshape(B, S, H)
    return o, lse


def dilated_attention_mixer(q, k, v, rel_bias_table, w_attn_out):
    B, S = q.shape[0], q.shape[1]
    outs, lses = [], []
    for g, (window, dilation) in enumerate(DILATED_GROUPS):
        tab = rel_bias_table[:, g * HEADS_PER_GROUP:(g + 1) * HEADS_PER_GROUP]
        o_g, lse_g = dilated_group_attention(q[:, :, g], k[:, :, g], v[:, :, g], tab, window, dilation)
        outs.append(o_g)
        lses.append(lse_g)
    alpha = jax.nn.softmax(jnp.stack(lses, axis=0), axis=0)
    o = jnp.einsum('gbsh,gbshd->bshd', alpha, jnp.stack(outs, axis=0))
    o = o.reshape(B, S, ATTN_OUT_W).astype(w_attn_out.dtype)
    return o @ w_attn_out


def conformer_conv_mixer(glu_in, b_glu, w_dw, b_dw, g_ln, b_ln, w_conv_out, b_conv_out):
    h = glu_in + b_glu
    u, gate = jnp.split(h, 2, axis=-1)
    u = u * jax.nn.sigmoid(gate)
    u = lax.conv_general_dilated(
        u, w_dw.reshape(CONV_WIDTH, 1, CONV_CHANNELS).astype(u.dtype),
        window_strides=(1,), padding=[(CONV_WIDTH - 1, 0)],
        dimension_numbers=('NWC', 'WIO', 'NWC'),
        feature_group_count=CONV_CHANNELS) + b_dw
    u = jax.nn.silu(layer_norm(u, g_ln, b_ln))
    return u @ w_conv_out + b_conv_out


def swiglu_ffn(h, w_ffn_in, w_ffn_out):
    gate, up = jnp.split(h @ w_ffn_in, 2, axis=-1)
    return (jax.nn.silu(gate) * up) @ w_ffn_out


def _fwd_setup_inputs(seed: int = 0) -> dict:
    key = jax.random.key(seed)
    ks = jax.random.split(key, 20)
    f32 = jnp.float32

    def nrm(k, shape, scale):
        return jax.random.normal(k, shape, f32) * scale

    def gain(k, shape):
        return 1.0 + 0.05 * jax.random.normal(k, shape, f32)

    return {
        "x": jax.random.normal(ks[0], (BATCH, SEQ, D_MODEL), f32),
        "rel_bias_table": nrm(ks[1], (REL_BUCKETS, N_ATTN_HEADS), 0.2),
        "g_pre_mix": gain(ks[2], (DEPTH, D_MODEL)),
        "w_in": nrm(ks[3], (DEPTH, D_MODEL, IN_W), D_MODEL ** -0.5),
        "b_glu": nrm(ks[4], (DEPTH, 2 * CONV_CHANNELS), 0.02),
        "w_dw": nrm(ks[5], (DEPTH, CONV_WIDTH, CONV_CHANNELS), CONV_WIDTH ** -0.5),
        "b_dw": nrm(ks[6], (DEPTH, CONV_CHANNELS), 0.02),
        "g_conv_ln": gain(ks[7], (DEPTH, CONV_CHANNELS)),
        "b_conv_ln": nrm(ks[8], (DEPTH, CONV_CHANNELS), 0.02),
        "w_conv_out": nrm(ks[9], (DEPTH, CONV_CHANNELS, D_MODEL), CONV_CHANNELS ** -0.5),
        "b_conv_out": nrm(ks[10], (DEPTH, D_MODEL), 0.02),
        "w_attn_out": nrm(ks[11], (DEPTH, ATTN_OUT_W, D_MODEL), ATTN_OUT_W ** -0.5),
        "w_mix_out": nrm(ks[12], (DEPTH, D_MODEL, D_MODEL), D_MODEL ** -0.5),
        "g_post_mix": gain(ks[13], (DEPTH, D_MODEL)),
        "g_pre_ffn": gain(ks[14], (DEPTH, D_MODEL)),
        "w_ffn_in": nrm(ks[15], (DEPTH, D_MODEL, 2 * FFN_HIDDEN), D_MODEL ** -0.5),
        "w_ffn_out": nrm(ks[16], (DEPTH, FFN_HIDDEN, D_MODEL), FFN_HIDDEN ** -0.5),
        "g_post_ffn": gain(ks[17], (DEPTH, D_MODEL)),
    }


def _fwd_reference(x, rel_bias_table, g_pre_mix, w_in, b_glu, w_dw, b_dw, g_conv_ln, b_conv_ln,
              w_conv_out, b_conv_out, w_attn_out, w_mix_out, g_post_mix, g_pre_ffn,
              w_ffn_in, w_ffn_out, g_post_ffn):
    B, S, D = x.shape
    for l in range(DEPTH):
        h = rms_norm(x, g_pre_mix[l])
        z = h @ w_in[l]
        q, k, v, glu_in, z_ga, z_gc = jnp.split(
            z, np.cumsum([ATTN_W, ATTN_W, ATTN_W, 2 * CONV_CHANNELS, D_MODEL]).tolist(), axis=-1)
        shp = (B, S, N_GROUPS, HEADS_PER_GROUP, HEAD_DIM)
        y_attn = dilated_attention_mixer(q.reshape(shp), k.reshape(shp), v.reshape(shp),
                                         rel_bias_table, w_attn_out[l])
        y_conv = conformer_conv_mixer(glu_in, b_glu[l], w_dw[l], b_dw[l], g_conv_ln[l],
                                      b_conv_ln[l], w_conv_out[l], b_conv_out[l])
        merged = jax.nn.sigmoid(z_ga) * y_attn + jax.nn.sigmoid(z_gc) * y_conv
        x = x + rms_norm(merged @ w_mix_out[l], g_post_mix[l])
        h = rms_norm(x, g_pre_ffn[l])
        x = x + rms_norm(swiglu_ffn(h, w_ffn_in[l], w_ffn_out[l]), g_post_ffn[l])
    return x


import jax as _jax
import jax.numpy as _jnp

TWIN_FORMAT = 'train_step'
FWD_PARAMS = ['x', 'rel_bias_table', 'g_pre_mix', 'w_in', 'b_glu', 'w_dw', 'b_dw', 'g_conv_ln', 'b_conv_ln', 'w_conv_out', 'b_conv_out', 'w_attn_out', 'w_mix_out', 'g_post_mix', 'g_pre_ffn', 'w_ffn_in', 'w_ffn_out', 'g_post_ffn']
TWIN_WEIGHTS = ['rel_bias_table', 'g_pre_mix', 'w_in', 'b_glu', 'w_dw', 'b_dw', 'g_conv_ln', 'b_conv_ln', 'w_conv_out', 'b_conv_out', 'w_attn_out', 'w_mix_out', 'g_post_mix', 'g_pre_ffn', 'w_ffn_in', 'w_ffn_out', 'g_post_ffn']
TWIN_DIFF_INPUT = 'x'
TWIN_INPUTS = ['x', 'rel_bias_table', 'g_pre_mix', 'w_in', 'b_glu', 'w_dw', 'b_dw', 'g_conv_ln', 'b_conv_ln', 'w_conv_out', 'b_conv_out', 'w_attn_out', 'w_mix_out', 'g_post_mix', 'g_pre_ffn', 'w_ffn_in', 'w_ffn_out', 'g_post_ffn', 'loss_target', 'm_rel_bias_table', 'm_g_pre_mix', 'm_w_in', 'm_b_glu', 'm_w_dw', 'm_b_dw', 'm_g_conv_ln', 'm_b_conv_ln', 'm_w_conv_out', 'm_b_conv_out', 'm_w_attn_out', 'm_w_mix_out', 'm_g_post_mix', 'm_g_pre_ffn', 'm_w_ffn_in', 'm_w_ffn_out', 'm_g_post_ffn', 'v_rel_bias_table', 'v_g_pre_mix', 'v_w_in', 'v_b_glu', 'v_w_dw', 'v_b_dw', 'v_g_conv_ln', 'v_b_conv_ln', 'v_w_conv_out', 'v_b_conv_out', 'v_w_attn_out', 'v_w_mix_out', 'v_g_post_mix', 'v_g_pre_ffn', 'v_w_ffn_in', 'v_w_ffn_out', 'v_g_post_ffn']
TWIN_OUTPUTS = ['loss', 'grad_x', 'grad_rel_bias_table', 'grad_g_pre_mix', 'grad_w_in', 'grad_b_glu', 'grad_w_dw', 'grad_b_dw', 'grad_g_conv_ln', 'grad_b_conv_ln', 'grad_w_conv_out', 'grad_b_conv_out', 'grad_w_attn_out', 'grad_w_mix_out', 'grad_g_post_mix', 'grad_g_pre_ffn', 'grad_w_ffn_in', 'grad_w_ffn_out', 'grad_g_post_ffn', 'delta_rel_bias_table', 'delta_g_pre_mix', 'delta_w_in', 'delta_b_glu', 'delta_w_dw', 'delta_b_dw', 'delta_g_conv_ln', 'delta_b_conv_ln', 'delta_w_conv_out', 'delta_b_conv_out', 'delta_w_attn_out', 'delta_w_mix_out', 'delta_g_post_mix', 'delta_g_pre_ffn', 'delta_w_ffn_in', 'delta_w_ffn_out', 'delta_g_post_ffn', 'new_m_rel_bias_table', 'new_m_g_pre_mix', 'new_m_w_in', 'new_m_b_glu', 'new_m_w_dw', 'new_m_b_dw', 'new_m_g_conv_ln', 'new_m_b_conv_ln', 'new_m_w_conv_out', 'new_m_b_conv_out', 'new_m_w_attn_out', 'new_m_w_mix_out', 'new_m_g_post_mix', 'new_m_g_pre_ffn', 'new_m_w_ffn_in', 'new_m_w_ffn_out', 'new_m_g_post_ffn', 'new_v_rel_bias_table', 'new_v_g_pre_mix', 'new_v_w_in', 'new_v_b_glu', 'new_v_w_dw', 'new_v_b_dw', 'new_v_g_conv_ln', 'new_v_b_conv_ln', 'new_v_w_conv_out', 'new_v_b_conv_out', 'new_v_w_attn_out', 'new_v_w_mix_out', 'new_v_g_post_mix', 'new_v_g_pre_ffn', 'new_v_w_ffn_in', 'new_v_w_ffn_out', 'new_v_g_post_ffn']
TWIN_LEAF_KINDS = {'loss': 'loss', 'grad_x': 'grad_x', 'grad_rel_bias_table': 'grad_w', 'grad_g_pre_mix': 'grad_w', 'grad_w_in': 'grad_w', 'grad_b_glu': 'grad_w', 'grad_w_dw': 'grad_w', 'grad_b_dw': 'grad_w', 'grad_g_conv_ln': 'grad_w', 'grad_b_conv_ln': 'grad_w', 'grad_w_conv_out': 'grad_w', 'grad_b_conv_out': 'grad_w', 'grad_w_attn_out': 'grad_w', 'grad_w_mix_out': 'grad_w', 'grad_g_post_mix': 'grad_w', 'grad_g_pre_ffn': 'grad_w', 'grad_w_ffn_in': 'grad_w', 'grad_w_ffn_out': 'grad_w', 'grad_g_post_ffn': 'grad_w', 'delta_rel_bias_table': 'delta_w', 'delta_g_pre_mix': 'delta_w', 'delta_w_in': 'delta_w', 'delta_b_glu': 'delta_w', 'delta_w_dw': 'delta_w', 'delta_b_dw': 'delta_w', 'delta_g_conv_ln': 'delta_w', 'delta_b_conv_ln': 'delta_w', 'delta_w_conv_out': 'delta_w', 'delta_b_conv_out': 'delta_w', 'delta_w_attn_out': 'delta_w', 'delta_w_mix_out': 'delta_w', 'delta_g_post_mix': 'delta_w', 'delta_g_pre_ffn': 'delta_w', 'delta_w_ffn_in': 'delta_w', 'delta_w_ffn_out': 'delta_w', 'delta_g_post_ffn': 'delta_w', 'new_m_rel_bias_table': 'new_m', 'new_m_g_pre_mix': 'new_m', 'new_m_w_in': 'new_m', 'new_m_b_glu': 'new_m', 'new_m_w_dw': 'new_m', 'new_m_b_dw': 'new_m', 'new_m_g_conv_ln': 'new_m', 'new_m_b_conv_ln': 'new_m', 'new_m_w_conv_out': 'new_m', 'new_m_b_conv_out': 'new_m', 'new_m_w_attn_out': 'new_m', 'new_m_w_mix_out': 'new_m', 'new_m_g_post_mix': 'new_m', 'new_m_g_pre_ffn': 'new_m', 'new_m_w_ffn_in': 'new_m', 'new_m_w_ffn_out': 'new_m', 'new_m_g_post_ffn': 'new_m', 'new_v_rel_bias_table': 'new_v', 'new_v_g_pre_mix': 'new_v', 'new_v_w_in': 'new_v', 'new_v_b_glu': 'new_v', 'new_v_w_dw': 'new_v', 'new_v_b_dw': 'new_v', 'new_v_g_conv_ln': 'new_v', 'new_v_b_conv_ln': 'new_v', 'new_v_w_conv_out': 'new_v', 'new_v_b_conv_out': 'new_v', 'new_v_w_attn_out': 'new_v', 'new_v_w_mix_out': 'new_v', 'new_v_g_post_mix': 'new_v', 'new_v_g_pre_ffn': 'new_v', 'new_v_w_ffn_in': 'new_v', 'new_v_w_ffn_out': 'new_v', 'new_v_g_post_ffn': 'new_v'}


def _forward(args):
    return _fwd_reference(*[args[k] for k in FWD_PARAMS])


def _output_shape():
    def fwd():
        inp = _fwd_setup_inputs(0)
        return _fwd_reference(*[inp[k] for k in FWD_PARAMS])
    out = _jax.eval_shape(fwd)
    return out.shape, out.dtype

N_MICROBATCH = 1
ADAM_LR = 0.001
ADAM_B1 = 0.9
ADAM_B2 = 0.999
ADAM_EPS = 1e-08
ADAM_WD = 0.01
ADAM_STEP = 10
PER_EXAMPLE_BATCH_AXIS = {'x': 0, 'loss_target': 0}
SHARED_INPUTS = []
_WEIGHT_DTYPES = {'rel_bias_table': _jnp.float32, 'g_pre_mix': _jnp.float32, 'w_in': _jnp.float32, 'b_glu': _jnp.float32, 'w_dw': _jnp.float32, 'b_dw': _jnp.float32, 'g_conv_ln': _jnp.float32, 'b_conv_ln': _jnp.float32, 'w_conv_out': _jnp.float32, 'b_conv_out': _jnp.float32, 'w_attn_out': _jnp.float32, 'w_mix_out': _jnp.float32, 'g_post_mix': _jnp.float32, 'g_pre_ffn': _jnp.float32, 'w_ffn_in': _jnp.float32, 'w_ffn_out': _jnp.float32, 'g_post_ffn': _jnp.float32}
MOMENT_SCALE = {'rel_bias_table': 2.487196e-01, 'g_pre_mix': 1.139507e+00, 'w_in': 3.732549e-01, 'b_glu': 9.486072e+00, 'w_dw': 1.294643e+00, 'b_dw': 2.305761e+01, 'g_conv_ln': 8.665832e+00, 'b_conv_ln': 1.349953e+01, 'w_conv_out': 4.471038e+00, 'b_conv_out': 2.791335e+01, 'w_attn_out': 2.688368e-01, 'w_mix_out': 4.543095e+00, 'g_post_mix': 1.299448e+02, 'g_pre_ffn': 4.082301e+00, 'w_ffn_in': 1.584141e+00, 'w_ffn_out': 3.509324e+00, 'g_post_ffn': 1.281795e+02}


def _to_microbatches(a, axis):
    t = _jnp.moveaxis(a, axis, 0)
    t = t.reshape((N_MICROBATCH, t.shape[0] // N_MICROBATCH) + t.shape[1:])
    return _jnp.moveaxis(t, 1, axis + 1)


def setup_inputs(seed: int = 0) -> dict:
    inp = _fwd_setup_inputs(seed)
    key = _jax.random.fold_in(_jax.random.key(seed), 7919)
    shape, _ = _output_shape()
    out = dict(inp)
    out["loss_target"] = _jax.random.normal(_jax.random.fold_in(key, 0), shape, _jnp.float32)
    for i, name in enumerate(TWIN_WEIGHTS):
        w = inp[name].astype(_jnp.float32)
        if MOMENT_SCALE is None:
            s = _jnp.sqrt(_jnp.mean(_jnp.square(w)) + 1e-30)
        else:
            s = MOMENT_SCALE[name]
        km, kv = _jax.random.split(_jax.random.fold_in(key, i + 1))
        out[name] = w
        out["m_" + name] = s * _jax.random.normal(km, w.shape, _jnp.float32)
        out["v_" + name] = (s * s) * _jax.random.uniform(kv, w.shape, _jnp.float32, 0.5, 1.5)
    if N_MICROBATCH > 1:
        for name, axis in PER_EXAMPLE_BATCH_AXIS.items():
            out[name] = _to_microbatches(out[name], axis)
    return {'x': out['x'], 'rel_bias_table': out['rel_bias_table'], 'g_pre_mix': out['g_pre_mix'], 'w_in': out['w_in'], 'b_glu': out['b_glu'], 'w_dw': out['w_dw'], 'b_dw': out['b_dw'], 'g_conv_ln': out['g_conv_ln'], 'b_conv_ln': out['b_conv_ln'], 'w_conv_out': out['w_conv_out'], 'b_conv_out': out['b_conv_out'], 'w_attn_out': out['w_attn_out'], 'w_mix_out': out['w_mix_out'], 'g_post_mix': out['g_post_mix'], 'g_pre_ffn': out['g_pre_ffn'], 'w_ffn_in': out['w_ffn_in'], 'w_ffn_out': out['w_ffn_out'], 'g_post_ffn': out['g_post_ffn'], 'loss_target': out['loss_target'], 'm_rel_bias_table': out['m_rel_bias_table'], 'm_g_pre_mix': out['m_g_pre_mix'], 'm_w_in': out['m_w_in'], 'm_b_glu': out['m_b_glu'], 'm_w_dw': out['m_w_dw'], 'm_b_dw': out['m_b_dw'], 'm_g_conv_ln': out['m_g_conv_ln'], 'm_b_conv_ln': out['m_b_conv_ln'], 'm_w_conv_out': out['m_w_conv_out'], 'm_b_conv_out': out['m_b_conv_out'], 'm_w_attn_out': out['m_w_attn_out'], 'm_w_mix_out': out['m_w_mix_out'], 'm_g_post_mix': out['m_g_post_mix'], 'm_g_pre_ffn': out['m_g_pre_ffn'], 'm_w_ffn_in': out['m_w_ffn_in'], 'm_w_ffn_out': out['m_w_ffn_out'], 'm_g_post_ffn': out['m_g_post_ffn'], 'v_rel_bias_table': out['v_rel_bias_table'], 'v_g_pre_mix': out['v_g_pre_mix'], 'v_w_in': out['v_w_in'], 'v_b_glu': out['v_b_glu'], 'v_w_dw': out['v_w_dw'], 'v_b_dw': out['v_b_dw'], 'v_g_conv_ln': out['v_g_conv_ln'], 'v_b_conv_ln': out['v_b_conv_ln'], 'v_w_conv_out': out['v_w_conv_out'], 'v_b_conv_out': out['v_b_conv_out'], 'v_w_attn_out': out['v_w_attn_out'], 'v_w_mix_out': out['v_w_mix_out'], 'v_g_post_mix': out['v_g_post_mix'], 'v_g_pre_ffn': out['v_g_pre_ffn'], 'v_w_ffn_in': out['v_w_ffn_in'], 'v_w_ffn_out': out['v_w_ffn_out'], 'v_g_post_ffn': out['v_g_post_ffn']}


def _loss(weights, diff, rest, loss_target):
    with _jax.named_scope("forward"):
        args = {**rest, TWIN_DIFF_INPUT: diff, **{k: w.astype(_WEIGHT_DTYPES[k]) for k, w in weights.items()}}
        y = _forward(args)
    with _jax.named_scope("loss_head"):
        err = _jnp.square(y.astype(_jnp.float32) - loss_target)
        return 0.5 * _jnp.sum(_jnp.mean(err, axis=-1)) if err.ndim else 0.5 * err


def _adamw(w, g, m, v):
    m = ADAM_B1 * m + (1.0 - ADAM_B1) * g
    v = ADAM_B2 * v + (1.0 - ADAM_B2) * _jnp.square(g)
    m_hat = m / (1.0 - ADAM_B1 ** ADAM_STEP)
    v_hat = v / (1.0 - ADAM_B2 ** ADAM_STEP)
    delta = -ADAM_LR * (m_hat / (_jnp.sqrt(v_hat) + ADAM_EPS) + ADAM_WD * w)
    return delta, m, v


def reference(x, rel_bias_table, g_pre_mix, w_in, b_glu, w_dw, b_dw, g_conv_ln, b_conv_ln, w_conv_out, b_conv_out, w_attn_out, w_mix_out, g_post_mix, g_pre_ffn, w_ffn_in, w_ffn_out, g_post_ffn, loss_target, m_rel_bias_table, m_g_pre_mix, m_w_in, m_b_glu, m_w_dw, m_b_dw, m_g_conv_ln, m_b_conv_ln, m_w_conv_out, m_b_conv_out, m_w_attn_out, m_w_mix_out, m_g_post_mix, m_g_pre_ffn, m_w_ffn_in, m_w_ffn_out, m_g_post_ffn, v_rel_bias_table, v_g_pre_mix, v_w_in, v_b_glu, v_w_dw, v_b_dw, v_g_conv_ln, v_b_conv_ln, v_w_conv_out, v_b_conv_out, v_w_attn_out, v_w_mix_out, v_g_post_mix, v_g_pre_ffn, v_w_ffn_in, v_w_ffn_out, v_g_post_ffn):
    given = dict(x=x, rel_bias_table=rel_bias_table, g_pre_mix=g_pre_mix, w_in=w_in, b_glu=b_glu, w_dw=w_dw, b_dw=b_dw, g_conv_ln=g_conv_ln, b_conv_ln=b_conv_ln, w_conv_out=w_conv_out, b_conv_out=b_conv_out, w_attn_out=w_attn_out, w_mix_out=w_mix_out, g_post_mix=g_post_mix, g_pre_ffn=g_pre_ffn, w_ffn_in=w_ffn_in, w_ffn_out=w_ffn_out, g_post_ffn=g_post_ffn, loss_target=loss_target, m_rel_bias_table=m_rel_bias_table, m_g_pre_mix=m_g_pre_mix, m_w_in=m_w_in, m_b_glu=m_b_glu, m_w_dw=m_w_dw, m_b_dw=m_b_dw, m_g_conv_ln=m_g_conv_ln, m_b_conv_ln=m_b_conv_ln, m_w_conv_out=m_w_conv_out, m_b_conv_out=m_b_conv_out, m_w_attn_out=m_w_attn_out, m_w_mix_out=m_w_mix_out, m_g_post_mix=m_g_post_mix, m_g_pre_ffn=m_g_pre_ffn, m_w_ffn_in=m_w_ffn_in, m_w_ffn_out=m_w_ffn_out, m_g_post_ffn=m_g_post_ffn, v_rel_bias_table=v_rel_bias_table, v_g_pre_mix=v_g_pre_mix, v_w_in=v_w_in, v_b_glu=v_b_glu, v_w_dw=v_w_dw, v_b_dw=v_b_dw, v_g_conv_ln=v_g_conv_ln, v_b_conv_ln=v_b_conv_ln, v_w_conv_out=v_w_conv_out, v_b_conv_out=v_b_conv_out, v_w_attn_out=v_w_attn_out, v_w_mix_out=v_w_mix_out, v_g_post_mix=v_g_post_mix, v_g_pre_ffn=v_g_pre_ffn, v_w_ffn_in=v_w_ffn_in, v_w_ffn_out=v_w_ffn_out, v_g_post_ffn=v_g_post_ffn)
    weights = {n: given[n] for n in TWIN_WEIGHTS}
    shared = {n: given[n] for n in SHARED_INPUTS}
    per_example = {n: given[n] for n in ['x']}
    grad_fn = _jax.value_and_grad(_loss, argnums=(0, 1))

    def one_microbatch(ex, loss_target):
        ex = dict(ex)
        diff = ex.pop(TWIN_DIFF_INPUT)
        return grad_fn(weights, diff, {**shared, **ex}, loss_target)

    if N_MICROBATCH == 1:
        loss, (grad_w, grad_x) = one_microbatch(per_example, given["loss_target"])
    else:
        def body(carry, xs):
            loss_sum, grad_sum = carry
            l_k, (gw_k, gx_k) = one_microbatch(xs[0], xs[1])
            with _jax.named_scope("update"):
                return (loss_sum + l_k, _jax.tree.map(_jnp.add, grad_sum, gw_k)), gx_k

        init = (_jnp.zeros((), _jnp.float32), _jax.tree.map(_jnp.zeros_like, weights))
        (loss, grad_w), grad_x = _jax.lax.scan(body, init, (per_example, given["loss_target"]))
    with _jax.named_scope("update"):
        delta_w, new_m, new_v = {}, {}, {}
        for n in TWIN_WEIGHTS:
            delta_w[n], new_m[n], new_v[n] = _adamw(weights[n], grad_w[n], given["m_" + n], given["v_" + n])
    return (loss, grad_x, *[grad_w[n] for n in TWIN_WEIGHTS], *[delta_w[n] for n in TWIN_WEIGHTS],
            *[new_m[n] for n in TWIN_WEIGHTS], *[new_v[n] for n in TWIN_WEIGHTS])
```

```python
import functools
import math

import numpy as np
import jax
import jax.numpy as jnp
from jax import lax
from jax.experimental import pallas as pl
from jax.experimental.pallas import tpu as pltpu

F32 = jnp.float32
BF16 = jnp.bfloat16
SDS = jax.ShapeDtypeStruct
MESH = pl.DeviceIdType.MESH
ANY = pl.BlockSpec(memory_space=pl.ANY)

D = 1024
HD = 64
NH = 8
GW = NH * HD
DILATIONS = (1, 4, 16)
SPAN = 128
QBLK = 128
KBLK = 2 * QBLK
CONV_W = 31
FFN_H = 2816
FFN_T = 256
NFT = FFN_H // FFN_T
RMS_EPS = 1e-6
LN_EPS = 1e-5
NEG_INF = -1e30
SCALE = HD ** -0.5
LANE_ROWS = 8

ADAM_LR, ADAM_B1, ADAM_B2, ADAM_EPS, ADAM_WD, ADAM_STEP = 0.001, 0.9, 0.999, 1e-08, 0.01, 10

VMEM_LIMIT = 56 * 1024 * 1024


def _cp(sem):
    return pltpu.CompilerParams(dimension_semantics=sem, vmem_limit_bytes=VMEM_LIMIT)


def _dot(a, b):
    return jnp.dot(a, b, preferred_element_type=F32)


def _dot_nt(a, b):
    return lax.dot_general(a, b, (((1,), (1,)), ((), ())), preferred_element_type=F32)


def _dot_tn(a, b):
    return lax.dot_general(a, b, (((0,), (0,)), ((), ())), preferred_element_type=F32)


def _sigmoid(v):
    return 1.0 / (1.0 + jnp.exp(-v))


def _colsum8(v):
    s = jnp.sum(v, axis=0, keepdims=True)
    row = lax.broadcasted_iota(jnp.int32, (LANE_ROWS, v.shape[1]), 0)
    return jnp.where(row == 0, jnp.broadcast_to(s, (LANE_ROWS, v.shape[1])), 0.0)


def _first_step(*ids):
    ok = ids[0] == 0
    for i in ids[1:]:
        ok = jnp.logical_and(ok, i == 0)
    return ok


def _in_proj(x, g, w):
    S = x.shape[0]
    N = w.shape[1]
    TM, TN = 1024, 512
    nj = N // TN
    nq = 3 * 3

    def body(x_ref, g_ref, w_ref, zq_ref, zr_ref, h_ref):
        j = pl.program_id(1)

        @pl.when(j == 0)
        def _():
            xf = x_ref[...]
            r = lax.rsqrt(jnp.mean(xf * xf, axis=-1, keepdims=True) + RMS_EPS)
            h_ref[...] = (xf * r * g_ref[...]).astype(BF16)

        zt = _dot(h_ref[...], w_ref[...]).astype(BF16)

        @pl.when(j < nq)
        def _():
            zq_ref[...] = zt

        @pl.when(j >= nq)
        def _():
            zr_ref[...] = zt

    return pl.pallas_call(
        body, name="in_proj", grid=(S // TM, nj),
        in_specs=[pl.BlockSpec((TM, D), lambda i, j: (i, 0)),
                  pl.BlockSpec((1, D), lambda i, j: (0, 0)),
                  pl.BlockSpec((D, TN), lambda i, j: (0, j))],
        out_specs=[pl.BlockSpec((TM, TN), lambda i, j: (i, jnp.minimum(j, nq - 1))),
                   pl.BlockSpec((TM, TN), lambda i, j: (i, jnp.maximum(j - nq, 0))),
                   pl.BlockSpec((TM, D), lambda i, j: (i, 0))],
        out_shape=[SDS((S, nq * TN), BF16), SDS((S, N - nq * TN), BF16), SDS((S, D), BF16)],
        compiler_params=_cp(("arbitrary", "arbitrary")),
    )(x, g, w)


def _bucket_tables():
    a = np.arange(QBLK, dtype=np.int32)[:, None]
    c = np.arange(KBLK, dtype=np.int32)[None, :]
    off = a - c + QBLK
    valid = ((off >= 0) & (off <= SPAN)).astype(np.float32)
    tabs = []
    for dil in DILATIONS:
        dist = np.maximum(off * dil, 0)
        df = np.maximum(dist, 1).astype(np.float32)
        large = 16 + (np.log(df / np.float32(16)) / np.float32(math.log(2048 / 16)) * np.float32(16)).astype(np.int32)
        large = np.minimum(large, 31)
        tabs.append(np.where(dist < 16, dist, large).astype(np.int32))
    return np.stack(tabs), valid


def _bias_expand(tab, buckets, valid):
    def body(tab_ref, b_ref, v_ref, o_ref):
        for gi in range(3):
            bk = b_ref[gi]
            for h in range(NH):
                acc = jnp.zeros((QBLK, KBLK), F32)
                for b in range(32):
                    acc = jnp.where(bk == b, tab_ref[b, gi * NH + h], acc)
                o_ref[gi * NH + h] = jnp.where(v_ref[...] > 0.5, acc, NEG_INF)

    return pl.pallas_call(
        body, name="bias_expand",
        in_specs=[pl.BlockSpec(memory_space=pltpu.SMEM),
                  pl.BlockSpec(memory_space=pltpu.VMEM), pl.BlockSpec(memory_space=pltpu.VMEM)],
        out_specs=pl.BlockSpec(memory_space=pltpu.VMEM),
        out_shape=SDS((3 * NH, QBLK, KBLK), F32),
    )(tab, buckets, valid)


def _bias_reduce(dbias, buckets):
    def body(d_ref, b_ref, o_ref):
        lane = lax.broadcasted_iota(jnp.int32, (1, D), 1)
        for gi in range(3):
            bk = b_ref[gi]
            for h in range(NH):
                dv = d_ref[gi * NH + h]
                row = jnp.zeros((1, D), F32)
                for b in range(32):
                    m = jnp.where(bk == b, dv, 0.0)
                    val = jnp.sum(jnp.sum(m, axis=0, keepdims=True), axis=1, keepdims=True)
                    row = jnp.where(lane == b, val, row)
                o_ref[gi * NH + h:gi * NH + h + 1, :] = row

    return pl.pallas_call(
        body, name="bias_reduce",
        in_specs=[pl.BlockSpec(memory_space=pltpu.VMEM), pl.BlockSpec(memory_space=pltpu.VMEM)],
        out_specs=pl.BlockSpec(memory_space=pltpu.VMEM),
        out_shape=SDS((3 * NH, D), F32),
    )(dbias, buckets)


def _attn_tile(S, dil):
    L = S // dil
    tq = min(512, L)
    return L, tq, tq // QBLK, L // tq


def _attn_fwd(zqkv, bias, gi):
    dil = DILATIONS[gi]
    S = zqkv.shape[0]
    L, TQ, QB, ns = _attn_tile(S, dil)
    nblk = zqkv.shape[1] // GW
    zv = zqkv.reshape(L, dil * zqkv.shape[1])

    def body(q_ref, kc_ref, kp_ref, vc_ref, vp_ref, b_ref, o_ref, l_ref, kext, vext):
        n = pl.program_id(1)
        kext[0:QBLK, :] = kp_ref[...]
        kext[QBLK:, :] = kc_ref[...]
        vext[0:QBLK, :] = vp_ref[...]
        vext[QBLK:, :] = vc_ref[...]
        col = lax.broadcasted_iota(jnp.int32, (QBLK, KBLK), 1)
        no_prev = jnp.logical_and(n == 0, col < QBLK)
        for b in range(QB):
            rows = slice(b * QBLK, (b + 1) * QBLK)
            krows = slice(b * QBLK, b * QBLK + KBLK)
            for h in range(NH):
                hc = slice(h * HD, (h + 1) * HD)
                s = _dot_nt(q_ref[rows, hc], kext[krows, hc]) * SCALE + b_ref[0, h]
                if b == 0:
                    s = jnp.where(no_prev, NEG_INF, s)
                m = jnp.max(s, axis=-1, keepdims=True)
                p = jnp.exp(s - m)
                l = jnp.sum(p, axis=-1, keepdims=True)
                o = _dot(p.astype(BF16), vext[krows, hc]) / l
                o_ref[rows, hc] = o.astype(BF16)
                l_ref[rows, hc] = jnp.broadcast_to(m + jnp.log(l), (QBLK, HD))

    def prev(n):
        return jnp.maximum(n * QB - 1, 0)

    o, lse = pl.pallas_call(
        body, name=f"attn_fwd_g{gi}", grid=(dil, ns),
        in_specs=[pl.BlockSpec((TQ, GW), lambda c, n: (n, c * nblk + gi)),
                  pl.BlockSpec((TQ, GW), lambda c, n: (n, c * nblk + 3 + gi)),
                  pl.BlockSpec((QBLK, GW), lambda c, n: (prev(n), c * nblk + 3 + gi)),
                  pl.BlockSpec((TQ, GW), lambda c, n: (n, c * nblk + 6 + gi)),
                  pl.BlockSpec((QBLK, GW), lambda c, n: (prev(n), c * nblk + 6 + gi)),
                  pl.BlockSpec((1, NH, QBLK, KBLK), lambda c, n: (gi, 0, 0, 0))],
        out_specs=[pl.BlockSpec((TQ, GW), lambda c, n: (n, c)),
                   pl.BlockSpec((TQ, GW), lambda c, n: (n, c))],
        out_shape=[SDS((L, dil * GW), BF16), SDS((L, dil * GW), F32)],
        scratch_shapes=[pltpu.VMEM((QBLK + TQ, GW), BF16), pltpu.VMEM((QBLK + TQ, GW), BF16)],
        compiler_params=_cp(("arbitrary", "arbitrary")),
    )(zv, zv, zv, zv, zv, bias)
    return o.reshape(S, GW), lse.reshape(S, GW)


def _conv_fwd(zrest, b_glu, wdw, b_dw, g_ln, b_ln):
    S = zrest.shape[0]
    TM = 512
    HALO = 32
    hb = TM // HALO

    def body(u_ref, g_ref, uh_ref, gh_ref, bg_ref, w_ref, bd_ref, gl_ref, bl_ref, cv_ref, a_ref, ext):
        i = pl.program_id(0)
        bu = bg_ref[:, 0:D]
        bgt = bg_ref[:, D:2 * D]
        uh = (uh_ref[...].astype(F32) + bu) * _sigmoid(gh_ref[...].astype(F32) + bgt)
        ext[0:HALO, :] = jnp.where(i == 0, 0.0, uh)
        ext[HALO:, :] = (u_ref[...].astype(F32) + bu) * _sigmoid(g_ref[...].astype(F32) + bgt)
        acc = jnp.zeros((TM, D), F32)
        for j in range(CONV_W):
            acc = acc + ext[HALO - (CONV_W - 1) + j:HALO - (CONV_W - 1) + j + TM, :] * w_ref[j:j + 1, :]
        cv = (acc + bd_ref[...]).astype(BF16)
        cv_ref[...] = cv
        cf = cv.astype(F32)
        mu = jnp.mean(cf, axis=-1, keepdims=True)
        xc = cf - mu
        y = xc * lax.rsqrt(jnp.mean(xc * xc, axis=-1, keepdims=True) + LN_EPS) * gl_ref[...] + bl_ref[...]
        a_ref[...] = (y * _sigmoid(y)).astype(BF16)

    vec = pl.BlockSpec((1, D), lambda i: (0, 0))
    return pl.pallas_call(
        body, name="conv_fwd", grid=(S // TM,),
        in_specs=[pl.BlockSpec((TM, D), lambda i: (i, 0)), pl.BlockSpec((TM, D), lambda i: (i, 1)),
                  pl.BlockSpec((HALO, D), lambda i: (jnp.maximum(i * hb - 1, 0), 0)),
                  pl.BlockSpec((HALO, D), lambda i: (jnp.maximum(i * hb - 1, 0), 1)),
                  pl.BlockSpec((1, 2 * D), lambda i: (0, 0)),
                  pl.BlockSpec((32, D), lambda i: (0, 0)), vec, vec, vec],
        out_specs=[pl.BlockSpec((TM, D), lambda i: (i, 0)), pl.BlockSpec((TM, D), lambda i: (i, 0))],
        out_shape=[SDS((S, D), BF16), SDS((S, D), BF16)],
        scratch_shapes=[pltpu.VMEM((HALO + TM, D), F32)],
        compiler_params=_cp(("arbitrary",)),
    )(zrest, zrest, zrest, zrest, b_glu, wdw, b_dw, g_ln, b_ln)


def _mix_fwd(og, lg, a, zrest, x, w_ao, w_co, b_co, w_mx, g_pm):
    S = x.shape[0]
    TM = 512

    def body(o0, o1, o2, l0, l1, l2, a_ref, ga_ref, gc_ref, x_ref, wa_ref, wc_ref, bc_ref, wm_ref, g_ref,
             o_ref, lse_ref, ya_ref, yc_ref, mg_ref, mm_ref, x1_ref):
        m = jnp.maximum(jnp.maximum(l0[...], l1[...]), l2[...])
        e0 = jnp.exp(l0[...] - m)
        e1 = jnp.exp(l1[...] - m)
        e2 = jnp.exp(l2[...] - m)
        den = e0 + e1 + e2
        o = ((e0 * o0[...].astype(F32) + e1 * o1[...].astype(F32) + e2 * o2[...].astype(F32)) / den).astype(BF16)
        o_ref[...] = o
        lse_ref[...] = m + jnp.log(den)
        ya = _dot(o, wa_ref[...]).astype(BF16)
        yc = (_dot(a_ref[...], wc_ref[...]) + bc_ref[...]).astype(BF16)
        ya_ref[...] = ya
        yc_ref[...] = yc
        mg = (_sigmoid(ga_ref[...].astype(F32)) * ya.astype(F32)
              + _sigmoid(gc_ref[...].astype(F32)) * yc.astype(F32)).astype(BF16)
        mg_ref[...] = mg
        mm = _dot(mg, wm_ref[...]).astype(BF16)
        mm_ref[...] = mm
        mf = mm.astype(F32)
        r = lax.rsqrt(jnp.mean(mf * mf, axis=-1, keepdims=True) + RMS_EPS)
        x1_ref[...] = x_ref[...] + mf * r * g_ref[...]

    row512 = pl.BlockSpec((TM, GW), lambda i: (i, 0))
    rowd = pl.BlockSpec((TM, D), lambda i: (i, 0))
    vec = pl.BlockSpec((1, D), lambda i: (0, 0))
    full = lambda r, c: pl.BlockSpec((r, c), lambda i: (0, 0))
    return pl.pallas_call(
        body, name="mix_fwd", grid=(S // TM,),
        in_specs=[row512] * 6 + [rowd, pl.BlockSpec((TM, D), lambda i: (i, 2)), pl.BlockSpec((TM, D), lambda i: (i, 3)),
                                 rowd, full(GW, D), full(D, D), vec, full(D, D), vec],
        out_specs=[row512, row512, rowd, rowd, rowd, rowd, rowd],
        out_shape=[SDS((S, GW), BF16), SDS((S, GW), F32), SDS((S, D), BF16), SDS((S, D), BF16),
                   SDS((S, D), BF16), SDS((S, D), BF16), SDS((S, D), F32)],
        compiler_params=_cp(("arbitrary",)),
    )(og[0], og[1], og[2], lg[0], lg[1], lg[2], a, zrest, zrest, x, w_ao, w_co, b_co, w_mx, g_pm)


def _ffn_fwd(x1, tgt, g_pre, g_post, w_fi, w_fo):
    S = x1.shape[0]
    TM = 512

    def body(x1_ref, t_ref, gp_ref, go_ref, wg_ref, wu_ref, wo_ref,
             h2_ref, gu_ref, df_ref, dx2_ref, loss_ref, dgo_ref, acc):
        i = pl.program_id(0)
        k = pl.program_id(1)

        @pl.when(_first_step(i, k))
        def _():
            loss_ref[...] = jnp.zeros_like(loss_ref)
            dgo_ref[...] = jnp.zeros_like(dgo_ref)

        @pl.when(k == 0)
        def _():
            xf = x1_ref[...]
            r = lax.rsqrt(jnp.mean(xf * xf, axis=-1, keepdims=True) + RMS_EPS)
            h2_ref[...] = (xf * r * gp_ref[...]).astype(BF16)
            acc[...] = jnp.zeros_like(acc)

        h2 = h2_ref[...]
        gt = _dot(h2, wg_ref[...]).astype(BF16)
        up = _dot(h2, wu_ref[...]).astype(BF16)
        gu_ref[:, 0:FFN_T] = gt
        gu_ref[:, FFN_T:] = up
        gf = gt.astype(F32)
        act = (gf * _sigmoid(gf) * up.astype(F32)).astype(BF16)
        acc[...] += _dot(act, wo_ref[...])

        @pl.when(k == NFT - 1)
        def _():
            f = acc[...]
            r = lax.rsqrt(jnp.mean(f * f, axis=-1, keepdims=True) + RMS_EPS)
            nrm = f * r
            e = x1_ref[...] + nrm * go_ref[...] - t_ref[...]
            tot = jnp.sum(jnp.sum(e * e, axis=-1, keepdims=True), axis=0, keepdims=True) * (0.5 / D)
            corner = jnp.logical_and(lax.broadcasted_iota(jnp.int32, (LANE_ROWS, D), 0) == 0,
                                     lax.broadcasted_iota(jnp.int32, (LANE_ROWS, D), 1) == 0)
            loss_ref[...] += jnp.where(corner, tot, 0.0)
            dx2 = e * (1.0 / D)
            dx2_ref[...] = dx2
            dgo_ref[...] += _colsum8(dx2 * nrm)
            dn = dx2 * go_ref[...]
            df_ref[...] = (r * (dn - nrm * jnp.mean(dn * nrm, axis=-1, keepdims=True))).astype(BF16)

    rowd = pl.BlockSpec((TM, D), lambda i, k: (i, 0))
    vec = pl.BlockSpec((1, D), lambda i, k: (0, 0))
    acc8 = pl.BlockSpec((LANE_ROWS, D), lambda i, k: (0, 0))
    return pl.pallas_call(
        body, name="ffn_fwd", grid=(S // TM, NFT),
        in_specs=[rowd, rowd, vec, vec,
                  pl.BlockSpec((D, FFN_T), lambda i, k: (0, k)),
                  pl.BlockSpec((D, FFN_T), lambda i, k: (0, NFT + k)),
                  pl.BlockSpec((FFN_T, D), lambda i, k: (k, 0))],
        out_specs=[rowd, pl.BlockSpec((TM, 2 * FFN_T), lambda i, k: (i, k)), rowd, rowd, acc8, acc8],
        out_shape=[SDS((S, D), BF16), SDS((S, 2 * FFN_H), BF16), SDS((S, D), BF16), SDS((S, D), F32),
                   SDS((LANE_ROWS, D), F32), SDS((LANE_ROWS, D), F32)],
        scratch_shapes=[pltpu.VMEM((TM, D), F32)],
        compiler_params=_cp(("arbitrary", "arbitrary")),
    )(x1, tgt, g_pre, g_post, w_fi, w_fi, w_fo)


def _ffn_bwd(df, gu, x1, dx2, g_pre, w_foT, w_fiT):
    S = x1.shape[0]
    TM = 512

    def body(df_ref, gu_ref, x1_ref, dx2_ref, gp_ref, wo_ref, wg_ref, wu_ref,
             dff_ref, act_ref, dx1_ref, dgp_ref, acc):
        i = pl.program_id(0)
        k = pl.program_id(1)

        @pl.when(_first_step(i, k))
        def _():
            dgp_ref[...] = jnp.zeros_like(dgp_ref)

        @pl.when(k == 0)
        def _():
            acc[...] = jnp.zeros_like(acc)

        dact = _dot(df_ref[...], wo_ref[...])
        g = gu_ref[:, 0:FFN_T].astype(F32)
        u = gu_ref[:, FFN_T:].astype(F32)
        sg = _sigmoid(g)
        sl = g * sg
        act_ref[...] = (sl * u).astype(BF16)
        dg = (dact * u * (sg * (1.0 + g * (1.0 - sg)))).astype(BF16)
        du = (dact * sl).astype(BF16)
        dff_ref[:, 0:FFN_T] = dg
        dff_ref[:, FFN_T:] = du
        acc[...] += _dot(dg, wg_ref[...]) + _dot(du, wu_ref[...])

        @pl.when(k == NFT - 1)
        def _():
            dh = acc[...]
            xf = x1_ref[...]
            r = lax.rsqrt(jnp.mean(xf * xf, axis=-1, keepdims=True) + RMS_EPS)
            nrm = xf * r
            dgp_ref[...] += _colsum8(dh * nrm)
            dn = dh * gp_ref[...]
            dx1_ref[...] = dx2_ref[...] + r * (dn - nrm * jnp.mean(dn * nrm, axis=-1, keepdims=True))

    rowd = pl.BlockSpec((TM, D), lambda i, k: (i, 0))
    return pl.pallas_call(
        body, name="ffn_bwd", grid=(S // TM, NFT),
        in_specs=[rowd, pl.BlockSpec((TM, 2 * FFN_T), lambda i, k: (i, k)), rowd, rowd,
                  pl.BlockSpec((1, D), lambda i, k: (0, 0)),
                  pl.BlockSpec((D, FFN_T), lambda i, k: (0, k)),
                  pl.BlockSpec((FFN_T, D), lambda i, k: (k, 0)),
                  pl.BlockSpec((FFN_T, D), lambda i, k: (NFT + k, 0))],
        out_specs=[pl.BlockSpec((TM, 2 * FFN_T), lambda i, k: (i, k)),
                   pl.BlockSpec((TM, FFN_T), lambda i, k: (i, k)), rowd,
                   pl.BlockSpec((LANE_ROWS, D), lambda i, k: (0, 0))],
        out_shape=[SDS((S, 2 * FFN_H), BF16), SDS((S, FFN_H), BF16), SDS((S, D), F32), SDS((LANE_ROWS, D), F32)],
        scratch_shapes=[pltpu.VMEM((TM, D), F32)],
        compiler_params=_cp(("arbitrary", "arbitrary")),
    )(df, gu, x1, dx2, g_pre, w_foT, w_fiT, w_fiT)


def _mix_bwd(dx1, mm, ya, yc, zrest, g_pm, w_mxT, w_aoT, w_coT):
    S = dx1.shape[0]
    TM = 512

    def body(dx_ref, mm_ref, ya_ref, yc_ref, ga_ref, gc_ref, g_ref, wm_ref, wa_ref, wc_ref,
             dmm_ref, dya_ref, dyc_ref, do_ref, da_ref, dzg_ref, dgpm_ref, dbco_ref):
        i = pl.program_id(0)

        @pl.when(i == 0)
        def _():
            dgpm_ref[...] = jnp.zeros_like(dgpm_ref)
            dbco_ref[...] = jnp.zeros_like(dbco_ref)

        mf = mm_ref[...].astype(F32)
        r = lax.rsqrt(jnp.mean(mf * mf, axis=-1, keepdims=True) + RMS_EPS)
        nrm = mf * r
        dx = dx_ref[...]
        dgpm_ref[...] += _colsum8(dx * nrm)
        dn = dx * g_ref[...]
        dmm = (r * (dn - nrm * jnp.mean(dn * nrm, axis=-1, keepdims=True))).astype(BF16)
        dmm_ref[...] = dmm
        dmg = _dot(dmm, wm_ref[...])
        sa = _sigmoid(ga_ref[...].astype(F32))
        sc = _sigmoid(gc_ref[...].astype(F32))
        dya = (dmg * sa).astype(BF16)
        dyc = (dmg * sc).astype(BF16)
        dya_ref[...] = dya
        dyc_ref[...] = dyc
        dbco_ref[...] += _colsum8(dyc.astype(F32))
        dzg_ref[:, 0:D] = (dmg * ya_ref[...].astype(F32) * (sa * (1.0 - sa))).astype(BF16)
        dzg_ref[:, D:] = (dmg * yc_ref[...].astype(F32) * (sc * (1.0 - sc))).astype(BF16)
        do_ref[...] = _dot(dya, wa_ref[...]).astype(BF16)
        da_ref[...] = _dot(dyc, wc_ref[...]).astype(BF16)

    rowd = pl.BlockSpec((TM, D), lambda i: (i, 0))
    full = lambda r, c: pl.BlockSpec((r, c), lambda i: (0, 0))
    acc8 = pl.BlockSpec((LANE_ROWS, D), lambda i: (0, 0))
    return pl.pallas_call(
        body, name="mix_bwd", grid=(S // TM,),
        in_specs=[rowd, rowd, rowd, rowd, pl.BlockSpec((TM, D), lambda i: (i, 2)),
                  pl.BlockSpec((TM, D), lambda i: (i, 3)), full(1, D), full(D, D), full(D, GW), full(D, D)],
        out_specs=[rowd, rowd, rowd, pl.BlockSpec((TM, GW), lambda i: (i, 0)), rowd,
                   pl.BlockSpec((TM, 2 * D), lambda i: (i, 0)), acc8, acc8],
        out_shape=[SDS((S, D), BF16), SDS((S, D), BF16), SDS((S, D), BF16), SDS((S, GW), BF16), SDS((S, D), BF16),
                   SDS((S, 2 * D), BF16), SDS((LANE_ROWS, D), F32), SDS((LANE_ROWS, D), F32)],
        compiler_params=_cp(("arbitrary",)),
    )(dx1, mm, ya, yc, zrest, zrest, g_pm, w_mxT, w_aoT, w_coT)


def _conv_bwd(da, cv, zrest, b_glu, wdw, g_ln, b_ln):
    S = da.shape[0]
    TM = 512
    HALO = 32
    hb = TM // HALO
    nh = S // HALO

    def body(da_ref, dan_ref, cv_ref, cvn_ref, u_ref, g_ref, uh_ref, gh_ref, bg_ref, w_ref, gl_ref, bl_ref,
             dglu_ref, dbu_ref, dbg_ref, dw_ref, dgl_ref, dbl_ref, dbd_ref, dext, uext):
        i = pl.program_id(0)
        last = i == pl.num_programs(0) - 1

        @pl.when(i == 0)
        def _():
            for ref in (dbu_ref, dbg_ref, dw_ref, dgl_ref, dbl_ref, dbd_ref):
                ref[...] = jnp.zeros_like(ref)

        def ln_bwd(da_v, cv_v):
            cf = cv_v.astype(F32)
            mu = jnp.mean(cf, axis=-1, keepdims=True)
            xc = cf - mu
            rstd = lax.rsqrt(jnp.mean(xc * xc, axis=-1, keepdims=True) + LN_EPS)
            xh = xc * rstd
            y = xh * gl_ref[...] + bl_ref[...]
            sy = _sigmoid(y)
            dy = da_v.astype(F32) * (sy * (1.0 + y * (1.0 - sy)))
            dxh = dy * gl_ref[...]
            dcv = rstd * (dxh - jnp.mean(dxh, axis=-1, keepdims=True)
                          - xh * jnp.mean(dxh * xh, axis=-1, keepdims=True))
            return dcv, dy, xh

        dcv, dy, xh = ln_bwd(da_ref[...], cv_ref[...])
        dgl_ref[...] += _colsum8(dy * xh)
        dbl_ref[...] += _colsum8(dy)
        dbd_ref[...] += _colsum8(dcv)
        dcvn, _, _ = ln_bwd(dan_ref[...], cvn_ref[...])
        dext[0:TM, :] = dcv
        dext[TM:, :] = jnp.where(last, 0.0, dcvn)

        bu = bg_ref[:, 0:D]
        bgt = bg_ref[:, D:2 * D]
        upre = u_ref[...].astype(F32) + bu
        sg = _sigmoid(g_ref[...].astype(F32) + bgt)
        uh = (uh_ref[...].astype(F32) + bu) * _sigmoid(gh_ref[...].astype(F32) + bgt)
        uext[0:HALO, :] = jnp.where(i == 0, 0.0, uh)
        uext[HALO:, :] = upre * sg

        du = jnp.zeros((TM, D), F32)
        for j in range(CONV_W):
            du = du + dext[CONV_W - 1 - j:CONV_W - 1 - j + TM, :] * w_ref[j:j + 1, :]
            sh = HALO - (CONV_W - 1) + j
            dw_ref[j:j + 1, :] += jnp.sum(dcv * uext[sh:sh + TM, :], axis=0, keepdims=True)
        dup = du * sg
        dgp = du * upre * (sg * (1.0 - sg))
        dglu_ref[:, 0:D] = dup.astype(BF16)
        dglu_ref[:, D:] = dgp.astype(BF16)
        dbu_ref[...] += _colsum8(dup.astype(BF16).astype(F32))
        dbg_ref[...] += _colsum8(dgp.astype(BF16).astype(F32))

    rowd = pl.BlockSpec((TM, D), lambda i: (i, 0))
    nxt = pl.BlockSpec((HALO, D), lambda i: (jnp.minimum((i + 1) * hb, nh - 1), 0))
    vec = pl.BlockSpec((1, D), lambda i: (0, 0))
    acc8 = pl.BlockSpec((LANE_ROWS, D), lambda i: (0, 0))
    return pl.pallas_call(
        body, name="conv_bwd", grid=(S // TM,),
        in_specs=[rowd, nxt, rowd, nxt,
                  pl.BlockSpec((TM, D), lambda i: (i, 0)), pl.BlockSpec((TM, D), lambda i: (i, 1)),
                  pl.BlockSpec((HALO, D), lambda i: (jnp.maximum(i * hb - 1, 0), 0)),
                  pl.BlockSpec((HALO, D), lambda i: (jnp.maximum(i * hb - 1, 0), 1)),
                  pl.BlockSpec((1, 2 * D), lambda i: (0, 0)), pl.BlockSpec((32, D), lambda i: (0, 0)), vec, vec],
        out_specs=[pl.BlockSpec((TM, 2 * D), lambda i: (i, 0)), acc8, acc8,
                   pl.BlockSpec((32, D), lambda i: (0, 0)), acc8, acc8, acc8],
        out_shape=[SDS((S, 2 * D), BF16), SDS((LANE_ROWS, D), F32), SDS((LANE_ROWS, D), F32), SDS((32, D), F32),
                   SDS((LANE_ROWS, D), F32), SDS((LANE_ROWS, D), F32), SDS((LANE_ROWS, D), F32)],
        scratch_shapes=[pltpu.VMEM((TM + HALO, D), F32), pltpu.VMEM((HALO + TM, D), F32)],
        compiler_params=_cp(("arbitrary",)),
    )(da, da, cv, cv, zrest, zrest, zrest, zrest, b_glu, wdw, g_ln, b_ln)


def _attn_bwd(zqkv, do, o, lse, bias, gi):
    dil = DILATIONS[gi]
    S = zqkv.shape[0]
    L, TQ, QB, ns = _attn_tile(S, dil)
    nblk = zqkv.shape[1] // GW
    zv = zqkv.reshape(L, dil * zqkv.shape[1])
    dov, ov, lv = (t.reshape(L, dil * GW) for t in (do, o, lse))

    def body(q_ref, kc_ref, kp_ref, vc_ref, vp_ref, do_ref, o_ref, l_ref, b_ref,
             out_ref, db_ref, kext, vext, dkx, dvx, dqn, dqc, dkc, dvc):
        c = pl.program_id(0)
        n = pl.program_id(1)

        @pl.when(_first_step(c, n))
        def _():
            db_ref[...] = jnp.zeros_like(db_ref)

        @pl.when(n < ns)
        def _():
            kext[0:QBLK, :] = kp_ref[...]
            kext[QBLK:, :] = kc_ref[...]
            vext[0:QBLK, :] = vp_ref[...]
            vext[QBLK:, :] = vc_ref[...]
            dkx[...] = jnp.zeros_like(dkx)
            dvx[...] = jnp.zeros_like(dvx)
            col = lax.broadcasted_iota(jnp.int32, (QBLK, KBLK), 1)
            no_prev = jnp.logical_and(n == 0, col < QBLK)
            for b in range(QB):
                rows = slice(b * QBLK, (b + 1) * QBLK)
                krows = slice(b * QBLK, b * QBLK + KBLK)
                for h in range(NH):
                    hc = slice(h * HD, (h + 1) * HD)
                    q = q_ref[rows, hc]
                    k = kext[krows, hc]
                    dob = do_ref[rows, hc]
                    delta = jnp.sum(dob.astype(F32) * o_ref[rows, hc].astype(F32), axis=-1, keepdims=True)
                    s = _dot_nt(q, k) * SCALE + b_ref[0, h]
                    if b == 0:
                        s = jnp.where(no_prev, NEG_INF, s)
                    p = jnp.exp(s - l_ref[rows, h * HD:h * HD + 1])
                    ds = p * (_dot_nt(dob, vext[krows, hc]) - delta)
                    db_ref[h] += ds
                    dsb = ds.astype(BF16)
                    dvx[krows, hc] += _dot_tn(p.astype(BF16), dob)
                    dkx[krows, hc] += _dot_tn(dsb, q) * SCALE
                    dqn[rows, hc] = _dot(dsb, k) * SCALE

        @pl.when(n > 0)
        def _():
            out_ref[:, 0:GW] = dqc[...].astype(BF16)
            out_ref[:, GW:2 * GW] = dkc[...].astype(BF16)
            out_ref[:, 2 * GW:] = dvc[...].astype(BF16)

        @pl.when(jnp.logical_and(n > 0, n < ns))
        def _():
            out_ref[TQ - QBLK:, GW:2 * GW] = (dkc[TQ - QBLK:, :] + dkx[0:QBLK, :]).astype(BF16)
            out_ref[TQ - QBLK:, 2 * GW:] = (dvc[TQ - QBLK:, :] + dvx[0:QBLK, :]).astype(BF16)

        @pl.when(n < ns)
        def _():
            dqc[...] = dqn[...]
            dkc[...] = dkx[QBLK:, :]
            dvc[...] = dvx[QBLK:, :]

    def cur(n):
        return jnp.minimum(n, ns - 1)

    def prev(n):
        return jnp.maximum(cur(n) * QB - 1, 0)

    rows = lambda c, n: (cur(n), c)
    out, dbias = pl.pallas_call(
        body, name=f"attn_bwd_g{gi}", grid=(dil, ns + 1),
        in_specs=[pl.BlockSpec((TQ, GW), lambda c, n: (cur(n), c * nblk + gi)),
                  pl.BlockSpec((TQ, GW), lambda c, n: (cur(n), c * nblk + 3 + gi)),
                  pl.BlockSpec((QBLK, GW), lambda c, n: (prev(n), c * nblk + 3 + gi)),
                  pl.BlockSpec((TQ, GW), lambda c, n: (cur(n), c * nblk + 6 + gi)),
                  pl.BlockSpec((QBLK, GW), lambda c, n: (prev(n), c * nblk + 6 + gi)),
                  pl.BlockSpec((TQ, GW), rows), pl.BlockSpec((TQ, GW), rows), pl.BlockSpec((TQ, GW), rows),
                  pl.BlockSpec((1, NH, QBLK, KBLK), lambda c, n: (gi, 0, 0, 0))],
        out_specs=[pl.BlockSpec((TQ, 3 * GW), lambda c, n: (jnp.maximum(n - 1, 0), c)),
                   pl.BlockSpec((NH, QBLK, KBLK), lambda c, n: (0, 0, 0))],
        out_shape=[SDS((L, dil * 3 * GW), BF16), SDS((NH, QBLK, KBLK), F32)],
        scratch_shapes=[pltpu.VMEM((QBLK + TQ, GW), BF16), pltpu.VMEM((QBLK + TQ, GW), BF16),
                        pltpu.VMEM((QBLK + TQ, GW), F32), pltpu.VMEM((QBLK + TQ, GW), F32),
                        pltpu.VMEM((TQ, GW), F32), pltpu.VMEM((TQ, GW), F32),
                        pltpu.VMEM((TQ, GW), F32), pltpu.VMEM((TQ, GW), F32)],
        compiler_params=_cp(("arbitrary", "arbitrary")),
    )(zv, zv, zv, zv, zv, dov, ov, lv, bias)
    return out.reshape(S, 3 * GW), dbias


def _dz_block(k):
    if k < 9:
        return k % 3, k // 3
    if k < 13:
        return 3, k - 9
    return 4, k - 13


_DZ_SRC = np.array([_dz_block(k)[0] for k in range(17)], np.int32)


def _dz_hold(s):
    uses = [(k, _dz_block(k)[1]) for k in range(17) if _dz_block(k)[0] == s]
    hold = []
    for k in range(17):
        nxt = [b for kk, b in uses if kk >= k]
        hold.append(nxt[0] if nxt else uses[-1][1])
    return np.array(hold, np.int32)


def _dz_specs(TM, tile_of_step, row_of_step, park_rows):
    specs = []
    for s in range(5):
        hold = _dz_hold(s)

        def imap(*ids, s=s, hold=hold):
            k = tile_of_step(*ids)
            r = row_of_step(*ids)
            if park_rows:
                r = jnp.where(_table(_DZ_SRC, k) == s, r, 0)
            return (r, _table(hold, k))
        specs.append(pl.BlockSpec((TM, GW), imap))
    return specs


def _table(tab, k):
    out = jnp.int32(int(tab[0]))
    for idx in range(1, len(tab)):
        out = jnp.where(k == idx, jnp.int32(int(tab[idx])), out)
    return out


def _in_bwd(dzs, w_inT, x, dx1, g):
    S = x.shape[0]
    TM = 512
    nk = 17

    def body(d0, d1, d2, d3, d4, w_ref, x_ref, dx1_ref, g_ref, gx_ref, dg_ref, acc):
        i = pl.program_id(0)
        k = pl.program_id(1)

        @pl.when(_first_step(i, k))
        def _():
            dg_ref[...] = jnp.zeros_like(dg_ref)

        @pl.when(k == 0)
        def _():
            acc[...] = jnp.zeros_like(acc)

        src = _table(_DZ_SRC, k)
        for s, ref in enumerate((d0, d1, d2, d3, d4)):
            @pl.when(src == s)
            def _(ref=ref):
                acc[...] += _dot(ref[...], w_ref[...])

        @pl.when(k == nk - 1)
        def _():
            dh = acc[...]
            xf = x_ref[...]
            r = lax.rsqrt(jnp.mean(xf * xf, axis=-1, keepdims=True) + RMS_EPS)
            nrm = xf * r
            dg_ref[...] += _colsum8(dh * nrm)
            dn = dh * g_ref[...]
            gx_ref[...] = dx1_ref[...] + r * (dn - nrm * jnp.mean(dn * nrm, axis=-1, keepdims=True))

    rowd = pl.BlockSpec((TM, D), lambda i, k: (i, 0))
    return pl.pallas_call(
        body, name="in_bwd", grid=(S // TM, nk),
        in_specs=_dz_specs(TM, lambda i, k: k, lambda i, k: i, False) + [pl.BlockSpec((GW, D), lambda i, k: (k, 0)), rowd, rowd,
                                                  pl.BlockSpec((1, D), lambda i, k: (0, 0))],
        out_specs=[rowd, pl.BlockSpec((LANE_ROWS, D), lambda i, k: (0, 0))],
        out_shape=[SDS((S, D), F32), SDS((LANE_ROWS, D), F32)],
        scratch_shapes=[pltpu.VMEM((TM, D), F32)],
        compiler_params=_cp(("arbitrary", "arbitrary")),
    )(*dzs, w_inT, x, dx1, g)


def _dw_in(dzs, h):
    S = h.shape[0]
    TS = 512
    nk = 17

    def body(d0, d1, d2, d3, d4, h_ref, o_ref, acc):
        m = pl.program_id(0)
        s_ = pl.program_id(1)

        @pl.when(s_ == 0)
        def _():
            acc[...] = jnp.zeros_like(acc)

        src = _table(_DZ_SRC, m)
        for s, ref in enumerate((d0, d1, d2, d3, d4)):
            @pl.when(src == s)
            def _(ref=ref):
                acc[...] += _dot_tn(ref[...], h_ref[...])

        @pl.when(s_ == pl.num_programs(1) - 1)
        def _():
            o_ref[...] = acc[...].astype(BF16)

    return pl.pallas_call(
        body, name="dw_in", grid=(nk, S // TS),
        in_specs=_dz_specs(TS, lambda m, s_: m, lambda m, s_: s_, True) + [pl.BlockSpec((TS, D), lambda m, s_: (s_, 0))],
        out_specs=pl.BlockSpec((GW, D), lambda m, s_: (m, 0)),
        out_shape=SDS((nk * GW, D), BF16),
        scratch_shapes=[pltpu.VMEM((GW, D), F32)],
        compiler_params=_cp(("arbitrary", "arbitrary")),
    )(*dzs, h)


def _mm_tn(a, b, tm, a_maps, name):
    S, N = b.shape
    parts = len(a_maps)
    tp = tm // parts
    nm = len(a_maps[0])
    TS = 512
    tabs = [np.array(t, np.int32) for t in a_maps]

    def body(*refs):
        a_refs = refs[:parts]
        b_ref, o_ref, acc = refs[parts:]
        s_ = pl.program_id(1)

        @pl.when(s_ == 0)
        def _():
            acc[...] = jnp.zeros_like(acc)

        for p, ar in enumerate(a_refs):
            acc[p * tp:(p + 1) * tp, :] += _dot_tn(ar[...], b_ref[...])

        @pl.when(s_ == pl.num_programs(1) - 1)
        def _():
            o_ref[...] = acc[...].astype(BF16)

    return pl.pallas_call(
        body, name=name, grid=(nm, S // TS),
        in_specs=[pl.BlockSpec((TS, tp), lambda m, s_, t=t: (s_, _table(t, m))) for t in tabs]
        + [pl.BlockSpec((TS, N), lambda m, s_: (s_, 0))],
        out_specs=pl.BlockSpec((tm, N), lambda m, s_: (m, 0)),
        out_shape=SDS((nm * tm, N), BF16),
        scratch_shapes=[pltpu.VMEM((tm, N), F32)],
        compiler_params=_cp(("arbitrary", "arbitrary")),
    )(*([a] * parts), b)


def _row_tile(rows, cols, limit=1 << 20):
    if rows * cols * 4 <= limit:
        return rows
    best = None
    for t in range(8, rows, 8):
        if rows % t == 0 and t * cols * 4 <= limit:
            best = t
    return best


def _adamw(w, g, m, v, name):
    R, C = w.shape
    tr = _row_tile(R, C)

    def body(w_ref, g_ref, m_ref, v_ref, d_ref, nm_ref, nv_ref):
        gg = g_ref[...]
        nm = ADAM_B1 * m_ref[...] + (1.0 - ADAM_B1) * gg
        nv = ADAM_B2 * v_ref[...] + (1.0 - ADAM_B2) * (gg * gg)
        m_hat = nm / (1.0 - ADAM_B1 ** ADAM_STEP)
        v_hat = nv / (1.0 - ADAM_B2 ** ADAM_STEP)
        d_ref[...] = -ADAM_LR * (m_hat / (jnp.sqrt(v_hat) + ADAM_EPS) + ADAM_WD * w_ref[...])
        nm_ref[...] = nm
        nv_ref[...] = nv

    spec = pl.BlockSpec((tr, C), lambda i: (i, 0))
    return pl.pallas_call(
        body, name=name, grid=(R // tr,), in_specs=[spec] * 4, out_specs=[spec] * 3,
        out_shape=[SDS((R, C), F32)] * 3, compiler_params=_cp(("arbitrary",)),
    )(w, g, m, v)


_FLIPS = ((1, 0), (0, 1), (1, 1))


def _place():
    x, y, c = lax.axis_index("x"), lax.axis_index("y"), lax.axis_index("c")
    return x, y, c


def _peer_chips(x, y):
    return [((x + fx) % 2, (y + fy) % 2) for fx, fy in _FLIPS]


def _gather_weights(shards):
    nw = len(shards)
    views = [s.reshape(2, s.shape[0] // 2, s.shape[1]) for s in shards]

    def body(*refs):
        ins = refs[:nw]
        outs = refs[nw:2 * nw]
        ici_send, ici_recv, d2d_send, d2d_recv, loc = refs[2 * nw:]
        x, y, c = _place()
        j = 2 * x + y
        chips = _peer_chips(x, y)
        copies = []
        for w in range(nw):
            cp = pltpu.make_async_copy(ins[w], outs[w].at[j], loc.at[w])
            cp.start()
            copies.append(cp)
        sends = []
        for w in range(nw):
            for k, (px, py) in enumerate(chips):
                cp = pltpu.make_async_remote_copy(
                    src_ref=ins[w].at[c], dst_ref=outs[w].at[j, c], send_sem=ici_send.at[w, k],
                    recv_sem=ici_recv.at[w, k], device_id=(px, py, c), device_id_type=MESH)
                cp.start()
                sends.append(cp)
        for w in range(nw):
            for k, (px, py) in enumerate(chips):
                jk = 2 * px + py
                land = outs[w].at[jk, c]
                pltpu.make_async_remote_copy(
                    src_ref=ins[w].at[c], dst_ref=land, send_sem=ici_send.at[w, k],
                    recv_sem=ici_recv.at[w, k], device_id=(px, py, c), device_id_type=MESH).wait_recv()
                cp = pltpu.make_async_remote_copy(
                    src_ref=land, dst_ref=land, send_sem=d2d_send.at[w, k],
                    recv_sem=d2d_recv.at[w, k], device_id=(x, y, 1 - c), device_id_type=MESH)
                cp.start()
                sends.append(cp)
        for w in range(nw):
            for k, (px, py) in enumerate(chips):
                jk = 2 * px + py
                land = outs[w].at[jk, 1 - c]
                pltpu.make_async_remote_copy(
                    src_ref=land, dst_ref=land, send_sem=d2d_send.at[w, k],
                    recv_sem=d2d_recv.at[w, k], device_id=(x, y, 1 - c), device_id_type=MESH).wait_recv()
        for cp in sends:
            cp.wait_send()
        for cp in copies:
            cp.wait()

    outs = pl.pallas_call(
        body, name="gather_weights",
        in_specs=[ANY] * nw, out_specs=[ANY] * nw,
        out_shape=[SDS((4,) + v.shape, BF16) for v in views],
        scratch_shapes=[pltpu.SemaphoreType.DMA((nw, 3)), pltpu.SemaphoreType.DMA((nw, 3)),
                        pltpu.SemaphoreType.DMA((nw, 3)), pltpu.SemaphoreType.DMA((nw, 3)),
                        pltpu.SemaphoreType.DMA((nw,))],
    )(*views)
    return [o.reshape(4 * s.shape[0], s.shape[1]) for o, s in zip(outs, shards)]


def _pair_exchange(grads):
    nw = len(grads)

    def body(*refs):
        ins = refs[:nw]
        outs = refs[nw:2 * nw]
        send, recv = refs[2 * nw:]
        x, y, c = _place()
        cps = []
        for w in range(nw):
            cp = pltpu.make_async_remote_copy(
                src_ref=ins[w].at[:, pl.ds(1 - c, 1)], dst_ref=outs[w], send_sem=send.at[w], recv_sem=recv.at[w],
                device_id=(x, y, 1 - c), device_id_type=MESH)
            cp.start()
            cps.append(cp)
        for cp in cps:
            cp.wait()

    return pl.pallas_call(
        body, name="grad_pair_exchange", in_specs=[ANY] * nw, out_specs=[ANY] * nw,
        out_shape=[SDS((4, 1) + g.shape[2:], BF16) for g in grads],
        scratch_shapes=[pltpu.SemaphoreType.DMA((nw,)), pltpu.SemaphoreType.DMA((nw,))],
    )(*grads)


def _half_tile(rh):
    best = 16
    for t in range(16, 545, 16):
        if rh % t == 0:
            best = t
    return best


def _pair_sum(c_arr, g, got, name):
    _, _, rh, n = g.shape
    tr = _half_tile(rh)

    def body(c_ref, a_ref, b_ref, o_ref):
        o_ref[...] = (a_ref[...].astype(F32) + b_ref[...].astype(F32)).astype(BF16)

    return pl.pallas_call(
        body, name=name,
        grid_spec=pltpu.PrefetchScalarGridSpec(
            num_scalar_prefetch=1, grid=(4, rh // tr),
            in_specs=[pl.BlockSpec((1, 1, tr, n), lambda s, i, c: (s, c[0], i, 0)),
                      pl.BlockSpec((1, 1, tr, n), lambda s, i, c: (s, 0, i, 0))],
            out_specs=pl.BlockSpec((1, 1, tr, n), lambda s, i, c: (s, 0, i, 0))),
        out_shape=SDS((4, 1, rh, n), BF16),
        compiler_params=_cp(("arbitrary", "arbitrary")),
    )(c_arr, g, got)


def _chip_exchange(parts):
    nw = len(parts)

    def body(*refs):
        ins = refs[:nw]
        outs = refs[nw:2 * nw]
        send, recv = refs[2 * nw:]
        x, y, c = _place()
        cps = []
        for w in range(nw):
            for k, (px, py) in enumerate(_peer_chips(x, y)):
                cp = pltpu.make_async_remote_copy(
                    src_ref=ins[w].at[2 * px + py], dst_ref=outs[w].at[k], send_sem=send.at[w, k],
                    recv_sem=recv.at[w, k], device_id=(px, py, c), device_id_type=MESH)
                cp.start()
                cps.append(cp)
        for cp in cps:
            cp.wait()

    return pl.pallas_call(
        body, name="grad_chip_exchange", in_specs=[ANY] * nw, out_specs=[ANY] * nw,
        out_shape=[SDS((3,) + p.shape[1:], BF16) for p in parts],
        scratch_shapes=[pltpu.SemaphoreType.DMA((nw, 3)), pltpu.SemaphoreType.DMA((nw, 3))],
    )(*parts)


def _chip_sum(jc_arr, part, got, name):
    _, _, rh, n = part.shape
    tr = _half_tile(rh)

    def body(jc_ref, a_ref, b_ref, o_ref):
        acc = a_ref[0, 0].astype(F32)
        for k in range(3):
            acc = acc + b_ref[k, 0].astype(F32)
        o_ref[0] = acc

    return pl.pallas_call(
        body, name=name,
        grid_spec=pltpu.PrefetchScalarGridSpec(
            num_scalar_prefetch=1, grid=(rh // tr,),
            in_specs=[pl.BlockSpec((1, 1, tr, n), lambda i, jc: (jc[0], 0, i, 0)),
                      pl.BlockSpec((3, 1, tr, n), lambda i, jc: (0, 0, i, 0))],
            out_specs=pl.BlockSpec((1, tr, n), lambda i, jc: (jc[1], i, 0))),
        out_shape=SDS((2, rh, n), F32),
        compiler_params=_cp(("arbitrary",)),
    )(jc_arr, part, got)


def _half_swap(halves):
    nw = len(halves)

    def body(*refs):
        ins = refs[:nw]
        outs = refs[nw:2 * nw]
        send, recv = refs[2 * nw:]
        x, y, c = _place()
        cps = []
        for w in range(nw):
            cp = pltpu.make_async_remote_copy(
                src_ref=ins[w].at[c], dst_ref=outs[w].at[c], send_sem=send.at[w], recv_sem=recv.at[w],
                device_id=(x, y, 1 - c), device_id_type=MESH)
            cp.start()
            cps.append(cp)
        for cp in cps:
            cp.wait()

    return pl.pallas_call(
        body, name="grad_half_swap", in_specs=[ANY] * nw, out_specs=[ANY] * nw,
        out_shape=[SDS(h.shape, F32) for h in halves],
        input_output_aliases={w: w for w in range(nw)},
        scratch_shapes=[pltpu.SemaphoreType.DMA((nw,)), pltpu.SemaphoreType.DMA((nw,))],
    )(*halves)


def _all_sum_small(part, name):
    R = part.shape[0]

    def body(p_ref, o_ref, land, send, recv):
        x, y, c = _place()
        me = 4 * x + 2 * y + c
        cps = []
        for d in range(1, 8):
            t = (me + d) % 8
            cp = pltpu.make_async_remote_copy(
                src_ref=p_ref, dst_ref=land.at[me], send_sem=send.at[d - 1], recv_sem=recv.at[d - 1],
                device_id=(t // 4, (t // 2) % 2, t % 2), device_id_type=MESH)
            cp.start()
            cps.append(cp)
        land[me] = p_ref[...]
        for cp in cps:
            cp.wait()
        acc = land[0]
        for d in range(1, 8):
            acc = acc + land[d]
        o_ref[...] = acc

    return pl.pallas_call(
        body, name=name,
        in_specs=[pl.BlockSpec(memory_space=pltpu.VMEM)], out_specs=pl.BlockSpec(memory_space=pltpu.VMEM),
        out_shape=SDS((R, D), F32),
        scratch_shapes=[pltpu.VMEM((8, R, D), F32), pltpu.SemaphoreType.DMA((7,)), pltpu.SemaphoreType.DMA((7,))],
        compiler_params=pltpu.CompilerParams(vmem_limit_bytes=VMEM_LIMIT),
    )(part)


def _pad_rows(a, rows):
    return jnp.pad(a, ((0, rows - a.shape[0]), (0, 0)))


def _vec_pack(vs):
    return jnp.concatenate([_pad_rows(v, LANE_ROWS) for v in vs], axis=0)


def kernel(x, rel_bias_table, g_pre_mix, w_in, b_glu, w_dw, b_dw, g_conv_ln, b_conv_ln, w_conv_out, b_conv_out, w_attn_out, w_mix_out, g_post_mix, g_pre_ffn, w_ffn_in, w_ffn_out, g_post_ffn, loss_target, m_rel_bias_table, m_g_pre_mix, m_w_in, m_b_glu, m_w_dw, m_b_dw, m_g_conv_ln, m_b_conv_ln, m_w_conv_out, m_b_conv_out, m_w_attn_out, m_w_mix_out, m_g_post_mix, m_g_pre_ffn, m_w_ffn_in, m_w_ffn_out, m_g_post_ffn, v_rel_bias_table, v_g_pre_mix, v_w_in, v_b_glu, v_w_dw, v_b_dw, v_g_conv_ln, v_b_conv_ln, v_w_conv_out, v_b_conv_out, v_w_attn_out, v_w_mix_out, v_g_post_mix, v_g_pre_ffn, v_w_ffn_in, v_w_ffn_out, v_g_post_ffn):
    S = x.shape[1]
    xs = x.reshape(S, D)
    tgt = loss_target.reshape(S, D)
    cx, cy, cc = _place()
    chip = 2 * cx + cy

    shards = [w_in[0].T.astype(BF16),
              w_ffn_in[0].T.astype(BF16),
              w_attn_out[0].T.astype(BF16),
              w_conv_out[0].astype(BF16),
              w_mix_out[0].astype(BF16),
              w_ffn_out[0].astype(BF16)]
    w_inT, w_fiT, w_aoT, w_co, w_mx, w_fo = _gather_weights(shards)
    w_inN, w_fiN, w_aoN = w_inT.T, w_fiT.T, w_aoT.T
    w_coT, w_mxT, w_foT = w_co.T, w_mx.T, w_fo.T

    buckets_np, valid_np = _bucket_tables()
    buckets = jnp.asarray(buckets_np)
    bias = _bias_expand(rel_bias_table, buckets, jnp.asarray(valid_np)).reshape(3, NH, QBLK, KBLK)
    wdw32 = _pad_rows(w_dw[0], 32)
    wdw_full = _gather_small_cols(wdw32, chip)

    zqkv, zrest, h = _in_proj(xs, g_pre_mix, w_inN)
    og, lg = [], []
    for gi in range(3):
        o_g, l_g = _attn_fwd(zqkv, bias, gi)
        og.append(o_g)
        lg.append(l_g)
    cv, a = _conv_fwd(zrest, b_glu, wdw_full, b_dw, g_conv_ln, b_conv_ln)
    o, lse, ya, yc, mg, mm, x1 = _mix_fwd(og, lg, a, zrest, xs, w_aoN, w_co, b_conv_out, w_mx, g_post_mix)
    h2, gu, df, dx2, loss8, dg_post_ffn = _ffn_fwd(x1, tgt, g_pre_ffn, g_post_ffn, w_fiN, w_fo)

    dff, act, dx1, dg_pre_ffn = _ffn_bwd(df, gu, x1, dx2, g_pre_ffn, w_foT, w_fiT)
    dmm, dya, dyc, do, da, dzg, dg_post_mix, db_conv_out = _mix_bwd(dx1, mm, ya, yc, zrest, g_post_mix, w_mxT, w_aoT, w_coT)
    dglu, db_glu_u, db_glu_g, dw_dw, dg_conv_ln, db_conv_ln, db_dw = _conv_bwd(da, cv, zrest, b_glu, wdw_full, g_conv_ln, b_conv_ln)
    dqkv, dbias = [], []
    for gi in range(3):
        d_g, db_g = _attn_bwd(zqkv, do, o, lse, bias, gi)
        dqkv.append(d_g)
        dbias.append(db_g)
    dtab = _bias_reduce(jnp.concatenate(dbias, axis=0), buckets)
    dzs = dqkv + [dglu, dzg]
    grad_x, dg_pre_mix = _in_bwd(dzs, w_inT, xs, dx1, g_pre_mix)

    ident = lambda n: [list(range(n))]
    g_inT = _dw_in(dzs, h)
    g_fiT = _mm_tn(dff, h2, 512, [[2 * t if t < NFT else 2 * (t - NFT) + 1 for t in range(0, 22, 2)],
                                  [2 * t if t < NFT else 2 * (t - NFT) + 1 for t in range(1, 22, 2)]], "dw_ffn_in")
    g_aoT = _mm_tn(dya, o, 512, ident(2), "dw_attn_out")
    g_co = _mm_tn(a, dyc, 512, ident(2), "dw_conv_out")
    g_mx = _mm_tn(mg, dmm, 512, ident(2), "dw_mix_out")
    g_fo = _mm_tn(act, df, 256, ident(NFT), "dw_ffn_out")

    partials = [g_inT, g_fiT, g_aoT, g_co, g_mx, g_fo]
    views = [g.reshape(4, 2, g.shape[0] // 8, g.shape[1]) for g in partials]
    got = _pair_exchange(views)
    c_arr = jnp.reshape(cc, (1,)).astype(jnp.int32)
    jc_arr = jnp.stack([chip, cc]).astype(jnp.int32)
    names = ("w_in", "w_ffn_in", "w_attn_out", "w_conv_out", "w_mix_out", "w_ffn_out")
    pair = [_pair_sum(c_arr, v, r, f"pair_sum_{n}") for v, r, n in zip(views, got, names)]
    got2 = _chip_exchange(pair)
    halves = [_chip_sum(jc_arr, p, r, f"chip_sum_{n}") for p, r, n in zip(pair, got2, names)]
    red = [t.reshape(t.shape[0] * t.shape[1], t.shape[2]) for t in _half_swap(halves)]
    gw_in, gw_ffn_in, gw_attn_out = red[0].T, red[1].T, red[2].T
    gw_conv_out, gw_mix_out, gw_ffn_out = red[3], red[4], red[5]

    small = jnp.concatenate([loss8, dg_pre_mix, db_glu_u, db_glu_g, db_dw, dg_conv_ln, db_conv_ln, db_conv_out,
                             dg_post_mix, dg_pre_ffn, dg_post_ffn, dtab, dw_dw], axis=0)
    tot = _all_sum_small(small, "small_all_sum")
    row = lambda i: tot[LANE_ROWS * i:LANE_ROWS * i + 1]
    loss = tot[0, 0]
    g_g_pre_mix, g_b_glu = row(1), jnp.concatenate([row(2), row(3)], axis=1)
    g_b_dw, g_g_conv_ln, g_b_conv_ln, g_b_conv_out = row(4), row(5), row(6), row(7)
    g_g_post_mix, g_g_pre_ffn, g_g_post_ffn = row(8), row(9), row(10)
    g_tab = tot[88:112, 0:32].T
    g_w_dw = lax.dynamic_slice(tot[112:112 + CONV_W], (0, 256 * chip), (CONV_W, 256))

    vec_names = ["g_pre_mix", "b_dw", "g_conv_ln", "b_conv_ln", "b_conv_out", "g_post_mix", "g_pre_ffn", "g_post_ffn"]
    vec_w = [g_pre_mix, b_dw, g_conv_ln, b_conv_ln, b_conv_out, g_post_mix, g_pre_ffn, g_post_ffn]
    vec_m = [m_g_pre_mix, m_b_dw, m_g_conv_ln, m_b_conv_ln, m_b_conv_out, m_g_post_mix, m_g_pre_ffn, m_g_post_ffn]
    vec_v = [v_g_pre_mix, v_b_dw, v_g_conv_ln, v_b_conv_ln, v_b_conv_out, v_g_post_mix, v_g_pre_ffn, v_g_post_ffn]
    vec_g = [g_g_pre_mix, g_b_dw, g_g_conv_ln, g_b_conv_ln, g_b_conv_out, g_g_post_mix, g_g_pre_ffn, g_g_post_ffn]

    def pack(vs, glu, tab, dw):
        return jnp.concatenate([_vec_pack(vs), _pad_rows(glu.reshape(2, D), LANE_ROWS),
                                _pad_rows(jnp.pad(tab.T, ((0, 0), (0, D - 32))), 24),
                                _pad_rows(jnp.pad(dw, ((0, 0), (0, D - 256))), 32)], axis=0)

    sw = pack(vec_w, b_glu, rel_bias_table, w_dw[0])
    sg = pack(vec_g, g_b_glu, g_tab, g_w_dw)
    sm = pack(vec_m, m_b_glu, m_rel_bias_table, m_w_dw[0])
    sv = pack(vec_v, v_b_glu, v_rel_bias_table, v_w_dw[0])
    s_out = _adamw(sw, sg, sm, sv, "adamw_small")

    def unpack(t):
        vecs = {n: t[LANE_ROWS * i:LANE_ROWS * i + 1] for i, n in enumerate(vec_names)}
        vecs["b_glu"] = t[64:66].reshape(1, 2 * D)
        vecs["rel_bias_table"] = t[72:96, 0:32].T
        vecs["w_dw"] = t[96:96 + CONV_W, 0:256][None]
        return vecs

    small_out = [unpack(t) for t in s_out]
    big = {}
    for n, w, g, m, v in (("w_in", w_in, gw_in, m_w_in, v_w_in),
                          ("w_conv_out", w_conv_out, gw_conv_out, m_w_conv_out, v_w_conv_out),
                          ("w_attn_out", w_attn_out, gw_attn_out, m_w_attn_out, v_w_attn_out),
                          ("w_mix_out", w_mix_out, gw_mix_out, m_w_mix_out, v_w_mix_out),
                          ("w_ffn_in", w_ffn_in, gw_ffn_in, m_w_ffn_in, v_w_ffn_in),
                          ("w_ffn_out", w_ffn_out, gw_ffn_out, m_w_ffn_out, v_w_ffn_out)):
        big[n] = [t[None] for t in _adamw(w[0], g, m[0], v[0], f"adamw_{n}")]

    order = ["rel_bias_table", "g_pre_mix", "w_in", "b_glu", "w_dw", "b_dw", "g_conv_ln", "b_conv_ln", "w_conv_out",
             "b_conv_out", "w_attn_out", "w_mix_out", "g_post_mix", "g_pre_ffn", "w_ffn_in", "w_ffn_out", "g_post_ffn"]
    grads = {"rel_bias_table": g_tab, "g_pre_mix": g_g_pre_mix, "w_in": gw_in[None], "b_glu": g_b_glu,
             "w_dw": g_w_dw[None], "b_dw": g_b_dw, "g_conv_ln": g_g_conv_ln, "b_conv_ln": g_b_conv_ln,
             "w_conv_out": gw_conv_out[None], "b_conv_out": g_b_conv_out, "w_attn_out": gw_attn_out[None],
             "w_mix_out": gw_mix_out[None], "g_post_mix": g_g_post_mix, "g_pre_ffn": g_g_pre_ffn,
             "w_ffn_in": gw_ffn_in[None], "w_ffn_out": gw_ffn_out[None], "g_post_ffn": g_g_post_ffn}
    outs = [loss, grad_x.reshape(1, S, D)] + [grads[n] for n in order]
    for slot in range(3):
        outs += [big[n][slot] if n in big else small_out[slot][n] for n in order]
    return tuple(outs)


def _gather_small_cols(wdw32, chip):
    placed = lax.dynamic_update_slice(jnp.zeros((32, D), F32), wdw32, (0, 256 * chip))
    return _all_sum_small(placed, "conv_taps_gather") * 0.5
```

```python
import functools
import math

import numpy as np
import jax
import jax.numpy as jnp
from jax import lax
from jax.experimental import pallas as pl
from jax.experimental.pallas import tpu as pltpu

F32 = jnp.float32
BF16 = jnp.bfloat16
SDS = jax.ShapeDtypeStruct
MESH = pl.DeviceIdType.MESH
ANY = pl.BlockSpec(memory_space=pl.ANY)

D = 1024
HD = 64
NH = 8
GW = NH * HD
DILATIONS = (1, 4, 16)
SPAN = 128
QBLK = 128
KBLK = 2 * QBLK
CONV_W = 31
FFN_H = 2816
FFN_T = 256
NFT = FFN_H // FFN_T
RMS_EPS = 1e-6
LN_EPS = 1e-5
NEG_INF = -1e30
SCALE = HD ** -0.5
LANE_ROWS = 8

ADAM_LR, ADAM_B1, ADAM_B2, ADAM_EPS, ADAM_WD, ADAM_STEP = 0.001, 0.9, 0.999, 1e-08, 0.01, 10

VMEM_LIMIT = 56 * 1024 * 1024


def _cp(sem):
    return pltpu.CompilerParams(dimension_semantics=sem, vmem_limit_bytes=VMEM_LIMIT)


def _dot(a, b):
    return jnp.dot(a, b, preferred_element_type=F32)


def _dot_nt(a, b):
    return lax.dot_general(a, b, (((1,), (1,)), ((), ())), preferred_element_type=F32)


def _dot_tn(a, b):
    return lax.dot_general(a, b, (((0,), (0,)), ((), ())), preferred_element_type=F32)


def _sigmoid(v):
    return 0.5 * jnp.tanh(0.5 * v) + 0.5


def _colsum8(v):
    s = jnp.sum(v, axis=0, keepdims=True)
    row = lax.broadcasted_iota(jnp.int32, (LANE_ROWS, v.shape[1]), 0)
    return jnp.where(row == 0, jnp.broadcast_to(s, (LANE_ROWS, v.shape[1])), 0.0)


def _first_step(*ids):
    ok = ids[0] == 0
    for i in ids[1:]:
        ok = jnp.logical_and(ok, i == 0)
    return ok


def _in_proj(x, g, w):
    S = x.shape[0]
    N = w.shape[1]
    TM, TN = 512, 512
    nq = 3 * 3
    NQ = nq * TN

    def body(x_ref, g_ref, w_ref, zq_ref, zr_ref, h_ref):
        xf = x_ref[...]
        r = lax.rsqrt(jnp.mean(xf * xf, axis=-1, keepdims=True) + RMS_EPS)
        h_ref[...] = (xf * r * g_ref[...]).astype(BF16)
        for j in range(N // TN):
            zt = _dot(h_ref[...], w_ref[:, j * TN:(j + 1) * TN]).astype(BF16)
            if j < nq:
                zq_ref[:, j * TN:(j + 1) * TN] = zt
            else:
                zr_ref[:, (j - nq) * TN:(j - nq + 1) * TN] = zt

    return pl.pallas_call(
        body, name="in_proj", grid=(S // TM,),
        in_specs=[pl.BlockSpec((TM, D), lambda i: (i, 0)),
                  pl.BlockSpec((1, D), lambda i: (0, 0)),
                  pl.BlockSpec((D, N), lambda i: (0, 0), pipeline_mode=pl.Buffered(1))],
        out_specs=[pl.BlockSpec((TM, NQ), lambda i: (i, 0)),
                   pl.BlockSpec((TM, N - NQ), lambda i: (i, 0)),
                   pl.BlockSpec((TM, D), lambda i: (i, 0))],
        out_shape=[SDS((S, NQ), BF16), SDS((S, N - NQ), BF16), SDS((S, D), BF16)],
        compiler_params=_cp(("arbitrary",)),
    )(x, g, w)


def _bucket_tables():
    a = np.arange(QBLK, dtype=np.int32)[:, None]
    c = np.arange(KBLK, dtype=np.int32)[None, :]
    off = a - c + QBLK
    valid = ((off >= 0) & (off <= SPAN)).astype(np.float32)
    tabs = []
    for dil in DILATIONS:
        dist = np.maximum(off * dil, 0)
        df = np.maximum(dist, 1).astype(np.float32)
        large = 16 + (np.log(df / np.float32(16)) / np.float32(math.log(2048 / 16)) * np.float32(16)).astype(np.int32)
        large = np.minimum(large, 31)
        tabs.append(np.where(dist < 16, dist, large).astype(np.int32))
    return np.stack(tabs), valid


def _bias_expand(tab, buckets, valid):
    def body(tab_ref, b_ref, v_ref, o_ref):
        for gi in range(3):
            bk = b_ref[gi]
            for h in range(NH):
                acc = jnp.zeros((QBLK, KBLK), F32)
                for b in range(32):
                    acc = jnp.where(bk == b, tab_ref[b, gi * NH + h], acc)
                o_ref[gi * NH + h] = jnp.where(v_ref[...] > 0.5, acc, NEG_INF)

    return pl.pallas_call(
        body, name="bias_expand",
        in_specs=[pl.BlockSpec(memory_space=pltpu.SMEM),
                  pl.BlockSpec(memory_space=pltpu.VMEM), pl.BlockSpec(memory_space=pltpu.VMEM)],
        out_specs=pl.BlockSpec(memory_space=pltpu.VMEM),
        out_shape=SDS((3 * NH, QBLK, KBLK), F32),
    )(tab, buckets, valid)


def _bias_reduce(dbias, buckets):
    def body(d_ref, b_ref, o_ref):
        lane = lax.broadcasted_iota(jnp.int32, (1, D), 1)
        for gi in range(3):
            bk = b_ref[gi]
            for h in range(NH):
                dv = d_ref[gi * NH + h]
                row = jnp.zeros((1, D), F32)
                for b in range(32):
                    m = jnp.where(bk == b, dv, 0.0)
                    val = jnp.sum(jnp.sum(m, axis=0, keepdims=True), axis=1, keepdims=True)
                    row = jnp.where(lane == b, val, row)
                o_ref[gi * NH + h:gi * NH + h + 1, :] = row

    return pl.pallas_call(
        body, name="bias_reduce",
        in_specs=[pl.BlockSpec(memory_space=pltpu.VMEM), pl.BlockSpec(memory_space=pltpu.VMEM)],
        out_specs=pl.BlockSpec(memory_space=pltpu.VMEM),
        out_shape=SDS((3 * NH, D), F32),
    )(dbias, buckets)


def _attn_tile(S, dil):
    L = S // dil
    tq = min(512, L)
    return L, tq, tq // QBLK, L // tq


def _rows_cat(parts):
    return parts[0] if len(parts) == 1 else jnp.concatenate(parts, axis=0)


def _band_scores(q_ref, kext, b_ref, h, QB, no_prev):
    hc = slice(h * HD, (h + 1) * HD)
    parts = []
    for b in range(QB):
        qs = q_ref[b * QBLK:(b + 1) * QBLK, hc] * SCALE
        s = _dot_nt(qs, kext[b * QBLK:b * QBLK + KBLK, hc]) + b_ref[0, h]
        if b == 0:
            s = jnp.where(no_prev, NEG_INF, s)
        parts.append(s)
    return _rows_cat(parts)


def _attn_fwd(zqkv, bias, gi):
    dil = DILATIONS[gi]
    S = zqkv.shape[0]
    L, TQ, QB, ns = _attn_tile(S, dil)
    nblk = zqkv.shape[1] // GW
    zv = zqkv.reshape(L, dil * zqkv.shape[1])

    def body(q_ref, kc_ref, kp_ref, vc_ref, vp_ref, b_ref, o_ref, l_ref, kext, vext):
        n = pl.program_id(1)
        kext[0:QBLK, :] = kp_ref[...]
        kext[QBLK:, :] = kc_ref[...]
        vext[0:QBLK, :] = vp_ref[...]
        vext[QBLK:, :] = vc_ref[...]
        col = lax.broadcasted_iota(jnp.int32, (QBLK, KBLK), 1)
        no_prev = jnp.logical_and(n == 0, col < QBLK)
        for h in range(NH):
            hc = slice(h * HD, (h + 1) * HD)
            s = _band_scores(q_ref, kext, b_ref, h, QB, no_prev)
            m = jnp.max(s, axis=-1, keepdims=True)
            p = jnp.exp(s - m)
            l = jnp.sum(p, axis=-1, keepdims=True)
            pb = p.astype(BF16)
            o = _rows_cat([_dot(pb[b * QBLK:(b + 1) * QBLK], vext[b * QBLK:b * QBLK + KBLK, hc])
                           for b in range(QB)]) / l
            o_ref[:, hc] = o.astype(BF16)
            l_ref[:, hc] = jnp.broadcast_to(m + jnp.log(l), (TQ, HD))

    def prev(n):
        return jnp.maximum(n * QB - 1, 0)

    o, lse = pl.pallas_call(
        body, name=f"attn_fwd_g{gi}", grid=(dil, ns),
        in_specs=[pl.BlockSpec((TQ, GW), lambda c, n: (n, c * nblk + gi)),
                  pl.BlockSpec((TQ, GW), lambda c, n: (n, c * nblk + 3 + gi)),
                  pl.BlockSpec((QBLK, GW), lambda c, n: (prev(n), c * nblk + 3 + gi)),
                  pl.BlockSpec((TQ, GW), lambda c, n: (n, c * nblk + 6 + gi)),
                  pl.BlockSpec((QBLK, GW), lambda c, n: (prev(n), c * nblk + 6 + gi)),
                  pl.BlockSpec((1, NH, QBLK, KBLK), lambda c, n: (gi, 0, 0, 0))],
        out_specs=[pl.BlockSpec((TQ, GW), lambda c, n: (n, c)),
                   pl.BlockSpec((TQ, GW), lambda c, n: (n, c))],
        out_shape=[SDS((L, dil * GW), BF16), SDS((L, dil * GW), F32)],
        scratch_shapes=[pltpu.VMEM((QBLK + TQ, GW), BF16), pltpu.VMEM((QBLK + TQ, GW), BF16)],
        compiler_params=_cp(("arbitrary", "arbitrary")),
    )(zv, zv, zv, zv, zv, bias)
    return o.reshape(S, GW), lse.reshape(S, GW)


CONV_TM = 256
SHIFT_PAD = 24


def _make_shifts(src, sh, n):
    for b in range(1, 8):
        sh[b - 1] = src[b:b + n + SHIFT_PAD, :]


def _shifted(src, sh, off, r0, n, lanes):
    a, b = divmod(off, 8)
    if b == 0:
        return src[8 * a + r0:8 * a + r0 + n, lanes]
    return sh[b - 1, 8 * a + r0:8 * a + r0 + n, lanes]


CONV_RC = 64
LANES = 128


def _tap_blocks(TM):
    return [(r0, slice(l0, l0 + LANES)) for l0 in range(0, D, LANES) for r0 in range(0, TM, CONV_RC)]


def _conv_fwd(zrest, b_glu, wdw, b_dw, g_ln, b_ln):
    S = zrest.shape[0]
    TM = CONV_TM
    HALO = 32
    hb = TM // HALO

    def body(u_ref, g_ref, uh_ref, gh_ref, bg_ref, w_ref, bd_ref, gl_ref, bl_ref, cv_ref, a_ref, ext, sh):
        i = pl.program_id(0)
        bu = bg_ref[:, 0:D]
        bgt = bg_ref[:, D:2 * D]
        uh = (uh_ref[...].astype(F32) + bu) * _sigmoid(gh_ref[...].astype(F32) + bgt)
        ext[0:HALO, :] = jnp.where(i == 0, 0.0, uh)
        ext[HALO:, :] = (u_ref[...].astype(F32) + bu) * _sigmoid(g_ref[...].astype(F32) + bgt)
        _make_shifts(ext, sh, TM)
        acc = jnp.zeros((TM, D), F32)
        for j in range(CONV_W):
            acc = acc + _shifted(ext, sh, HALO - (CONV_W - 1) + j, 0, TM, slice(None)) * w_ref[j:j + 1, :]
        cv = (acc + bd_ref[...]).astype(BF16)
        cv_ref[...] = cv
        cf = cv.astype(F32)
        mu = jnp.mean(cf, axis=-1, keepdims=True)
        xc = cf - mu
        y = xc * lax.rsqrt(jnp.mean(xc * xc, axis=-1, keepdims=True) + LN_EPS) * gl_ref[...] + bl_ref[...]
        a_ref[...] = (y * _sigmoid(y)).astype(BF16)

    vec = pl.BlockSpec((1, D), lambda i: (0, 0))
    return pl.pallas_call(
        body, name="conv_fwd", grid=(S // TM,),
        in_specs=[pl.BlockSpec((TM, D), lambda i: (i, 0)), pl.BlockSpec((TM, D), lambda i: (i, 1)),
                  pl.BlockSpec((HALO, D), lambda i: (jnp.maximum(i * hb - 1, 0), 0)),
                  pl.BlockSpec((HALO, D), lambda i: (jnp.maximum(i * hb - 1, 0), 1)),
                  pl.BlockSpec((1, 2 * D), lambda i: (0, 0)),
                  pl.BlockSpec((32, D), lambda i: (0, 0)), vec, vec, vec],
        out_specs=[pl.BlockSpec((TM, D), lambda i: (i, 0)), pl.BlockSpec((TM, D), lambda i: (i, 0))],
        out_shape=[SDS((S, D), BF16), SDS((S, D), BF16)],
        scratch_shapes=[pltpu.VMEM((HALO + TM, D), F32), pltpu.VMEM((7, TM + SHIFT_PAD, D), F32)],
        compiler_params=_cp(("arbitrary",)),
    )(zrest, zrest, zrest, zrest, b_glu, wdw, b_dw, g_ln, b_ln)


def _mix_fwd(og, lg, a, zrest, x, w_ao, w_co, b_co, w_mx, g_pm):
    S = x.shape[0]
    TM = 512

    def body(o0, o1, o2, l0, l1, l2, a_ref, ga_ref, gc_ref, x_ref, wa_ref, wc_ref, bc_ref, wm_ref, g_ref,
             o_ref, lse_ref, ya_ref, yc_ref, mg_ref, mm_ref, x1_ref):
        m = jnp.maximum(jnp.maximum(l0[...], l1[...]), l2[...])
        e0 = jnp.exp(l0[...] - m)
        e1 = jnp.exp(l1[...] - m)
        e2 = jnp.exp(l2[...] - m)
        den = e0 + e1 + e2
        o = ((e0 * o0[...].astype(F32) + e1 * o1[...].astype(F32) + e2 * o2[...].astype(F32)) / den).astype(BF16)
        o_ref[...] = o
        lse_ref[...] = m + jnp.log(den)
        ya = _dot(o, wa_ref[...]).astype(BF16)
        yc = (_dot(a_ref[...], wc_ref[...]) + bc_ref[...]).astype(BF16)
        ya_ref[...] = ya
        yc_ref[...] = yc
        mg = (_sigmoid(ga_ref[...].astype(F32)) * ya.astype(F32)
              + _sigmoid(gc_ref[...].astype(F32)) * yc.astype(F32)).astype(BF16)
        mg_ref[...] = mg
        mm = _dot(mg, wm_ref[...]).astype(BF16)
        mm_ref[...] = mm
        mf = mm.astype(F32)
        r = lax.rsqrt(jnp.mean(mf * mf, axis=-1, keepdims=True) + RMS_EPS)
        x1_ref[...] = x_ref[...] + mf * r * g_ref[...]

    row512 = pl.BlockSpec((TM, GW), lambda i: (i, 0))
    rowd = pl.BlockSpec((TM, D), lambda i: (i, 0))
    vec = pl.BlockSpec((1, D), lambda i: (0, 0))
    full = lambda r, c: pl.BlockSpec((r, c), lambda i: (0, 0))
    return pl.pallas_call(
        body, name="mix_fwd", grid=(S // TM,),
        in_specs=[row512] * 6 + [rowd, pl.BlockSpec((TM, D), lambda i: (i, 2)), pl.BlockSpec((TM, D), lambda i: (i, 3)),
                                 rowd, full(GW, D), full(D, D), vec, full(D, D), vec],
        out_specs=[row512, row512, rowd, rowd, rowd, rowd, rowd],
        out_shape=[SDS((S, GW), BF16), SDS((S, GW), F32), SDS((S, D), BF16), SDS((S, D), BF16),
                   SDS((S, D), BF16), SDS((S, D), BF16), SDS((S, D), F32)],
        compiler_params=_cp(("arbitrary",)),
    )(og[0], og[1], og[2], lg[0], lg[1], lg[2], a, zrest, zrest, x, w_ao, w_co, b_co, w_mx, g_pm)


def _ffn_fwd(x1, tgt, g_pre, g_post, w_fi, w_fo):
    S = x1.shape[0]
    TM = 512

    def body(x1_ref, t_ref, gp_ref, go_ref, wi_ref, wo_ref,
             h2_ref, gu_ref, df_ref, dx2_ref, loss_ref, dgo_ref):
        i = pl.program_id(0)

        @pl.when(i == 0)
        def _():
            loss_ref[...] = jnp.zeros_like(loss_ref)
            dgo_ref[...] = jnp.zeros_like(dgo_ref)

        xf = x1_ref[...]
        r = lax.rsqrt(jnp.mean(xf * xf, axis=-1, keepdims=True) + RMS_EPS)
        h2_ref[...] = (xf * r * gp_ref[...]).astype(BF16)
        f = jnp.zeros((TM, D), F32)
        for k in range(NFT):
            gt = _dot(h2_ref[...], wi_ref[:, k * FFN_T:(k + 1) * FFN_T]).astype(BF16)
            up = _dot(h2_ref[...], wi_ref[:, FFN_H + k * FFN_T:FFN_H + (k + 1) * FFN_T]).astype(BF16)
            gu_ref[:, 2 * k * FFN_T:(2 * k + 1) * FFN_T] = gt
            gu_ref[:, (2 * k + 1) * FFN_T:(2 * k + 2) * FFN_T] = up
            gf = gt.astype(F32)
            act = (gf * _sigmoid(gf) * up.astype(F32)).astype(BF16)
            f = f + _dot(act, wo_ref[k * FFN_T:(k + 1) * FFN_T, :])
        r = lax.rsqrt(jnp.mean(f * f, axis=-1, keepdims=True) + RMS_EPS)
        nrm = f * r
        e = x1_ref[...] + nrm * go_ref[...] - t_ref[...]
        tot = jnp.sum(jnp.sum(e * e, axis=-1, keepdims=True), axis=0, keepdims=True) * (0.5 / D)
        corner = jnp.logical_and(lax.broadcasted_iota(jnp.int32, (LANE_ROWS, D), 0) == 0,
                                 lax.broadcasted_iota(jnp.int32, (LANE_ROWS, D), 1) == 0)
        loss_ref[...] += jnp.where(corner, tot, 0.0)
        dx2 = e * (1.0 / D)
        dx2_ref[...] = dx2
        dgo_ref[...] += _colsum8(dx2 * nrm)
        dn = dx2 * go_ref[...]
        df_ref[...] = (r * (dn - nrm * jnp.mean(dn * nrm, axis=-1, keepdims=True))).astype(BF16)

    rowd = pl.BlockSpec((TM, D), lambda i: (i, 0))
    vec = pl.BlockSpec((1, D), lambda i: (0, 0))
    acc8 = pl.BlockSpec((LANE_ROWS, D), lambda i: (0, 0))
    return pl.pallas_call(
        body, name="ffn_fwd", grid=(S // TM,),
        in_specs=[rowd, rowd, vec, vec,
                  pl.BlockSpec((D, 2 * FFN_H), lambda i: (0, 0), pipeline_mode=pl.Buffered(1)),
                  pl.BlockSpec((FFN_H, D), lambda i: (0, 0), pipeline_mode=pl.Buffered(1))],
        out_specs=[rowd, pl.BlockSpec((TM, 2 * FFN_H), lambda i: (i, 0)), rowd, rowd, acc8, acc8],
        out_shape=[SDS((S, D), BF16), SDS((S, 2 * FFN_H), BF16), SDS((S, D), BF16), SDS((S, D), F32),
                   SDS((LANE_ROWS, D), F32), SDS((LANE_ROWS, D), F32)],
        compiler_params=_cp(("arbitrary",)),
    )(x1, tgt, g_pre, g_post, w_fi, w_fo)


def _ffn_bwd(df, gu, x1, dx2, g_pre, w_foT, w_fiT):
    S = x1.shape[0]
    TM = 512

    def body_act(df_ref, gu_ref, wo_ref, dff_ref, act_ref):
        for k in range(NFT):
            dact = _dot(df_ref[...], wo_ref[:, k * FFN_T:(k + 1) * FFN_T])
            g = gu_ref[:, 2 * k * FFN_T:(2 * k + 1) * FFN_T].astype(F32)
            u = gu_ref[:, (2 * k + 1) * FFN_T:(2 * k + 2) * FFN_T].astype(F32)
            sg = _sigmoid(g)
            sl = g * sg
            act_ref[:, k * FFN_T:(k + 1) * FFN_T] = (sl * u).astype(BF16)
            dff_ref[:, 2 * k * FFN_T:(2 * k + 1) * FFN_T] = (dact * u * (sg * (1.0 + g * (1.0 - sg)))).astype(BF16)
            dff_ref[:, (2 * k + 1) * FFN_T:(2 * k + 2) * FFN_T] = (dact * sl).astype(BF16)

    rowd = pl.BlockSpec((TM, D), lambda i: (i, 0))
    wide = pl.BlockSpec((TM, 2 * FFN_H), lambda i: (i, 0))
    dff, act = pl.pallas_call(
        body_act, name="ffn_bwd_act", grid=(S // TM,),
        in_specs=[rowd, wide, pl.BlockSpec((D, FFN_H), lambda i: (0, 0), pipeline_mode=pl.Buffered(1))],
        out_specs=[wide, pl.BlockSpec((TM, FFN_H), lambda i: (i, 0))],
        out_shape=[SDS((S, 2 * FFN_H), BF16), SDS((S, FFN_H), BF16)],
        compiler_params=_cp(("arbitrary",)),
    )(df, gu, w_foT)

    KC = 512
    nkc = 2 * FFN_H // KC

    def body_in(dff_ref, x1_ref, dx2_ref, gp_ref, wi_ref, dx1_ref, dgp_ref):
        i = pl.program_id(0)

        @pl.when(i == 0)
        def _():
            dgp_ref[...] = jnp.zeros_like(dgp_ref)

        dh = jnp.zeros((TM, D), F32)
        for k in range(nkc):
            dh = dh + _dot(dff_ref[:, k * KC:k * KC + FFN_T], wi_ref[k * FFN_T:(k + 1) * FFN_T, :]) \
                + _dot(dff_ref[:, k * KC + FFN_T:(k + 1) * KC], wi_ref[FFN_H + k * FFN_T:FFN_H + (k + 1) * FFN_T, :])
        xf = x1_ref[...]
        r = lax.rsqrt(jnp.mean(xf * xf, axis=-1, keepdims=True) + RMS_EPS)
        nrm = xf * r
        dgp_ref[...] += _colsum8(dh * nrm)
        dn = dh * gp_ref[...]
        dx1_ref[...] = dx2_ref[...] + r * (dn - nrm * jnp.mean(dn * nrm, axis=-1, keepdims=True))

    dx1, dgp = pl.pallas_call(
        body_in, name="ffn_bwd_in", grid=(S // TM,),
        in_specs=[wide, rowd, rowd, pl.BlockSpec((1, D), lambda i: (0, 0)),
                  pl.BlockSpec((2 * FFN_H, D), lambda i: (0, 0), pipeline_mode=pl.Buffered(1))],
        out_specs=[rowd, pl.BlockSpec((LANE_ROWS, D), lambda i: (0, 0))],
        out_shape=[SDS((S, D), F32), SDS((LANE_ROWS, D), F32)],
        compiler_params=_cp(("arbitrary",)),
    )(dff, x1, dx2, g_pre, w_fiT)
    return dff, act, dx1, dgp


def _mix_bwd(dx1, mm, ya, yc, zrest, g_pm, w_mxT, w_aoT, w_coT):
    S = dx1.shape[0]
    TM = 512

    def body(dx_ref, mm_ref, ya_ref, yc_ref, ga_ref, gc_ref, g_ref, wm_ref, wa_ref, wc_ref,
             dmm_ref, dya_ref, dyc_ref, do_ref, da_ref, dzg_ref, dgpm_ref, dbco_ref):
        i = pl.program_id(0)

        @pl.when(i == 0)
        def _():
            dgpm_ref[...] = jnp.zeros_like(dgpm_ref)
            dbco_ref[...] = jnp.zeros_like(dbco_ref)

        mf = mm_ref[...].astype(F32)
        r = lax.rsqrt(jnp.mean(mf * mf, axis=-1, keepdims=True) + RMS_EPS)
        nrm = mf * r
        dx = dx_ref[...]
        dgpm_ref[...] += _colsum8(dx * nrm)
        dn = dx * g_ref[...]
        dmm = (r * (dn - nrm * jnp.mean(dn * nrm, axis=-1, keepdims=True))).astype(BF16)
        dmm_ref[...] = dmm
        dmg = _dot(dmm, wm_ref[...])
        sa = _sigmoid(ga_ref[...].astype(F32))
        sc = _sigmoid(gc_ref[...].astype(F32))
        dya = (dmg * sa).astype(BF16)
        dyc = (dmg * sc).astype(BF16)
        dya_ref[...] = dya
        dyc_ref[...] = dyc
        dbco_ref[...] += _colsum8(dyc.astype(F32))
        dzg_ref[:, 0:D] = (dmg * ya_ref[...].astype(F32) * (sa * (1.0 - sa))).astype(BF16)
        dzg_ref[:, D:] = (dmg * yc_ref[...].astype(F32) * (sc * (1.0 - sc))).astype(BF16)
        do_ref[...] = _dot(dya, wa_ref[...]).astype(BF16)
        da_ref[...] = _dot(dyc, wc_ref[...]).astype(BF16)

    rowd = pl.BlockSpec((TM, D), lambda i: (i, 0))
    full = lambda r, c: pl.BlockSpec((r, c), lambda i: (0, 0))
    acc8 = pl.BlockSpec((LANE_ROWS, D), lambda i: (0, 0))
    return pl.pallas_call(
        body, name="mix_bwd", grid=(S // TM,),
        in_specs=[rowd, rowd, rowd, rowd, pl.BlockSpec((TM, D), lambda i: (i, 2)),
                  pl.BlockSpec((TM, D), lambda i: (i, 3)), full(1, D), full(D, D), full(D, GW), full(D, D)],
        out_specs=[rowd, rowd, rowd, pl.BlockSpec((TM, GW), lambda i: (i, 0)), rowd,
                   pl.BlockSpec((TM, 2 * D), lambda i: (i, 0)), acc8, acc8],
        out_shape=[SDS((S, D), BF16), SDS((S, D), BF16), SDS((S, D), BF16), SDS((S, GW), BF16), SDS((S, D), BF16),
                   SDS((S, 2 * D), BF16), SDS((LANE_ROWS, D), F32), SDS((LANE_ROWS, D), F32)],
        compiler_params=_cp(("arbitrary",)),
    )(dx1, mm, ya, yc, zrest, zrest, g_pm, w_mxT, w_aoT, w_coT)


def _conv_bwd(da, cv, zrest, b_glu, wdw, g_ln, b_ln):
    S = da.shape[0]
    TM = CONV_TM
    HALO = 32
    hb = TM // HALO
    nh = S // HALO

    def body(da_ref, dan_ref, cv_ref, cvn_ref, u_ref, g_ref, uh_ref, gh_ref, bg_ref, w_ref, gl_ref, bl_ref,
             dglu_ref, dbu_ref, dbg_ref, dw_ref, dgl_ref, dbl_ref, dbd_ref, dext, uext, dsh, ush, du_scr, dw8):
        i = pl.program_id(0)
        last = i == pl.num_programs(0) - 1

        @pl.when(i == 0)
        def _():
            for ref in (dbu_ref, dbg_ref, dw8, dgl_ref, dbl_ref, dbd_ref):
                ref[...] = jnp.zeros_like(ref)

        def ln_bwd(da_v, cv_v):
            cf = cv_v.astype(F32)
            mu = jnp.mean(cf, axis=-1, keepdims=True)
            xc = cf - mu
            rstd = lax.rsqrt(jnp.mean(xc * xc, axis=-1, keepdims=True) + LN_EPS)
            xh = xc * rstd
            y = xh * gl_ref[...] + bl_ref[...]
            sy = _sigmoid(y)
            dy = da_v.astype(F32) * (sy * (1.0 + y * (1.0 - sy)))
            dxh = dy * gl_ref[...]
            dcv = rstd * (dxh - jnp.mean(dxh, axis=-1, keepdims=True)
                          - xh * jnp.mean(dxh * xh, axis=-1, keepdims=True))
            return dcv, dy, xh

        dcv, dy, xh = ln_bwd(da_ref[...], cv_ref[...])
        dgl_ref[...] += _colsum8(dy * xh)
        dbl_ref[...] += _colsum8(dy)
        dbd_ref[...] += _colsum8(dcv)
        dcvn, _, _ = ln_bwd(dan_ref[...], cvn_ref[...])
        dext[0:TM, :] = dcv
        dext[TM:, :] = jnp.where(last, 0.0, dcvn)

        bu = bg_ref[:, 0:D]
        bgt = bg_ref[:, D:2 * D]
        upre = u_ref[...].astype(F32) + bu
        sg = _sigmoid(g_ref[...].astype(F32) + bgt)
        uh = (uh_ref[...].astype(F32) + bu) * _sigmoid(gh_ref[...].astype(F32) + bgt)
        uext[0:HALO, :] = jnp.where(i == 0, 0.0, uh)
        uext[HALO:, :] = upre * sg

        _make_shifts(dext, dsh, TM)
        _make_shifts(uext, ush, TM)
        for r0, lanes in _tap_blocks(TM):
            acc = jnp.zeros((CONV_RC, LANES), F32)
            for j in range(CONV_W):
                acc = acc + _shifted(dext, dsh, CONV_W - 1 - j, r0, CONV_RC, lanes) * w_ref[j:j + 1, lanes]
            du_scr[r0:r0 + CONV_RC, lanes] = acc
        for l0 in range(0, D, LANES):
            lanes = slice(l0, l0 + LANES)
            accs = [jnp.zeros((LANE_ROWS, LANES), F32)] * CONV_W
            for r0 in range(0, TM, CONV_RC):
                dc = dext[r0:r0 + CONV_RC, lanes]
                for j in range(CONV_W):
                    prod = dc * _shifted(uext, ush, HALO - (CONV_W - 1) + j, r0, CONV_RC, lanes)
                    accs[j] = accs[j] + jnp.sum(prod.reshape(CONV_RC // LANE_ROWS, LANE_ROWS, LANES), axis=0)
            for j in range(CONV_W):
                dw8[j, :, lanes] += accs[j]

        @pl.when(last)
        def _():
            for j in range(CONV_W):
                dw_ref[j:j + 1, :] = jnp.sum(dw8[j], axis=0, keepdims=True)
            dw_ref[CONV_W:, :] = jnp.zeros((32 - CONV_W, D), F32)

        du = du_scr[...]
        dup = du * sg
        dgp = du * upre * (sg * (1.0 - sg))
        dglu_ref[:, 0:D] = dup.astype(BF16)
        dglu_ref[:, D:] = dgp.astype(BF16)
        dbu_ref[...] += _colsum8(dup.astype(BF16).astype(F32))
        dbg_ref[...] += _colsum8(dgp.astype(BF16).astype(F32))

    rowd = pl.BlockSpec((TM, D), lambda i: (i, 0))
    nxt = pl.BlockSpec((HALO, D), lambda i: (jnp.minimum((i + 1) * hb, nh - 1), 0))
    vec = pl.BlockSpec((1, D), lambda i: (0, 0))
    acc8 = pl.BlockSpec((LANE_ROWS, D), lambda i: (0, 0))
    return pl.pallas_call(
        body, name="conv_bwd", grid=(S // TM,),
        in_specs=[rowd, nxt, rowd, nxt,
                  pl.BlockSpec((TM, D), lambda i: (i, 0)), pl.BlockSpec((TM, D), lambda i: (i, 1)),
                  pl.BlockSpec((HALO, D), lambda i: (jnp.maximum(i * hb - 1, 0), 0)),
                  pl.BlockSpec((HALO, D), lambda i: (jnp.maximum(i * hb - 1, 0), 1)),
                  pl.BlockSpec((1, 2 * D), lambda i: (0, 0)), pl.BlockSpec((32, D), lambda i: (0, 0)), vec, vec],
        out_specs=[pl.BlockSpec((TM, 2 * D), lambda i: (i, 0)), acc8, acc8,
                   pl.BlockSpec((32, D), lambda i: (0, 0)), acc8, acc8, acc8],
        out_shape=[SDS((S, 2 * D), BF16), SDS((LANE_ROWS, D), F32), SDS((LANE_ROWS, D), F32), SDS((32, D), F32),
                   SDS((LANE_ROWS, D), F32), SDS((LANE_ROWS, D), F32), SDS((LANE_ROWS, D), F32)],
        scratch_shapes=[pltpu.VMEM((TM + HALO, D), F32), pltpu.VMEM((HALO + TM, D), F32),
                        pltpu.VMEM((7, TM + SHIFT_PAD, D), F32), pltpu.VMEM((7, TM + SHIFT_PAD, D), F32),
                        pltpu.VMEM((TM, D), F32), pltpu.VMEM((32, LANE_ROWS, D), F32)],
        compiler_params=_cp(("arbitrary",)),
    )(da, da, cv, cv, zrest, zrest, zrest, zrest, b_glu, wdw, g_ln, b_ln)


def _attn_bwd(zqkv, do, o, lse, bias, gi):
    dil = DILATIONS[gi]
    S = zqkv.shape[0]
    L, TQ, QB, ns = _attn_tile(S, dil)
    nblk = zqkv.shape[1] // GW
    zv = zqkv.reshape(L, dil * zqkv.shape[1])
    dov, ov, lv = (t.reshape(L, dil * GW) for t in (do, o, lse))

    def body(q_ref, kc_ref, kp_ref, vc_ref, vp_ref, do_ref, o_ref, l_ref, b_ref,
             out_ref, db_ref, kext, vext, dkx, dvx, dqn, dqc, dkc, dvc):
        c = pl.program_id(0)
        n = pl.program_id(1)

        @pl.when(_first_step(c, n))
        def _():
            db_ref[...] = jnp.zeros_like(db_ref)

        @pl.when(n < ns)
        def _():
            kext[0:QBLK, :] = kp_ref[...]
            kext[QBLK:, :] = kc_ref[...]
            vext[0:QBLK, :] = vp_ref[...]
            vext[QBLK:, :] = vc_ref[...]
            col = lax.broadcasted_iota(jnp.int32, (QBLK, KBLK), 1)
            no_prev = jnp.logical_and(n == 0, col < QBLK)

            def overlap_add(parts):
                segs = [parts[0][0:QBLK]]
                for b in range(1, QB):
                    segs.append(parts[b - 1][QBLK:] + parts[b][0:QBLK])
                segs.append(parts[QB - 1][QBLK:])
                return jnp.concatenate(segs, axis=0)

            for h in range(NH):
                hc = slice(h * HD, (h + 1) * HD)
                blk = lambda b: slice(b * QBLK, (b + 1) * QBLK)
                win = lambda b: slice(b * QBLK, b * QBLK + KBLK)
                s = _band_scores(q_ref, kext, b_ref, h, QB, no_prev)
                p = jnp.exp(s - l_ref[:, h * HD:h * HD + 1])
                do_h = do_ref[:, hc]
                delta = jnp.sum(do_h.astype(F32) * o_ref[:, hc].astype(F32), axis=-1, keepdims=True)
                dp = _rows_cat([_dot_nt(do_h[blk(b)], vext[win(b), hc]) for b in range(QB)])
                ds = p * (dp - delta)
                dsum = ds[blk(0)]
                for b in range(1, QB):
                    dsum = dsum + ds[blk(b)]
                db_ref[h] += dsum
                pb = p.astype(BF16)
                dsb = ds.astype(BF16)
                dvx[:, hc] = overlap_add([_dot_tn(pb[blk(b)], do_h[blk(b)]) for b in range(QB)])
                dkx[:, hc] = overlap_add([_dot_tn(dsb[blk(b)], q_ref[blk(b), hc] * SCALE) for b in range(QB)])
                dqn[:, hc] = _rows_cat([_dot(dsb[blk(b)], kext[win(b), hc]) for b in range(QB)]) * SCALE

        @pl.when(n > 0)
        def _():
            out_ref[:, 0:GW] = dqc[...].astype(BF16)
            out_ref[:, GW:2 * GW] = dkc[...].astype(BF16)
            out_ref[:, 2 * GW:] = dvc[...].astype(BF16)

        @pl.when(jnp.logical_and(n > 0, n < ns))
        def _():
            out_ref[TQ - QBLK:, GW:2 * GW] = (dkc[TQ - QBLK:, :] + dkx[0:QBLK, :]).astype(BF16)
            out_ref[TQ - QBLK:, 2 * GW:] = (dvc[TQ - QBLK:, :] + dvx[0:QBLK, :]).astype(BF16)

        @pl.when(n < ns)
        def _():
            dqc[...] = dqn[...]
            dkc[...] = dkx[QBLK:, :]
            dvc[...] = dvx[QBLK:, :]

    def cur(n):
        return jnp.minimum(n, ns - 1)

    def prev(n):
        return jnp.maximum(cur(n) * QB - 1, 0)

    rows = lambda c, n: (cur(n), c)
    out, dbias = pl.pallas_call(
        body, name=f"attn_bwd_g{gi}", grid=(dil, ns + 1),
        in_specs=[pl.BlockSpec((TQ, GW), lambda c, n: (cur(n), c * nblk + gi)),
                  pl.BlockSpec((TQ, GW), lambda c, n: (cur(n), c * nblk + 3 + gi)),
                  pl.BlockSpec((QBLK, GW), lambda c, n: (prev(n), c * nblk + 3 + gi)),
                  pl.BlockSpec((TQ, GW), lambda c, n: (cur(n), c * nblk + 6 + gi)),
                  pl.BlockSpec((QBLK, GW), lambda c, n: (prev(n), c * nblk + 6 + gi)),
                  pl.BlockSpec((TQ, GW), rows), pl.BlockSpec((TQ, GW), rows), pl.BlockSpec((TQ, GW), rows),
                  pl.BlockSpec((1, NH, QBLK, KBLK), lambda c, n: (gi, 0, 0, 0))],
        out_specs=[pl.BlockSpec((TQ, 3 * GW), lambda c, n: (jnp.maximum(n - 1, 0), c)),
                   pl.BlockSpec((NH, QBLK, KBLK), lambda c, n: (0, 0, 0))],
        out_shape=[SDS((L, dil * 3 * GW), BF16), SDS((NH, QBLK, KBLK), F32)],
        scratch_shapes=[pltpu.VMEM((QBLK + TQ, GW), BF16), pltpu.VMEM((QBLK + TQ, GW), BF16),
                        pltpu.VMEM((QBLK + TQ, GW), F32), pltpu.VMEM((QBLK + TQ, GW), F32),
                        pltpu.VMEM((TQ, GW), F32), pltpu.VMEM((TQ, GW), F32),
                        pltpu.VMEM((TQ, GW), F32), pltpu.VMEM((TQ, GW), F32)],
        compiler_params=_cp(("arbitrary", "arbitrary")),
    )(zv, zv, zv, zv, zv, dov, ov, lv, bias)
    return out.reshape(S, 3 * GW), dbias


def _dz_block(k):
    if k < 9:
        return k % 3, k // 3
    if k < 13:
        return 3, k - 9
    return 4, k - 13


_DZ_SRC = np.array([_dz_block(k)[0] for k in range(17)], np.int32)


def _dz_hold(s):
    uses = [(k, _dz_block(k)[1]) for k in range(17) if _dz_block(k)[0] == s]
    hold = []
    for k in range(17):
        nxt = [b for kk, b in uses if kk >= k]
        hold.append(nxt[0] if nxt else uses[-1][1])
    return np.array(hold, np.int32)


def _dz_specs(TM, tile_of_step, row_of_step, park_rows):
    specs = []
    for s in range(5):
        hold = _dz_hold(s)

        def imap(*ids, s=s, hold=hold):
            k = tile_of_step(*ids)
            r = row_of_step(*ids)
            if park_rows:
                r = jnp.where(_table(_DZ_SRC, k) == s, r, 0)
            return (r, _table(hold, k))
        specs.append(pl.BlockSpec((TM, GW), imap))
    return specs


def _table(tab, k):
    out = jnp.int32(int(tab[0]))
    for idx in range(1, len(tab)):
        out = jnp.where(k == idx, jnp.int32(int(tab[idx])), out)
    return out


def _in_bwd(dzs, w_inT, x, dx1, g):
    S = x.shape[0]
    TM = 512

    def body(d0, d1, d2, d3, d4, w_ref, x_ref, dx1_ref, g_ref, gx_ref, dg_ref):
        i = pl.program_id(0)

        @pl.when(i == 0)
        def _():
            dg_ref[...] = jnp.zeros_like(dg_ref)

        srcs = (d0, d1, d2, d3, d4)
        dh = jnp.zeros((TM, D), F32)
        for k in range(17):
            s, blk = _dz_block(k)
            dh = dh + _dot(srcs[s][:, blk * GW:(blk + 1) * GW], w_ref[k * GW:(k + 1) * GW, :])
        xf = x_ref[...]
        r = lax.rsqrt(jnp.mean(xf * xf, axis=-1, keepdims=True) + RMS_EPS)
        nrm = xf * r
        dg_ref[...] += _colsum8(dh * nrm)
        dn = dh * g_ref[...]
        gx_ref[...] = dx1_ref[...] + r * (dn - nrm * jnp.mean(dn * nrm, axis=-1, keepdims=True))

    rowd = pl.BlockSpec((TM, D), lambda i: (i, 0))
    return pl.pallas_call(
        body, name="in_bwd", grid=(S // TM,),
        in_specs=[pl.BlockSpec((TM, t.shape[1]), lambda i: (i, 0)) for t in dzs]
        + [pl.BlockSpec(w_inT.shape, lambda i: (0, 0), pipeline_mode=pl.Buffered(1)), rowd, rowd,
           pl.BlockSpec((1, D), lambda i: (0, 0))],
        out_specs=[rowd, pl.BlockSpec((LANE_ROWS, D), lambda i: (0, 0))],
        out_shape=[SDS((S, D), F32), SDS((LANE_ROWS, D), F32)],
        compiler_params=_cp(("arbitrary",)),
    )(*dzs, w_inT, x, dx1, g)


def _dw_in(dzs, h):
    S = h.shape[0]
    TS = min(2048, S)
    nk = 17

    def body(d0, d1, d2, d3, d4, h_ref, o_ref, acc):
        m = pl.program_id(0)
        s_ = pl.program_id(1)

        @pl.when(s_ == 0)
        def _():
            acc[...] = jnp.zeros_like(acc)

        src = _table(_DZ_SRC, m)
        for s, ref in enumerate((d0, d1, d2, d3, d4)):
            @pl.when(src == s)
            def _(ref=ref):
                acc[...] += _dot_tn(ref[...], h_ref[...])

        @pl.when(s_ == pl.num_programs(1) - 1)
        def _():
            o_ref[...] = acc[...].astype(BF16)

    return pl.pallas_call(
        body, name="dw_in", grid=(nk, S // TS),
        in_specs=_dz_specs(TS, lambda m, s_: m, lambda m, s_: s_, True) + [pl.BlockSpec((TS, D), lambda m, s_: (s_, 0))],
        out_specs=pl.BlockSpec((GW, D), lambda m, s_: (m, 0)),
        out_shape=SDS((nk * GW, D), BF16),
        scratch_shapes=[pltpu.VMEM((GW, D), F32)],
        compiler_params=_cp(("arbitrary", "arbitrary")),
    )(*dzs, h)


def _mm_tn(a, b, tm, a_maps, name):
    S, N = b.shape
    parts = len(a_maps)
    tp = tm // parts
    nm = len(a_maps[0])
    TS = min(2048, S)
    tabs = [np.array(t, np.int32) for t in a_maps]

    def body(*refs):
        a_refs = refs[:parts]
        b_ref, o_ref, acc = refs[parts:]
        s_ = pl.program_id(1)

        @pl.when(s_ == 0)
        def _():
            acc[...] = jnp.zeros_like(acc)

        for p, ar in enumerate(a_refs):
            acc[p * tp:(p + 1) * tp, :] += _dot_tn(ar[...], b_ref[...])

        @pl.when(s_ == pl.num_programs(1) - 1)
        def _():
            o_ref[...] = acc[...].astype(BF16)

    return pl.pallas_call(
        body, name=name, grid=(nm, S // TS),
        in_specs=[pl.BlockSpec((TS, tp), lambda m, s_, t=t: (s_, _table(t, m))) for t in tabs]
        + [pl.BlockSpec((TS, N), lambda m, s_: (s_, 0))],
        out_specs=pl.BlockSpec((tm, N), lambda m, s_: (m, 0)),
        out_shape=SDS((nm * tm, N), BF16),
        scratch_shapes=[pltpu.VMEM((tm, N), F32)],
        compiler_params=_cp(("arbitrary", "arbitrary")),
    )(*([a] * parts), b)


def _row_tile(rows, cols, limit=1 << 20):
    if rows * cols * 4 <= limit:
        return rows
    best = None
    for t in range(8, rows, 8):
        if rows % t == 0 and t * cols * 4 <= limit:
            best = t
    return best


def _adamw(w, g, m, v, name):
    R, C = w.shape
    tr = _row_tile(R, C)

    def body(w_ref, g_ref, m_ref, v_ref, d_ref, nm_ref, nv_ref):
        gg = g_ref[...]
        nm = ADAM_B1 * m_ref[...] + (1.0 - ADAM_B1) * gg
        nv = ADAM_B2 * v_ref[...] + (1.0 - ADAM_B2) * (gg * gg)
        m_hat = nm / (1.0 - ADAM_B1 ** ADAM_STEP)
        v_hat = nv / (1.0 - ADAM_B2 ** ADAM_STEP)
        d_ref[...] = -ADAM_LR * (m_hat / (jnp.sqrt(v_hat) + ADAM_EPS) + ADAM_WD * w_ref[...])
        nm_ref[...] = nm
        nv_ref[...] = nv

    spec = pl.BlockSpec((tr, C), lambda i: (i, 0))
    return pl.pallas_call(
        body, name=name, grid=(R // tr,), in_specs=[spec] * 4, out_specs=[spec] * 3,
        out_shape=[SDS((R, C), F32)] * 3, compiler_params=_cp(("arbitrary",)),
    )(w, g, m, v)


_FLIPS = ((1, 0), (0, 1), (1, 1))


def _place():
    x, y, c = lax.axis_index("x"), lax.axis_index("y"), lax.axis_index("c")
    return x, y, c


def _peer_chips(x, y):
    return [((x + fx) % 2, (y + fy) % 2) for fx, fy in _FLIPS]


def _gather_weights(shards):
    nw = len(shards)
    views = [s.reshape(2, s.shape[0] // 2, s.shape[1]) for s in shards]

    def body(*refs):
        ins = refs[:nw]
        outs = refs[nw:2 * nw]
        ici_send, ici_recv, d2d_send, d2d_recv, loc = refs[2 * nw:]
        x, y, c = _place()
        j = 2 * x + y
        chips = _peer_chips(x, y)
        copies = []
        for w in range(nw):
            cp = pltpu.make_async_copy(ins[w], outs[w].at[j], loc.at[w])
            cp.start()
            copies.append(cp)
        sends = []
        for w in range(nw):
            for k, (px, py) in enumerate(chips):
                cp = pltpu.make_async_remote_copy(
                    src_ref=ins[w].at[c], dst_ref=outs[w].at[j, c], send_sem=ici_send.at[w, k],
                    recv_sem=ici_recv.at[w, k], device_id=(px, py, c), device_id_type=MESH)
                cp.start()
                sends.append(cp)
        for w in range(nw):
            for k, (px, py) in enumerate(chips):
                jk = 2 * px + py
                land = outs[w].at[jk, c]
                pltpu.make_async_remote_copy(
                    src_ref=ins[w].at[c], dst_ref=land, send_sem=ici_send.at[w, k],
                    recv_sem=ici_recv.at[w, k], device_id=(px, py, c), device_id_type=MESH).wait_recv()
                cp = pltpu.make_async_remote_copy(
                    src_ref=land, dst_ref=land, send_sem=d2d_send.at[w, k],
                    recv_sem=d2d_recv.at[w, k], device_id=(x, y, 1 - c), device_id_type=MESH)
                cp.start()
                sends.append(cp)
        for w in range(nw):
            for k, (px, py) in enumerate(chips):
                jk = 2 * px + py
                land = outs[w].at[jk, 1 - c]
                pltpu.make_async_remote_copy(
                    src_ref=land, dst_ref=land, send_sem=d2d_send.at[w, k],
                    recv_sem=d2d_recv.at[w, k], device_id=(x, y, 1 - c), device_id_type=MESH).wait_recv()
        for cp in sends:
            cp.wait_send()
        for cp in copies:
            cp.wait()

    outs = pl.pallas_call(
        body, name="gather_weights",
        in_specs=[ANY] * nw, out_specs=[ANY] * nw,
        out_shape=[SDS((4,) + v.shape, BF16) for v in views],
        scratch_shapes=[pltpu.SemaphoreType.DMA((nw, 3)), pltpu.SemaphoreType.DMA((nw, 3)),
                        pltpu.SemaphoreType.DMA((nw, 3)), pltpu.SemaphoreType.DMA((nw, 3)),
                        pltpu.SemaphoreType.DMA((nw,))],
    )(*views)
    return [o.reshape(4 * s.shape[0], s.shape[1]) for o, s in zip(outs, shards)]


def _pair_exchange(grads):
    nw = len(grads)

    def body(*refs):
        ins = refs[:nw]
        outs = refs[nw:2 * nw]
        send, recv = refs[2 * nw:]
        x, y, c = _place()
        cps = []
        for w in range(nw):
            cp = pltpu.make_async_remote_copy(
                src_ref=ins[w].at[:, pl.ds(1 - c, 1)], dst_ref=outs[w], send_sem=send.at[w], recv_sem=recv.at[w],
                device_id=(x, y, 1 - c), device_id_type=MESH)
            cp.start()
            cps.append(cp)
        for cp in cps:
            cp.wait()

    return pl.pallas_call(
        body, name="grad_pair_exchange", in_specs=[ANY] * nw, out_specs=[ANY] * nw,
        out_shape=[SDS((4, 1) + g.shape[2:], BF16) for g in grads],
        scratch_shapes=[pltpu.SemaphoreType.DMA((nw,)), pltpu.SemaphoreType.DMA((nw,))],
    )(*grads)


def _half_tile(rh):
    best = 16
    for t in range(16, 545, 16):
        if rh % t == 0:
            best = t
    return best


def _pair_sum(c_arr, g, got, name):
    _, _, rh, n = g.shape
    tr = _half_tile(rh)

    def body(c_ref, a_ref, b_ref, o_ref):
        o_ref[...] = (a_ref[...].astype(F32) + b_ref[...].astype(F32)).astype(BF16)

    return pl.pallas_call(
        body, name=name,
        grid_spec=pltpu.PrefetchScalarGridSpec(
            num_scalar_prefetch=1, grid=(4, rh // tr),
            in_specs=[pl.BlockSpec((1, 1, tr, n), lambda s, i, c: (s, c[0], i, 0)),
                      pl.BlockSpec((1, 1, tr, n), lambda s, i, c: (s, 0, i, 0))],
            out_specs=pl.BlockSpec((1, 1, tr, n), lambda s, i, c: (s, 0, i, 0))),
        out_shape=SDS((4, 1, rh, n), BF16),
        compiler_params=_cp(("arbitrary", "arbitrary")),
    )(c_arr, g, got)


def _chip_exchange(parts):
    nw = len(parts)

    def body(*refs):
        ins = refs[:nw]
        outs = refs[nw:2 * nw]
        send, recv = refs[2 * nw:]
        x, y, c = _place()
        cps = []
        for w in range(nw):
            for k, (px, py) in enumerate(_peer_chips(x, y)):
                cp = pltpu.make_async_remote_copy(
                    src_ref=ins[w].at[2 * px + py], dst_ref=outs[w].at[k], send_sem=send.at[w, k],
                    recv_sem=recv.at[w, k], device_id=(px, py, c), device_id_type=MESH)
                cp.start()
                cps.append(cp)
        for cp in cps:
            cp.wait()

    return pl.pallas_call(
        body, name="grad_chip_exchange", in_specs=[ANY] * nw, out_specs=[ANY] * nw,
        out_shape=[SDS((3,) + p.shape[1:], BF16) for p in parts],
        scratch_shapes=[pltpu.SemaphoreType.DMA((nw, 3)), pltpu.SemaphoreType.DMA((nw, 3))],
    )(*parts)


def _chip_sum(jc_arr, part, got, name):
    _, _, rh, n = part.shape
    tr = _half_tile(rh)

    def body(jc_ref, a_ref, b_ref, o_ref):
        acc = a_ref[0, 0].astype(F32)
        for k in range(3):
            acc = acc + b_ref[k, 0].astype(F32)
        o_ref[0] = acc

    return pl.pallas_call(
        body, name=name,
        grid_spec=pltpu.PrefetchScalarGridSpec(
            num_scalar_prefetch=1, grid=(rh // tr,),
            in_specs=[pl.BlockSpec((1, 1, tr, n), lambda i, jc: (jc[0], 0, i, 0)),
                      pl.BlockSpec((3, 1, tr, n), lambda i, jc: (0, 0, i, 0))],
            out_specs=pl.BlockSpec((1, tr, n), lambda i, jc: (jc[1], i, 0))),
        out_shape=SDS((2, rh, n), F32),
        compiler_params=_cp(("arbitrary",)),
    )(jc_arr, part, got)


def _half_swap(halves):
    nw = len(halves)

    def body(*refs):
        ins = refs[:nw]
        outs = refs[nw:2 * nw]
        send, recv = refs[2 * nw:]
        x, y, c = _place()
        cps = []
        for w in range(nw):
            cp = pltpu.make_async_remote_copy(
                src_ref=ins[w].at[c], dst_ref=outs[w].at[c], send_sem=send.at[w], recv_sem=recv.at[w],
                device_id=(x, y, 1 - c), device_id_type=MESH)
            cp.start()
            cps.append(cp)
        for cp in cps:
            cp.wait()

    return pl.pallas_call(
        body, name="grad_half_swap", in_specs=[ANY] * nw, out_specs=[ANY] * nw,
        out_shape=[SDS(h.shape, F32) for h in halves],
        input_output_aliases={w: w for w in range(nw)},
        scratch_shapes=[pltpu.SemaphoreType.DMA((nw,)), pltpu.SemaphoreType.DMA((nw,))],
    )(*halves)


def _all_sum_small(part, name):
    R = part.shape[0]

    def body(p_ref, o_ref, land, send, recv):
        x, y, c = _place()
        me = 4 * x + 2 * y + c
        cps = []
        for d in range(1, 8):
            t = (me + d) % 8
            cp = pltpu.make_async_remote_copy(
                src_ref=p_ref, dst_ref=land.at[me], send_sem=send.at[d - 1], recv_sem=recv.at[d - 1],
                device_id=(t // 4, (t // 2) % 2, t % 2), device_id_type=MESH)
            cp.start()
            cps.append(cp)
        land[me] = p_ref[...]
        for cp in cps:
            cp.wait()
        acc = land[0]
        for d in range(1, 8):
            acc = acc + land[d]
        o_ref[...] = acc

    return pl.pallas_call(
        body, name=name,
        in_specs=[pl.BlockSpec(memory_space=pltpu.VMEM)], out_specs=pl.BlockSpec(memory_space=pltpu.VMEM),
        out_shape=SDS((R, D), F32),
        scratch_shapes=[pltpu.VMEM((8, R, D), F32), pltpu.SemaphoreType.DMA((7,)), pltpu.SemaphoreType.DMA((7,))],
        compiler_params=pltpu.CompilerParams(vmem_limit_bytes=VMEM_LIMIT),
    )(part)


def _pad_rows(a, rows):
    return jnp.pad(a, ((0, rows - a.shape[0]), (0, 0)))


def _vec_pack(vs):
    return jnp.concatenate([_pad_rows(v, LANE_ROWS) for v in vs], axis=0)


def kernel(x, rel_bias_table, g_pre_mix, w_in, b_glu, w_dw, b_dw, g_conv_ln, b_conv_ln, w_conv_out, b_conv_out, w_attn_out, w_mix_out, g_post_mix, g_pre_ffn, w_ffn_in, w_ffn_out, g_post_ffn, loss_target, m_rel_bias_table, m_g_pre_mix, m_w_in, m_b_glu, m_w_dw, m_b_dw, m_g_conv_ln, m_b_conv_ln, m_w_conv_out, m_b_conv_out, m_w_attn_out, m_w_mix_out, m_g_post_mix, m_g_pre_ffn, m_w_ffn_in, m_w_ffn_out, m_g_post_ffn, v_rel_bias_table, v_g_pre_mix, v_w_in, v_b_glu, v_w_dw, v_b_dw, v_g_conv_ln, v_b_conv_ln, v_w_conv_out, v_b_conv_out, v_w_attn_out, v_w_mix_out, v_g_post_mix, v_g_pre_ffn, v_w_ffn_in, v_w_ffn_out, v_g_post_ffn):
    S = x.shape[1]
    xs = x.reshape(S, D)
    tgt = loss_target.reshape(S, D)
    cx, cy, cc = _place()
    chip = 2 * cx + cy

    shards = [w_in[0].T.astype(BF16),
              w_ffn_in[0].T.astype(BF16),
              w_attn_out[0].T.astype(BF16),
              w_conv_out[0].astype(BF16),
              w_mix_out[0].astype(BF16),
              w_ffn_out[0].astype(BF16)]
    w_inT, w_fiT, w_aoT, w_co, w_mx, w_fo = _gather_weights(shards)
    w_inN, w_fiN, w_aoN = w_inT.T, w_fiT.T, w_aoT.T
    w_coT, w_mxT, w_foT = w_co.T, w_mx.T, w_fo.T

    buckets_np, valid_np = _bucket_tables()
    buckets = jnp.asarray(buckets_np)
    bias = _bias_expand(rel_bias_table, buckets, jnp.asarray(valid_np)).reshape(3, NH, QBLK, KBLK)
    wdw32 = _pad_rows(w_dw[0], 32)
    wdw_full = _gather_small_cols(wdw32, chip)

    zqkv, zrest, h = _in_proj(xs, g_pre_mix, w_inN)
    og, lg = [], []
    for gi in range(3):
        o_g, l_g = _attn_fwd(zqkv, bias, gi)
        og.append(o_g)
        lg.append(l_g)
    cv, a = _conv_fwd(zrest, b_glu, wdw_full, b_dw, g_conv_ln, b_conv_ln)
    o, lse, ya, yc, mg, mm, x1 = _mix_fwd(og, lg, a, zrest, xs, w_aoN, w_co, b_conv_out, w_mx, g_post_mix)
    h2, gu, df, dx2, loss8, dg_post_ffn = _ffn_fwd(x1, tgt, g_pre_ffn, g_post_ffn, w_fiN, w_fo)

    dff, act, dx1, dg_pre_ffn = _ffn_bwd(df, gu, x1, dx2, g_pre_ffn, w_foT, w_fiT)
    dmm, dya, dyc, do, da, dzg, dg_post_mix, db_conv_out = _mix_bwd(dx1, mm, ya, yc, zrest, g_post_mix, w_mxT, w_aoT, w_coT)
    dglu, db_glu_u, db_glu_g, dw_dw, dg_conv_ln, db_conv_ln, db_dw = _conv_bwd(da, cv, zrest, b_glu, wdw_full, g_conv_ln, b_conv_ln)
    dqkv, dbias = [], []
    for gi in range(3):
        d_g, db_g = _attn_bwd(zqkv, do, o, lse, bias, gi)
        dqkv.append(d_g)
        dbias.append(db_g)
    dtab = _bias_reduce(jnp.concatenate(dbias, axis=0), buckets)
    dzs = dqkv + [dglu, dzg]
    grad_x, dg_pre_mix = _in_bwd(dzs, w_inT, xs, dx1, g_pre_mix)

    ident = lambda n: [list(range(n))]
    g_inT = _dw_in(dzs, h)
    g_fiT = _mm_tn(dff, h2, 512, [[2 * t if t < NFT else 2 * (t - NFT) + 1 for t in range(0, 22, 2)],
                                  [2 * t if t < NFT else 2 * (t - NFT) + 1 for t in range(1, 22, 2)]], "dw_ffn_in")
    g_aoT = _mm_tn(dya, o, 512, ident(2), "dw_attn_out")
    g_co = _mm_tn(a, dyc, 512, ident(2), "dw_conv_out")
    g_mx = _mm_tn(mg, dmm, 512, ident(2), "dw_mix_out")
    g_fo = _mm_tn(act, df, FFN_H // 2, ident(2), "dw_ffn_out")

    partials = [g_inT, g_fiT, g_aoT, g_co, g_mx, g_fo]
    views = [g.reshape(4, 2, g.shape[0] // 8, g.shape[1]) for g in partials]
    got = _pair_exchange(views)
    c_arr = jnp.reshape(cc, (1,)).astype(jnp.int32)
    jc_arr = jnp.stack([chip, cc]).astype(jnp.int32)
    names = ("w_in", "w_ffn_in", "w_attn_out", "w_conv_out", "w_mix_out", "w_ffn_out")
    pair = [_pair_sum(c_arr, v, r, f"pair_sum_{n}") for v, r, n in zip(views, got, names)]
    got2 = _chip_exchange(pair)
    halves = [_chip_sum(jc_arr, p, r, f"chip_sum_{n}") for p, r, n in zip(pair, got2, names)]
    red = [t.reshape(t.shape[0] * t.shape[1], t.shape[2]) for t in _half_swap(halves)]
    gw_in, gw_ffn_in, gw_attn_out = red[0].T, red[1].T, red[2].T
    gw_conv_out, gw_mix_out, gw_ffn_out = red[3], red[4], red[5]

    small = jnp.concatenate([loss8, dg_pre_mix, db_glu_u, db_glu_g, db_dw, dg_conv_ln, db_conv_ln, db_conv_out,
                             dg_post_mix, dg_pre_ffn, dg_post_ffn, dtab, dw_dw], axis=0)
    tot = _all_sum_small(small, "small_all_sum")
    row = lambda i: tot[LANE_ROWS * i:LANE_ROWS * i + 1]
    loss = tot[0, 0]
    g_g_pre_mix, g_b_glu = row(1), jnp.concatenate([row(2), row(3)], axis=1)
    g_b_dw, g_g_conv_ln, g_b_conv_ln, g_b_conv_out = row(4), row(5), row(6), row(7)
    g_g_post_mix, g_g_pre_ffn, g_g_post_ffn = row(8), row(9), row(10)
    g_tab = tot[88:112, 0:32].T
    g_w_dw = lax.dynamic_slice(tot[112:112 + CONV_W], (0, 256 * chip), (CONV_W, 256))

    vec_names = ["g_pre_mix", "b_dw", "g_conv_ln", "b_conv_ln", "b_conv_out", "g_post_mix", "g_pre_ffn", "g_post_ffn"]
    vec_w = [g_pre_mix, b_dw, g_conv_ln, b_conv_ln, b_conv_out, g_post_mix, g_pre_ffn, g_post_ffn]
    vec_m = [m_g_pre_mix, m_b_dw, m_g_conv_ln, m_b_conv_ln, m_b_conv_out, m_g_post_mix, m_g_pre_ffn, m_g_post_ffn]
    vec_v = [v_g_pre_mix, v_b_dw, v_g_conv_ln, v_b_conv_ln, v_b_conv_out, v_g_post_mix, v_g_pre_ffn, v_g_post_ffn]
    vec_g = [g_g_pre_mix, g_b_dw, g_g_conv_ln, g_b_conv_ln, g_b_conv_out, g_g_post_mix, g_g_pre_ffn, g_g_post_ffn]

    def pack(vs, glu, tab, dw):
        return jnp.concatenate([_vec_pack(vs), _pad_rows(glu.reshape(2, D), LANE_ROWS),
                                _pad_rows(jnp.pad(tab.T, ((0, 0), (0, D - 32))), 24),
                                _pad_rows(jnp.pad(dw, ((0, 0), (0, D - 256))), 32)], axis=0)

    sw = pack(vec_w, b_glu, rel_bias_table, w_dw[0])
    sg = pack(vec_g, g_b_glu, g_tab, g_w_dw)
    sm = pack(vec_m, m_b_glu, m_rel_bias_table, m_w_dw[0])
    sv = pack(vec_v, v_b_glu, v_rel_bias_table, v_w_dw[0])
    s_out = _adamw(sw, sg, sm, sv, "adamw_small")

    def unpack(t):
        vecs = {n: t[LANE_ROWS * i:LANE_ROWS * i + 1] for i, n in enumerate(vec_names)}
        vecs["b_glu"] = t[64:66].reshape(1, 2 * D)
        vecs["rel_bias_table"] = t[72:96, 0:32].T
        vecs["w_dw"] = t[96:96 + CONV_W, 0:256][None]
        return vecs

    small_out = [unpack(t) for t in s_out]
    big = {}
    for n, w, g, m, v in (("w_in", w_in, gw_in, m_w_in, v_w_in),
                          ("w_conv_out", w_conv_out, gw_conv_out, m_w_conv_out, v_w_conv_out),
                          ("w_attn_out", w_attn_out, gw_attn_out, m_w_attn_out, v_w_attn_out),
                          ("w_mix_out", w_mix_out, gw_mix_out, m_w_mix_out, v_w_mix_out),
                          ("w_ffn_in", w_ffn_in, gw_ffn_in, m_w_ffn_in, v_w_ffn_in),
                          ("w_ffn_out", w_ffn_out, gw_ffn_out, m_w_ffn_out, v_w_ffn_out)):
        big[n] = [t[None] for t in _adamw(w[0], g, m[0], v[0], f"adamw_{n}")]

    order = ["rel_bias_table", "g_pre_mix", "w_in", "b_glu", "w_dw", "b_dw", "g_conv_ln", "b_conv_ln", "w_conv_out",
             "b_conv_out", "w_attn_out", "w_mix_out", "g_post_mix", "g_pre_ffn", "w_ffn_in", "w_ffn_out", "g_post_ffn"]
    grads = {"rel_bias_table": g_tab, "g_pre_mix": g_g_pre_mix, "w_in": gw_in[None], "b_glu": g_b_glu,
             "w_dw": g_w_dw[None], "b_dw": g_b_dw, "g_conv_ln": g_g_conv_ln, "b_conv_ln": g_b_conv_ln,
             "w_conv_out": gw_conv_out[None], "b_conv_out": g_b_conv_out, "w_attn_out": gw_attn_out[None],
             "w_mix_out": gw_mix_out[None], "g_post_mix": g_g_post_mix, "g_pre_ffn": g_g_pre_ffn,
             "w_ffn_in": gw_ffn_in[None], "w_ffn_out": gw_ffn_out[None], "g_post_ffn": g_g_post_ffn}
    outs = [loss, grad_x.reshape(1, S, D)] + [grads[n] for n in order]
    for slot in range(3):
        outs += [big[n][slot] if n in big else small_out[slot][n] for n in order]
    return tuple(outs)


def _gather_small_cols(wdw32, chip):
    placed = lax.dynamic_update_slice(jnp.zeros((32, D), F32), wdw32, (0, 256 * chip))
    return _all_sum_small(placed, "conv_taps_gather") * 0.5
```

```python
import functools
import math

import numpy as np
import jax
import jax.numpy as jnp
from jax import lax
from jax.experimental import pallas as pl
from jax.experimental.pallas import tpu as pltpu

F32 = jnp.float32
BF16 = jnp.bfloat16
SDS = jax.ShapeDtypeStruct
MESH = pl.DeviceIdType.MESH
ANY = pl.BlockSpec(memory_space=pl.ANY)

D = 1024
HD = 64
NH = 8
GW = NH * HD
ATTN_COLS = 3 * GW
DILATIONS = (1, 4, 16)
SPAN = 128
QBLK = 128
KBLK = 2 * QBLK
CONV_W = 31
FFN_H = 2816
FFN_T = 256
NFT = FFN_H // FFN_T
RMS_EPS = 1e-6
LN_EPS = 1e-5
NEG_INF = -1e30
SCALE = HD ** -0.5
LANE_ROWS = 8
LANES = 128

ADAM_LR, ADAM_B1, ADAM_B2, ADAM_EPS, ADAM_WD, ADAM_STEP = 0.001, 0.9, 0.999, 1e-08, 0.01, 10

VMEM_LIMIT = 56 * 1024 * 1024


def _cp(sem):
    return pltpu.CompilerParams(dimension_semantics=sem, vmem_limit_bytes=VMEM_LIMIT)


def _dot(a, b):
    return jnp.dot(a, b, preferred_element_type=F32)


def _dot_nt(a, b):
    return lax.dot_general(a, b, (((1,), (1,)), ((), ())), preferred_element_type=F32)


def _dot_tn(a, b):
    return lax.dot_general(a, b, (((0,), (0,)), ((), ())), preferred_element_type=F32)


def _sigmoid(v):
    return 0.5 * jnp.tanh(0.5 * v) + 0.5


def _colsum8(v):
    s = jnp.sum(v, axis=0, keepdims=True)
    row = lax.broadcasted_iota(jnp.int32, (LANE_ROWS, v.shape[1]), 0)
    return jnp.where(row == 0, jnp.broadcast_to(s, (LANE_ROWS, v.shape[1])), 0.0)


def _first_step(*ids):
    ok = ids[0] == 0
    for i in ids[1:]:
        ok = jnp.logical_and(ok, i == 0)
    return ok


def _col_scratch(n, width):
    return pltpu.VMEM((width // LANES, n, LANES), F32)


def _store_cols(scr, v):
    for lb in range(scr.shape[0]):
        scr[lb] = v[:, lb * LANES:(lb + 1) * LANES]


def _load_cols(scr):
    return jnp.concatenate([scr[lb] for lb in range(scr.shape[0])], axis=1)


def _split_residues(scr, dil, put):
    nb, n, _ = scr.shape
    for c in range(dil):
        put(c, jnp.concatenate([scr[lb, pl.ds(c, n // dil, stride=dil), :] for lb in range(nb)], axis=1))


def _merge_residues(scr, dil, get):
    nb, n, _ = scr.shape
    for c in range(dil):
        v = get(c)
        for lb in range(nb):
            scr[lb, pl.ds(c, n // dil, stride=dil), :] = v[:, lb * LANES:(lb + 1) * LANES]


def _residue_shape(S, dil, width):
    return (dil, S // dil, width)


def _residue_spec(TM, dil, width):
    return pl.BlockSpec((dil, TM // dil, width), lambda i: (0, i, 0))


def _in_proj_rest(x, g, w):
    S = x.shape[0]
    N = w.shape[1]
    TM, TN = 512, 512

    def body(x_ref, g_ref, w_ref, zr_ref, h0_ref, h1_ref, h2_ref, hf_scr):
        xf = x_ref[...]
        r = lax.rsqrt(jnp.mean(xf * xf, axis=-1, keepdims=True) + RMS_EPS)
        hf = xf * r * g_ref[...]
        h0_ref[...] = hf.astype(BF16)
        _store_cols(hf_scr, hf)
        for dil, ref in ((DILATIONS[1], h1_ref), (DILATIONS[2], h2_ref)):
            def put(c, v, ref=ref):
                ref[c] = v.astype(BF16)
            _split_residues(hf_scr, dil, put)
        for j in range(N // TN):
            zr_ref[:, j * TN:(j + 1) * TN] = _dot(h0_ref[...], w_ref[:, j * TN:(j + 1) * TN]).astype(BF16)

    return pl.pallas_call(
        body, name="in_proj_rest", grid=(S // TM,),
        in_specs=[pl.BlockSpec((TM, D), lambda i: (i, 0)),
                  pl.BlockSpec((1, D), lambda i: (0, 0)),
                  pl.BlockSpec((D, N), lambda i: (0, 0), pipeline_mode=pl.Buffered(1))],
        out_specs=[pl.BlockSpec((TM, N), lambda i: (i, 0)), pl.BlockSpec((TM, D), lambda i: (i, 0)),
                   _residue_spec(TM, DILATIONS[1], D), _residue_spec(TM, DILATIONS[2], D)],
        out_shape=[SDS((S, N), BF16), SDS((S, D), BF16),
                   SDS(_residue_shape(S, DILATIONS[1], D), BF16), SDS(_residue_shape(S, DILATIONS[2], D), BF16)],
        scratch_shapes=[_col_scratch(TM, D)],
        compiler_params=_cp(("arbitrary",)),
    )(x, g, w)


def _in_proj_qkv(h, w):
    S = h.shape[0]
    TM = 512

    def body(h_ref, w_ref, z0_ref, z1_ref, z2_ref, scr):
        outs = (z0_ref, z1_ref, z2_ref)
        for j in range(9):
            t, gi = j // 3, j % 3
            cols = slice(t * GW, (t + 1) * GW)
            zt = _dot(h_ref[...], w_ref[:, j * GW:(j + 1) * GW])
            if gi == 0:
                z0_ref[0, :, cols] = zt.astype(BF16)
            else:
                slot = scr.at[2 * t + gi - 1]
                _store_cols(slot, zt)

                def put(c, v, ref=outs[gi], cols=cols):
                    ref[c, :, cols] = v.astype(BF16)
                _split_residues(slot, DILATIONS[gi], put)

    return pl.pallas_call(
        body, name="in_proj_qkv", grid=(S // TM,),
        in_specs=[pl.BlockSpec((TM, D), lambda i: (i, 0)),
                  pl.BlockSpec(w.shape, lambda i: (0, 0), pipeline_mode=pl.Buffered(1))],
        out_specs=[_residue_spec(TM, d, 3 * GW) for d in DILATIONS],
        out_shape=[SDS(_residue_shape(S, d, 3 * GW), BF16) for d in DILATIONS],
        scratch_shapes=[pltpu.VMEM((6, GW // LANES, TM, LANES), F32)],
        compiler_params=_cp(("arbitrary",)),
    )(h, w)


def _bucket_tables():
    a = np.arange(QBLK, dtype=np.int32)[:, None]
    c = np.arange(KBLK, dtype=np.int32)[None, :]
    off = a - c + QBLK
    valid = ((off >= 0) & (off <= SPAN)).astype(np.float32)
    tabs = []
    for dil in DILATIONS:
        dist = np.maximum(off * dil, 0)
        df = np.maximum(dist, 1).astype(np.float32)
        large = 16 + (np.log(df / np.float32(16)) / np.float32(math.log(2048 / 16)) * np.float32(16)).astype(np.int32)
        large = np.minimum(large, 31)
        tabs.append(np.where(dist < 16, dist, large).astype(np.int32))
    return np.stack(tabs), valid


def _bias_expand(tab, buckets, valid):
    def body(tab_ref, b_ref, v_ref, o_ref):
        for gi in range(3):
            bk = b_ref[gi]
            for h in range(NH):
                acc = jnp.zeros((QBLK, KBLK), F32)
                for b in range(32):
                    acc = jnp.where(bk == b, tab_ref[b, gi * NH + h], acc)
                o_ref[gi * NH + h] = jnp.where(v_ref[...] > 0.5, acc, NEG_INF)

    return pl.pallas_call(
        body, name="bias_expand",
        in_specs=[pl.BlockSpec(memory_space=pltpu.SMEM),
                  pl.BlockSpec(memory_space=pltpu.VMEM), pl.BlockSpec(memory_space=pltpu.VMEM)],
        out_specs=pl.BlockSpec(memory_space=pltpu.VMEM),
        out_shape=SDS((3 * NH, QBLK, KBLK), F32),
    )(tab, buckets, valid)


def _bias_reduce(dbias, buckets):
    def body(d_ref, b_ref, o_ref):
        lane = lax.broadcasted_iota(jnp.int32, (1, D), 1)
        for gi in range(3):
            bk = b_ref[gi]
            for h in range(NH):
                dv = d_ref[gi * NH + h]
                row = jnp.zeros((1, D), F32)
                for b in range(32):
                    m = jnp.where(bk == b, dv, 0.0)
                    val = jnp.sum(jnp.sum(m, axis=0, keepdims=True), axis=1, keepdims=True)
                    row = jnp.where(lane == b, val, row)
                o_ref[gi * NH + h:gi * NH + h + 1, :] = row

    return pl.pallas_call(
        body, name="bias_reduce",
        in_specs=[pl.BlockSpec(memory_space=pltpu.VMEM), pl.BlockSpec(memory_space=pltpu.VMEM)],
        out_specs=pl.BlockSpec(memory_space=pltpu.VMEM),
        out_shape=SDS((3 * NH, D), F32),
    )(dbias, buckets)


def _attn_tile(S, dil):
    L = S // dil
    tq = min(512, L)
    return L, tq, tq // QBLK, L // tq


def _pair_stack(ref, rows, lanes, scale=None):
    blk = ref[rows, lanes]
    if scale is not None:
        blk = blk * scale
    lane = lax.broadcasted_iota(jnp.int32, blk.shape, 1)
    zero = jnp.zeros_like(blk)
    return jnp.concatenate([jnp.where(lane < HD, blk, zero), jnp.where(lane >= HD, blk, zero)], axis=0)


def _attn_fwd(zq, bias2, gi):
    dil, L, _ = zq.shape
    _, TQ, QB, ns = _attn_tile(L * dil, dil)
    NP = NH // 2

    def body(q_ref, kc_ref, kp_ref, vc_ref, vp_ref, b_ref, o_ref, l_ref, kext, vext):
        n = pl.program_id(1)
        kext[0:QBLK, :] = kp_ref[0]
        kext[QBLK:, :] = kc_ref[0]
        vext[0:QBLK, :] = vp_ref[0]
        vext[QBLK:, :] = vc_ref[0]
        col = lax.broadcasted_iota(jnp.int32, (2 * QBLK, KBLK), 1)
        no_prev = jnp.logical_and(n == 0, col < QBLK)
        lane = lax.broadcasted_iota(jnp.int32, (QBLK, LANES), 1)
        for hp in range(NP):
            pl_ = slice(hp * LANES, (hp + 1) * LANES)
            for b in range(QB):
                rows = slice(b * QBLK, (b + 1) * QBLK)
                win = slice(b * QBLK, b * QBLK + KBLK)
                s = _dot_nt(_pair_stack(q_ref.at[0], rows, pl_, SCALE), kext[win, pl_]) + b_ref[0, hp]
                if b == 0:
                    s = jnp.where(no_prev, NEG_INF, s)
                m = jnp.max(s, axis=-1, keepdims=True)
                p = jnp.exp(s - m)
                l = jnp.sum(p, axis=-1, keepdims=True)
                o2 = _dot(p.astype(BF16), vext[win, pl_]) / l
                lse2 = jnp.broadcast_to(m + jnp.log(l), (2 * QBLK, LANES))
                o_ref[0, rows, pl_] = jnp.where(lane < HD, o2[0:QBLK], o2[QBLK:]).astype(BF16)
                l_ref[0, rows, pl_] = jnp.where(lane < HD, lse2[0:QBLK], lse2[QBLK:])

    def prev(n):
        return jnp.maximum(n * QB - 1, 0)

    return pl.pallas_call(
        body, name=f"attn_fwd_g{gi}", grid=(dil, ns),
        in_specs=[pl.BlockSpec((1, TQ, GW), lambda c, n: (c, n, 0)),
                  pl.BlockSpec((1, TQ, GW), lambda c, n: (c, n, 1)),
                  pl.BlockSpec((1, QBLK, GW), lambda c, n: (c, prev(n), 1)),
                  pl.BlockSpec((1, TQ, GW), lambda c, n: (c, n, 2)),
                  pl.BlockSpec((1, QBLK, GW), lambda c, n: (c, prev(n), 2)),
                  pl.BlockSpec((1, NP, 2 * QBLK, KBLK), lambda c, n: (gi, 0, 0, 0))],
        out_specs=[pl.BlockSpec((1, TQ, GW), lambda c, n: (c, n, 0)),
                   pl.BlockSpec((1, TQ, GW), lambda c, n: (c, n, 0))],
        out_shape=[SDS((dil, L, GW), BF16), SDS((dil, L, GW), F32)],
        scratch_shapes=[pltpu.VMEM((QBLK + TQ, GW), BF16), pltpu.VMEM((QBLK + TQ, GW), BF16)],
        compiler_params=_cp(("arbitrary", "arbitrary")),
    )(zq, zq, zq, zq, zq, bias2)


CONV_TM = 256
SHIFT_PAD = 24


def _make_shifts(src, sh, n):
    for b in range(1, 8):
        sh[b - 1] = src[b:b + n + SHIFT_PAD, :]


def _shifted(src, sh, off, r0, n, lanes):
    a, b = divmod(off, 8)
    if b == 0:
        return src[8 * a + r0:8 * a + r0 + n, lanes]
    return sh[b - 1, 8 * a + r0:8 * a + r0 + n, lanes]


CONV_RC = 64


def _tap_blocks(TM):
    return [(r0, slice(l0, l0 + LANES)) for l0 in range(0, D, LANES) for r0 in range(0, TM, CONV_RC)]


def _conv_fwd(zrest, b_glu, wdw, b_dw, g_ln, b_ln):
    S = zrest.shape[0]
    TM = CONV_TM
    HALO = 32
    hb = TM // HALO

    def body(u_ref, g_ref, uh_ref, gh_ref, bg_ref, w_ref, bd_ref, gl_ref, bl_ref, cv_ref, a_ref, ext, sh):
        i = pl.program_id(0)
        bu = bg_ref[:, 0:D]
        bgt = bg_ref[:, D:2 * D]
        uh = (uh_ref[...].astype(F32) + bu) * _sigmoid(gh_ref[...].astype(F32) + bgt)
        ext[0:HALO, :] = jnp.where(i == 0, 0.0, uh)
        ext[HALO:, :] = (u_ref[...].astype(F32) + bu) * _sigmoid(g_ref[...].astype(F32) + bgt)
        _make_shifts(ext, sh, TM)
        acc = jnp.zeros((TM, D), F32)
        for j in range(CONV_W):
            acc = acc + _shifted(ext, sh, HALO - (CONV_W - 1) + j, 0, TM, slice(None)) * w_ref[j:j + 1, :]
        cv = (acc + bd_ref[...]).astype(BF16)
        cv_ref[...] = cv
        cf = cv.astype(F32)
        mu = jnp.mean(cf, axis=-1, keepdims=True)
        xc = cf - mu
        y = xc * lax.rsqrt(jnp.mean(xc * xc, axis=-1, keepdims=True) + LN_EPS) * gl_ref[...] + bl_ref[...]
        a_ref[...] = (y * _sigmoid(y)).astype(BF16)

    vec = pl.BlockSpec((1, D), lambda i: (0, 0))
    return pl.pallas_call(
        body, name="conv_fwd", grid=(S // TM,),
        in_specs=[pl.BlockSpec((TM, D), lambda i: (i, 0)), pl.BlockSpec((TM, D), lambda i: (i, 1)),
                  pl.BlockSpec((HALO, D), lambda i: (jnp.maximum(i * hb - 1, 0), 0)),
                  pl.BlockSpec((HALO, D), lambda i: (jnp.maximum(i * hb - 1, 0), 1)),
                  pl.BlockSpec((1, 2 * D), lambda i: (0, 0)),
                  pl.BlockSpec((32, D), lambda i: (0, 0)), vec, vec, vec],
        out_specs=[pl.BlockSpec((TM, D), lambda i: (i, 0)), pl.BlockSpec((TM, D), lambda i: (i, 0))],
        out_shape=[SDS((S, D), BF16), SDS((S, D), BF16)],
        scratch_shapes=[pltpu.VMEM((HALO + TM, D), F32), pltpu.VMEM((7, TM + SHIFT_PAD, D), F32)],
        compiler_params=_cp(("arbitrary",)),
    )(zrest, zrest, zrest, zrest, b_glu, wdw, b_dw, g_ln, b_ln)


def _mix_fwd(og, lg, a, zrest, x, w_ao, w_co, b_co, w_mx, g_pm):
    S = x.shape[0]
    TM = 512

    def body(o0, o1, o2, l0, l1, l2, a_ref, ga_ref, gc_ref, x_ref, wa_ref, wc_ref, bc_ref, wm_ref, g_ref,
             o_ref, oa_ref, ob_ref, lse_ref, lsea_ref, lseb_ref, ya_ref, yc_ref, mg_ref, mm_ref, x1_ref,
             so1, so2, sl1, sl2, so, sl):
        for dil, src, dst, cast in ((DILATIONS[1], o1, so1, True), (DILATIONS[2], o2, so2, True),
                                    (DILATIONS[1], l1, sl1, False), (DILATIONS[2], l2, sl2, False)):
            _merge_residues(dst, dil, (lambda c, src=src: src[c].astype(F32)) if cast else (lambda c, src=src: src[c]))
        la, lb, lc = l0[0], _load_cols(sl1), _load_cols(sl2)
        m = jnp.maximum(jnp.maximum(la, lb), lc)
        e0 = jnp.exp(la - m)
        e1 = jnp.exp(lb - m)
        e2 = jnp.exp(lc - m)
        den = e0 + e1 + e2
        of = (e0 * o0[0].astype(F32) + e1 * _load_cols(so1) + e2 * _load_cols(so2)) / den
        o = of.astype(BF16)
        o_ref[...] = o
        lse = m + jnp.log(den)
        lse_ref[...] = lse
        _store_cols(so, of)
        _store_cols(sl, lse)
        for dil, oref, lref in ((DILATIONS[1], oa_ref, lsea_ref), (DILATIONS[2], ob_ref, lseb_ref)):
            def put_o(c, v, oref=oref):
                oref[c] = v.astype(BF16)

            def put_l(c, v, lref=lref):
                lref[c] = v
            _split_residues(so, dil, put_o)
            _split_residues(sl, dil, put_l)
        ya = _dot(o, wa_ref[...]).astype(BF16)
        yc = (_dot(a_ref[...], wc_ref[...]) + bc_ref[...]).astype(BF16)
        ya_ref[...] = ya
        yc_ref[...] = yc
        mg = (_sigmoid(ga_ref[...].astype(F32)) * ya.astype(F32)
              + _sigmoid(gc_ref[...].astype(F32)) * yc.astype(F32)).astype(BF16)
        mg_ref[...] = mg
        mm = _dot(mg, wm_ref[...]).astype(BF16)
        mm_ref[...] = mm
        mf = mm.astype(F32)
        r = lax.rsqrt(jnp.mean(mf * mf, axis=-1, keepdims=True) + RMS_EPS)
        x1_ref[...] = x_ref[...] + mf * r * g_ref[...]

    row512 = pl.BlockSpec((TM, GW), lambda i: (i, 0))
    rowd = pl.BlockSpec((TM, D), lambda i: (i, 0))
    vec = pl.BlockSpec((1, D), lambda i: (0, 0))
    full = lambda r, c: pl.BlockSpec((r, c), lambda i: (0, 0))
    res = [_residue_spec(TM, d, GW) for d in DILATIONS]
    rshape = lambda d, t: SDS(_residue_shape(S, d, GW), t)
    scr = _col_scratch(TM, GW)
    return pl.pallas_call(
        body, name="mix_fwd", grid=(S // TM,),
        in_specs=res + res + [rowd, pl.BlockSpec((TM, D), lambda i: (i, 2)), pl.BlockSpec((TM, D), lambda i: (i, 3)),
                              rowd, full(GW, D), full(D, D), vec, full(D, D), vec],
        out_specs=[row512, res[1], res[2], row512, res[1], res[2], rowd, rowd, rowd, rowd, rowd],
        out_shape=[SDS((S, GW), BF16), rshape(DILATIONS[1], BF16), rshape(DILATIONS[2], BF16),
                   SDS((S, GW), F32), rshape(DILATIONS[1], F32), rshape(DILATIONS[2], F32),
                   SDS((S, D), BF16), SDS((S, D), BF16), SDS((S, D), BF16), SDS((S, D), BF16), SDS((S, D), F32)],
        scratch_shapes=[scr] * 6,
        compiler_params=_cp(("arbitrary",)),
    )(og[0], og[1], og[2], lg[0], lg[1], lg[2], a, zrest, zrest, x, w_ao, w_co, b_co, w_mx, g_pm)


def _ffn_fwd(x1, tgt, g_pre, g_post, w_fi, w_fo):
    S = x1.shape[0]
    TM = 512

    def body(x1_ref, t_ref, gp_ref, go_ref, wi_ref, wo_ref,
             h2_ref, gu_ref, df_ref, dx2_ref, loss_ref, dgo_ref):
        i = pl.program_id(0)

        @pl.when(i == 0)
        def _():
            loss_ref[...] = jnp.zeros_like(loss_ref)
            dgo_ref[...] = jnp.zeros_like(dgo_ref)

        xf = x1_ref[...]
        r = lax.rsqrt(jnp.mean(xf * xf, axis=-1, keepdims=True) + RMS_EPS)
        h2_ref[...] = (xf * r * gp_ref[...]).astype(BF16)
        f = jnp.zeros((TM, D), F32)
        for k in range(NFT):
            gt = _dot(h2_ref[...], wi_ref[:, k * FFN_T:(k + 1) * FFN_T]).astype(BF16)
            up = _dot(h2_ref[...], wi_ref[:, FFN_H + k * FFN_T:FFN_H + (k + 1) * FFN_T]).astype(BF16)
            gu_ref[:, 2 * k * FFN_T:(2 * k + 1) * FFN_T] = gt
            gu_ref[:, (2 * k + 1) * FFN_T:(2 * k + 2) * FFN_T] = up
            gf = gt.astype(F32)
            act = (gf * _sigmoid(gf) * up.astype(F32)).astype(BF16)
            f = f + _dot(act, wo_ref[k * FFN_T:(k + 1) * FFN_T, :])
        r = lax.rsqrt(jnp.mean(f * f, axis=-1, keepdims=True) + RMS_EPS)
        nrm = f * r
        e = x1_ref[...] + nrm * go_ref[...] - t_ref[...]
        tot = jnp.sum(jnp.sum(e * e, axis=-1, keepdims=True), axis=0, keepdims=True) * (0.5 / D)
        corner = jnp.logical_and(lax.broadcasted_iota(jnp.int32, (LANE_ROWS, D), 0) == 0,
                                 lax.broadcasted_iota(jnp.int32, (LANE_ROWS, D), 1) == 0)
        loss_ref[...] += jnp.where(corner, tot, 0.0)
        dx2 = e * (1.0 / D)
        dx2_ref[...] = dx2
        dgo_ref[...] += _colsum8(dx2 * nrm)
        dn = dx2 * go_ref[...]
        df_ref[...] = (r * (dn - nrm * jnp.mean(dn * nrm, axis=-1, keepdims=True))).astype(BF16)

    rowd = pl.BlockSpec((TM, D), lambda i: (i, 0))
    vec = pl.BlockSpec((1, D), lambda i: (0, 0))
    acc8 = pl.BlockSpec((LANE_ROWS, D), lambda i: (0, 0))
    return pl.pallas_call(
        body, name="ffn_fwd", grid=(S // TM,),
        in_specs=[rowd, rowd, vec, vec,
                  pl.BlockSpec((D, 2 * FFN_H), lambda i: (0, 0), pipeline_mode=pl.Buffered(1)),
                  pl.BlockSpec((FFN_H, D), lambda i: (0, 0), pipeline_mode=pl.Buffered(1))],
        out_specs=[rowd, pl.BlockSpec((TM, 2 * FFN_H), lambda i: (i, 0)), rowd, rowd, acc8, acc8],
        out_shape=[SDS((S, D), BF16), SDS((S, 2 * FFN_H), BF16), SDS((S, D), BF16), SDS((S, D), F32),
                   SDS((LANE_ROWS, D), F32), SDS((LANE_ROWS, D), F32)],
        compiler_params=_cp(("arbitrary",)),
    )(x1, tgt, g_pre, g_post, w_fi, w_fo)


def _ffn_bwd(df, gu, x1, dx2, g_pre, w_foT, w_fiT):
    S = x1.shape[0]
    TM = 512

    def body_act(df_ref, gu_ref, wo_ref, dff_ref, act_ref):
        for k in range(NFT):
            dact = _dot(df_ref[...], wo_ref[:, k * FFN_T:(k + 1) * FFN_T])
            g = gu_ref[:, 2 * k * FFN_T:(2 * k + 1) * FFN_T].astype(F32)
            u = gu_ref[:, (2 * k + 1) * FFN_T:(2 * k + 2) * FFN_T].astype(F32)
            sg = _sigmoid(g)
            sl = g * sg
            act_ref[:, k * FFN_T:(k + 1) * FFN_T] = (sl * u).astype(BF16)
            dff_ref[:, 2 * k * FFN_T:(2 * k + 1) * FFN_T] = (dact * u * (sg * (1.0 + g * (1.0 - sg)))).astype(BF16)
            dff_ref[:, (2 * k + 1) * FFN_T:(2 * k + 2) * FFN_T] = (dact * sl).astype(BF16)

    rowd = pl.BlockSpec((TM, D), lambda i: (i, 0))
    wide = pl.BlockSpec((TM, 2 * FFN_H), lambda i: (i, 0))
    dff, act = pl.pallas_call(
        body_act, name="ffn_bwd_act", grid=(S // TM,),
        in_specs=[rowd, wide, pl.BlockSpec((D, FFN_H), lambda i: (0, 0), pipeline_mode=pl.Buffered(1))],
        out_specs=[wide, pl.BlockSpec((TM, FFN_H), lambda i: (i, 0))],
        out_shape=[SDS((S, 2 * FFN_H), BF16), SDS((S, FFN_H), BF16)],
        compiler_params=_cp(("arbitrary",)),
    )(df, gu, w_foT)

    KC = 512
    nkc = 2 * FFN_H // KC

    def body_in(dff_ref, x1_ref, dx2_ref, gp_ref, wi_ref, dx1_ref, dgp_ref):
        i = pl.program_id(0)

        @pl.when(i == 0)
        def _():
            dgp_ref[...] = jnp.zeros_like(dgp_ref)

        dh = jnp.zeros((TM, D), F32)
        for k in range(nkc):
            dh = dh + _dot(dff_ref[:, k * KC:k * KC + FFN_T], wi_ref[k * FFN_T:(k + 1) * FFN_T, :]) \
                + _dot(dff_ref[:, k * KC + FFN_T:(k + 1) * KC], wi_ref[FFN_H + k * FFN_T:FFN_H + (k + 1) * FFN_T, :])
        xf = x1_ref[...]
        r = lax.rsqrt(jnp.mean(xf * xf, axis=-1, keepdims=True) + RMS_EPS)
        nrm = xf * r
        dgp_ref[...] += _colsum8(dh * nrm)
        dn = dh * gp_ref[...]
        dx1_ref[...] = dx2_ref[...] + r * (dn - nrm * jnp.mean(dn * nrm, axis=-1, keepdims=True))

    dx1, dgp = pl.pallas_call(
        body_in, name="ffn_bwd_in", grid=(S // TM,),
        in_specs=[wide, rowd, rowd, pl.BlockSpec((1, D), lambda i: (0, 0)),
                  pl.BlockSpec((2 * FFN_H, D), lambda i: (0, 0), pipeline_mode=pl.Buffered(1))],
        out_specs=[rowd, pl.BlockSpec((LANE_ROWS, D), lambda i: (0, 0))],
        out_shape=[SDS((S, D), F32), SDS((LANE_ROWS, D), F32)],
        compiler_params=_cp(("arbitrary",)),
    )(dff, x1, dx2, g_pre, w_fiT)
    return dff, act, dx1, dgp


def _mix_bwd(dx1, mm, ya, yc, zrest, g_pm, w_mxT, w_aoT, w_coT):
    S = dx1.shape[0]
    TM = 512

    def body(dx_ref, mm_ref, ya_ref, yc_ref, ga_ref, gc_ref, g_ref, wm_ref, wa_ref, wc_ref,
             dmm_ref, dya_ref, dyc_ref, do_ref, doa_ref, dob_ref, da_ref, dzg_ref, dgpm_ref, dbco_ref, sdo):
        i = pl.program_id(0)

        @pl.when(i == 0)
        def _():
            dgpm_ref[...] = jnp.zeros_like(dgpm_ref)
            dbco_ref[...] = jnp.zeros_like(dbco_ref)

        mf = mm_ref[...].astype(F32)
        r = lax.rsqrt(jnp.mean(mf * mf, axis=-1, keepdims=True) + RMS_EPS)
        nrm = mf * r
        dx = dx_ref[...]
        dgpm_ref[...] += _colsum8(dx * nrm)
        dn = dx * g_ref[...]
        dmm = (r * (dn - nrm * jnp.mean(dn * nrm, axis=-1, keepdims=True))).astype(BF16)
        dmm_ref[...] = dmm
        dmg = _dot(dmm, wm_ref[...])
        sa = _sigmoid(ga_ref[...].astype(F32))
        sc = _sigmoid(gc_ref[...].astype(F32))
        dya = (dmg * sa).astype(BF16)
        dyc = (dmg * sc).astype(BF16)
        dya_ref[...] = dya
        dyc_ref[...] = dyc
        dbco_ref[...] += _colsum8(dyc.astype(F32))
        dzg_ref[:, 0:D] = (dmg * ya_ref[...].astype(F32) * (sa * (1.0 - sa))).astype(BF16)
        dzg_ref[:, D:] = (dmg * yc_ref[...].astype(F32) * (sc * (1.0 - sc))).astype(BF16)
        dof = _dot(dya, wa_ref[...])
        do_ref[...] = dof.astype(BF16)
        _store_cols(sdo, dof)
        for dil, ref in ((DILATIONS[1], doa_ref), (DILATIONS[2], dob_ref)):
            def put(c, v, ref=ref):
                ref[c] = v.astype(BF16)
            _split_residues(sdo, dil, put)
        da_ref[...] = _dot(dyc, wc_ref[...]).astype(BF16)

    rowd = pl.BlockSpec((TM, D), lambda i: (i, 0))
    full = lambda r, c: pl.BlockSpec((r, c), lambda i: (0, 0))
    acc8 = pl.BlockSpec((LANE_ROWS, D), lambda i: (0, 0))
    return pl.pallas_call(
        body, name="mix_bwd", grid=(S // TM,),
        in_specs=[rowd, rowd, rowd, rowd, pl.BlockSpec((TM, D), lambda i: (i, 2)),
                  pl.BlockSpec((TM, D), lambda i: (i, 3)), full(1, D), full(D, D), full(D, GW), full(D, D)],
        out_specs=[rowd, rowd, rowd, pl.BlockSpec((TM, GW), lambda i: (i, 0)),
                   _residue_spec(TM, DILATIONS[1], GW), _residue_spec(TM, DILATIONS[2], GW), rowd,
                   pl.BlockSpec((TM, 2 * D), lambda i: (i, 0)), acc8, acc8],
        out_shape=[SDS((S, D), BF16), SDS((S, D), BF16), SDS((S, D), BF16), SDS((S, GW), BF16),
                   SDS(_residue_shape(S, DILATIONS[1], GW), BF16), SDS(_residue_shape(S, DILATIONS[2], GW), BF16),
                   SDS((S, D), BF16), SDS((S, 2 * D), BF16), SDS((LANE_ROWS, D), F32), SDS((LANE_ROWS, D), F32)],
        scratch_shapes=[_col_scratch(TM, GW)],
        compiler_params=_cp(("arbitrary",)),
    )(dx1, mm, ya, yc, zrest, zrest, g_pm, w_mxT, w_aoT, w_coT)


def _conv_bwd(da, cv, zrest, b_glu, wdw, g_ln, b_ln):
    S = da.shape[0]
    TM = CONV_TM
    HALO = 32
    hb = TM // HALO
    nh = S // HALO

    def body(da_ref, dan_ref, cv_ref, cvn_ref, u_ref, g_ref, uh_ref, gh_ref, bg_ref, w_ref, gl_ref, bl_ref,
             dglu_ref, dbu_ref, dbg_ref, dw_ref, dgl_ref, dbl_ref, dbd_ref, dext, uext, dsh, ush, du_scr, dw8):
        i = pl.program_id(0)
        last = i == pl.num_programs(0) - 1

        @pl.when(i == 0)
        def _():
            for ref in (dbu_ref, dbg_ref, dw8, dgl_ref, dbl_ref, dbd_ref):
                ref[...] = jnp.zeros_like(ref)

        def ln_bwd(da_v, cv_v):
            cf = cv_v.astype(F32)
            mu = jnp.mean(cf, axis=-1, keepdims=True)
            xc = cf - mu
            rstd = lax.rsqrt(jnp.mean(xc * xc, axis=-1, keepdims=True) + LN_EPS)
            xh = xc * rstd
            y = xh * gl_ref[...] + bl_ref[...]
            sy = _sigmoid(y)
            dy = da_v.astype(F32) * (sy * (1.0 + y * (1.0 - sy)))
            dxh = dy * gl_ref[...]
            dcv = rstd * (dxh - jnp.mean(dxh, axis=-1, keepdims=True)
                          - xh * jnp.mean(dxh * xh, axis=-1, keepdims=True))
            return dcv, dy, xh

        dcv, dy, xh = ln_bwd(da_ref[...], cv_ref[...])
        dgl_ref[...] += _colsum8(dy * xh)
        dbl_ref[...] += _colsum8(dy)
        dbd_ref[...] += _colsum8(dcv)
        dcvn, _, _ = ln_bwd(dan_ref[...], cvn_ref[...])
        dext[0:TM, :] = dcv
        dext[TM:, :] = jnp.where(last, 0.0, dcvn)

        bu = bg_ref[:, 0:D]
        bgt = bg_ref[:, D:2 * D]
        upre = u_ref[...].astype(F32) + bu
        sg = _sigmoid(g_ref[...].astype(F32) + bgt)
        uh = (uh_ref[...].astype(F32) + bu) * _sigmoid(gh_ref[...].astype(F32) + bgt)
        uext[0:HALO, :] = jnp.where(i == 0, 0.0, uh)
        uext[HALO:, :] = upre * sg

        _make_shifts(dext, dsh, TM)
        _make_shifts(uext, ush, TM)
        for r0, lanes in _tap_blocks(TM):
            acc = jnp.zeros((CONV_RC, LANES), F32)
            for j in range(CONV_W):
                acc = acc + _shifted(dext, dsh, CONV_W - 1 - j, r0, CONV_RC, lanes) * w_ref[j:j + 1, lanes]
            du_scr[r0:r0 + CONV_RC, lanes] = acc
        for l0 in range(0, D, LANES):
            lanes = slice(l0, l0 + LANES)
            accs = [jnp.zeros((LANE_ROWS, LANES), F32)] * CONV_W
            for r0 in range(0, TM, CONV_RC):
                dc = dext[r0:r0 + CONV_RC, lanes]
                for j in range(CONV_W):
                    prod = dc * _shifted(uext, ush, HALO - (CONV_W - 1) + j, r0, CONV_RC, lanes)
                    accs[j] = accs[j] + jnp.sum(prod.reshape(CONV_RC // LANE_ROWS, LANE_ROWS, LANES), axis=0)
            for j in range(CONV_W):
                dw8[j, :, lanes] += accs[j]

        @pl.when(last)
        def _():
            for j in range(CONV_W):
                dw_ref[j:j + 1, :] = jnp.sum(dw8[j], axis=0, keepdims=True)
            dw_ref[CONV_W:, :] = jnp.zeros((32 - CONV_W, D), F32)

        du = du_scr[...]
        dup = du * sg
        dgp = du * upre * (sg * (1.0 - sg))
        dglu_ref[:, 0:D] = dup.astype(BF16)
        dglu_ref[:, D:] = dgp.astype(BF16)
        dbu_ref[...] += _colsum8(dup.astype(BF16).astype(F32))
        dbg_ref[...] += _colsum8(dgp.astype(BF16).astype(F32))

    rowd = pl.BlockSpec((TM, D), lambda i: (i, 0))
    nxt = pl.BlockSpec((HALO, D), lambda i: (jnp.minimum((i + 1) * hb, nh - 1), 0))
    vec = pl.BlockSpec((1, D), lambda i: (0, 0))
    acc8 = pl.BlockSpec((LANE_ROWS, D), lambda i: (0, 0))
    return pl.pallas_call(
        body, name="conv_bwd", grid=(S // TM,),
        in_specs=[rowd, nxt, rowd, nxt,
                  pl.BlockSpec((TM, D), lambda i: (i, 0)), pl.BlockSpec((TM, D), lambda i: (i, 1)),
                  pl.BlockSpec((HALO, D), lambda i: (jnp.maximum(i * hb - 1, 0), 0)),
                  pl.BlockSpec((HALO, D), lambda i: (jnp.maximum(i * hb - 1, 0), 1)),
                  pl.BlockSpec((1, 2 * D), lambda i: (0, 0)), pl.BlockSpec((32, D), lambda i: (0, 0)), vec, vec],
        out_specs=[pl.BlockSpec((TM, 2 * D), lambda i: (i, 0)), acc8, acc8,
                   pl.BlockSpec((32, D), lambda i: (0, 0)), acc8, acc8, acc8],
        out_shape=[SDS((S, 2 * D), BF16), SDS((LANE_ROWS, D), F32), SDS((LANE_ROWS, D), F32), SDS((32, D), F32),
                   SDS((LANE_ROWS, D), F32), SDS((LANE_ROWS, D), F32), SDS((LANE_ROWS, D), F32)],
        scratch_shapes=[pltpu.VMEM((TM + HALO, D), F32), pltpu.VMEM((HALO + TM, D), F32),
                        pltpu.VMEM((7, TM + SHIFT_PAD, D), F32), pltpu.VMEM((7, TM + SHIFT_PAD, D), F32),
                        pltpu.VMEM((TM, D), F32), pltpu.VMEM((32, LANE_ROWS, D), F32)],
        compiler_params=_cp(("arbitrary",)),
    )(da, da, cv, cv, zrest, zrest, zrest, zrest, b_glu, wdw, g_ln, b_ln)


def _attn_bwd(zq, do, o, lse, bias_t, gi):
    dil, L, _ = zq.shape
    _, TQ, QB, ns = _attn_tile(L * dil, dil)
    NP = NH // 2

    def body(q3, kc3, kp3, vc3, vp3, do3, o3, l3, b_ref,
             out3, db_ref, kext, vext, dkx, dvx, dqn, dqc, dkc, dvc):
        q_ref, kc_ref, kp_ref, vc_ref, vp_ref, do_ref, o_ref, l_ref, out_ref = (
            r.at[0] for r in (q3, kc3, kp3, vc3, vp3, do3, o3, l3, out3))
        c = pl.program_id(0)
        n = pl.program_id(1)

        @pl.when(_first_step(c, n))
        def _():
            db_ref[...] = jnp.zeros_like(db_ref)

        @pl.when(n < ns)
        def _():
            kext[0:QBLK, :] = kp_ref[...]
            kext[QBLK:, :] = kc_ref[...]
            vext[0:QBLK, :] = vp_ref[...]
            vext[QBLK:, :] = vc_ref[...]
            krow = lax.broadcasted_iota(jnp.int32, (KBLK, 2 * QBLK), 0)
            no_prev = jnp.logical_and(n == 0, krow < QBLK)
            lane = lax.broadcasted_iota(jnp.int32, (QBLK, LANES), 1)

            def overlap_add(parts):
                segs = [parts[0][0:QBLK]]
                for b in range(1, QB):
                    segs.append(parts[b - 1][QBLK:] + parts[b][0:QBLK])
                segs.append(parts[QB - 1][QBLK:])
                return jnp.concatenate(segs, axis=0)

            for hp in range(NP):
                pl_ = slice(hp * LANES, (hp + 1) * LANES)
                dv_parts, dk_parts, dbsum = [], [], None
                for b in range(QB):
                    rows = slice(b * QBLK, (b + 1) * QBLK)
                    win = slice(b * QBLK, b * QBLK + KBLK)
                    q2 = _pair_stack(q_ref, rows, pl_, SCALE)
                    do2 = _pair_stack(do_ref, rows, pl_)
                    lse_t = l_ref[rows, pl_].T
                    lse_row = jnp.concatenate([lse_t[0:1], lse_t[HD:HD + 1]], axis=1)
                    prod_t = (do_ref[rows, pl_].astype(F32) * o_ref[rows, pl_].astype(F32)).T
                    delta_row = jnp.concatenate([jnp.sum(prod_t[0:HD], axis=0, keepdims=True),
                                                 jnp.sum(prod_t[HD:], axis=0, keepdims=True)], axis=1)
                    st = _dot_nt(kext[win, pl_], q2) + b_ref[0, hp]
                    if b == 0:
                        st = jnp.where(no_prev, NEG_INF, st)
                    pt = jnp.exp(st - lse_row)
                    dst = pt * (_dot_nt(vext[win, pl_], do2) - delta_row)
                    dbsum = dst if dbsum is None else dbsum + dst
                    dstb = dst.astype(BF16)
                    dv_parts.append(_dot(pt.astype(BF16), do2))
                    dk_parts.append(_dot(dstb, q2))
                    dq2 = _dot_tn(dstb, kext[win, pl_])
                    dqn[rows, pl_] = jnp.where(lane < HD, dq2[0:QBLK], dq2[QBLK:]) * SCALE
                db_ref[hp] += dbsum
                dvx[:, pl_] = overlap_add(dv_parts)
                dkx[:, pl_] = overlap_add(dk_parts)

        @pl.when(n > 0)
        def _():
            out_ref[:, 0:GW] = dqc[...].astype(BF16)
            out_ref[:, GW:2 * GW] = dkc[...].astype(BF16)
            out_ref[:, 2 * GW:] = dvc[...].astype(BF16)

        @pl.when(jnp.logical_and(n > 0, n < ns))
        def _():
            out_ref[TQ - QBLK:, GW:2 * GW] = (dkc[TQ - QBLK:, :] + dkx[0:QBLK, :]).astype(BF16)
            out_ref[TQ - QBLK:, 2 * GW:] = (dvc[TQ - QBLK:, :] + dvx[0:QBLK, :]).astype(BF16)

        @pl.when(n < ns)
        def _():
            dqc[...] = dqn[...]
            dkc[...] = dkx[QBLK:, :]
            dvc[...] = dvx[QBLK:, :]

    def cur(n):
        return jnp.minimum(n, ns - 1)

    def prev(n):
        return jnp.maximum(cur(n) * QB - 1, 0)

    rows = lambda c, n: (c, cur(n), 0)
    return pl.pallas_call(
        body, name=f"attn_bwd_g{gi}", grid=(dil, ns + 1),
        in_specs=[pl.BlockSpec((1, TQ, GW), lambda c, n: (c, cur(n), 0)),
                  pl.BlockSpec((1, TQ, GW), lambda c, n: (c, cur(n), 1)),
                  pl.BlockSpec((1, QBLK, GW), lambda c, n: (c, prev(n), 1)),
                  pl.BlockSpec((1, TQ, GW), lambda c, n: (c, cur(n), 2)),
                  pl.BlockSpec((1, QBLK, GW), lambda c, n: (c, prev(n), 2)),
                  pl.BlockSpec((1, TQ, GW), rows), pl.BlockSpec((1, TQ, GW), rows), pl.BlockSpec((1, TQ, GW), rows),
                  pl.BlockSpec((1, NP, KBLK, 2 * QBLK), lambda c, n: (gi, 0, 0, 0))],
        out_specs=[pl.BlockSpec((1, TQ, 3 * GW), lambda c, n: (c, jnp.maximum(n - 1, 0), 0)),
                   pl.BlockSpec((NP, KBLK, 2 * QBLK), lambda c, n: (0, 0, 0))],
        out_shape=[SDS((dil, L, 3 * GW), BF16), SDS((NP, KBLK, 2 * QBLK), F32)],
        scratch_shapes=[pltpu.VMEM((QBLK + TQ, GW), BF16), pltpu.VMEM((QBLK + TQ, GW), BF16),
                        pltpu.VMEM((QBLK + TQ, GW), F32), pltpu.VMEM((QBLK + TQ, GW), F32),
                        pltpu.VMEM((TQ, GW), F32), pltpu.VMEM((TQ, GW), F32),
                        pltpu.VMEM((TQ, GW), F32), pltpu.VMEM((TQ, GW), F32)],
        compiler_params=_cp(("arbitrary", "arbitrary")),
    )(zq, zq, zq, zq, zq, do, o, lse, bias_t)


def _dz_block(k):
    if k < 9:
        return k % 3, k // 3
    if k < 13:
        return 3, k - 9
    return 4, k - 13


_DZ_SRC = np.array([_dz_block(k)[0] for k in range(17)], np.int32)


def _dz_hold(s):
    uses = [(k, _dz_block(k)[1]) for k in range(17) if _dz_block(k)[0] == s]
    hold = []
    for k in range(17):
        nxt = [b for kk, b in uses if kk >= k]
        hold.append(nxt[0] if nxt else uses[-1][1])
    return np.array(hold, np.int32)


def _table(tab, k):
    out = jnp.int32(int(tab[0]))
    for idx in range(1, len(tab)):
        out = jnp.where(k == idx, jnp.int32(int(tab[idx])), out)
    return out


def _w_in_tile(s, blk):
    return blk * 3 + s if s < 3 else (9 if s == 3 else 13) + blk


def _in_bwd(dqkv, dglu, dzg, w_inT, x, dx1, g):
    S = x.shape[0]
    TM = 512

    def body(d0, d1, d2, d3, d4, w_ref, x_ref, dx1_ref, g_ref, gx_ref, dg_ref, scr):
        i = pl.program_id(0)

        @pl.when(i == 0)
        def _():
            dg_ref[...] = jnp.zeros_like(dg_ref)

        def rows(s, blk):
            k = _w_in_tile(s, blk)
            return w_ref[k * GW:(k + 1) * GW, :]

        dh = jnp.zeros((TM, D), F32)
        for blk in range(3):
            dh = dh + _dot(d0[0, :, blk * GW:(blk + 1) * GW], rows(0, blk))
        for s, ref in ((3, d3), (4, d4)):
            for blk in range(4):
                dh = dh + _dot(ref[:, blk * GW:(blk + 1) * GW], rows(s, blk))
        for s, ref in ((1, d1), (2, d2)):
            dil = DILATIONS[s]
            part = jnp.zeros((TM, D), F32)
            for blk in range(3):
                part = part + _dot(ref[:, :, blk * GW:(blk + 1) * GW].reshape(TM, GW), rows(s, blk))
            _merge_residues(scr, dil, lambda c, part=part, dil=dil: part[c * (TM // dil):(c + 1) * (TM // dil)])
            dh = dh + _load_cols(scr)
        xf = x_ref[...]
        r = lax.rsqrt(jnp.mean(xf * xf, axis=-1, keepdims=True) + RMS_EPS)
        nrm = xf * r
        dg_ref[...] += _colsum8(dh * nrm)
        dn = dh * g_ref[...]
        gx_ref[...] = dx1_ref[...] + r * (dn - nrm * jnp.mean(dn * nrm, axis=-1, keepdims=True))

    rowd = pl.BlockSpec((TM, D), lambda i: (i, 0))
    wide = pl.BlockSpec((TM, 2 * D), lambda i: (i, 0))
    return pl.pallas_call(
        body, name="in_bwd", grid=(S // TM,),
        in_specs=[_residue_spec(TM, d, 3 * GW) for d in DILATIONS] + [wide, wide]
        + [pl.BlockSpec(w_inT.shape, lambda i: (0, 0), pipeline_mode=pl.Buffered(1)), rowd, rowd,
           pl.BlockSpec((1, D), lambda i: (0, 0))],
        out_specs=[rowd, pl.BlockSpec((LANE_ROWS, D), lambda i: (0, 0))],
        out_shape=[SDS((S, D), F32), SDS((LANE_ROWS, D), F32)],
        scratch_shapes=[_col_scratch(TM, D)],
        compiler_params=_cp(("arbitrary",)),
    )(*dqkv, dglu, dzg, w_inT, x, dx1, g)


def _dw_in(dqkv, dglu, dzg, hs):
    S = hs[0].shape[0]
    TS = min(2048, S)
    nk = 17
    holds = [_dz_hold(s) for s in range(5)]
    h_of = (0, 1, 2, 0, 0)

    def body(d0, d1, d2, d3, d4, h0, h1, h2, o_ref, acc):
        m = pl.program_id(0)
        s_ = pl.program_id(1)

        @pl.when(s_ == 0)
        def _():
            acc[...] = jnp.zeros_like(acc)

        src = _table(_DZ_SRC, m)
        pairs = ((d0, h0), (d1, h1), (d2, h2), (d3, h0), (d4, h0))
        for s, (dref, href) in enumerate(pairs):
            @pl.when(src == s)
            def _(dref=dref, href=href):
                acc[...] += _dot_tn(dref[...].reshape(TS, GW), href[...].reshape(TS, D))

        @pl.when(s_ == pl.num_programs(1) - 1)
        def _():
            o_ref[...] = acc[...].astype(BF16)

    def row(s, m, s_):
        return jnp.where(_table(_DZ_SRC, m) == s, s_, 0)

    def dspec(s):
        if s < 3:
            dil = DILATIONS[s]
            return pl.BlockSpec((dil, TS // dil, GW), lambda m, s_: (0, row(s, m, s_), _table(holds[s], m)))
        return pl.BlockSpec((TS, GW), lambda m, s_: (row(s, m, s_), _table(holds[s], m)))

    def hrow(j, m, s_):
        used = _table(np.array([int(h_of[_dz_block(k)[0]] == j) for k in range(nk)], np.int32), m)
        return jnp.where(used == 1, s_, 0)

    hspecs = [pl.BlockSpec((TS, D), lambda m, s_: (hrow(0, m, s_), 0))] + [
        pl.BlockSpec((DILATIONS[j], TS // DILATIONS[j], D), lambda m, s_, j=j: (0, hrow(j, m, s_), 0)) for j in (1, 2)]
    return pl.pallas_call(
        body, name="dw_in", grid=(nk, S // TS),
        in_specs=[dspec(s) for s in range(5)] + hspecs,
        out_specs=pl.BlockSpec((GW, D), lambda m, s_: (m, 0)),
        out_shape=SDS((nk * GW, D), BF16),
        scratch_shapes=[pltpu.VMEM((GW, D), F32)],
        compiler_params=_cp(("arbitrary", "arbitrary")),
    )(*dqkv, dglu, dzg, *hs)


def _mm_tn(a, b, tm, a_maps, name):
    S, N = b.shape
    parts = len(a_maps)
    tp = tm // parts
    nm = len(a_maps[0])
    TS = min(2048, S)
    tabs = [np.array(t, np.int32) for t in a_maps]

    def body(*refs):
        a_refs = refs[:parts]
        b_ref, o_ref, acc = refs[parts:]
        s_ = pl.program_id(1)

        @pl.when(s_ == 0)
        def _():
            acc[...] = jnp.zeros_like(acc)

        for p, ar in enumerate(a_refs):
            acc[p * tp:(p + 1) * tp, :] += _dot_tn(ar[...], b_ref[...])

        @pl.when(s_ == pl.num_programs(1) - 1)
        def _():
            o_ref[...] = acc[...].astype(BF16)

    return pl.pallas_call(
        body, name=name, grid=(nm, S // TS),
        in_specs=[pl.BlockSpec((TS, tp), lambda m, s_, t=t: (s_, _table(t, m))) for t in tabs]
        + [pl.BlockSpec((TS, N), lambda m, s_: (s_, 0))],
        out_specs=pl.BlockSpec((tm, N), lambda m, s_: (m, 0)),
        out_shape=SDS((nm * tm, N), BF16),
        scratch_shapes=[pltpu.VMEM((tm, N), F32)],
        compiler_params=_cp(("arbitrary", "arbitrary")),
    )(*([a] * parts), b)


def _row_tile(rows, cols, limit=1 << 20):
    if rows * cols * 4 <= limit:
        return rows
    best = None
    for t in range(8, rows, 8):
        if rows % t == 0 and t * cols * 4 <= limit:
            best = t
    return best


def _adamw(w, g, m, v, name):
    R, C = w.shape
    tr = _row_tile(R, C)

    def body(w_ref, g_ref, m_ref, v_ref, d_ref, nm_ref, nv_ref):
        gg = g_ref[...]
        nm = ADAM_B1 * m_ref[...] + (1.0 - ADAM_B1) * gg
        nv = ADAM_B2 * v_ref[...] + (1.0 - ADAM_B2) * (gg * gg)
        m_hat = nm / (1.0 - ADAM_B1 ** ADAM_STEP)
        v_hat = nv / (1.0 - ADAM_B2 ** ADAM_STEP)
        d_ref[...] = -ADAM_LR * (m_hat / (jnp.sqrt(v_hat) + ADAM_EPS) + ADAM_WD * w_ref[...])
        nm_ref[...] = nm
        nv_ref[...] = nv

    spec = pl.BlockSpec((tr, C), lambda i: (i, 0))
    return pl.pallas_call(
        body, name=name, grid=(R // tr,), in_specs=[spec] * 4, out_specs=[spec] * 3,
        out_shape=[SDS((R, C), F32)] * 3, compiler_params=_cp(("arbitrary",)),
    )(w, g, m, v)


_FLIPS = ((1, 0), (0, 1), (1, 1))


def _place():
    x, y, c = lax.axis_index("x"), lax.axis_index("y"), lax.axis_index("c")
    return x, y, c


def _peer_chips(x, y):
    return [((x + fx) % 2, (y + fy) % 2) for fx, fy in _FLIPS]


def _gather_weights(shards):
    nw = len(shards)
    views = [s.reshape(2, s.shape[0] // 2, s.shape[1]) for s in shards]

    def body(*refs):
        ins = refs[:nw]
        outs = refs[nw:2 * nw]
        ici_send, ici_recv, d2d_send, d2d_recv, loc = refs[2 * nw:]
        x, y, c = _place()
        j = 2 * x + y
        chips = _peer_chips(x, y)
        copies = []
        for w in range(nw):
            cp = pltpu.make_async_copy(ins[w], outs[w].at[j], loc.at[w])
            cp.start()
            copies.append(cp)
        sends = []
        for w in range(nw):
            for k, (px, py) in enumerate(chips):
                cp = pltpu.make_async_remote_copy(
                    src_ref=ins[w].at[c], dst_ref=outs[w].at[j, c], send_sem=ici_send.at[w, k],
                    recv_sem=ici_recv.at[w, k], device_id=(px, py, c), device_id_type=MESH)
                cp.start()
                sends.append(cp)
        for w in range(nw):
            for k, (px, py) in enumerate(chips):
                jk = 2 * px + py
                land = outs[w].at[jk, c]
                pltpu.make_async_remote_copy(
                    src_ref=ins[w].at[c], dst_ref=land, send_sem=ici_send.at[w, k],
                    recv_sem=ici_recv.at[w, k], device_id=(px, py, c), device_id_type=MESH).wait_recv()
                cp = pltpu.make_async_remote_copy(
                    src_ref=land, dst_ref=land, send_sem=d2d_send.at[w, k],
                    recv_sem=d2d_recv.at[w, k], device_id=(x, y, 1 - c), device_id_type=MESH)
                cp.start()
                sends.append(cp)
        for w in range(nw):
            for k, (px, py) in enumerate(chips):
                jk = 2 * px + py
                land = outs[w].at[jk, 1 - c]
                pltpu.make_async_remote_copy(
                    src_ref=land, dst_ref=land, send_sem=d2d_send.at[w, k],
                    recv_sem=d2d_recv.at[w, k], device_id=(x, y, 1 - c), device_id_type=MESH).wait_recv()
        for cp in sends:
            cp.wait_send()
        for cp in copies:
            cp.wait()

    outs = pl.pallas_call(
        body, name="gather_weights",
        in_specs=[ANY] * nw, out_specs=[ANY] * nw,
        out_shape=[SDS((4,) + v.shape, BF16) for v in views],
        scratch_shapes=[pltpu.SemaphoreType.DMA((nw, 3)), pltpu.SemaphoreType.DMA((nw, 3)),
                        pltpu.SemaphoreType.DMA((nw, 3)), pltpu.SemaphoreType.DMA((nw, 3)),
                        pltpu.SemaphoreType.DMA((nw,))],
    )(*views)
    return [o.reshape(4 * s.shape[0], s.shape[1]) for o, s in zip(outs, shards)]


def _pair_exchange(grads):
    nw = len(grads)

    def body(*refs):
        ins = refs[:nw]
        outs = refs[nw:2 * nw]
        send, recv = refs[2 * nw:]
        x, y, c = _place()
        cps = []
        for w in range(nw):
            cp = pltpu.make_async_remote_copy(
                src_ref=ins[w].at[:, pl.ds(1 - c, 1)], dst_ref=outs[w], send_sem=send.at[w], recv_sem=recv.at[w],
                device_id=(x, y, 1 - c), device_id_type=MESH)
            cp.start()
            cps.append(cp)
        for cp in cps:
            cp.wait()

    return pl.pallas_call(
        body, name="grad_pair_exchange", in_specs=[ANY] * nw, out_specs=[ANY] * nw,
        out_shape=[SDS((4, 1) + g.shape[2:], BF16) for g in grads],
        scratch_shapes=[pltpu.SemaphoreType.DMA((nw,)), pltpu.SemaphoreType.DMA((nw,))],
    )(*grads)


def _half_tile(rh):
    best = 16
    for t in range(16, 545, 16):
        if rh % t == 0:
            best = t
    return best


def _pair_sum(c_arr, g, got, name):
    _, _, rh, n = g.shape
    tr = _half_tile(rh)

    def body(c_ref, a_ref, b_ref, o_ref):
        o_ref[...] = (a_ref[...].astype(F32) + b_ref[...].astype(F32)).astype(BF16)

    return pl.pallas_call(
        body, name=name,
        grid_spec=pltpu.PrefetchScalarGridSpec(
            num_scalar_prefetch=1, grid=(4, rh // tr),
            in_specs=[pl.BlockSpec((1, 1, tr, n), lambda s, i, c: (s, c[0], i, 0)),
                      pl.BlockSpec((1, 1, tr, n), lambda s, i, c: (s, 0, i, 0))],
            out_specs=pl.BlockSpec((1, 1, tr, n), lambda s, i, c: (s, 0, i, 0))),
        out_shape=SDS((4, 1, rh, n), BF16),
        compiler_params=_cp(("arbitrary", "arbitrary")),
    )(c_arr, g, got)


def _chip_exchange(parts):
    nw = len(parts)

    def body(*refs):
        ins = refs[:nw]
        outs = refs[nw:2 * nw]
        send, recv = refs[2 * nw:]
        x, y, c = _place()
        cps = []
        for w in range(nw):
            for k, (px, py) in enumerate(_peer_chips(x, y)):
                cp = pltpu.make_async_remote_copy(
                    src_ref=ins[w].at[2 * px + py], dst_ref=outs[w].at[k], send_sem=send.at[w, k],
                    recv_sem=recv.at[w, k], device_id=(px, py, c), device_id_type=MESH)
                cp.start()
                cps.append(cp)
        for cp in cps:
            cp.wait()

    return pl.pallas_call(
        body, name="grad_chip_exchange", in_specs=[ANY] * nw, out_specs=[ANY] * nw,
        out_shape=[SDS((3,) + p.shape[1:], BF16) for p in parts],
        scratch_shapes=[pltpu.SemaphoreType.DMA((nw, 3)), pltpu.SemaphoreType.DMA((nw, 3))],
    )(*parts)


def _chip_sum(jc_arr, part, got, name):
    _, _, rh, n = part.shape
    tr = _half_tile(rh)

    def body(jc_ref, a_ref, b_ref, o_ref):
        acc = a_ref[0, 0].astype(F32)
        for k in range(3):
            acc = acc + b_ref[k, 0].astype(F32)
        o_ref[0] = acc

    return pl.pallas_call(
        body, name=name,
        grid_spec=pltpu.PrefetchScalarGridSpec(
            num_scalar_prefetch=1, grid=(rh // tr,),
            in_specs=[pl.BlockSpec((1, 1, tr, n), lambda i, jc: (jc[0], 0, i, 0)),
                      pl.BlockSpec((3, 1, tr, n), lambda i, jc: (0, 0, i, 0))],
            out_specs=pl.BlockSpec((1, tr, n), lambda i, jc: (jc[1], i, 0))),
        out_shape=SDS((2, rh, n), F32),
        compiler_params=_cp(("arbitrary",)),
    )(jc_arr, part, got)


def _half_swap(halves):
    nw = len(halves)

    def body(*refs):
        ins = refs[:nw]
        outs = refs[nw:2 * nw]
        send, recv = refs[2 * nw:]
        x, y, c = _place()
        cps = []
        for w in range(nw):
            cp = pltpu.make_async_remote_copy(
                src_ref=ins[w].at[c], dst_ref=outs[w].at[c], send_sem=send.at[w], recv_sem=recv.at[w],
                device_id=(x, y, 1 - c), device_id_type=MESH)
            cp.start()
            cps.append(cp)
        for cp in cps:
            cp.wait()

    return pl.pallas_call(
        body, name="grad_half_swap", in_specs=[ANY] * nw, out_specs=[ANY] * nw,
        out_shape=[SDS(h.shape, F32) for h in halves],
        input_output_aliases={w: w for w in range(nw)},
        scratch_shapes=[pltpu.SemaphoreType.DMA((nw,)), pltpu.SemaphoreType.DMA((nw,))],
    )(*halves)


def _all_sum_small(part, name):
    R = part.shape[0]

    def body(p_ref, o_ref, land, send, recv):
        x, y, c = _place()
        me = 4 * x + 2 * y + c
        cps = []
        for d in range(1, 8):
            t = (me + d) % 8
            cp = pltpu.make_async_remote_copy(
                src_ref=p_ref, dst_ref=land.at[me], send_sem=send.at[d - 1], recv_sem=recv.at[d - 1],
                device_id=(t // 4, (t // 2) % 2, t % 2), device_id_type=MESH)
            cp.start()
            cps.append(cp)
        land[me] = p_ref[...]
        for cp in cps:
            cp.wait()
        acc = land[0]
        for d in range(1, 8):
            acc = acc + land[d]
        o_ref[...] = acc

    return pl.pallas_call(
        body, name=name,
        in_specs=[pl.BlockSpec(memory_space=pltpu.VMEM)], out_specs=pl.BlockSpec(memory_space=pltpu.VMEM),
        out_shape=SDS((R, D), F32),
        scratch_shapes=[pltpu.VMEM((8, R, D), F32), pltpu.SemaphoreType.DMA((7,)), pltpu.SemaphoreType.DMA((7,))],
        compiler_params=pltpu.CompilerParams(vmem_limit_bytes=VMEM_LIMIT),
    )(part)


def _pad_rows(a, rows):
    return jnp.pad(a, ((0, rows - a.shape[0]), (0, 0)))


def _vec_pack(vs):
    return jnp.concatenate([_pad_rows(v, LANE_ROWS) for v in vs], axis=0)


def kernel(x, rel_bias_table, g_pre_mix, w_in, b_glu, w_dw, b_dw, g_conv_ln, b_conv_ln, w_conv_out, b_conv_out, w_attn_out, w_mix_out, g_post_mix, g_pre_ffn, w_ffn_in, w_ffn_out, g_post_ffn, loss_target, m_rel_bias_table, m_g_pre_mix, m_w_in, m_b_glu, m_w_dw, m_b_dw, m_g_conv_ln, m_b_conv_ln, m_w_conv_out, m_b_conv_out, m_w_attn_out, m_w_mix_out, m_g_post_mix, m_g_pre_ffn, m_w_ffn_in, m_w_ffn_out, m_g_post_ffn, v_rel_bias_table, v_g_pre_mix, v_w_in, v_b_glu, v_w_dw, v_b_dw, v_g_conv_ln, v_b_conv_ln, v_w_conv_out, v_b_conv_out, v_w_attn_out, v_w_mix_out, v_g_post_mix, v_g_pre_ffn, v_w_ffn_in, v_w_ffn_out, v_g_post_ffn):
    S = x.shape[1]
    xs = x.reshape(S, D)
    tgt = loss_target.reshape(S, D)
    cx, cy, cc = _place()
    chip = 2 * cx + cy

    shards = [w_in[0].T.astype(BF16),
              w_ffn_in[0].T.astype(BF16),
              w_attn_out[0].T.astype(BF16),
              w_conv_out[0].astype(BF16),
              w_mix_out[0].astype(BF16),
              w_ffn_out[0].astype(BF16)]
    w_inT, w_fiT, w_aoT, w_co, w_mx, w_fo = _gather_weights(shards)
    w_inN, w_fiN, w_aoN = w_inT.T, w_fiT.T, w_aoT.T
    w_coT, w_mxT, w_foT = w_co.T, w_mx.T, w_fo.T

    buckets_np, valid_np = _bucket_tables()
    buckets = jnp.asarray(buckets_np)
    bias = _bias_expand(rel_bias_table, buckets, jnp.asarray(valid_np)).reshape(3, NH, QBLK, KBLK)
    bias2 = bias.reshape(3, NH // 2, 2 * QBLK, KBLK)
    bias_t = bias.reshape(3, NH // 2, 2, QBLK, KBLK).transpose(0, 1, 4, 2, 3).reshape(3, NH // 2, KBLK, 2 * QBLK)
    wdw32 = _pad_rows(w_dw[0], 32)
    wdw_full = _gather_small_cols(wdw32, chip)

    zrest, h, h_r4, h_r16 = _in_proj_rest(xs, g_pre_mix, w_inN[:, 3 * ATTN_COLS:])
    zq = _in_proj_qkv(h, w_inN[:, :3 * ATTN_COLS])
    og, lg = [], []
    for gi in range(3):
        o_g, l_g = _attn_fwd(zq[gi], bias2, gi)
        og.append(o_g)
        lg.append(l_g)
    cv, a = _conv_fwd(zrest, b_glu, wdw_full, b_dw, g_conv_ln, b_conv_ln)
    o, o_r4, o_r16, lse, lse_r4, lse_r16, ya, yc, mg, mm, x1 = _mix_fwd(
        og, lg, a, zrest, xs, w_aoN, w_co, b_conv_out, w_mx, g_post_mix)
    h2, gu, df, dx2, loss8, dg_post_ffn = _ffn_fwd(x1, tgt, g_pre_ffn, g_post_ffn, w_fiN, w_fo)

    dff, act, dx1, dg_pre_ffn = _ffn_bwd(df, gu, x1, dx2, g_pre_ffn, w_foT, w_fiT)
    dmm, dya, dyc, do, do_r4, do_r16, da, dzg, dg_post_mix, db_conv_out = _mix_bwd(
        dx1, mm, ya, yc, zrest, g_post_mix, w_mxT, w_aoT, w_coT)
    dglu, db_glu_u, db_glu_g, dw_dw, dg_conv_ln, db_conv_ln, db_dw = _conv_bwd(da, cv, zrest, b_glu, wdw_full, g_conv_ln, b_conv_ln)
    first = lambda t: t.reshape(1, S, GW)
    dqkv, dbias = [], []
    for gi, (do_g, o_g, lse_g) in enumerate(((first(do), first(o), first(lse)), (do_r4, o_r4, lse_r4),
                                            (do_r16, o_r16, lse_r16))):
        d_g, db_g = _attn_bwd(zq[gi], do_g, o_g, lse_g, bias_t, gi)
        dqkv.append(d_g)
        dbias.append(db_g.reshape(NH // 2, KBLK, 2, QBLK).transpose(0, 2, 3, 1).reshape(NH, QBLK, KBLK))
    dtab = _bias_reduce(jnp.concatenate(dbias, axis=0), buckets)
    grad_x, dg_pre_mix = _in_bwd(dqkv, dglu, dzg, w_inT, xs, dx1, g_pre_mix)

    ident = lambda n: [list(range(n))]
    g_inT = _dw_in(dqkv, dglu, dzg, (h, h_r4, h_r16))
    g_fiT = _mm_tn(dff, h2, 512, [[2 * t if t < NFT else 2 * (t - NFT) + 1 for t in range(0, 22, 2)],
                                  [2 * t if t < NFT else 2 * (t - NFT) + 1 for t in range(1, 22, 2)]], "dw_ffn_in")
    g_aoT = _mm_tn(dya, o, 512, ident(2), "dw_attn_out")
    g_co = _mm_tn(a, dyc, 512, ident(2), "dw_conv_out")
    g_mx = _mm_tn(mg, dmm, 512, ident(2), "dw_mix_out")
    g_fo = _mm_tn(act, df, FFN_H // 2, ident(2), "dw_ffn_out")

    partials = [g_inT, g_fiT, g_aoT, g_co, g_mx, g_fo]
    views = [g.reshape(4, 2, g.shape[0] // 8, g.shape[1]) for g in partials]
    got = _pair_exchange(views)
    c_arr = jnp.reshape(cc, (1,)).astype(jnp.int32)
    jc_arr = jnp.stack([chip, cc]).astype(jnp.int32)
    names = ("w_in", "w_ffn_in", "w_attn_out", "w_conv_out", "w_mix_out", "w_ffn_out")
    pair = [_pair_sum(c_arr, v, r, f"pair_sum_{n}") for v, r, n in zip(views, got, names)]
    got2 = _chip_exchange(pair)
    halves = [_chip_sum(jc_arr, p, r, f"chip_sum_{n}") for p, r, n in zip(pair, got2, names)]
    red = [t.reshape(t.shape[0] * t.shape[1], t.shape[2]) for t in _half_swap(halves)]
    gw_in, gw_ffn_in, gw_attn_out = red[0].T, red[1].T, red[2].T
    gw_conv_out, gw_mix_out, gw_ffn_out = red[3], red[4], red[5]

    small = jnp.concatenate([loss8, dg_pre_mix, db_glu_u, db_glu_g, db_dw, dg_conv_ln, db_conv_ln, db_conv_out,
                             dg_post_mix, dg_pre_ffn, dg_post_ffn, dtab, dw_dw], axis=0)
    tot = _all_sum_small(small, "small_all_sum")
    row = lambda i: tot[LANE_ROWS * i:LANE_ROWS * i + 1]
    loss = tot[0, 0]
    g_g_pre_mix, g_b_glu = row(1), jnp.concatenate([row(2), row(3)], axis=1)
    g_b_dw, g_g_conv_ln, g_b_conv_ln, g_b_conv_out = row(4), row(5), row(6), row(7)
    g_g_post_mix, g_g_pre_ffn, g_g_post_ffn = row(8), row(9), row(10)
    g_tab = tot[88:112, 0:32].T
    g_w_dw = lax.dynamic_slice(tot[112:112 + CONV_W], (0, 256 * chip), (CONV_W, 256))

    vec_names = ["g_pre_mix", "b_dw", "g_conv_ln", "b_conv_ln", "b_conv_out", "g_post_mix", "g_pre_ffn", "g_post_ffn"]
    vec_w = [g_pre_mix, b_dw, g_conv_ln, b_conv_ln, b_conv_out, g_post_mix, g_pre_ffn, g_post_ffn]
    vec_m = [m_g_pre_mix, m_b_dw, m_g_conv_ln, m_b_conv_ln, m_b_conv_out, m_g_post_mix, m_g_pre_ffn, m_g_post_ffn]
    vec_v = [v_g_pre_mix, v_b_dw, v_g_conv_ln, v_b_conv_ln, v_b_conv_out, v_g_post_mix, v_g_pre_ffn, v_g_post_ffn]
    vec_g = [g_g_pre_mix, g_b_dw, g_g_conv_ln, g_b_conv_ln, g_b_conv_out, g_g_post_mix, g_g_pre_ffn, g_g_post_ffn]

    def pack(vs, glu, tab, dw):
        return jnp.concatenate([_vec_pack(vs), _pad_rows(glu.reshape(2, D), LANE_ROWS),
                                _pad_rows(jnp.pad(tab.T, ((0, 0), (0, D - 32))), 24),
                                _pad_rows(jnp.pad(dw, ((0, 0), (0, D - 256))), 32)], axis=0)

    sw = pack(vec_w, b_glu, rel_bias_table, w_dw[0])
    sg = pack(vec_g, g_b_glu, g_tab, g_w_dw)
    sm = pack(vec_m, m_b_glu, m_rel_bias_table, m_w_dw[0])
    sv = pack(vec_v, v_b_glu, v_rel_bias_table, v_w_dw[0])
    s_out = _adamw(sw, sg, sm, sv, "adamw_small")

    def unpack(t):
        vecs = {n: t[LANE_ROWS * i:LANE_ROWS * i + 1] for i, n in enumerate(vec_names)}
        vecs["b_glu"] = t[64:66].reshape(1, 2 * D)
        vecs["rel_bias_table"] = t[72:96, 0:32].T
        vecs["w_dw"] = t[96:96 + CONV_W, 0:256][None]
        return vecs

    small_out = [unpack(t) for t in s_out]
    big = {}
    for n, w, g, m, v in (("w_in", w_in, gw_in, m_w_in, v_w_in),
                          ("w_conv_out", w_conv_out, gw_conv_out, m_w_conv_out, v_w_conv_out),
                          ("w_attn_out", w_attn_out, gw_attn_out, m_w_attn_out, v_w_attn_out),
                          ("w_mix_out", w_mix_out, gw_mix_out, m_w_mix_out, v_w_mix_out),
                          ("w_ffn_in", w_ffn_in, gw_ffn_in, m_w_ffn_in, v_w_ffn_in),
                          ("w_ffn_out", w_ffn_out, gw_ffn_out, m_w_ffn_out, v_w_ffn_out)):
        big[n] = [t[None] for t in _adamw(w[0], g, m[0], v[0], f"adamw_{n}")]

    order = ["rel_bias_table", "g_pre_mix", "w_in", "b_glu", "w_dw", "b_dw", "g_conv_ln", "b_conv_ln", "w_conv_out",
             "b_conv_out", "w_attn_out", "w_mix_out", "g_post_mix", "g_pre_ffn", "w_ffn_in", "w_ffn_out", "g_post_ffn"]
    grads = {"rel_bias_table": g_tab, "g_pre_mix": g_g_pre_mix, "w_in": gw_in[None], "b_glu": g_b_glu,
             "w_dw": g_w_dw[None], "b_dw": g_b_dw, "g_conv_ln": g_g_conv_ln, "b_conv_ln": g_b_conv_ln,
             "w_conv_out": gw_conv_out[None], "b_conv_out": g_b_conv_out, "w_attn_out": gw_attn_out[None],
             "w_mix_out": gw_mix_out[None], "g_post_mix": g_g_post_mix, "g_pre_ffn": g_g_pre_ffn,
             "w_ffn_in": gw_ffn_in[None], "w_ffn_out": gw_ffn_out[None], "g_post_ffn": g_g_post_ffn}
    outs = [loss, grad_x.reshape(1, S, D)] + [grads[n] for n in order]
    for slot in range(3):
        outs += [big[n][slot] if n in big else small_out[slot][n] for n in order]
    return tuple(outs)


def _gather_small_cols(wdw32, chip):
    placed = lax.dynamic_update_slice(jnp.zeros((32, D), F32), wdw32, (0, 256 * chip))
    return _all_sum_small(placed, "conv_taps_gather") * 0.5
```

```python
import functools
import math

import numpy as np
import jax
import jax.numpy as jnp
from jax import lax
from jax.experimental import pallas as pl
from jax.experimental.pallas import tpu as pltpu

F32 = jnp.float32
BF16 = jnp.bfloat16
SDS = jax.ShapeDtypeStruct
MESH = pl.DeviceIdType.MESH
ANY = pl.BlockSpec(memory_space=pl.ANY)

D = 1024
HD = 64
NH = 8
GW = NH * HD
ATTN_COLS = 3 * GW
DILATIONS = (1, 4, 16)
SPAN = 128
QBLK = 128
KBLK = 2 * QBLK
CONV_W = 31
FFN_H = 2816
FFN_T = 256
NFT = FFN_H // FFN_T
RMS_EPS = 1e-6
LN_EPS = 1e-5
NEG_INF = -1e30
SCALE = HD ** -0.5
LANE_ROWS = 8
LANES = 128

ADAM_LR, ADAM_B1, ADAM_B2, ADAM_EPS, ADAM_WD, ADAM_STEP = 0.001, 0.9, 0.999, 1e-08, 0.01, 10

VMEM_LIMIT = 56 * 1024 * 1024


def _cp(sem):
    return pltpu.CompilerParams(dimension_semantics=sem, vmem_limit_bytes=VMEM_LIMIT)


def _dot(a, b):
    return jnp.dot(a, b, preferred_element_type=F32)


def _dot_nt(a, b):
    return lax.dot_general(a, b, (((1,), (1,)), ((), ())), preferred_element_type=F32)


def _dot_tn(a, b):
    return lax.dot_general(a, b, (((0,), (0,)), ((), ())), preferred_element_type=F32)


def _sigmoid(v):
    return 0.5 * jnp.tanh(0.5 * v) + 0.5


def _colsum8(v):
    s = jnp.sum(v, axis=0, keepdims=True)
    row = lax.broadcasted_iota(jnp.int32, (LANE_ROWS, v.shape[1]), 0)
    return jnp.where(row == 0, jnp.broadcast_to(s, (LANE_ROWS, v.shape[1])), 0.0)


def _first_step(*ids):
    ok = ids[0] == 0
    for i in ids[1:]:
        ok = jnp.logical_and(ok, i == 0)
    return ok


def _col_scratch(n, width):
    return pltpu.VMEM((width // LANES, n, LANES), F32)


def _store_cols(scr, v):
    for lb in range(scr.shape[0]):
        scr[lb] = v[:, lb * LANES:(lb + 1) * LANES]


def _load_cols(scr):
    return jnp.concatenate([scr[lb] for lb in range(scr.shape[0])], axis=1)


def _split_residues(scr, dil, put):
    nb, n, _ = scr.shape
    for c in range(dil):
        put(c, jnp.concatenate([scr[lb, pl.ds(c, n // dil, stride=dil), :] for lb in range(nb)], axis=1))


def _merge_residues(scr, dil, get):
    nb, n, _ = scr.shape
    for c in range(dil):
        v = get(c)
        for lb in range(nb):
            scr[lb, pl.ds(c, n // dil, stride=dil), :] = v[:, lb * LANES:(lb + 1) * LANES]


def _residue_shape(S, dil, width):
    return (dil, S // dil, width)


def _residue_spec(TM, dil, width):
    return pl.BlockSpec((dil, TM // dil, width), lambda i: (0, i, 0))


def _in_proj_rest(x, g, w, rider):
    S = x.shape[0]
    N = w.shape[1]
    TM, TN = 512, 512

    def body(x_ref, g_ref, w_ref, zr_ref, h0_ref, h1_ref, h2_ref, hf_scr):
        xf = x_ref[...]
        r = lax.rsqrt(jnp.mean(xf * xf, axis=-1, keepdims=True) + RMS_EPS)
        hf = xf * r * g_ref[...]
        h0_ref[...] = hf.astype(BF16)
        _store_cols(hf_scr, hf)
        for dil, ref in ((DILATIONS[1], h1_ref), (DILATIONS[2], h2_ref)):
            def put(c, v, ref=ref):
                ref[c] = v.astype(BF16)
            _split_residues(hf_scr, dil, put)
        for j in range(N // TN):
            zr_ref[:, j * TN:(j + 1) * TN] = _dot(h0_ref[...], w_ref[:, j * TN:(j + 1) * TN]).astype(BF16)

    body, r_in, r_out, r_shape, r_scr = _ride(body, 3, 4, 1, rider, S // TM)
    return pl.pallas_call(
        body, name="in_proj_rest", grid=(S // TM,),
        in_specs=[pl.BlockSpec((TM, D), lambda i: (i, 0)),
                  pl.BlockSpec((1, D), lambda i: (0, 0)),
                  pl.BlockSpec((D, N), lambda i: (0, 0), pipeline_mode=pl.Buffered(1))] + r_in,
        out_specs=[pl.BlockSpec((TM, N), lambda i: (i, 0)), pl.BlockSpec((TM, D), lambda i: (i, 0)),
                   _residue_spec(TM, DILATIONS[1], D), _residue_spec(TM, DILATIONS[2], D)] + r_out,
        out_shape=[SDS((S, N), BF16), SDS((S, D), BF16),
                   SDS(_residue_shape(S, DILATIONS[1], D), BF16),
                   SDS(_residue_shape(S, DILATIONS[2], D), BF16)] + r_shape,
        scratch_shapes=[_col_scratch(TM, D)] + r_scr,
        compiler_params=_cp(("arbitrary",)),
    )(x, g, w, *rider.ins)


def _in_proj_qkv(h, w):
    S = h.shape[0]
    TM = 512

    def body(h_ref, w_ref, z0_ref, z1_ref, z2_ref, scr):
        outs = (z0_ref, z1_ref, z2_ref)
        for j in range(9):
            t, gi = j // 3, j % 3
            cols = slice(t * GW, (t + 1) * GW)
            zt = _dot(h_ref[...], w_ref[:, j * GW:(j + 1) * GW])
            if gi == 0:
                z0_ref[0, :, cols] = zt.astype(BF16)
            else:
                slot = scr.at[2 * t + gi - 1]
                _store_cols(slot, zt)

                def put(c, v, ref=outs[gi], cols=cols):
                    ref[c, :, cols] = v.astype(BF16)
                _split_residues(slot, DILATIONS[gi], put)

    return pl.pallas_call(
        body, name="in_proj_qkv", grid=(S // TM,),
        in_specs=[pl.BlockSpec((TM, D), lambda i: (i, 0)),
                  pl.BlockSpec(w.shape, lambda i: (0, 0), pipeline_mode=pl.Buffered(1))],
        out_specs=[_residue_spec(TM, d, 3 * GW) for d in DILATIONS],
        out_shape=[SDS(_residue_shape(S, d, 3 * GW), BF16) for d in DILATIONS],
        scratch_shapes=[pltpu.VMEM((6, GW // LANES, TM, LANES), F32)],
        compiler_params=_cp(("arbitrary",)),
    )(h, w)


def _bucket_tables():
    a = np.arange(QBLK, dtype=np.int32)[:, None]
    c = np.arange(KBLK, dtype=np.int32)[None, :]
    off = a - c + QBLK
    valid = ((off >= 0) & (off <= SPAN)).astype(np.float32)
    tabs = []
    for dil in DILATIONS:
        dist = np.maximum(off * dil, 0)
        df = np.maximum(dist, 1).astype(np.float32)
        large = 16 + (np.log(df / np.float32(16)) / np.float32(math.log(2048 / 16)) * np.float32(16)).astype(np.int32)
        large = np.minimum(large, 31)
        tabs.append(np.where(dist < 16, dist, large).astype(np.int32))
    return np.stack(tabs), valid


def _bias_expand(tab, buckets, valid):
    def body(tab_ref, b_ref, v_ref, o_ref):
        for gi in range(3):
            bk = b_ref[gi]
            for h in range(NH):
                acc = jnp.zeros((QBLK, KBLK), F32)
                for b in range(32):
                    acc = jnp.where(bk == b, tab_ref[b, gi * NH + h], acc)
                o_ref[gi * NH + h] = jnp.where(v_ref[...] > 0.5, acc, NEG_INF)

    return pl.pallas_call(
        body, name="bias_expand",
        in_specs=[pl.BlockSpec(memory_space=pltpu.SMEM),
                  pl.BlockSpec(memory_space=pltpu.VMEM), pl.BlockSpec(memory_space=pltpu.VMEM)],
        out_specs=pl.BlockSpec(memory_space=pltpu.VMEM),
        out_shape=SDS((3 * NH, QBLK, KBLK), F32),
    )(tab, buckets, valid)


def _bias_reduce(dbias, buckets):
    def body(d_ref, b_ref, o_ref):
        lane = lax.broadcasted_iota(jnp.int32, (1, D), 1)
        for gi in range(3):
            bk = b_ref[gi]
            for h in range(NH):
                dv = d_ref[gi * NH + h]
                row = jnp.zeros((1, D), F32)
                for b in range(32):
                    m = jnp.where(bk == b, dv, 0.0)
                    val = jnp.sum(jnp.sum(m, axis=0, keepdims=True), axis=1, keepdims=True)
                    row = jnp.where(lane == b, val, row)
                o_ref[gi * NH + h:gi * NH + h + 1, :] = row

    return pl.pallas_call(
        body, name="bias_reduce",
        in_specs=[pl.BlockSpec(memory_space=pltpu.VMEM), pl.BlockSpec(memory_space=pltpu.VMEM)],
        out_specs=pl.BlockSpec(memory_space=pltpu.VMEM),
        out_shape=SDS((3 * NH, D), F32),
    )(dbias, buckets)


def _attn_tile(S, dil):
    L = S // dil
    tq = min(512, L)
    return L, tq, tq // QBLK, L // tq


def _pair_stack(ref, rows, lanes, scale=None):
    blk = ref[rows, lanes]
    if scale is not None:
        blk = blk * scale
    lane = lax.broadcasted_iota(jnp.int32, blk.shape, 1)
    zero = jnp.zeros_like(blk)
    return jnp.concatenate([jnp.where(lane < HD, blk, zero), jnp.where(lane >= HD, blk, zero)], axis=0)


def _attn_fwd(zq, bias2, gi):
    dil, L, _ = zq.shape
    _, TQ, QB, ns = _attn_tile(L * dil, dil)
    NP = NH // 2

    def body(q_ref, kc_ref, kp_ref, vc_ref, vp_ref, b_ref, o_ref, l_ref, kext, vext):
        n = pl.program_id(1)
        kext[0:QBLK, :] = kp_ref[0]
        kext[QBLK:, :] = kc_ref[0]
        vext[0:QBLK, :] = vp_ref[0]
        vext[QBLK:, :] = vc_ref[0]
        col = lax.broadcasted_iota(jnp.int32, (2 * QBLK, KBLK), 1)
        no_prev = jnp.logical_and(n == 0, col < QBLK)
        lane = lax.broadcasted_iota(jnp.int32, (QBLK, LANES), 1)
        for hp in range(NP):
            pl_ = slice(hp * LANES, (hp + 1) * LANES)
            for b in range(QB):
                rows = slice(b * QBLK, (b + 1) * QBLK)
                win = slice(b * QBLK, b * QBLK + KBLK)
                s = _dot_nt(_pair_stack(q_ref.at[0], rows, pl_, SCALE), kext[win, pl_]) + b_ref[0, hp]
                if b == 0:
                    s = jnp.where(no_prev, NEG_INF, s)
                m = jnp.max(s, axis=-1, keepdims=True)
                p = jnp.exp(s - m)
                l = jnp.sum(p, axis=-1, keepdims=True)
                o2 = _dot(p.astype(BF16), vext[win, pl_]) / l
                lse2 = jnp.broadcast_to(m + jnp.log(l), (2 * QBLK, LANES))
                o_ref[0, rows, pl_] = jnp.where(lane < HD, o2[0:QBLK], o2[QBLK:]).astype(BF16)
                l_ref[0, rows, pl_] = jnp.where(lane < HD, lse2[0:QBLK], lse2[QBLK:])

    def prev(n):
        return jnp.maximum(n * QB - 1, 0)

    return pl.pallas_call(
        body, name=f"attn_fwd_g{gi}", grid=(dil, ns),
        in_specs=[pl.BlockSpec((1, TQ, GW), lambda c, n: (c, n, 0)),
                  pl.BlockSpec((1, TQ, GW), lambda c, n: (c, n, 1)),
                  pl.BlockSpec((1, QBLK, GW), lambda c, n: (c, prev(n), 1)),
                  pl.BlockSpec((1, TQ, GW), lambda c, n: (c, n, 2)),
                  pl.BlockSpec((1, QBLK, GW), lambda c, n: (c, prev(n), 2)),
                  pl.BlockSpec((1, NP, 2 * QBLK, KBLK), lambda c, n: (gi, 0, 0, 0))],
        out_specs=[pl.BlockSpec((1, TQ, GW), lambda c, n: (c, n, 0)),
                   pl.BlockSpec((1, TQ, GW), lambda c, n: (c, n, 0))],
        out_shape=[SDS((dil, L, GW), BF16), SDS((dil, L, GW), F32)],
        scratch_shapes=[pltpu.VMEM((QBLK + TQ, GW), BF16), pltpu.VMEM((QBLK + TQ, GW), BF16)],
        compiler_params=_cp(("arbitrary", "arbitrary")),
    )(zq, zq, zq, zq, zq, bias2)


CONV_TM = 256
SHIFT_PAD = 24


def _make_shifts(src, sh, n):
    for b in range(1, 8):
        sh[b - 1] = src[b:b + n + SHIFT_PAD, :]


def _shifted(src, sh, off, r0, n, lanes):
    a, b = divmod(off, 8)
    if b == 0:
        return src[8 * a + r0:8 * a + r0 + n, lanes]
    return sh[b - 1, 8 * a + r0:8 * a + r0 + n, lanes]


CONV_RC = 64


def _tap_blocks(TM):
    return [(r0, slice(l0, l0 + LANES)) for l0 in range(0, D, LANES) for r0 in range(0, TM, CONV_RC)]


def _conv_fwd(zrest, b_glu, wdw, b_dw, g_ln, b_ln):
    S = zrest.shape[0]
    TM = CONV_TM
    HALO = 32
    hb = TM // HALO

    def body(u_ref, g_ref, uh_ref, gh_ref, bg_ref, w_ref, bd_ref, gl_ref, bl_ref, cv_ref, a_ref, ext, sh):
        i = pl.program_id(0)
        bu = bg_ref[:, 0:D]
        bgt = bg_ref[:, D:2 * D]
        uh = (uh_ref[...].astype(F32) + bu) * _sigmoid(gh_ref[...].astype(F32) + bgt)
        ext[0:HALO, :] = jnp.where(i == 0, 0.0, uh)
        ext[HALO:, :] = (u_ref[...].astype(F32) + bu) * _sigmoid(g_ref[...].astype(F32) + bgt)
        _make_shifts(ext, sh, TM)
        acc = jnp.zeros((TM, D), F32)
        for j in range(CONV_W):
            acc = acc + _shifted(ext, sh, HALO - (CONV_W - 1) + j, 0, TM, slice(None)) * w_ref[j:j + 1, :]
        cv = (acc + bd_ref[...]).astype(BF16)
        cv_ref[...] = cv
        cf = cv.astype(F32)
        mu = jnp.mean(cf, axis=-1, keepdims=True)
        xc = cf - mu
        y = xc * lax.rsqrt(jnp.mean(xc * xc, axis=-1, keepdims=True) + LN_EPS) * gl_ref[...] + bl_ref[...]
        a_ref[...] = (y * _sigmoid(y)).astype(BF16)

    vec = pl.BlockSpec((1, D), lambda i: (0, 0))
    return pl.pallas_call(
        body, name="conv_fwd", grid=(S // TM,),
        in_specs=[pl.BlockSpec((TM, D), lambda i: (i, 0)), pl.BlockSpec((TM, D), lambda i: (i, 1)),
                  pl.BlockSpec((HALO, D), lambda i: (jnp.maximum(i * hb - 1, 0), 0)),
                  pl.BlockSpec((HALO, D), lambda i: (jnp.maximum(i * hb - 1, 0), 1)),
                  pl.BlockSpec((1, 2 * D), lambda i: (0, 0)),
                  pl.BlockSpec((32, D), lambda i: (0, 0)), vec, vec, vec],
        out_specs=[pl.BlockSpec((TM, D), lambda i: (i, 0)), pl.BlockSpec((TM, D), lambda i: (i, 0))],
        out_shape=[SDS((S, D), BF16), SDS((S, D), BF16)],
        scratch_shapes=[pltpu.VMEM((HALO + TM, D), F32), pltpu.VMEM((7, TM + SHIFT_PAD, D), F32)],
        compiler_params=_cp(("arbitrary",)),
    )(zrest, zrest, zrest, zrest, b_glu, wdw, b_dw, g_ln, b_ln)


def _mix_fwd(og, lg, a, zrest, x, w_ao, w_co, b_co, w_mx, g_pm):
    S = x.shape[0]
    TM = 512

    def body(o0, o1, o2, l0, l1, l2, a_ref, ga_ref, gc_ref, x_ref, wa_ref, wc_ref, bc_ref, wm_ref, g_ref,
             o_ref, oa_ref, ob_ref, lse_ref, lsea_ref, lseb_ref, ya_ref, yc_ref, mg_ref, mm_ref, x1_ref,
             so1, so2, sl1, sl2, so, sl):
        for dil, src, dst, cast in ((DILATIONS[1], o1, so1, True), (DILATIONS[2], o2, so2, True),
                                    (DILATIONS[1], l1, sl1, False), (DILATIONS[2], l2, sl2, False)):
            _merge_residues(dst, dil, (lambda c, src=src: src[c].astype(F32)) if cast else (lambda c, src=src: src[c]))
        la, lb, lc = l0[0], _load_cols(sl1), _load_cols(sl2)
        m = jnp.maximum(jnp.maximum(la, lb), lc)
        e0 = jnp.exp(la - m)
        e1 = jnp.exp(lb - m)
        e2 = jnp.exp(lc - m)
        den = e0 + e1 + e2
        of = (e0 * o0[0].astype(F32) + e1 * _load_cols(so1) + e2 * _load_cols(so2)) / den
        o = of.astype(BF16)
        o_ref[...] = o
        lse = m + jnp.log(den)
        lse_ref[...] = lse
        _store_cols(so, of)
        _store_cols(sl, lse)
        for dil, oref, lref in ((DILATIONS[1], oa_ref, lsea_ref), (DILATIONS[2], ob_ref, lseb_ref)):
            def put_o(c, v, oref=oref):
                oref[c] = v.astype(BF16)

            def put_l(c, v, lref=lref):
                lref[c] = v
            _split_residues(so, dil, put_o)
            _split_residues(sl, dil, put_l)
        ya = _dot(o, wa_ref[...]).astype(BF16)
        yc = (_dot(a_ref[...], wc_ref[...]) + bc_ref[...]).astype(BF16)
        ya_ref[...] = ya
        yc_ref[...] = yc
        mg = (_sigmoid(ga_ref[...].astype(F32)) * ya.astype(F32)
              + _sigmoid(gc_ref[...].astype(F32)) * yc.astype(F32)).astype(BF16)
        mg_ref[...] = mg
        mm = _dot(mg, wm_ref[...]).astype(BF16)
        mm_ref[...] = mm
        mf = mm.astype(F32)
        r = lax.rsqrt(jnp.mean(mf * mf, axis=-1, keepdims=True) + RMS_EPS)
        x1_ref[...] = x_ref[...] + mf * r * g_ref[...]

    row512 = pl.BlockSpec((TM, GW), lambda i: (i, 0))
    rowd = pl.BlockSpec((TM, D), lambda i: (i, 0))
    vec = pl.BlockSpec((1, D), lambda i: (0, 0))
    full = lambda r, c: pl.BlockSpec((r, c), lambda i: (0, 0))
    res = [_residue_spec(TM, d, GW) for d in DILATIONS]
    rshape = lambda d, t: SDS(_residue_shape(S, d, GW), t)
    scr = _col_scratch(TM, GW)
    return pl.pallas_call(
        body, name="mix_fwd", grid=(S // TM,),
        in_specs=res + res + [rowd, pl.BlockSpec((TM, D), lambda i: (i, 2)), pl.BlockSpec((TM, D), lambda i: (i, 3)),
                              rowd, full(GW, D), full(D, D), vec, full(D, D), vec],
        out_specs=[row512, res[1], res[2], row512, res[1], res[2], rowd, rowd, rowd, rowd, rowd],
        out_shape=[SDS((S, GW), BF16), rshape(DILATIONS[1], BF16), rshape(DILATIONS[2], BF16),
                   SDS((S, GW), F32), rshape(DILATIONS[1], F32), rshape(DILATIONS[2], F32),
                   SDS((S, D), BF16), SDS((S, D), BF16), SDS((S, D), BF16), SDS((S, D), BF16), SDS((S, D), F32)],
        scratch_shapes=[scr] * 6,
        compiler_params=_cp(("arbitrary",)),
    )(og[0], og[1], og[2], lg[0], lg[1], lg[2], a, zrest, zrest, x, w_ao, w_co, b_co, w_mx, g_pm)


def _ffn_fwd(x1, tgt, g_pre, g_post, w_fi, w_fo):
    S = x1.shape[0]
    TM = 512

    def body(x1_ref, t_ref, gp_ref, go_ref, wi_ref, wo_ref,
             h2_ref, gu_ref, df_ref, dx2_ref, loss_ref, dgo_ref):
        i = pl.program_id(0)

        @pl.when(i == 0)
        def _():
            loss_ref[...] = jnp.zeros_like(loss_ref)
            dgo_ref[...] = jnp.zeros_like(dgo_ref)

        xf = x1_ref[...]
        r = lax.rsqrt(jnp.mean(xf * xf, axis=-1, keepdims=True) + RMS_EPS)
        h2_ref[...] = (xf * r * gp_ref[...]).astype(BF16)
        f = jnp.zeros((TM, D), F32)
        for k in range(NFT):
            gt = _dot(h2_ref[...], wi_ref[:, k * FFN_T:(k + 1) * FFN_T]).astype(BF16)
            up = _dot(h2_ref[...], wi_ref[:, FFN_H + k * FFN_T:FFN_H + (k + 1) * FFN_T]).astype(BF16)
            gu_ref[:, 2 * k * FFN_T:(2 * k + 1) * FFN_T] = gt
            gu_ref[:, (2 * k + 1) * FFN_T:(2 * k + 2) * FFN_T] = up
            gf = gt.astype(F32)
            act = (gf * _sigmoid(gf) * up.astype(F32)).astype(BF16)
            f = f + _dot(act, wo_ref[k * FFN_T:(k + 1) * FFN_T, :])
        r = lax.rsqrt(jnp.mean(f * f, axis=-1, keepdims=True) + RMS_EPS)
        nrm = f * r
        e = x1_ref[...] + nrm * go_ref[...] - t_ref[...]
        tot = jnp.sum(jnp.sum(e * e, axis=-1, keepdims=True), axis=0, keepdims=True) * (0.5 / D)
        corner = jnp.logical_and(lax.broadcasted_iota(jnp.int32, (LANE_ROWS, D), 0) == 0,
                                 lax.broadcasted_iota(jnp.int32, (LANE_ROWS, D), 1) == 0)
        loss_ref[...] += jnp.where(corner, tot, 0.0)
        dx2 = e * (1.0 / D)
        dx2_ref[...] = dx2
        dgo_ref[...] += _colsum8(dx2 * nrm)
        dn = dx2 * go_ref[...]
        df_ref[...] = (r * (dn - nrm * jnp.mean(dn * nrm, axis=-1, keepdims=True))).astype(BF16)

    rowd = pl.BlockSpec((TM, D), lambda i: (i, 0))
    vec = pl.BlockSpec((1, D), lambda i: (0, 0))
    acc8 = pl.BlockSpec((LANE_ROWS, D), lambda i: (0, 0))
    return pl.pallas_call(
        body, name="ffn_fwd", grid=(S // TM,),
        in_specs=[rowd, rowd, vec, vec,
                  pl.BlockSpec((D, 2 * FFN_H), lambda i: (0, 0), pipeline_mode=pl.Buffered(1)),
                  pl.BlockSpec((FFN_H, D), lambda i: (0, 0), pipeline_mode=pl.Buffered(1))],
        out_specs=[rowd, pl.BlockSpec((TM, 2 * FFN_H), lambda i: (i, 0)), rowd, rowd, acc8, acc8],
        out_shape=[SDS((S, D), BF16), SDS((S, 2 * FFN_H), BF16), SDS((S, D), BF16), SDS((S, D), F32),
                   SDS((LANE_ROWS, D), F32), SDS((LANE_ROWS, D), F32)],
        compiler_params=_cp(("arbitrary",)),
    )(x1, tgt, g_pre, g_post, w_fi, w_fo)


def _ffn_bwd_act(df, gu, w_foT):
    S = df.shape[0]
    TM = 512

    def body_act(df_ref, gu_ref, wo_ref, dff_ref, act_ref):
        for k in range(NFT):
            dact = _dot(df_ref[...], wo_ref[:, k * FFN_T:(k + 1) * FFN_T])
            g = gu_ref[:, 2 * k * FFN_T:(2 * k + 1) * FFN_T].astype(F32)
            u = gu_ref[:, (2 * k + 1) * FFN_T:(2 * k + 2) * FFN_T].astype(F32)
            sg = _sigmoid(g)
            sl = g * sg
            act_ref[:, k * FFN_T:(k + 1) * FFN_T] = (sl * u).astype(BF16)
            dff_ref[:, 2 * k * FFN_T:(2 * k + 1) * FFN_T] = (dact * u * (sg * (1.0 + g * (1.0 - sg)))).astype(BF16)
            dff_ref[:, (2 * k + 1) * FFN_T:(2 * k + 2) * FFN_T] = (dact * sl).astype(BF16)

    rowd = pl.BlockSpec((TM, D), lambda i: (i, 0))
    wide = pl.BlockSpec((TM, 2 * FFN_H), lambda i: (i, 0))
    return pl.pallas_call(
        body_act, name="ffn_bwd_act", grid=(S // TM,),
        in_specs=[rowd, wide, pl.BlockSpec((D, FFN_H), lambda i: (0, 0), pipeline_mode=pl.Buffered(1))],
        out_specs=[wide, pl.BlockSpec((TM, FFN_H), lambda i: (i, 0))],
        out_shape=[SDS((S, 2 * FFN_H), BF16), SDS((S, FFN_H), BF16)],
        compiler_params=_cp(("arbitrary",)),
    )(df, gu, w_foT)


def _ffn_bwd_in(dff, x1, dx2, g_pre, w_fiT, rider):
    S = x1.shape[0]
    TM = 512
    rowd = pl.BlockSpec((TM, D), lambda i: (i, 0))
    wide = pl.BlockSpec((TM, 2 * FFN_H), lambda i: (i, 0))
    KC = 512
    nkc = 2 * FFN_H // KC

    def body_in(dff_ref, x1_ref, dx2_ref, gp_ref, wi_ref, dx1_ref, dgp_ref):
        i = pl.program_id(0)

        @pl.when(i == 0)
        def _():
            dgp_ref[...] = jnp.zeros_like(dgp_ref)

        dh = jnp.zeros((TM, D), F32)
        for k in range(nkc):
            dh = dh + _dot(dff_ref[:, k * KC:k * KC + FFN_T], wi_ref[k * FFN_T:(k + 1) * FFN_T, :]) \
                + _dot(dff_ref[:, k * KC + FFN_T:(k + 1) * KC], wi_ref[FFN_H + k * FFN_T:FFN_H + (k + 1) * FFN_T, :])
        xf = x1_ref[...]
        r = lax.rsqrt(jnp.mean(xf * xf, axis=-1, keepdims=True) + RMS_EPS)
        nrm = xf * r
        dgp_ref[...] += _colsum8(dh * nrm)
        dn = dh * gp_ref[...]
        dx1_ref[...] = dx2_ref[...] + r * (dn - nrm * jnp.mean(dn * nrm, axis=-1, keepdims=True))

    body_in, r_in, r_out, r_shape, r_scr = _ride(body_in, 5, 2, 0, rider, S // TM)
    return pl.pallas_call(
        body_in, name="ffn_bwd_in", grid=(S // TM,),
        in_specs=[wide, rowd, rowd, pl.BlockSpec((1, D), lambda i: (0, 0)),
                  pl.BlockSpec((2 * FFN_H, D), lambda i: (0, 0), pipeline_mode=pl.Buffered(1))] + r_in,
        out_specs=[rowd, pl.BlockSpec((LANE_ROWS, D), lambda i: (0, 0))] + r_out,
        out_shape=[SDS((S, D), F32), SDS((LANE_ROWS, D), F32)] + r_shape,
        scratch_shapes=r_scr,
        compiler_params=_cp(("arbitrary",)),
    )(dff, x1, dx2, g_pre, w_fiT, *rider.ins)


def _mix_bwd(dx1, mm, ya, yc, zrest, g_pm, w_mxT, w_aoT, w_coT):
    S = dx1.shape[0]
    TM = 512

    def body(dx_ref, mm_ref, ya_ref, yc_ref, ga_ref, gc_ref, g_ref, wm_ref, wa_ref, wc_ref,
             dmm_ref, dya_ref, dyc_ref, do_ref, doa_ref, dob_ref, da_ref, dzg_ref, dgpm_ref, dbco_ref, sdo):
        i = pl.program_id(0)

        @pl.when(i == 0)
        def _():
            dgpm_ref[...] = jnp.zeros_like(dgpm_ref)
            dbco_ref[...] = jnp.zeros_like(dbco_ref)

        mf = mm_ref[...].astype(F32)
        r = lax.rsqrt(jnp.mean(mf * mf, axis=-1, keepdims=True) + RMS_EPS)
        nrm = mf * r
        dx = dx_ref[...]
        dgpm_ref[...] += _colsum8(dx * nrm)
        dn = dx * g_ref[...]
        dmm = (r * (dn - nrm * jnp.mean(dn * nrm, axis=-1, keepdims=True))).astype(BF16)
        dmm_ref[...] = dmm
        dmg = _dot(dmm, wm_ref[...])
        sa = _sigmoid(ga_ref[...].astype(F32))
        sc = _sigmoid(gc_ref[...].astype(F32))
        dya = (dmg * sa).astype(BF16)
        dyc = (dmg * sc).astype(BF16)
        dya_ref[...] = dya
        dyc_ref[...] = dyc
        dbco_ref[...] += _colsum8(dyc.astype(F32))
        dzg_ref[:, 0:D] = (dmg * ya_ref[...].astype(F32) * (sa * (1.0 - sa))).astype(BF16)
        dzg_ref[:, D:] = (dmg * yc_ref[...].astype(F32) * (sc * (1.0 - sc))).astype(BF16)
        dof = _dot(dya, wa_ref[...])
        do_ref[...] = dof.astype(BF16)
        _store_cols(sdo, dof)
        for dil, ref in ((DILATIONS[1], doa_ref), (DILATIONS[2], dob_ref)):
            def put(c, v, ref=ref):
                ref[c] = v.astype(BF16)
            _split_residues(sdo, dil, put)
        da_ref[...] = _dot(dyc, wc_ref[...]).astype(BF16)

    rowd = pl.BlockSpec((TM, D), lambda i: (i, 0))
    full = lambda r, c: pl.BlockSpec((r, c), lambda i: (0, 0))
    acc8 = pl.BlockSpec((LANE_ROWS, D), lambda i: (0, 0))
    return pl.pallas_call(
        body, name="mix_bwd", grid=(S // TM,),
        in_specs=[rowd, rowd, rowd, rowd, pl.BlockSpec((TM, D), lambda i: (i, 2)),
                  pl.BlockSpec((TM, D), lambda i: (i, 3)), full(1, D), full(D, D), full(D, GW), full(D, D)],
        out_specs=[rowd, rowd, rowd, pl.BlockSpec((TM, GW), lambda i: (i, 0)),
                   _residue_spec(TM, DILATIONS[1], GW), _residue_spec(TM, DILATIONS[2], GW), rowd,
                   pl.BlockSpec((TM, 2 * D), lambda i: (i, 0)), acc8, acc8],
        out_shape=[SDS((S, D), BF16), SDS((S, D), BF16), SDS((S, D), BF16), SDS((S, GW), BF16),
                   SDS(_residue_shape(S, DILATIONS[1], GW), BF16), SDS(_residue_shape(S, DILATIONS[2], GW), BF16),
                   SDS((S, D), BF16), SDS((S, 2 * D), BF16), SDS((LANE_ROWS, D), F32), SDS((LANE_ROWS, D), F32)],
        scratch_shapes=[_col_scratch(TM, GW)],
        compiler_params=_cp(("arbitrary",)),
    )(dx1, mm, ya, yc, zrest, zrest, g_pm, w_mxT, w_aoT, w_coT)


def _conv_bwd(da, cv, zrest, b_glu, wdw, g_ln, b_ln):
    S = da.shape[0]
    TM = CONV_TM
    HALO = 32
    hb = TM // HALO
    nh = S // HALO

    def body(da_ref, dan_ref, cv_ref, cvn_ref, u_ref, g_ref, uh_ref, gh_ref, bg_ref, w_ref, gl_ref, bl_ref,
             dglu_ref, dbu_ref, dbg_ref, dw_ref, dgl_ref, dbl_ref, dbd_ref, dext, uext, dsh, ush, du_scr, dw8):
        i = pl.program_id(0)
        last = i == pl.num_programs(0) - 1

        @pl.when(i == 0)
        def _():
            for ref in (dbu_ref, dbg_ref, dw8, dgl_ref, dbl_ref, dbd_ref):
                ref[...] = jnp.zeros_like(ref)

        def ln_bwd(da_v, cv_v):
            cf = cv_v.astype(F32)
            mu = jnp.mean(cf, axis=-1, keepdims=True)
            xc = cf - mu
            rstd = lax.rsqrt(jnp.mean(xc * xc, axis=-1, keepdims=True) + LN_EPS)
            xh = xc * rstd
            y = xh * gl_ref[...] + bl_ref[...]
            sy = _sigmoid(y)
            dy = da_v.astype(F32) * (sy * (1.0 + y * (1.0 - sy)))
            dxh = dy * gl_ref[...]
            dcv = rstd * (dxh - jnp.mean(dxh, axis=-1, keepdims=True)
                          - xh * jnp.mean(dxh * xh, axis=-1, keepdims=True))
            return dcv, dy, xh

        dcv, dy, xh = ln_bwd(da_ref[...], cv_ref[...])
        dgl_ref[...] += _colsum8(dy * xh)
        dbl_ref[...] += _colsum8(dy)
        dbd_ref[...] += _colsum8(dcv)
        dcvn, _, _ = ln_bwd(dan_ref[...], cvn_ref[...])
        dext[0:TM, :] = dcv
        dext[TM:, :] = jnp.where(last, 0.0, dcvn)

        bu = bg_ref[:, 0:D]
        bgt = bg_ref[:, D:2 * D]
        upre = u_ref[...].astype(F32) + bu
        sg = _sigmoid(g_ref[...].astype(F32) + bgt)
        uh = (uh_ref[...].astype(F32) + bu) * _sigmoid(gh_ref[...].astype(F32) + bgt)
        uext[0:HALO, :] = jnp.where(i == 0, 0.0, uh)
        uext[HALO:, :] = upre * sg

        _make_shifts(dext, dsh, TM)
        _make_shifts(uext, ush, TM)
        for r0, lanes in _tap_blocks(TM):
            acc = jnp.zeros((CONV_RC, LANES), F32)
            for j in range(CONV_W):
                acc = acc + _shifted(dext, dsh, CONV_W - 1 - j, r0, CONV_RC, lanes) * w_ref[j:j + 1, lanes]
            du_scr[r0:r0 + CONV_RC, lanes] = acc
        for l0 in range(0, D, LANES):
            lanes = slice(l0, l0 + LANES)
            accs = [jnp.zeros((LANE_ROWS, LANES), F32)] * CONV_W
            for r0 in range(0, TM, CONV_RC):
                dc = dext[r0:r0 + CONV_RC, lanes]
                for j in range(CONV_W):
                    prod = dc * _shifted(uext, ush, HALO - (CONV_W - 1) + j, r0, CONV_RC, lanes)
                    accs[j] = accs[j] + jnp.sum(prod.reshape(CONV_RC // LANE_ROWS, LANE_ROWS, LANES), axis=0)
            for j in range(CONV_W):
                dw8[j, :, lanes] += accs[j]

        @pl.when(last)
        def _():
            for j in range(CONV_W):
                dw_ref[j:j + 1, :] = jnp.sum(dw8[j], axis=0, keepdims=True)
            dw_ref[CONV_W:, :] = jnp.zeros((32 - CONV_W, D), F32)

        du = du_scr[...]
        dup = du * sg
        dgp = du * upre * (sg * (1.0 - sg))
        dglu_ref[:, 0:D] = dup.astype(BF16)
        dglu_ref[:, D:] = dgp.astype(BF16)
        dbu_ref[...] += _colsum8(dup.astype(BF16).astype(F32))
        dbg_ref[...] += _colsum8(dgp.astype(BF16).astype(F32))

    rowd = pl.BlockSpec((TM, D), lambda i: (i, 0))
    nxt = pl.BlockSpec((HALO, D), lambda i: (jnp.minimum((i + 1) * hb, nh - 1), 0))
    vec = pl.BlockSpec((1, D), lambda i: (0, 0))
    acc8 = pl.BlockSpec((LANE_ROWS, D), lambda i: (0, 0))
    return pl.pallas_call(
        body, name="conv_bwd", grid=(S // TM,),
        in_specs=[rowd, nxt, rowd, nxt,
                  pl.BlockSpec((TM, D), lambda i: (i, 0)), pl.BlockSpec((TM, D), lambda i: (i, 1)),
                  pl.BlockSpec((HALO, D), lambda i: (jnp.maximum(i * hb - 1, 0), 0)),
                  pl.BlockSpec((HALO, D), lambda i: (jnp.maximum(i * hb - 1, 0), 1)),
                  pl.BlockSpec((1, 2 * D), lambda i: (0, 0)), pl.BlockSpec((32, D), lambda i: (0, 0)), vec, vec],
        out_specs=[pl.BlockSpec((TM, 2 * D), lambda i: (i, 0)), acc8, acc8,
                   pl.BlockSpec((32, D), lambda i: (0, 0)), acc8, acc8, acc8],
        out_shape=[SDS((S, 2 * D), BF16), SDS((LANE_ROWS, D), F32), SDS((LANE_ROWS, D), F32), SDS((32, D), F32),
                   SDS((LANE_ROWS, D), F32), SDS((LANE_ROWS, D), F32), SDS((LANE_ROWS, D), F32)],
        scratch_shapes=[pltpu.VMEM((TM + HALO, D), F32), pltpu.VMEM((HALO + TM, D), F32),
                        pltpu.VMEM((7, TM + SHIFT_PAD, D), F32), pltpu.VMEM((7, TM + SHIFT_PAD, D), F32),
                        pltpu.VMEM((TM, D), F32), pltpu.VMEM((32, LANE_ROWS, D), F32)],
        compiler_params=_cp(("arbitrary",)),
    )(da, da, cv, cv, zrest, zrest, zrest, zrest, b_glu, wdw, g_ln, b_ln)


def _attn_bwd(zq, do, o, lse, bias_t, gi):
    dil, L, _ = zq.shape
    _, TQ, QB, ns = _attn_tile(L * dil, dil)
    NP = NH // 2

    def body(q3, kc3, kp3, vc3, vp3, do3, o3, l3, b_ref,
             out3, db_ref, kext, vext, dkx, dvx, dqn, dqc, dkc, dvc):
        q_ref, kc_ref, kp_ref, vc_ref, vp_ref, do_ref, o_ref, l_ref, out_ref = (
            r.at[0] for r in (q3, kc3, kp3, vc3, vp3, do3, o3, l3, out3))
        c = pl.program_id(0)
        n = pl.program_id(1)

        @pl.when(_first_step(c, n))
        def _():
            db_ref[...] = jnp.zeros_like(db_ref)

        @pl.when(n < ns)
        def _():
            kext[0:QBLK, :] = kp_ref[...]
            kext[QBLK:, :] = kc_ref[...]
            vext[0:QBLK, :] = vp_ref[...]
            vext[QBLK:, :] = vc_ref[...]
            krow = lax.broadcasted_iota(jnp.int32, (KBLK, 2 * QBLK), 0)
            no_prev = jnp.logical_and(n == 0, krow < QBLK)
            lane = lax.broadcasted_iota(jnp.int32, (QBLK, LANES), 1)

            def overlap_add(parts):
                segs = [parts[0][0:QBLK]]
                for b in range(1, QB):
                    segs.append(parts[b - 1][QBLK:] + parts[b][0:QBLK])
                segs.append(parts[QB - 1][QBLK:])
                return jnp.concatenate(segs, axis=0)

            for hp in range(NP):
                pl_ = slice(hp * LANES, (hp + 1) * LANES)
                dv_parts, dk_parts, dbsum = [], [], None
                for b in range(QB):
                    rows = slice(b * QBLK, (b + 1) * QBLK)
                    win = slice(b * QBLK, b * QBLK + KBLK)
                    q2 = _pair_stack(q_ref, rows, pl_, SCALE)
                    do2 = _pair_stack(do_ref, rows, pl_)
                    lse_t = l_ref[rows, pl_].T
                    lse_row = jnp.concatenate([lse_t[0:1], lse_t[HD:HD + 1]], axis=1)
                    prod_t = (do_ref[rows, pl_].astype(F32) * o_ref[rows, pl_].astype(F32)).T
                    delta_row = jnp.concatenate([jnp.sum(prod_t[0:HD], axis=0, keepdims=True),
                                                 jnp.sum(prod_t[HD:], axis=0, keepdims=True)], axis=1)
                    st = _dot_nt(kext[win, pl_], q2) + b_ref[0, hp]
                    if b == 0:
                        st = jnp.where(no_prev, NEG_INF, st)
                    pt = jnp.exp(st - lse_row)
                    dst = pt * (_dot_nt(vext[win, pl_], do2) - delta_row)
                    dbsum = dst if dbsum is None else dbsum + dst
                    dstb = dst.astype(BF16)
                    dv_parts.append(_dot(pt.astype(BF16), do2))
                    dk_parts.append(_dot(dstb, q2))
                    dq2 = _dot_tn(dstb, kext[win, pl_])
                    dqn[rows, pl_] = jnp.where(lane < HD, dq2[0:QBLK], dq2[QBLK:]) * SCALE
                db_ref[hp] += dbsum
                dvx[:, pl_] = overlap_add(dv_parts)
                dkx[:, pl_] = overlap_add(dk_parts)

        @pl.when(n > 0)
        def _():
            out_ref[:, 0:GW] = dqc[...].astype(BF16)
            out_ref[:, GW:2 * GW] = dkc[...].astype(BF16)
            out_ref[:, 2 * GW:] = dvc[...].astype(BF16)

        @pl.when(jnp.logical_and(n > 0, n < ns))
        def _():
            out_ref[TQ - QBLK:, GW:2 * GW] = (dkc[TQ - QBLK:, :] + dkx[0:QBLK, :]).astype(BF16)
            out_ref[TQ - QBLK:, 2 * GW:] = (dvc[TQ - QBLK:, :] + dvx[0:QBLK, :]).astype(BF16)

        @pl.when(n < ns)
        def _():
            dqc[...] = dqn[...]
            dkc[...] = dkx[QBLK:, :]
            dvc[...] = dvx[QBLK:, :]

    def cur(n):
        return jnp.minimum(n, ns - 1)

    def prev(n):
        return jnp.maximum(cur(n) * QB - 1, 0)

    rows = lambda c, n: (c, cur(n), 0)
    return pl.pallas_call(
        body, name=f"attn_bwd_g{gi}", grid=(dil, ns + 1),
        in_specs=[pl.BlockSpec((1, TQ, GW), lambda c, n: (c, cur(n), 0)),
                  pl.BlockSpec((1, TQ, GW), lambda c, n: (c, cur(n), 1)),
                  pl.BlockSpec((1, QBLK, GW), lambda c, n: (c, prev(n), 1)),
                  pl.BlockSpec((1, TQ, GW), lambda c, n: (c, cur(n), 2)),
                  pl.BlockSpec((1, QBLK, GW), lambda c, n: (c, prev(n), 2)),
                  pl.BlockSpec((1, TQ, GW), rows), pl.BlockSpec((1, TQ, GW), rows), pl.BlockSpec((1, TQ, GW), rows),
                  pl.BlockSpec((1, NP, KBLK, 2 * QBLK), lambda c, n: (gi, 0, 0, 0))],
        out_specs=[pl.BlockSpec((1, TQ, 3 * GW), lambda c, n: (c, jnp.maximum(n - 1, 0), 0)),
                   pl.BlockSpec((NP, KBLK, 2 * QBLK), lambda c, n: (0, 0, 0))],
        out_shape=[SDS((dil, L, 3 * GW), BF16), SDS((NP, KBLK, 2 * QBLK), F32)],
        scratch_shapes=[pltpu.VMEM((QBLK + TQ, GW), BF16), pltpu.VMEM((QBLK + TQ, GW), BF16),
                        pltpu.VMEM((QBLK + TQ, GW), F32), pltpu.VMEM((QBLK + TQ, GW), F32),
                        pltpu.VMEM((TQ, GW), F32), pltpu.VMEM((TQ, GW), F32),
                        pltpu.VMEM((TQ, GW), F32), pltpu.VMEM((TQ, GW), F32)],
        compiler_params=_cp(("arbitrary", "arbitrary")),
    )(zq, zq, zq, zq, zq, do, o, lse, bias_t)


def _dz_block(k):
    if k < 9:
        return k % 3, k // 3
    if k < 13:
        return 3, k - 9
    return 4, k - 13


_DZ_SRC = np.array([_dz_block(k)[0] for k in range(17)], np.int32)


def _dz_hold(s):
    uses = [(k, _dz_block(k)[1]) for k in range(17) if _dz_block(k)[0] == s]
    hold = []
    for k in range(17):
        nxt = [b for kk, b in uses if kk >= k]
        hold.append(nxt[0] if nxt else uses[-1][1])
    return np.array(hold, np.int32)


def _table(tab, k):
    out = jnp.int32(int(tab[0]))
    for idx in range(1, len(tab)):
        out = jnp.where(k == idx, jnp.int32(int(tab[idx])), out)
    return out


def _w_in_tile(s, blk):
    return blk * 3 + s if s < 3 else (9 if s == 3 else 13) + blk


def _in_bwd(dqkv, dglu, dzg, w_inT, x, dx1, g, rider):
    S = x.shape[0]
    TM = 512

    def body(d0, d1, d2, d3, d4, w_ref, x_ref, dx1_ref, g_ref, gx_ref, dg_ref, scr):
        i = pl.program_id(0)

        @pl.when(i == 0)
        def _():
            dg_ref[...] = jnp.zeros_like(dg_ref)

        def rows(s, blk):
            k = _w_in_tile(s, blk)
            return w_ref[k * GW:(k + 1) * GW, :]

        dh = jnp.zeros((TM, D), F32)
        for blk in range(3):
            dh = dh + _dot(d0[0, :, blk * GW:(blk + 1) * GW], rows(0, blk))
        for s, ref in ((3, d3), (4, d4)):
            for blk in range(4):
                dh = dh + _dot(ref[:, blk * GW:(blk + 1) * GW], rows(s, blk))
        for s, ref in ((1, d1), (2, d2)):
            dil = DILATIONS[s]
            part = jnp.zeros((TM, D), F32)
            for blk in range(3):
                part = part + _dot(ref[:, :, blk * GW:(blk + 1) * GW].reshape(TM, GW), rows(s, blk))
            _merge_residues(scr, dil, lambda c, part=part, dil=dil: part[c * (TM // dil):(c + 1) * (TM // dil)])
            dh = dh + _load_cols(scr)
        xf = x_ref[...]
        r = lax.rsqrt(jnp.mean(xf * xf, axis=-1, keepdims=True) + RMS_EPS)
        nrm = xf * r
        dg_ref[...] += _colsum8(dh * nrm)
        dn = dh * g_ref[...]
        gx_ref[...] = dx1_ref[...] + r * (dn - nrm * jnp.mean(dn * nrm, axis=-1, keepdims=True))

    rowd = pl.BlockSpec((TM, D), lambda i: (i, 0))
    wide = pl.BlockSpec((TM, 2 * D), lambda i: (i, 0))
    body, r_in, r_out, r_shape, r_scr = _ride(body, 9, 2, 1, rider, S // TM)
    return pl.pallas_call(
        body, name="in_bwd", grid=(S // TM,),
        in_specs=[_residue_spec(TM, d, 3 * GW) for d in DILATIONS] + [wide, wide]
        + [pl.BlockSpec(w_inT.shape, lambda i: (0, 0), pipeline_mode=pl.Buffered(1)), rowd, rowd,
           pl.BlockSpec((1, D), lambda i: (0, 0))] + r_in,
        out_specs=[rowd, pl.BlockSpec((LANE_ROWS, D), lambda i: (0, 0))] + r_out,
        out_shape=[SDS((S, D), F32), SDS((LANE_ROWS, D), F32)] + r_shape,
        scratch_shapes=[_col_scratch(TM, D)] + r_scr,
        compiler_params=_cp(("arbitrary",)),
    )(*dqkv, dglu, dzg, w_inT, x, dx1, g, *rider.ins)


def _dw_in(dqkv, dglu, dzg, hs):
    S = hs[0].shape[0]
    TS = min(2048, S)
    nk = 17
    holds = [_dz_hold(s) for s in range(5)]
    h_of = (0, 1, 2, 0, 0)

    def body(d0, d1, d2, d3, d4, h0, h1, h2, o_ref, acc):
        m = pl.program_id(0)
        s_ = pl.program_id(1)

        @pl.when(s_ == 0)
        def _():
            acc[...] = jnp.zeros_like(acc)

        src = _table(_DZ_SRC, m)
        pairs = ((d0, h0), (d1, h1), (d2, h2), (d3, h0), (d4, h0))
        for s, (dref, href) in enumerate(pairs):
            @pl.when(src == s)
            def _(dref=dref, href=href):
                acc[...] += _dot_tn(dref[...].reshape(TS, GW), href[...].reshape(TS, D))

        @pl.when(s_ == pl.num_programs(1) - 1)
        def _():
            o_ref[...] = acc[...].astype(BF16)

    def row(s, m, s_):
        return jnp.where(_table(_DZ_SRC, m) == s, s_, 0)

    def dspec(s):
        if s < 3:
            dil = DILATIONS[s]
            return pl.BlockSpec((dil, TS // dil, GW), lambda m, s_: (0, row(s, m, s_), _table(holds[s], m)))
        return pl.BlockSpec((TS, GW), lambda m, s_: (row(s, m, s_), _table(holds[s], m)))

    def hrow(j, m, s_):
        used = _table(np.array([int(h_of[_dz_block(k)[0]] == j) for k in range(nk)], np.int32), m)
        return jnp.where(used == 1, s_, 0)

    hspecs = [pl.BlockSpec((TS, D), lambda m, s_: (hrow(0, m, s_), 0))] + [
        pl.BlockSpec((DILATIONS[j], TS // DILATIONS[j], D), lambda m, s_, j=j: (0, hrow(j, m, s_), 0)) for j in (1, 2)]
    return pl.pallas_call(
        body, name="dw_in", grid=(nk, S // TS),
        in_specs=[dspec(s) for s in range(5)] + hspecs,
        out_specs=pl.BlockSpec((GW, D), lambda m, s_: (m, 0)),
        out_shape=SDS((nk * GW, D), BF16),
        scratch_shapes=[pltpu.VMEM((GW, D), F32)],
        compiler_params=_cp(("arbitrary", "arbitrary")),
    )(*dqkv, dglu, dzg, *hs)


def _mm_tn(a, b, tm, a_maps, name):
    S, N = b.shape
    parts = len(a_maps)
    tp = tm // parts
    nm = len(a_maps[0])
    TS = min(2048, S)
    tabs = [np.array(t, np.int32) for t in a_maps]

    def body(*refs):
        a_refs = refs[:parts]
        b_ref, o_ref, acc = refs[parts:]
        s_ = pl.program_id(1)

        @pl.when(s_ == 0)
        def _():
            acc[...] = jnp.zeros_like(acc)

        for p, ar in enumerate(a_refs):
            acc[p * tp:(p + 1) * tp, :] += _dot_tn(ar[...], b_ref[...])

        @pl.when(s_ == pl.num_programs(1) - 1)
        def _():
            o_ref[...] = acc[...].astype(BF16)

    return pl.pallas_call(
        body, name=name, grid=(nm, S // TS),
        in_specs=[pl.BlockSpec((TS, tp), lambda m, s_, t=t: (s_, _table(t, m))) for t in tabs]
        + [pl.BlockSpec((TS, N), lambda m, s_: (s_, 0))],
        out_specs=pl.BlockSpec((tm, N), lambda m, s_: (m, 0)),
        out_shape=SDS((nm * tm, N), BF16),
        scratch_shapes=[pltpu.VMEM((tm, N), F32)],
        compiler_params=_cp(("arbitrary", "arbitrary")),
    )(*([a] * parts), b)


def _row_tile(rows, cols, limit=1 << 20):
    if rows * cols * 4 <= limit:
        return rows
    best = None
    for t in range(8, rows, 8):
        if rows % t == 0 and t * cols * 4 <= limit:
            best = t
    return best


def _adamw(w, g, m, v, name):
    R, C = w.shape
    tr = _row_tile(R, C)

    def body(w_ref, g_ref, m_ref, v_ref, d_ref, nm_ref, nv_ref):
        gg = g_ref[...]
        nm = ADAM_B1 * m_ref[...] + (1.0 - ADAM_B1) * gg
        nv = ADAM_B2 * v_ref[...] + (1.0 - ADAM_B2) * (gg * gg)
        m_hat = nm / (1.0 - ADAM_B1 ** ADAM_STEP)
        v_hat = nv / (1.0 - ADAM_B2 ** ADAM_STEP)
        d_ref[...] = -ADAM_LR * (m_hat / (jnp.sqrt(v_hat) + ADAM_EPS) + ADAM_WD * w_ref[...])
        nm_ref[...] = nm
        nv_ref[...] = nv

    spec = pl.BlockSpec((tr, C), lambda i: (i, 0))
    return pl.pallas_call(
        body, name=name, grid=(R // tr,), in_specs=[spec] * 4, out_specs=[spec] * 3,
        out_shape=[SDS((R, C), F32)] * 3, compiler_params=_cp(("arbitrary",)),
    )(w, g, m, v)


_FLIPS = ((1, 0), (0, 1), (1, 1))


def _place():
    x, y, c = lax.axis_index("x"), lax.axis_index("y"), lax.axis_index("c")
    return x, y, c


def _peer_chips(x, y):
    return [((x + fx) % 2, (y + fy) % 2) for fx, fy in _FLIPS]


def _gather_weights(shards):
    nw = len(shards)
    views = [s.reshape(2, s.shape[0] // 2, s.shape[1]) for s in shards]

    def body(*refs):
        ins = refs[:nw]
        outs = refs[nw:2 * nw]
        ici_send, ici_recv, d2d_send, d2d_recv, loc = refs[2 * nw:]
        x, y, c = _place()
        j = 2 * x + y
        chips = _peer_chips(x, y)
        copies = []
        for w in range(nw):
            cp = pltpu.make_async_copy(ins[w], outs[w].at[j], loc.at[w])
            cp.start()
            copies.append(cp)
        sends = []
        for w in range(nw):
            for k, (px, py) in enumerate(chips):
                cp = pltpu.make_async_remote_copy(
                    src_ref=ins[w].at[c], dst_ref=outs[w].at[j, c], send_sem=ici_send.at[w, k],
                    recv_sem=ici_recv.at[w, k], device_id=(px, py, c), device_id_type=MESH)
                cp.start()
                sends.append(cp)
        for w in range(nw):
            for k, (px, py) in enumerate(chips):
                jk = 2 * px + py
                land = outs[w].at[jk, c]
                pltpu.make_async_remote_copy(
                    src_ref=ins[w].at[c], dst_ref=land, send_sem=ici_send.at[w, k],
                    recv_sem=ici_recv.at[w, k], device_id=(px, py, c), device_id_type=MESH).wait_recv()
                cp = pltpu.make_async_remote_copy(
                    src_ref=land, dst_ref=land, send_sem=d2d_send.at[w, k],
                    recv_sem=d2d_recv.at[w, k], device_id=(x, y, 1 - c), device_id_type=MESH)
                cp.start()
                sends.append(cp)
        for w in range(nw):
            for k, (px, py) in enumerate(chips):
                jk = 2 * px + py
                land = outs[w].at[jk, 1 - c]
                pltpu.make_async_remote_copy(
                    src_ref=land, dst_ref=land, send_sem=d2d_send.at[w, k],
                    recv_sem=d2d_recv.at[w, k], device_id=(x, y, 1 - c), device_id_type=MESH).wait_recv()
        for cp in sends:
            cp.wait_send()
        for cp in copies:
            cp.wait()

    outs = pl.pallas_call(
        body, name="gather_weights",
        in_specs=[ANY] * nw, out_specs=[ANY] * nw,
        out_shape=[SDS((4,) + v.shape, BF16) for v in views],
        scratch_shapes=[pltpu.SemaphoreType.DMA((nw, 3)), pltpu.SemaphoreType.DMA((nw, 3)),
                        pltpu.SemaphoreType.DMA((nw, 3)), pltpu.SemaphoreType.DMA((nw, 3)),
                        pltpu.SemaphoreType.DMA((nw,))],
    )(*views)
    return [o.reshape(4 * s.shape[0], s.shape[1]) for o, s in zip(outs, shards)]


class _Rider:
    def __init__(self, ins, out_shape, scratch, start, finish, mid=None):
        self.ins, self.out_shape, self.scratch = list(ins), list(out_shape), list(scratch)
        self.start, self.finish, self.mid = start, finish, mid


def _ride(body, n_in, n_out, n_scr, rider, steps):
    if rider is None:
        return body, [], [], [], []
    ri, ro = len(rider.ins), len(rider.out_shape)

    def wrapped(*refs):
        ins, r_ins = refs[:n_in], refs[n_in:n_in + ri]
        o0 = n_in + ri
        outs, r_outs = refs[o0:o0 + n_out], refs[o0 + n_out:o0 + n_out + ro]
        s0 = o0 + n_out + ro
        scr, r_scr = refs[s0:s0 + n_scr], refs[s0 + n_scr:]
        i = pl.program_id(0)

        @pl.when(i == 0)
        def _():
            rider.start(r_ins, r_outs, r_scr)

        if rider.mid is not None:
            @pl.when(i == (3 * steps) // 4)
            def _():
                rider.mid(r_ins, r_outs, r_scr)

        body(*ins, *outs, *scr)

        @pl.when(i == steps - 1)
        def _():
            rider.finish(r_ins, r_outs, r_scr)

    return wrapped, [ANY] * ri, [ANY] * ro, rider.out_shape, rider.scratch


def _gather_rider(shards):
    nw = len(shards)
    views = [s.reshape(2, s.shape[0] // 2, s.shape[1]) for s in shards]

    def parts(ins, outs, sems):
        ici_send, ici_recv, d2d_send, d2d_recv, loc = sems
        x, y, c = _place()
        j = 2 * x + y
        local, ici, land_ici, fwd, land_fwd = [], [], [], [], []
        for w in range(nw):
            local.append(pltpu.make_async_copy(ins[w], outs[w].at[j], loc.at[w]))
            for k, (px, py) in enumerate(_peer_chips(x, y)):
                jk = 2 * px + py
                ici.append(pltpu.make_async_remote_copy(
                    src_ref=ins[w].at[c], dst_ref=outs[w].at[j, c], send_sem=ici_send.at[w, k],
                    recv_sem=ici_recv.at[w, k], device_id=(px, py, c), device_id_type=MESH))
                mine = outs[w].at[jk, c]
                land_ici.append(pltpu.make_async_remote_copy(
                    src_ref=ins[w].at[c], dst_ref=mine, send_sem=ici_send.at[w, k],
                    recv_sem=ici_recv.at[w, k], device_id=(px, py, c), device_id_type=MESH))
                fwd.append(pltpu.make_async_remote_copy(
                    src_ref=mine, dst_ref=mine, send_sem=d2d_send.at[w, k],
                    recv_sem=d2d_recv.at[w, k], device_id=(x, y, 1 - c), device_id_type=MESH))
                theirs = outs[w].at[jk, 1 - c]
                land_fwd.append(pltpu.make_async_remote_copy(
                    src_ref=theirs, dst_ref=theirs, send_sem=d2d_send.at[w, k],
                    recv_sem=d2d_recv.at[w, k], device_id=(x, y, 1 - c), device_id_type=MESH))
        return local, ici, land_ici, fwd, land_fwd

    def start(ins, outs, sems):
        local, ici, _, _, _ = parts(ins, outs, sems)
        for cp in local + ici:
            cp.start()

    def mid(ins, outs, sems):
        _, _, land_ici, fwd, _ = parts(ins, outs, sems)
        for landed, cp in zip(land_ici, fwd):
            landed.wait_recv()
            cp.start()

    def finish(ins, outs, sems):
        local, ici, _, fwd, land_fwd = parts(ins, outs, sems)
        for cp in land_fwd:
            cp.wait_recv()
        for cp in ici + fwd:
            cp.wait_send()
        for cp in local:
            cp.wait()

    sem = pltpu.SemaphoreType.DMA
    return _Rider(views, [SDS((4,) + v.shape, BF16) for v in views],
                  [sem((nw, 3)), sem((nw, 3)), sem((nw, 3)), sem((nw, 3)), sem((nw,))], start, finish, mid)


def _chip_exchange_rider(parts):
    nw = len(parts)

    def copies(ins, outs, sems):
        send, recv = sems
        x, y, c = _place()
        return [pltpu.make_async_remote_copy(
            src_ref=ins[w].at[2 * px + py], dst_ref=outs[w].at[k], send_sem=send.at[w, k],
            recv_sem=recv.at[w, k], device_id=(px, py, c), device_id_type=MESH)
            for w in range(nw) for k, (px, py) in enumerate(_peer_chips(x, y))]

    def start(ins, outs, sems):
        for cp in copies(ins, outs, sems):
            cp.start()

    def finish(ins, outs, sems):
        for cp in copies(ins, outs, sems):
            cp.wait()

    sem = pltpu.SemaphoreType.DMA
    return _Rider(parts, [SDS((3,) + p.shape[1:], BF16) for p in parts], [sem((nw, 3)), sem((nw, 3))], start, finish)


def _pair_exchange(grads, name):
    nw = len(grads)

    def body(*refs):
        ins = refs[:nw]
        outs = refs[nw:2 * nw]
        send, recv = refs[2 * nw:]
        x, y, c = _place()
        cps = []
        for w in range(nw):
            cp = pltpu.make_async_remote_copy(
                src_ref=ins[w].at[:, pl.ds(1 - c, 1)], dst_ref=outs[w], send_sem=send.at[w], recv_sem=recv.at[w],
                device_id=(x, y, 1 - c), device_id_type=MESH)
            cp.start()
            cps.append(cp)
        for cp in cps:
            cp.wait()

    return pl.pallas_call(
        body, name=name, in_specs=[ANY] * nw, out_specs=[ANY] * nw,
        out_shape=[SDS((4, 1) + g.shape[2:], BF16) for g in grads],
        scratch_shapes=[pltpu.SemaphoreType.DMA((nw,)), pltpu.SemaphoreType.DMA((nw,))],
    )(*grads)


def _half_tile(rh):
    best = 16
    for t in range(16, 545, 16):
        if rh % t == 0:
            best = t
    return best


def _pair_sum(c_arr, g, got, name):
    _, _, rh, n = g.shape
    tr = _half_tile(rh)

    def body(c_ref, a_ref, b_ref, o_ref):
        o_ref[...] = (a_ref[...].astype(F32) + b_ref[...].astype(F32)).astype(BF16)

    return pl.pallas_call(
        body, name=name,
        grid_spec=pltpu.PrefetchScalarGridSpec(
            num_scalar_prefetch=1, grid=(4, rh // tr),
            in_specs=[pl.BlockSpec((1, 1, tr, n), lambda s, i, c: (s, c[0], i, 0)),
                      pl.BlockSpec((1, 1, tr, n), lambda s, i, c: (s, 0, i, 0))],
            out_specs=pl.BlockSpec((1, 1, tr, n), lambda s, i, c: (s, 0, i, 0))),
        out_shape=SDS((4, 1, rh, n), BF16),
        compiler_params=_cp(("arbitrary", "arbitrary")),
    )(c_arr, g, got)


def _chip_sum(jc_arr, part, got, name):
    _, _, rh, n = part.shape
    tr = _half_tile(rh)

    def body(jc_ref, a_ref, b_ref, o_ref):
        acc = a_ref[0, 0].astype(F32)
        for k in range(3):
            acc = acc + b_ref[k, 0].astype(F32)
        o_ref[0] = acc

    return pl.pallas_call(
        body, name=name,
        grid_spec=pltpu.PrefetchScalarGridSpec(
            num_scalar_prefetch=1, grid=(rh // tr,),
            in_specs=[pl.BlockSpec((1, 1, tr, n), lambda i, jc: (jc[0], 0, i, 0)),
                      pl.BlockSpec((3, 1, tr, n), lambda i, jc: (0, 0, i, 0))],
            out_specs=pl.BlockSpec((1, tr, n), lambda i, jc: (jc[1], i, 0))),
        out_shape=SDS((2, rh, n), F32),
        compiler_params=_cp(("arbitrary",)),
    )(jc_arr, part, got)


def _half_swap(halves):
    nw = len(halves)

    def body(*refs):
        ins = refs[:nw]
        outs = refs[nw:2 * nw]
        send, recv = refs[2 * nw:]
        x, y, c = _place()
        cps = []
        for w in range(nw):
            cp = pltpu.make_async_remote_copy(
                src_ref=ins[w].at[c], dst_ref=outs[w].at[c], send_sem=send.at[w], recv_sem=recv.at[w],
                device_id=(x, y, 1 - c), device_id_type=MESH)
            cp.start()
            cps.append(cp)
        for cp in cps:
            cp.wait()

    return pl.pallas_call(
        body, name="grad_half_swap", in_specs=[ANY] * nw, out_specs=[ANY] * nw,
        out_shape=[SDS(h.shape, F32) for h in halves],
        input_output_aliases={w: w for w in range(nw)},
        scratch_shapes=[pltpu.SemaphoreType.DMA((nw,)), pltpu.SemaphoreType.DMA((nw,))],
    )(*halves)


def _all_sum_small(part, name):
    R = part.shape[0]

    def body(p_ref, o_ref, land, send, recv):
        x, y, c = _place()
        me = 4 * x + 2 * y + c
        cps = []
        for d in range(1, 8):
            t = (me + d) % 8
            cp = pltpu.make_async_remote_copy(
                src_ref=p_ref, dst_ref=land.at[me], send_sem=send.at[d - 1], recv_sem=recv.at[d - 1],
                device_id=(t // 4, (t // 2) % 2, t % 2), device_id_type=MESH)
            cp.start()
            cps.append(cp)
        land[me] = p_ref[...]
        for cp in cps:
            cp.wait()
        acc = land[0]
        for d in range(1, 8):
            acc = acc + land[d]
        o_ref[...] = acc

    return pl.pallas_call(
        body, name=name,
        in_specs=[pl.BlockSpec(memory_space=pltpu.VMEM)], out_specs=pl.BlockSpec(memory_space=pltpu.VMEM),
        out_shape=SDS((R, D), F32),
        scratch_shapes=[pltpu.VMEM((8, R, D), F32), pltpu.SemaphoreType.DMA((7,)), pltpu.SemaphoreType.DMA((7,))],
        compiler_params=pltpu.CompilerParams(vmem_limit_bytes=VMEM_LIMIT),
    )(part)


def _pad_rows(a, rows):
    return jnp.pad(a, ((0, rows - a.shape[0]), (0, 0)))


def _vec_pack(vs):
    return jnp.concatenate([_pad_rows(v, LANE_ROWS) for v in vs], axis=0)


def kernel(x, rel_bias_table, g_pre_mix, w_in, b_glu, w_dw, b_dw, g_conv_ln, b_conv_ln, w_conv_out, b_conv_out, w_attn_out, w_mix_out, g_post_mix, g_pre_ffn, w_ffn_in, w_ffn_out, g_post_ffn, loss_target, m_rel_bias_table, m_g_pre_mix, m_w_in, m_b_glu, m_w_dw, m_b_dw, m_g_conv_ln, m_b_conv_ln, m_w_conv_out, m_b_conv_out, m_w_attn_out, m_w_mix_out, m_g_post_mix, m_g_pre_ffn, m_w_ffn_in, m_w_ffn_out, m_g_post_ffn, v_rel_bias_table, v_g_pre_mix, v_w_in, v_b_glu, v_w_dw, v_b_dw, v_g_conv_ln, v_b_conv_ln, v_w_conv_out, v_b_conv_out, v_w_attn_out, v_w_mix_out, v_g_post_mix, v_g_pre_ffn, v_w_ffn_in, v_w_ffn_out, v_g_post_ffn):
    S = x.shape[1]
    xs = x.reshape(S, D)
    tgt = loss_target.reshape(S, D)
    cx, cy, cc = _place()
    chip = 2 * cx + cy

    shards = [w_in[0].T.astype(BF16),
              w_ffn_in[0].T.astype(BF16),
              w_attn_out[0].T.astype(BF16),
              w_conv_out[0].astype(BF16),
              w_mix_out[0].astype(BF16),
              w_ffn_out[0].astype(BF16)]
    (w_inT,) = _gather_weights(shards[:1])
    w_inN = w_inT.T

    buckets_np, valid_np = _bucket_tables()
    buckets = jnp.asarray(buckets_np)
    bias = _bias_expand(rel_bias_table, buckets, jnp.asarray(valid_np)).reshape(3, NH, QBLK, KBLK)
    bias2 = bias.reshape(3, NH // 2, 2 * QBLK, KBLK)
    bias_t = bias.reshape(3, NH // 2, 2, QBLK, KBLK).transpose(0, 1, 4, 2, 3).reshape(3, NH // 2, KBLK, 2 * QBLK)
    wdw32 = _pad_rows(w_dw[0], 32)
    wdw_full = _gather_small_cols(wdw32, chip)

    zrest, h, h_r4, h_r16, *gathered = _in_proj_rest(xs, g_pre_mix, w_inN[:, 3 * ATTN_COLS:], _gather_rider(shards[1:]))
    w_fiT, w_aoT, w_co, w_mx, w_fo = (t.reshape(4 * s.shape[0], s.shape[1]) for t, s in zip(gathered, shards[1:]))
    w_fiN, w_aoN = w_fiT.T, w_aoT.T
    w_coT, w_mxT, w_foT = w_co.T, w_mx.T, w_fo.T
    zq = _in_proj_qkv(h, w_inN[:, :3 * ATTN_COLS])
    og, lg = [], []
    for gi in range(3):
        o_g, l_g = _attn_fwd(zq[gi], bias2, gi)
        og.append(o_g)
        lg.append(l_g)
    cv, a = _conv_fwd(zrest, b_glu, wdw_full, b_dw, g_conv_ln, b_conv_ln)
    o, o_r4, o_r16, lse, lse_r4, lse_r16, ya, yc, mg, mm, x1 = _mix_fwd(
        og, lg, a, zrest, xs, w_aoN, w_co, b_conv_out, w_mx, g_post_mix)
    h2, gu, df, dx2, loss8, dg_post_ffn = _ffn_fwd(x1, tgt, g_pre_ffn, g_post_ffn, w_fiN, w_fo)

    c_arr = jnp.reshape(cc, (1,)).astype(jnp.int32)
    jc_arr = jnp.stack([chip, cc]).astype(jnp.int32)
    ident = lambda n: [list(range(n))]

    def pair_sums(partials, names, tag):
        views = [g.reshape(4, 2, g.shape[0] // 8, g.shape[1]) for g in partials]
        got = _pair_exchange(views, f"grad_pair_exchange_{tag}")
        return [_pair_sum(c_arr, v, r, f"pair_sum_{n}") for v, r, n in zip(views, got, names)]

    def chip_sums(pair, got, names):
        return [_chip_sum(jc_arr, p, r, f"chip_sum_{n}") for p, r, n in zip(pair, got, names)]

    dff, act = _ffn_bwd_act(df, gu, w_foT)
    g_fiT = _mm_tn(dff, h2, 512, [[2 * t if t < NFT else 2 * (t - NFT) + 1 for t in range(0, 22, 2)],
                                  [2 * t if t < NFT else 2 * (t - NFT) + 1 for t in range(1, 22, 2)]], "dw_ffn_in")
    g_fo = _mm_tn(act, df, FFN_H // 2, ident(2), "dw_ffn_out")
    names_a = ("w_ffn_in", "w_ffn_out")
    pair_a = pair_sums([g_fiT, g_fo], names_a, "ffn")
    dx1, dg_pre_ffn, *got_a = _ffn_bwd_in(dff, x1, dx2, g_pre_ffn, w_fiT, _chip_exchange_rider(pair_a))
    halves_a = chip_sums(pair_a, got_a, names_a)
    dmm, dya, dyc, do, do_r4, do_r16, da, dzg, dg_post_mix, db_conv_out = _mix_bwd(
        dx1, mm, ya, yc, zrest, g_post_mix, w_mxT, w_aoT, w_coT)
    dglu, db_glu_u, db_glu_g, dw_dw, dg_conv_ln, db_conv_ln, db_dw = _conv_bwd(da, cv, zrest, b_glu, wdw_full, g_conv_ln, b_conv_ln)
    first = lambda t: t.reshape(1, S, GW)
    dqkv, dbias = [], []
    for gi, (do_g, o_g, lse_g) in enumerate(((first(do), first(o), first(lse)), (do_r4, o_r4, lse_r4),
                                            (do_r16, o_r16, lse_r16))):
        d_g, db_g = _attn_bwd(zq[gi], do_g, o_g, lse_g, bias_t, gi)
        dqkv.append(d_g)
        dbias.append(db_g.reshape(NH // 2, KBLK, 2, QBLK).transpose(0, 2, 3, 1).reshape(NH, QBLK, KBLK))
    dtab = _bias_reduce(jnp.concatenate(dbias, axis=0), buckets)

    g_inT = _dw_in(dqkv, dglu, dzg, (h, h_r4, h_r16))
    g_aoT = _mm_tn(dya, o, 512, ident(2), "dw_attn_out")
    g_co = _mm_tn(a, dyc, 512, ident(2), "dw_conv_out")
    g_mx = _mm_tn(mg, dmm, 512, ident(2), "dw_mix_out")
    names_b = ("w_in", "w_attn_out", "w_conv_out", "w_mix_out")
    pair_b = pair_sums([g_inT, g_aoT, g_co, g_mx], names_b, "rest")
    grad_x, dg_pre_mix, *got_b = _in_bwd(dqkv, dglu, dzg, w_inT, xs, dx1, g_pre_mix, _chip_exchange_rider(pair_b))
    halves_b = chip_sums(pair_b, got_b, names_b)

    red = [t.reshape(t.shape[0] * t.shape[1], t.shape[2]) for t in _half_swap(halves_a + halves_b)]
    gw_ffn_in, gw_ffn_out, gw_in, gw_attn_out, gw_conv_out, gw_mix_out = (
        red[0].T, red[1], red[2].T, red[3].T, red[4], red[5])

    small = jnp.concatenate([loss8, dg_pre_mix, db_glu_u, db_glu_g, db_dw, dg_conv_ln, db_conv_ln, db_conv_out,
                             dg_post_mix, dg_pre_ffn, dg_post_ffn, dtab, dw_dw], axis=0)
    tot = _all_sum_small(small, "small_all_sum")
    row = lambda i: tot[LANE_ROWS * i:LANE_ROWS * i + 1]
    loss = tot[0, 0]
    g_g_pre_mix, g_b_glu = row(1), jnp.concatenate([row(2), row(3)], axis=1)
    g_b_dw, g_g_conv_ln, g_b_conv_ln, g_b_conv_out = row(4), row(5), row(6), row(7)
    g_g_post_mix, g_g_pre_ffn, g_g_post_ffn = row(8), row(9), row(10)
    g_tab = tot[88:112, 0:32].T
    g_w_dw = lax.dynamic_slice(tot[112:112 + CONV_W], (0, 256 * chip), (CONV_W, 256))

    vec_names = ["g_pre_mix", "b_dw", "g_conv_ln", "b_conv_ln", "b_conv_out", "g_post_mix", "g_pre_ffn", "g_post_ffn"]
    vec_w = [g_pre_mix, b_dw, g_conv_ln, b_conv_ln, b_conv_out, g_post_mix, g_pre_ffn, g_post_ffn]
    vec_m = [m_g_pre_mix, m_b_dw, m_g_conv_ln, m_b_conv_ln, m_b_conv_out, m_g_post_mix, m_g_pre_ffn, m_g_post_ffn]
    vec_v = [v_g_pre_mix, v_b_dw, v_g_conv_ln, v_b_conv_ln, v_b_conv_out, v_g_post_mix, v_g_pre_ffn, v_g_post_ffn]
    vec_g = [g_g_pre_mix, g_b_dw, g_g_conv_ln, g_b_conv_ln, g_b_conv_out, g_g_post_mix, g_g_pre_ffn, g_g_post_ffn]

    def pack(vs, glu, tab, dw):
        return jnp.concatenate([_vec_pack(vs), _pad_rows(glu.reshape(2, D), LANE_ROWS),
                                _pad_rows(jnp.pad(tab.T, ((0, 0), (0, D - 32))), 24),
                                _pad_rows(jnp.pad(dw, ((0, 0), (0, D - 256))), 32)], axis=0)

    sw = pack(vec_w, b_glu, rel_bias_table, w_dw[0])
    sg = pack(vec_g, g_b_glu, g_tab, g_w_dw)
    sm = pack(vec_m, m_b_glu, m_rel_bias_table, m_w_dw[0])
    sv = pack(vec_v, v_b_glu, v_rel_bias_table, v_w_dw[0])
    s_out = _adamw(sw, sg, sm, sv, "adamw_small")

    def unpack(t):
        vecs = {n: t[LANE_ROWS * i:LANE_ROWS * i + 1] for i, n in enumerate(vec_names)}
        vecs["b_glu"] = t[64:66].reshape(1, 2 * D)
        vecs["rel_bias_table"] = t[72:96, 0:32].T
        vecs["w_dw"] = t[96:96 + CONV_W, 0:256][None]
        return vecs

    small_out = [unpack(t) for t in s_out]
    big = {}
    for n, w, g, m, v in (("w_in", w_in, gw_in, m_w_in, v_w_in),
                          ("w_conv_out", w_conv_out, gw_conv_out, m_w_conv_out, v_w_conv_out),
                          ("w_attn_out", w_attn_out, gw_attn_out, m_w_attn_out, v_w_attn_out),
                          ("w_mix_out", w_mix_out, gw_mix_out, m_w_mix_out, v_w_mix_out),
                          ("w_ffn_in", w_ffn_in, gw_ffn_in, m_w_ffn_in, v_w_ffn_in),
                          ("w_ffn_out", w_ffn_out, gw_ffn_out, m_w_ffn_out, v_w_ffn_out)):
        big[n] = [t[None] for t in _adamw(w[0], g, m[0], v[0], f"adamw_{n}")]

    order = ["rel_bias_table", "g_pre_mix", "w_in", "b_glu", "w_dw", "b_dw", "g_conv_ln", "b_conv_ln", "w_conv_out",
             "b_conv_out", "w_attn_out", "w_mix_out", "g_post_mix", "g_pre_ffn", "w_ffn_in", "w_ffn_out", "g_post_ffn"]
    grads = {"rel_bias_table": g_tab, "g_pre_mix": g_g_pre_mix, "w_in": gw_in[None], "b_glu": g_b_glu,
             "w_dw": g_w_dw[None], "b_dw": g_b_dw, "g_conv_ln": g_g_conv_ln, "b_conv_ln": g_b_conv_ln,
             "w_conv_out": gw_conv_out[None], "b_conv_out": g_b_conv_out, "w_attn_out": gw_attn_out[None],
             "w_mix_out": gw_mix_out[None], "g_post_mix": g_g_post_mix, "g_pre_ffn": g_g_pre_ffn,
             "w_ffn_in": gw_ffn_in[None], "w_ffn_out": gw_ffn_out[None], "g_post_ffn": g_g_post_ffn}
    outs = [loss, grad_x.reshape(1, S, D)] + [grads[n] for n in order]
    for slot in range(3):
        outs += [big[n][slot] if n in big else small_out[slot][n] for n in order]
    return tuple(outs)


def _gather_small_cols(wdw32, chip):
    placed = lax.dynamic_update_slice(jnp.zeros((32, D), F32), wdw32, (0, 256 * chip))
    return _all_sum_small(placed, "conv_taps_gather") * 0.5
```

```python
import functools
import math

import numpy as np
import jax
import jax.numpy as jnp
from jax import lax
from jax.experimental import pallas as pl
from jax.experimental.pallas import tpu as pltpu

F32 = jnp.float32
BF16 = jnp.bfloat16
SDS = jax.ShapeDtypeStruct
MESH = pl.DeviceIdType.MESH
ANY = pl.BlockSpec(memory_space=pl.ANY)

D = 1024
HD = 64
NH = 8
GW = NH * HD
ATTN_COLS = 3 * GW
DILATIONS = (1, 4, 16)
SPAN = 128
QBLK = 128
KBLK = 2 * QBLK
CONV_W = 31
FFN_H = 2816
FFN_T = 256
NFT = FFN_H // FFN_T
RMS_EPS = 1e-6
LN_EPS = 1e-5
NEG_INF = -1e30
SCALE = HD ** -0.5
LANE_ROWS = 8
LANES = 128

ADAM_LR, ADAM_B1, ADAM_B2, ADAM_EPS, ADAM_WD, ADAM_STEP = 0.001, 0.9, 0.999, 1e-08, 0.01, 10

VMEM_LIMIT = 56 * 1024 * 1024


def _cp(sem):
    return pltpu.CompilerParams(dimension_semantics=sem, vmem_limit_bytes=VMEM_LIMIT)


def _dot(a, b):
    return jnp.dot(a, b, preferred_element_type=F32)


def _dot_nt(a, b):
    return lax.dot_general(a, b, (((1,), (1,)), ((), ())), preferred_element_type=F32)


def _dot_tn(a, b):
    return lax.dot_general(a, b, (((0,), (0,)), ((), ())), preferred_element_type=F32)


def _sigmoid(v):
    return 0.5 * jnp.tanh(0.5 * v) + 0.5


def _colsum8(v):
    s = jnp.sum(v, axis=0, keepdims=True)
    row = lax.broadcasted_iota(jnp.int32, (LANE_ROWS, v.shape[1]), 0)
    return jnp.where(row == 0, jnp.broadcast_to(s, (LANE_ROWS, v.shape[1])), 0.0)


def _first_step(*ids):
    ok = ids[0] == 0
    for i in ids[1:]:
        ok = jnp.logical_and(ok, i == 0)
    return ok


def _col_scratch(n, width):
    return pltpu.VMEM((width // LANES, n, LANES), F32)


def _store_cols(scr, v):
    for lb in range(scr.shape[0]):
        scr[lb] = v[:, lb * LANES:(lb + 1) * LANES]


def _load_cols(scr):
    return jnp.concatenate([scr[lb] for lb in range(scr.shape[0])], axis=1)


def _split_residues(scr, dil, put):
    nb, n, _ = scr.shape
    for c in range(dil):
        put(c, jnp.concatenate([scr[lb, pl.ds(c, n // dil, stride=dil), :] for lb in range(nb)], axis=1))


def _merge_residues(scr, dil, get):
    nb, n, _ = scr.shape
    for c in range(dil):
        v = get(c)
        for lb in range(nb):
            scr[lb, pl.ds(c, n // dil, stride=dil), :] = v[:, lb * LANES:(lb + 1) * LANES]


def _residue_shape(S, dil, width):
    return (dil, S // dil, width)


def _residue_spec(TM, dil, width):
    return pl.BlockSpec((dil, TM // dil, width), lambda i: (0, i, 0))


def _in_proj_rest(x, g, w, rider):
    S = x.shape[0]
    N = w.shape[1]
    TM, TN = 512, 512

    def body(x_ref, g_ref, w_ref, zr_ref, h0_ref, h1_ref, h2_ref, hf_scr):
        xf = x_ref[...]
        r = lax.rsqrt(jnp.mean(xf * xf, axis=-1, keepdims=True) + RMS_EPS)
        hf = xf * r * g_ref[...]
        h0_ref[...] = hf.astype(BF16)
        _store_cols(hf_scr, hf)
        for dil, ref in ((DILATIONS[1], h1_ref), (DILATIONS[2], h2_ref)):
            def put(c, v, ref=ref):
                ref[c] = v.astype(BF16)
            _split_residues(hf_scr, dil, put)
        for j in range(N // TN):
            zr_ref[:, j * TN:(j + 1) * TN] = _dot(h0_ref[...], w_ref[:, j * TN:(j + 1) * TN]).astype(BF16)

    body, r_in, r_out, r_shape, r_scr = _ride(body, 3, 4, 1, rider, S // TM)
    return pl.pallas_call(
        body, name="in_proj_rest", grid=(S // TM,),
        in_specs=[pl.BlockSpec((TM, D), lambda i: (i, 0)),
                  pl.BlockSpec((1, D), lambda i: (0, 0)),
                  pl.BlockSpec((D, N), lambda i: (0, 0), pipeline_mode=pl.Buffered(1))] + r_in,
        out_specs=[pl.BlockSpec((TM, N), lambda i: (i, 0)), pl.BlockSpec((TM, D), lambda i: (i, 0)),
                   _residue_spec(TM, DILATIONS[1], D), _residue_spec(TM, DILATIONS[2], D)] + r_out,
        out_shape=[SDS((S, N), BF16), SDS((S, D), BF16),
                   SDS(_residue_shape(S, DILATIONS[1], D), BF16),
                   SDS(_residue_shape(S, DILATIONS[2], D), BF16)] + r_shape,
        scratch_shapes=[_col_scratch(TM, D)] + r_scr,
        compiler_params=_cp(("arbitrary",)),
    )(x, g, w, *rider.ins)


def _in_proj_qkv(h, w):
    S = h.shape[0]
    TM = 512

    def body(h_ref, w_ref, z0_ref, z1_ref, z2_ref, scr):
        outs = (z0_ref, z1_ref, z2_ref)
        for j in range(9):
            t, gi = j // 3, j % 3
            cols = slice(t * GW, (t + 1) * GW)
            zt = _dot(h_ref[...], w_ref[:, j * GW:(j + 1) * GW])
            if gi == 0:
                z0_ref[0, :, cols] = zt.astype(BF16)
            else:
                slot = scr.at[2 * t + gi - 1]
                _store_cols(slot, zt)

                def put(c, v, ref=outs[gi], cols=cols):
                    ref[c, :, cols] = v.astype(BF16)
                _split_residues(slot, DILATIONS[gi], put)

    return pl.pallas_call(
        body, name="in_proj_qkv", grid=(S // TM,),
        in_specs=[pl.BlockSpec((TM, D), lambda i: (i, 0)),
                  pl.BlockSpec(w.shape, lambda i: (0, 0), pipeline_mode=pl.Buffered(1))],
        out_specs=[_residue_spec(TM, d, 3 * GW) for d in DILATIONS],
        out_shape=[SDS(_residue_shape(S, d, 3 * GW), BF16) for d in DILATIONS],
        scratch_shapes=[pltpu.VMEM((6, GW // LANES, TM, LANES), F32)],
        compiler_params=_cp(("arbitrary",)),
    )(h, w)


def _bucket_tables():
    a = np.arange(QBLK, dtype=np.int32)[:, None]
    c = np.arange(KBLK, dtype=np.int32)[None, :]
    off = a - c + QBLK
    valid = ((off >= 0) & (off <= SPAN)).astype(np.float32)
    tabs = []
    for dil in DILATIONS:
        dist = np.maximum(off * dil, 0)
        df = np.maximum(dist, 1).astype(np.float32)
        large = 16 + (np.log(df / np.float32(16)) / np.float32(math.log(2048 / 16)) * np.float32(16)).astype(np.int32)
        large = np.minimum(large, 31)
        tabs.append(np.where(dist < 16, dist, large).astype(np.int32))
    return np.stack(tabs), valid


def _bias_expand(tab, buckets, valid):
    def body(tab_ref, b_ref, v_ref, o_ref):
        for gi in range(3):
            bk = b_ref[gi]
            for h in range(NH):
                acc = jnp.zeros((QBLK, KBLK), F32)
                for b in range(32):
                    acc = jnp.where(bk == b, tab_ref[b, gi * NH + h], acc)
                o_ref[gi * NH + h] = jnp.where(v_ref[...] > 0.5, acc, NEG_INF)

    return pl.pallas_call(
        body, name="bias_expand",
        in_specs=[pl.BlockSpec(memory_space=pltpu.SMEM),
                  pl.BlockSpec(memory_space=pltpu.VMEM), pl.BlockSpec(memory_space=pltpu.VMEM)],
        out_specs=pl.BlockSpec(memory_space=pltpu.VMEM),
        out_shape=SDS((3 * NH, QBLK, KBLK), F32),
    )(tab, buckets, valid)


def _bias_reduce(dbias, buckets):
    def body(d_ref, b_ref, o_ref):
        lane = lax.broadcasted_iota(jnp.int32, (1, D), 1)
        for gi in range(3):
            bk = b_ref[gi]
            for h in range(NH):
                dv = d_ref[gi * NH + h]
                row = jnp.zeros((1, D), F32)
                for b in range(32):
                    m = jnp.where(bk == b, dv, 0.0)
                    val = jnp.sum(jnp.sum(m, axis=0, keepdims=True), axis=1, keepdims=True)
                    row = jnp.where(lane == b, val, row)
                o_ref[gi * NH + h:gi * NH + h + 1, :] = row

    return pl.pallas_call(
        body, name="bias_reduce",
        in_specs=[pl.BlockSpec(memory_space=pltpu.VMEM), pl.BlockSpec(memory_space=pltpu.VMEM)],
        out_specs=pl.BlockSpec(memory_space=pltpu.VMEM),
        out_shape=SDS((3 * NH, D), F32),
    )(dbias, buckets)


def _attn_tile(S, dil):
    L = S // dil
    tq = min(512, L)
    return L, tq, tq // QBLK, L // tq


def _pair_stack(ref, rows, lanes, scale=None):
    blk = ref[rows, lanes]
    if scale is not None:
        blk = blk * scale
    lane = lax.broadcasted_iota(jnp.int32, blk.shape, 1)
    zero = jnp.zeros_like(blk)
    return jnp.concatenate([jnp.where(lane < HD, blk, zero), jnp.where(lane >= HD, blk, zero)], axis=0)


def _attn_fwd(zq, bias2, gi):
    dil, L, _ = zq.shape
    _, TQ, QB, ns = _attn_tile(L * dil, dil)
    NP = NH // 2

    def body(q_ref, kc_ref, kp_ref, vc_ref, vp_ref, b_ref, o_ref, l_ref, kext, vext):
        n = pl.program_id(1)
        kext[0:QBLK, :] = kp_ref[0]
        kext[QBLK:, :] = kc_ref[0]
        vext[0:QBLK, :] = vp_ref[0]
        vext[QBLK:, :] = vc_ref[0]
        col = lax.broadcasted_iota(jnp.int32, (2 * QBLK, KBLK), 1)
        no_prev = jnp.logical_and(n == 0, col < QBLK)
        lane = lax.broadcasted_iota(jnp.int32, (QBLK, LANES), 1)
        lanes_of = [slice(hp * LANES, (hp + 1) * LANES) for hp in range(NP)]
        for b in range(QB):
            rows = slice(b * QBLK, (b + 1) * QBLK)
            win = slice(b * QBLK, b * QBLK + KBLK)
            s = [_dot_nt(_pair_stack(q_ref.at[0], rows, pl_, SCALE), kext[win, pl_]) + b_ref[0, hp]
                 for hp, pl_ in enumerate(lanes_of)]
            if b == 0:
                s = [jnp.where(no_prev, NEG_INF, v) for v in s]
            m = [jnp.max(v, axis=-1, keepdims=True) for v in s]
            p = [jnp.exp(v - mv) for v, mv in zip(s, m)]
            l = [jnp.sum(v, axis=-1, keepdims=True) for v in p]
            o2 = [_dot(v.astype(BF16), vext[win, pl_]) / lv for v, lv, pl_ in zip(p, l, lanes_of)]
            for hp, pl_ in enumerate(lanes_of):
                lse2 = jnp.broadcast_to(m[hp] + jnp.log(l[hp]), (2 * QBLK, LANES))
                o_ref[0, rows, pl_] = jnp.where(lane < HD, o2[hp][0:QBLK], o2[hp][QBLK:]).astype(BF16)
                l_ref[0, rows, pl_] = jnp.where(lane < HD, lse2[0:QBLK], lse2[QBLK:])

    def prev(n):
        return jnp.maximum(n * QB - 1, 0)

    return pl.pallas_call(
        body, name=f"attn_fwd_g{gi}", grid=(dil, ns),
        in_specs=[pl.BlockSpec((1, TQ, GW), lambda c, n: (c, n, 0)),
                  pl.BlockSpec((1, TQ, GW), lambda c, n: (c, n, 1)),
                  pl.BlockSpec((1, QBLK, GW), lambda c, n: (c, prev(n), 1)),
                  pl.BlockSpec((1, TQ, GW), lambda c, n: (c, n, 2)),
                  pl.BlockSpec((1, QBLK, GW), lambda c, n: (c, prev(n), 2)),
                  pl.BlockSpec((1, NP, 2 * QBLK, KBLK), lambda c, n: (gi, 0, 0, 0))],
        out_specs=[pl.BlockSpec((1, TQ, GW), lambda c, n: (c, n, 0)),
                   pl.BlockSpec((1, TQ, GW), lambda c, n: (c, n, 0))],
        out_shape=[SDS((dil, L, GW), BF16), SDS((dil, L, GW), F32)],
        scratch_shapes=[pltpu.VMEM((QBLK + TQ, GW), BF16), pltpu.VMEM((QBLK + TQ, GW), BF16)],
        compiler_params=_cp(("arbitrary", "arbitrary")),
    )(zq, zq, zq, zq, zq, bias2)


CONV_TM = 256
SHIFT_PAD = 24


def _make_shifts(src, sh, n):
    for b in range(1, 8):
        sh[b - 1] = src[b:b + n + SHIFT_PAD, :]


def _shifted(src, sh, off, r0, n, lanes):
    a, b = divmod(off, 8)
    if b == 0:
        return src[8 * a + r0:8 * a + r0 + n, lanes]
    return sh[b - 1, 8 * a + r0:8 * a + r0 + n, lanes]


CONV_RC = 64


def _tap_blocks(TM):
    return [(r0, slice(l0, l0 + LANES)) for l0 in range(0, D, LANES) for r0 in range(0, TM, CONV_RC)]


def _conv_fwd(zrest, b_glu, wdw, b_dw, g_ln, b_ln):
    S = zrest.shape[0]
    TM = CONV_TM
    HALO = 32
    hb = TM // HALO

    def body(u_ref, g_ref, uh_ref, gh_ref, bg_ref, w_ref, bd_ref, gl_ref, bl_ref, cv_ref, a_ref, ext, sh):
        i = pl.program_id(0)
        bu = bg_ref[:, 0:D]
        bgt = bg_ref[:, D:2 * D]
        uh = (uh_ref[...].astype(F32) + bu) * _sigmoid(gh_ref[...].astype(F32) + bgt)
        ext[0:HALO, :] = jnp.where(i == 0, 0.0, uh)
        ext[HALO:, :] = (u_ref[...].astype(F32) + bu) * _sigmoid(g_ref[...].astype(F32) + bgt)
        _make_shifts(ext, sh, TM)
        acc = jnp.zeros((TM, D), F32)
        for j in range(CONV_W):
            acc = acc + _shifted(ext, sh, HALO - (CONV_W - 1) + j, 0, TM, slice(None)) * w_ref[j:j + 1, :]
        cv = (acc + bd_ref[...]).astype(BF16)
        cv_ref[...] = cv
        cf = cv.astype(F32)
        mu = jnp.mean(cf, axis=-1, keepdims=True)
        xc = cf - mu
        y = xc * lax.rsqrt(jnp.mean(xc * xc, axis=-1, keepdims=True) + LN_EPS) * gl_ref[...] + bl_ref[...]
        a_ref[...] = (y * _sigmoid(y)).astype(BF16)

    vec = pl.BlockSpec((1, D), lambda i: (0, 0))
    return pl.pallas_call(
        body, name="conv_fwd", grid=(S // TM,),
        in_specs=[pl.BlockSpec((TM, D), lambda i: (i, 0)), pl.BlockSpec((TM, D), lambda i: (i, 1)),
                  pl.BlockSpec((HALO, D), lambda i: (jnp.maximum(i * hb - 1, 0), 0)),
                  pl.BlockSpec((HALO, D), lambda i: (jnp.maximum(i * hb - 1, 0), 1)),
                  pl.BlockSpec((1, 2 * D), lambda i: (0, 0)),
                  pl.BlockSpec((32, D), lambda i: (0, 0)), vec, vec, vec],
        out_specs=[pl.BlockSpec((TM, D), lambda i: (i, 0)), pl.BlockSpec((TM, D), lambda i: (i, 0))],
        out_shape=[SDS((S, D), BF16), SDS((S, D), BF16)],
        scratch_shapes=[pltpu.VMEM((HALO + TM, D), F32), pltpu.VMEM((7, TM + SHIFT_PAD, D), F32)],
        compiler_params=_cp(("arbitrary",)),
    )(zrest, zrest, zrest, zrest, b_glu, wdw, b_dw, g_ln, b_ln)


def _mix_fwd(og, lg, a, zrest, x, w_ao, w_co, b_co, w_mx, g_pm):
    S = x.shape[0]
    TM = 512

    def body(o0, o1, o2, l0, l1, l2, a_ref, ga_ref, gc_ref, x_ref, wa_ref, wc_ref, bc_ref, wm_ref, g_ref,
             o_ref, oa_ref, ob_ref, lse_ref, lsea_ref, lseb_ref, ya_ref, yc_ref, mg_ref, mm_ref, x1_ref,
             so1, so2, sl1, sl2, so, sl):
        for dil, src, dst, cast in ((DILATIONS[1], o1, so1, True), (DILATIONS[2], o2, so2, True),
                                    (DILATIONS[1], l1, sl1, False), (DILATIONS[2], l2, sl2, False)):
            _merge_residues(dst, dil, (lambda c, src=src: src[c].astype(F32)) if cast else (lambda c, src=src: src[c]))
        la, lb, lc = l0[0], _load_cols(sl1), _load_cols(sl2)
        m = jnp.maximum(jnp.maximum(la, lb), lc)
        e0 = jnp.exp(la - m)
        e1 = jnp.exp(lb - m)
        e2 = jnp.exp(lc - m)
        den = e0 + e1 + e2
        of = (e0 * o0[0].astype(F32) + e1 * _load_cols(so1) + e2 * _load_cols(so2)) / den
        o = of.astype(BF16)
        o_ref[...] = o
        lse = m + jnp.log(den)
        lse_ref[...] = lse
        _store_cols(so, of)
        _store_cols(sl, lse)
        for dil, oref, lref in ((DILATIONS[1], oa_ref, lsea_ref), (DILATIONS[2], ob_ref, lseb_ref)):
            def put_o(c, v, oref=oref):
                oref[c] = v.astype(BF16)

            def put_l(c, v, lref=lref):
                lref[c] = v
            _split_residues(so, dil, put_o)
            _split_residues(sl, dil, put_l)
        ya = _dot(o, wa_ref[...]).astype(BF16)
        yc = (_dot(a_ref[...], wc_ref[...]) + bc_ref[...]).astype(BF16)
        ya_ref[...] = ya
        yc_ref[...] = yc
        mg = (_sigmoid(ga_ref[...].astype(F32)) * ya.astype(F32)
              + _sigmoid(gc_ref[...].astype(F32)) * yc.astype(F32)).astype(BF16)
        mg_ref[...] = mg
        mm = _dot(mg, wm_ref[...]).astype(BF16)
        mm_ref[...] = mm
        mf = mm.astype(F32)
        r = lax.rsqrt(jnp.mean(mf * mf, axis=-1, keepdims=True) + RMS_EPS)
        x1_ref[...] = x_ref[...] + mf * r * g_ref[...]

    row512 = pl.BlockSpec((TM, GW), lambda i: (i, 0))
    rowd = pl.BlockSpec((TM, D), lambda i: (i, 0))
    vec = pl.BlockSpec((1, D), lambda i: (0, 0))
    full = lambda r, c: pl.BlockSpec((r, c), lambda i: (0, 0))
    res = [_residue_spec(TM, d, GW) for d in DILATIONS]
    rshape = lambda d, t: SDS(_residue_shape(S, d, GW), t)
    scr = _col_scratch(TM, GW)
    return pl.pallas_call(
        body, name="mix_fwd", grid=(S // TM,),
        in_specs=res + res + [rowd, pl.BlockSpec((TM, D), lambda i: (i, 2)), pl.BlockSpec((TM, D), lambda i: (i, 3)),
                              rowd, full(GW, D), full(D, D), vec, full(D, D), vec],
        out_specs=[row512, res[1], res[2], row512, res[1], res[2], rowd, rowd, rowd, rowd, rowd],
        out_shape=[SDS((S, GW), BF16), rshape(DILATIONS[1], BF16), rshape(DILATIONS[2], BF16),
                   SDS((S, GW), F32), rshape(DILATIONS[1], F32), rshape(DILATIONS[2], F32),
                   SDS((S, D), BF16), SDS((S, D), BF16), SDS((S, D), BF16), SDS((S, D), BF16), SDS((S, D), F32)],
        scratch_shapes=[scr] * 6,
        compiler_params=_cp(("arbitrary",)),
    )(og[0], og[1], og[2], lg[0], lg[1], lg[2], a, zrest, zrest, x, w_ao, w_co, b_co, w_mx, g_pm)


def _ffn_fwd(x1, tgt, g_pre, g_post, w_fi, w_fo):
    S = x1.shape[0]
    TM = 512

    def body(x1_ref, t_ref, gp_ref, go_ref, wi_ref, wo_ref,
             h2_ref, gu_ref, df_ref, dx2_ref, loss_ref, dgo_ref):
        i = pl.program_id(0)

        @pl.when(i == 0)
        def _():
            loss_ref[...] = jnp.zeros_like(loss_ref)
            dgo_ref[...] = jnp.zeros_like(dgo_ref)

        xf = x1_ref[...]
        r = lax.rsqrt(jnp.mean(xf * xf, axis=-1, keepdims=True) + RMS_EPS)
        h2_ref[...] = (xf * r * gp_ref[...]).astype(BF16)
        for k in range(NFT):
            gu_ref[:, 2 * k * FFN_T:(2 * k + 1) * FFN_T] = _dot(
                h2_ref[...], wi_ref[:, k * FFN_T:(k + 1) * FFN_T]).astype(BF16)
            gu_ref[:, (2 * k + 1) * FFN_T:(2 * k + 2) * FFN_T] = _dot(
                h2_ref[...], wi_ref[:, FFN_H + k * FFN_T:FFN_H + (k + 1) * FFN_T]).astype(BF16)
        f = jnp.zeros((TM, D), F32)
        for k in range(NFT):
            gf = gu_ref[:, 2 * k * FFN_T:(2 * k + 1) * FFN_T].astype(F32)
            uf = gu_ref[:, (2 * k + 1) * FFN_T:(2 * k + 2) * FFN_T].astype(F32)
            act = (gf * _sigmoid(gf) * uf).astype(BF16)
            f = f + _dot(act, wo_ref[k * FFN_T:(k + 1) * FFN_T, :])
        r = lax.rsqrt(jnp.mean(f * f, axis=-1, keepdims=True) + RMS_EPS)
        nrm = f * r
        e = x1_ref[...] + nrm * go_ref[...] - t_ref[...]
        tot = jnp.sum(jnp.sum(e * e, axis=-1, keepdims=True), axis=0, keepdims=True) * (0.5 / D)
        corner = jnp.logical_and(lax.broadcasted_iota(jnp.int32, (LANE_ROWS, D), 0) == 0,
                                 lax.broadcasted_iota(jnp.int32, (LANE_ROWS, D), 1) == 0)
        loss_ref[...] += jnp.where(corner, tot, 0.0)
        dx2 = e * (1.0 / D)
        dx2_ref[...] = dx2
        dgo_ref[...] += _colsum8(dx2 * nrm)
        dn = dx2 * go_ref[...]
        df_ref[...] = (r * (dn - nrm * jnp.mean(dn * nrm, axis=-1, keepdims=True))).astype(BF16)

    rowd = pl.BlockSpec((TM, D), lambda i: (i, 0))
    vec = pl.BlockSpec((1, D), lambda i: (0, 0))
    acc8 = pl.BlockSpec((LANE_ROWS, D), lambda i: (0, 0))
    return pl.pallas_call(
        body, name="ffn_fwd", grid=(S // TM,),
        in_specs=[rowd, rowd, vec, vec,
                  pl.BlockSpec((D, 2 * FFN_H), lambda i: (0, 0), pipeline_mode=pl.Buffered(1)),
                  pl.BlockSpec((FFN_H, D), lambda i: (0, 0), pipeline_mode=pl.Buffered(1))],
        out_specs=[rowd, pl.BlockSpec((TM, 2 * FFN_H), lambda i: (i, 0)), rowd, rowd, acc8, acc8],
        out_shape=[SDS((S, D), BF16), SDS((S, 2 * FFN_H), BF16), SDS((S, D), BF16), SDS((S, D), F32),
                   SDS((LANE_ROWS, D), F32), SDS((LANE_ROWS, D), F32)],
        compiler_params=_cp(("arbitrary",)),
    )(x1, tgt, g_pre, g_post, w_fi, w_fo)


def _ffn_bwd_act(df, gu, w_foT):
    S = df.shape[0]
    TM = 512

    def body_act(df_ref, gu_ref, wo_ref, dff_ref, act_ref):
        dacts = [_dot(df_ref[...], wo_ref[:, k * FFN_T:(k + 1) * FFN_T]) for k in range(NFT)]
        for k in range(NFT):
            dact = dacts[k]
            g = gu_ref[:, 2 * k * FFN_T:(2 * k + 1) * FFN_T].astype(F32)
            u = gu_ref[:, (2 * k + 1) * FFN_T:(2 * k + 2) * FFN_T].astype(F32)
            sg = _sigmoid(g)
            sl = g * sg
            act_ref[:, k * FFN_T:(k + 1) * FFN_T] = (sl * u).astype(BF16)
            dff_ref[:, 2 * k * FFN_T:(2 * k + 1) * FFN_T] = (dact * u * (sg * (1.0 + g * (1.0 - sg)))).astype(BF16)
            dff_ref[:, (2 * k + 1) * FFN_T:(2 * k + 2) * FFN_T] = (dact * sl).astype(BF16)

    rowd = pl.BlockSpec((TM, D), lambda i: (i, 0))
    wide = pl.BlockSpec((TM, 2 * FFN_H), lambda i: (i, 0))
    return pl.pallas_call(
        body_act, name="ffn_bwd_act", grid=(S // TM,),
        in_specs=[rowd, wide, pl.BlockSpec((D, FFN_H), lambda i: (0, 0), pipeline_mode=pl.Buffered(1))],
        out_specs=[wide, pl.BlockSpec((TM, FFN_H), lambda i: (i, 0))],
        out_shape=[SDS((S, 2 * FFN_H), BF16), SDS((S, FFN_H), BF16)],
        compiler_params=_cp(("arbitrary",)),
    )(df, gu, w_foT)


def _ffn_bwd_in(dff, x1, dx2, g_pre, w_fiT, rider):
    S = x1.shape[0]
    TM = 512
    rowd = pl.BlockSpec((TM, D), lambda i: (i, 0))
    wide = pl.BlockSpec((TM, 2 * FFN_H), lambda i: (i, 0))
    KC = 512
    nkc = 2 * FFN_H // KC

    def body_in(dff_ref, x1_ref, dx2_ref, gp_ref, wi_ref, dx1_ref, dgp_ref):
        i = pl.program_id(0)

        @pl.when(i == 0)
        def _():
            dgp_ref[...] = jnp.zeros_like(dgp_ref)

        dh = jnp.zeros((TM, D), F32)
        for k in range(nkc):
            dh = dh + _dot(dff_ref[:, k * KC:k * KC + FFN_T], wi_ref[k * FFN_T:(k + 1) * FFN_T, :]) \
                + _dot(dff_ref[:, k * KC + FFN_T:(k + 1) * KC], wi_ref[FFN_H + k * FFN_T:FFN_H + (k + 1) * FFN_T, :])
        xf = x1_ref[...]
        r = lax.rsqrt(jnp.mean(xf * xf, axis=-1, keepdims=True) + RMS_EPS)
        nrm = xf * r
        dgp_ref[...] += _colsum8(dh * nrm)
        dn = dh * gp_ref[...]
        dx1_ref[...] = dx2_ref[...] + r * (dn - nrm * jnp.mean(dn * nrm, axis=-1, keepdims=True))

    body_in, r_in, r_out, r_shape, r_scr = _ride(body_in, 5, 2, 0, rider, S // TM)
    return pl.pallas_call(
        body_in, name="ffn_bwd_in", grid=(S // TM,),
        in_specs=[wide, rowd, rowd, pl.BlockSpec((1, D), lambda i: (0, 0)),
                  pl.BlockSpec((2 * FFN_H, D), lambda i: (0, 0), pipeline_mode=pl.Buffered(1))] + r_in,
        out_specs=[rowd, pl.BlockSpec((LANE_ROWS, D), lambda i: (0, 0))] + r_out,
        out_shape=[SDS((S, D), F32), SDS((LANE_ROWS, D), F32)] + r_shape,
        scratch_shapes=r_scr,
        compiler_params=_cp(("arbitrary",)),
    )(dff, x1, dx2, g_pre, w_fiT, *rider.ins)


def _mix_bwd(dx1, mm, ya, yc, zrest, g_pm, w_mxT, w_aoT, w_coT):
    S = dx1.shape[0]
    TM = 512

    def body(dx_ref, mm_ref, ya_ref, yc_ref, ga_ref, gc_ref, g_ref, wm_ref, wa_ref, wc_ref,
             dmm_ref, dya_ref, dyc_ref, do_ref, doa_ref, dob_ref, da_ref, dzg_ref, dgpm_ref, dbco_ref, sdo):
        i = pl.program_id(0)

        @pl.when(i == 0)
        def _():
            dgpm_ref[...] = jnp.zeros_like(dgpm_ref)
            dbco_ref[...] = jnp.zeros_like(dbco_ref)

        mf = mm_ref[...].astype(F32)
        r = lax.rsqrt(jnp.mean(mf * mf, axis=-1, keepdims=True) + RMS_EPS)
        nrm = mf * r
        dx = dx_ref[...]
        dgpm_ref[...] += _colsum8(dx * nrm)
        dn = dx * g_ref[...]
        dmm = (r * (dn - nrm * jnp.mean(dn * nrm, axis=-1, keepdims=True))).astype(BF16)
        dmm_ref[...] = dmm
        dmg = _dot(dmm, wm_ref[...])
        sa = _sigmoid(ga_ref[...].astype(F32))
        sc = _sigmoid(gc_ref[...].astype(F32))
        dya = (dmg * sa).astype(BF16)
        dyc = (dmg * sc).astype(BF16)
        dya_ref[...] = dya
        dyc_ref[...] = dyc
        dbco_ref[...] += _colsum8(dyc.astype(F32))
        dzg_ref[:, 0:D] = (dmg * ya_ref[...].astype(F32) * (sa * (1.0 - sa))).astype(BF16)
        dzg_ref[:, D:] = (dmg * yc_ref[...].astype(F32) * (sc * (1.0 - sc))).astype(BF16)
        dof = _dot(dya, wa_ref[...])
        do_ref[...] = dof.astype(BF16)
        _store_cols(sdo, dof)
        for dil, ref in ((DILATIONS[1], doa_ref), (DILATIONS[2], dob_ref)):
            def put(c, v, ref=ref):
                ref[c] = v.astype(BF16)
            _split_residues(sdo, dil, put)
        da_ref[...] = _dot(dyc, wc_ref[...]).astype(BF16)

    rowd = pl.BlockSpec((TM, D), lambda i: (i, 0))
    full = lambda r, c: pl.BlockSpec((r, c), lambda i: (0, 0))
    acc8 = pl.BlockSpec((LANE_ROWS, D), lambda i: (0, 0))
    return pl.pallas_call(
        body, name="mix_bwd", grid=(S // TM,),
        in_specs=[rowd, rowd, rowd, rowd, pl.BlockSpec((TM, D), lambda i: (i, 2)),
                  pl.BlockSpec((TM, D), lambda i: (i, 3)), full(1, D), full(D, D), full(D, GW), full(D, D)],
        out_specs=[rowd, rowd, rowd, pl.BlockSpec((TM, GW), lambda i: (i, 0)),
                   _residue_spec(TM, DILATIONS[1], GW), _residue_spec(TM, DILATIONS[2], GW), rowd,
                   pl.BlockSpec((TM, 2 * D), lambda i: (i, 0)), acc8, acc8],
        out_shape=[SDS((S, D), BF16), SDS((S, D), BF16), SDS((S, D), BF16), SDS((S, GW), BF16),
                   SDS(_residue_shape(S, DILATIONS[1], GW), BF16), SDS(_residue_shape(S, DILATIONS[2], GW), BF16),
                   SDS((S, D), BF16), SDS((S, 2 * D), BF16), SDS((LANE_ROWS, D), F32), SDS((LANE_ROWS, D), F32)],
        scratch_shapes=[_col_scratch(TM, GW)],
        compiler_params=_cp(("arbitrary",)),
    )(dx1, mm, ya, yc, zrest, zrest, g_pm, w_mxT, w_aoT, w_coT)


def _conv_bwd(da, cv, zrest, b_glu, wdw, g_ln, b_ln):
    S = da.shape[0]
    TM = CONV_TM
    HALO = 32
    hb = TM // HALO
    nh = S // HALO

    def body(da_ref, dan_ref, cv_ref, cvn_ref, u_ref, g_ref, uh_ref, gh_ref, bg_ref, w_ref, gl_ref, bl_ref,
             dglu_ref, dbu_ref, dbg_ref, dw_ref, dgl_ref, dbl_ref, dbd_ref, dext, uext, dsh, ush, du_scr, dw8):
        i = pl.program_id(0)
        last = i == pl.num_programs(0) - 1

        @pl.when(i == 0)
        def _():
            for ref in (dbu_ref, dbg_ref, dw8, dgl_ref, dbl_ref, dbd_ref):
                ref[...] = jnp.zeros_like(ref)

        def ln_bwd(da_v, cv_v):
            cf = cv_v.astype(F32)
            mu = jnp.mean(cf, axis=-1, keepdims=True)
            xc = cf - mu
            rstd = lax.rsqrt(jnp.mean(xc * xc, axis=-1, keepdims=True) + LN_EPS)
            xh = xc * rstd
            y = xh * gl_ref[...] + bl_ref[...]
            sy = _sigmoid(y)
            dy = da_v.astype(F32) * (sy * (1.0 + y * (1.0 - sy)))
            dxh = dy * gl_ref[...]
            dcv = rstd * (dxh - jnp.mean(dxh, axis=-1, keepdims=True)
                          - xh * jnp.mean(dxh * xh, axis=-1, keepdims=True))
            return dcv, dy, xh

        dcv, dy, xh = ln_bwd(da_ref[...], cv_ref[...])
        dgl_ref[...] += _colsum8(dy * xh)
        dbl_ref[...] += _colsum8(dy)
        dbd_ref[...] += _colsum8(dcv)
        dcvn, _, _ = ln_bwd(dan_ref[...], cvn_ref[...])
        dext[0:TM, :] = dcv
        dext[TM:, :] = jnp.where(last, 0.0, dcvn)

        bu = bg_ref[:, 0:D]
        bgt = bg_ref[:, D:2 * D]
        upre = u_ref[...].astype(F32) + bu
        sg = _sigmoid(g_ref[...].astype(F32) + bgt)
        uh = (uh_ref[...].astype(F32) + bu) * _sigmoid(gh_ref[...].astype(F32) + bgt)
        uext[0:HALO, :] = jnp.where(i == 0, 0.0, uh)
        uext[HALO:, :] = upre * sg

        _make_shifts(dext, dsh, TM)
        _make_shifts(uext, ush, TM)
        for r0, lanes in _tap_blocks(TM):
            acc = jnp.zeros((CONV_RC, LANES), F32)
            for j in range(CONV_W):
                acc = acc + _shifted(dext, dsh, CONV_W - 1 - j, r0, CONV_RC, lanes) * w_ref[j:j + 1, lanes]
            du_scr[r0:r0 + CONV_RC, lanes] = acc
        for l0 in range(0, D, LANES):
            lanes = slice(l0, l0 + LANES)
            accs = [jnp.zeros((LANE_ROWS, LANES), F32)] * CONV_W
            for r0 in range(0, TM, CONV_RC):
                dc = dext[r0:r0 + CONV_RC, lanes]
                for j in range(CONV_W):
                    prod = dc * _shifted(uext, ush, HALO - (CONV_W - 1) + j, r0, CONV_RC, lanes)
                    accs[j] = accs[j] + jnp.sum(prod.reshape(CONV_RC // LANE_ROWS, LANE_ROWS, LANES), axis=0)
            for j in range(CONV_W):
                dw8[j, :, lanes] += accs[j]

        @pl.when(last)
        def _():
            for j in range(CONV_W):
                dw_ref[j:j + 1, :] = jnp.sum(dw8[j], axis=0, keepdims=True)
            dw_ref[CONV_W:, :] = jnp.zeros((32 - CONV_W, D), F32)

        du = du_scr[...]
        dup = du * sg
        dgp = du * upre * (sg * (1.0 - sg))
        dglu_ref[:, 0:D] = dup.astype(BF16)
        dglu_ref[:, D:] = dgp.astype(BF16)
        dbu_ref[...] += _colsum8(dup.astype(BF16).astype(F32))
        dbg_ref[...] += _colsum8(dgp.astype(BF16).astype(F32))

    rowd = pl.BlockSpec((TM, D), lambda i: (i, 0))
    nxt = pl.BlockSpec((HALO, D), lambda i: (jnp.minimum((i + 1) * hb, nh - 1), 0))
    vec = pl.BlockSpec((1, D), lambda i: (0, 0))
    acc8 = pl.BlockSpec((LANE_ROWS, D), lambda i: (0, 0))
    return pl.pallas_call(
        body, name="conv_bwd", grid=(S // TM,),
        in_specs=[rowd, nxt, rowd, nxt,
                  pl.BlockSpec((TM, D), lambda i: (i, 0)), pl.BlockSpec((TM, D), lambda i: (i, 1)),
                  pl.BlockSpec((HALO, D), lambda i: (jnp.maximum(i * hb - 1, 0), 0)),
                  pl.BlockSpec((HALO, D), lambda i: (jnp.maximum(i * hb - 1, 0), 1)),
                  pl.BlockSpec((1, 2 * D), lambda i: (0, 0)), pl.BlockSpec((32, D), lambda i: (0, 0)), vec, vec],
        out_specs=[pl.BlockSpec((TM, 2 * D), lambda i: (i, 0)), acc8, acc8,
                   pl.BlockSpec((32, D), lambda i: (0, 0)), acc8, acc8, acc8],
        out_shape=[SDS((S, 2 * D), BF16), SDS((LANE_ROWS, D), F32), SDS((LANE_ROWS, D), F32), SDS((32, D), F32),
                   SDS((LANE_ROWS, D), F32), SDS((LANE_ROWS, D), F32), SDS((LANE_ROWS, D), F32)],
        scratch_shapes=[pltpu.VMEM((TM + HALO, D), F32), pltpu.VMEM((HALO + TM, D), F32),
                        pltpu.VMEM((7, TM + SHIFT_PAD, D), F32), pltpu.VMEM((7, TM + SHIFT_PAD, D), F32),
                        pltpu.VMEM((TM, D), F32), pltpu.VMEM((32, LANE_ROWS, D), F32)],
        compiler_params=_cp(("arbitrary",)),
    )(da, da, cv, cv, zrest, zrest, zrest, zrest, b_glu, wdw, g_ln, b_ln)


def _attn_bwd(zq, do, o, lse, bias_t, gi):
    dil, L, _ = zq.shape
    _, TQ, QB, ns = _attn_tile(L * dil, dil)
    NP = NH // 2

    def body(q3, kc3, kp3, vc3, vp3, do3, o3, l3, b_ref,
             out3, db_ref, kext, vext, dkx, dvx, dqn, dqc, dkc, dvc):
        q_ref, kc_ref, kp_ref, vc_ref, vp_ref, do_ref, o_ref, l_ref, out_ref = (
            r.at[0] for r in (q3, kc3, kp3, vc3, vp3, do3, o3, l3, out3))
        c = pl.program_id(0)
        n = pl.program_id(1)

        @pl.when(_first_step(c, n))
        def _():
            db_ref[...] = jnp.zeros_like(db_ref)

        @pl.when(n < ns)
        def _():
            kext[0:QBLK, :] = kp_ref[...]
            kext[QBLK:, :] = kc_ref[...]
            vext[0:QBLK, :] = vp_ref[...]
            vext[QBLK:, :] = vc_ref[...]
            krow = lax.broadcasted_iota(jnp.int32, (KBLK, 2 * QBLK), 0)
            no_prev = jnp.logical_and(n == 0, krow < QBLK)
            lane = lax.broadcasted_iota(jnp.int32, (QBLK, LANES), 1)

            def overlap_add(parts):
                segs = [parts[0][0:QBLK]]
                for b in range(1, QB):
                    segs.append(parts[b - 1][QBLK:] + parts[b][0:QBLK])
                segs.append(parts[QB - 1][QBLK:])
                return jnp.concatenate(segs, axis=0)

            lanes_of = [slice(hp * LANES, (hp + 1) * LANES) for hp in range(NP)]
            dv_parts = [[] for _ in range(NP)]
            dk_parts = [[] for _ in range(NP)]
            dbsum = [None] * NP
            for b in range(QB):
                rows = slice(b * QBLK, (b + 1) * QBLK)
                win = slice(b * QBLK, b * QBLK + KBLK)
                q2 = [_pair_stack(q_ref, rows, pl_, SCALE) for pl_ in lanes_of]
                do2 = [_pair_stack(do_ref, rows, pl_) for pl_ in lanes_of]
                st = [_dot_nt(kext[win, pl_], q2[hp]) + b_ref[0, hp] for hp, pl_ in enumerate(lanes_of)]
                dpt = [_dot_nt(vext[win, pl_], do2[hp]) for hp, pl_ in enumerate(lanes_of)]
                lse_t = l_ref[rows, :].T
                prod_t = (do_ref[rows, :].astype(F32) * o_ref[rows, :].astype(F32)).T
                dst = []
                for hp in range(NP):
                    lo = hp * LANES
                    lse_row = jnp.concatenate([lse_t[lo:lo + 1], lse_t[lo + HD:lo + HD + 1]], axis=1)
                    delta_row = jnp.concatenate([jnp.sum(prod_t[lo:lo + HD], axis=0, keepdims=True),
                                                 jnp.sum(prod_t[lo + HD:lo + LANES], axis=0, keepdims=True)], axis=1)
                    s_hp = jnp.where(no_prev, NEG_INF, st[hp]) if b == 0 else st[hp]
                    pt = jnp.exp(s_hp - lse_row)
                    d = pt * (dpt[hp] - delta_row)
                    dbsum[hp] = d if dbsum[hp] is None else dbsum[hp] + d
                    dst.append(d.astype(BF16))
                    dv_parts[hp].append(_dot(pt.astype(BF16), do2[hp]))
                for hp, pl_ in enumerate(lanes_of):
                    dk_parts[hp].append(_dot(dst[hp], q2[hp]))
                    dq2 = _dot_tn(dst[hp], kext[win, pl_])
                    dqn[rows, pl_] = jnp.where(lane < HD, dq2[0:QBLK], dq2[QBLK:]) * SCALE
            for hp, pl_ in enumerate(lanes_of):
                db_ref[hp] += dbsum[hp]
                dvx[:, pl_] = overlap_add(dv_parts[hp])
                dkx[:, pl_] = overlap_add(dk_parts[hp])

        @pl.when(n > 0)
        def _():
            out_ref[:, 0:GW] = dqc[...].astype(BF16)
            out_ref[:, GW:2 * GW] = dkc[...].astype(BF16)
            out_ref[:, 2 * GW:] = dvc[...].astype(BF16)

        @pl.when(jnp.logical_and(n > 0, n < ns))
        def _():
            out_ref[TQ - QBLK:, GW:2 * GW] = (dkc[TQ - QBLK:, :] + dkx[0:QBLK, :]).astype(BF16)
            out_ref[TQ - QBLK:, 2 * GW:] = (dvc[TQ - QBLK:, :] + dvx[0:QBLK, :]).astype(BF16)

        @pl.when(n < ns)
        def _():
            dqc[...] = dqn[...]
            dkc[...] = dkx[QBLK:, :]
            dvc[...] = dvx[QBLK:, :]

    def cur(n):
        return jnp.minimum(n, ns - 1)

    def prev(n):
        return jnp.maximum(cur(n) * QB - 1, 0)

    rows = lambda c, n: (c, cur(n), 0)
    return pl.pallas_call(
        body, name=f"attn_bwd_g{gi}", grid=(dil, ns + 1),
        in_specs=[pl.BlockSpec((1, TQ, GW), lambda c, n: (c, cur(n), 0)),
                  pl.BlockSpec((1, TQ, GW), lambda c, n: (c, cur(n), 1)),
                  pl.BlockSpec((1, QBLK, GW), lambda c, n: (c, prev(n), 1)),
                  pl.BlockSpec((1, TQ, GW), lambda c, n: (c, cur(n), 2)),
                  pl.BlockSpec((1, QBLK, GW), lambda c, n: (c, prev(n), 2)),
                  pl.BlockSpec((1, TQ, GW), rows), pl.BlockSpec((1, TQ, GW), rows), pl.BlockSpec((1, TQ, GW), rows),
                  pl.BlockSpec((1, NP, KBLK, 2 * QBLK), lambda c, n: (gi, 0, 0, 0))],
        out_specs=[pl.BlockSpec((1, TQ, 3 * GW), lambda c, n: (c, jnp.maximum(n - 1, 0), 0)),
                   pl.BlockSpec((NP, KBLK, 2 * QBLK), lambda c, n: (0, 0, 0))],
        out_shape=[SDS((dil, L, 3 * GW), BF16), SDS((NP, KBLK, 2 * QBLK), F32)],
        scratch_shapes=[pltpu.VMEM((QBLK + TQ, GW), BF16), pltpu.VMEM((QBLK + TQ, GW), BF16),
                        pltpu.VMEM((QBLK + TQ, GW), F32), pltpu.VMEM((QBLK + TQ, GW), F32),
                        pltpu.VMEM((TQ, GW), F32), pltpu.VMEM((TQ, GW), F32),
                        pltpu.VMEM((TQ, GW), F32), pltpu.VMEM((TQ, GW), F32)],
        compiler_params=_cp(("arbitrary", "arbitrary")),
    )(zq, zq, zq, zq, zq, do, o, lse, bias_t)


def _dz_block(k):
    if k < 9:
        return k % 3, k // 3
    if k < 13:
        return 3, k - 9
    return 4, k - 13


_DZ_SRC = np.array([_dz_block(k)[0] for k in range(17)], np.int32)


def _dz_hold(s):
    uses = [(k, _dz_block(k)[1]) for k in range(17) if _dz_block(k)[0] == s]
    hold = []
    for k in range(17):
        nxt = [b for kk, b in uses if kk >= k]
        hold.append(nxt[0] if nxt else uses[-1][1])
    return np.array(hold, np.int32)


def _table(tab, k):
    out = jnp.int32(int(tab[0]))
    for idx in range(1, len(tab)):
        out = jnp.where(k == idx, jnp.int32(int(tab[idx])), out)
    return out


def _w_in_tile(s, blk):
    return blk * 3 + s if s < 3 else (9 if s == 3 else 13) + blk


def _in_bwd(dqkv, dglu, dzg, w_inT, x, dx1, g, rider):
    S = x.shape[0]
    TM = 512

    def body(d0, d1, d2, d3, d4, w_ref, x_ref, dx1_ref, g_ref, gx_ref, dg_ref, scr):
        i = pl.program_id(0)

        @pl.when(i == 0)
        def _():
            dg_ref[...] = jnp.zeros_like(dg_ref)

        def rows(s, blk):
            k = _w_in_tile(s, blk)
            return w_ref[k * GW:(k + 1) * GW, :]

        dh = jnp.zeros((TM, D), F32)
        for blk in range(3):
            dh = dh + _dot(d0[0, :, blk * GW:(blk + 1) * GW], rows(0, blk))
        for s, ref in ((3, d3), (4, d4)):
            for blk in range(4):
                dh = dh + _dot(ref[:, blk * GW:(blk + 1) * GW], rows(s, blk))
        for s, ref in ((1, d1), (2, d2)):
            dil = DILATIONS[s]
            part = jnp.zeros((TM, D), F32)
            for blk in range(3):
                part = part + _dot(ref[:, :, blk * GW:(blk + 1) * GW].reshape(TM, GW), rows(s, blk))
            _merge_residues(scr, dil, lambda c, part=part, dil=dil: part[c * (TM // dil):(c + 1) * (TM // dil)])
            dh = dh + _load_cols(scr)
        xf = x_ref[...]
        r = lax.rsqrt(jnp.mean(xf * xf, axis=-1, keepdims=True) + RMS_EPS)
        nrm = xf * r
        dg_ref[...] += _colsum8(dh * nrm)
        dn = dh * g_ref[...]
        gx_ref[...] = dx1_ref[...] + r * (dn - nrm * jnp.mean(dn * nrm, axis=-1, keepdims=True))

    rowd = pl.BlockSpec((TM, D), lambda i: (i, 0))
    wide = pl.BlockSpec((TM, 2 * D), lambda i: (i, 0))
    body, r_in, r_out, r_shape, r_scr = _ride(body, 9, 2, 1, rider, S // TM)
    return pl.pallas_call(
        body, name="in_bwd", grid=(S // TM,),
        in_specs=[_residue_spec(TM, d, 3 * GW) for d in DILATIONS] + [wide, wide]
        + [pl.BlockSpec(w_inT.shape, lambda i: (0, 0), pipeline_mode=pl.Buffered(1)), rowd, rowd,
           pl.BlockSpec((1, D), lambda i: (0, 0))] + r_in,
        out_specs=[rowd, pl.BlockSpec((LANE_ROWS, D), lambda i: (0, 0))] + r_out,
        out_shape=[SDS((S, D), F32), SDS((LANE_ROWS, D), F32)] + r_shape,
        scratch_shapes=[_col_scratch(TM, D)] + r_scr,
        compiler_params=_cp(("arbitrary",)),
    )(*dqkv, dglu, dzg, w_inT, x, dx1, g, *rider.ins)


def _dw_in(dqkv, dglu, dzg, hs):
    S = hs[0].shape[0]
    TS = min(2048, S)
    nk = 17
    holds = [_dz_hold(s) for s in range(5)]
    h_of = (0, 1, 2, 0, 0)

    def body(d0, d1, d2, d3, d4, h0, h1, h2, o_ref, acc):
        m = pl.program_id(0)
        s_ = pl.program_id(1)

        @pl.when(s_ == 0)
        def _():
            acc[...] = jnp.zeros_like(acc)

        src = _table(_DZ_SRC, m)
        pairs = ((d0, h0), (d1, h1), (d2, h2), (d3, h0), (d4, h0))
        for s, (dref, href) in enumerate(pairs):
            @pl.when(src == s)
            def _(dref=dref, href=href):
                acc[...] += _dot_tn(dref[...].reshape(TS, GW), href[...].reshape(TS, D))

        @pl.when(s_ == pl.num_programs(1) - 1)
        def _():
            o_ref[...] = acc[...].astype(BF16)

    def row(s, m, s_):
        return jnp.where(_table(_DZ_SRC, m) == s, s_, 0)

    def dspec(s):
        if s < 3:
            dil = DILATIONS[s]
            return pl.BlockSpec((dil, TS // dil, GW), lambda m, s_: (0, row(s, m, s_), _table(holds[s], m)))
        return pl.BlockSpec((TS, GW), lambda m, s_: (row(s, m, s_), _table(holds[s], m)))

    def hrow(j, m, s_):
        used = _table(np.array([int(h_of[_dz_block(k)[0]] == j) for k in range(nk)], np.int32), m)
        return jnp.where(used == 1, s_, 0)

    hspecs = [pl.BlockSpec((TS, D), lambda m, s_: (hrow(0, m, s_), 0))] + [
        pl.BlockSpec((DILATIONS[j], TS // DILATIONS[j], D), lambda m, s_, j=j: (0, hrow(j, m, s_), 0)) for j in (1, 2)]
    return pl.pallas_call(
        body, name="dw_in", grid=(nk, S // TS),
        in_specs=[dspec(s) for s in range(5)] + hspecs,
        out_specs=pl.BlockSpec((GW, D), lambda m, s_: (m, 0)),
        out_shape=SDS((nk * GW, D), BF16),
        scratch_shapes=[pltpu.VMEM((GW, D), F32)],
        compiler_params=_cp(("arbitrary", "arbitrary")),
    )(*dqkv, dglu, dzg, *hs)


def _mm_tn(a, b, tm, a_maps, name):
    S, N = b.shape
    parts = len(a_maps)
    tp = tm // parts
    nm = len(a_maps[0])
    TS = min(2048, S)
    tabs = [np.array(t, np.int32) for t in a_maps]

    def body(*refs):
        a_refs = refs[:parts]
        b_ref, o_ref, acc = refs[parts:]
        s_ = pl.program_id(1)

        @pl.when(s_ == 0)
        def _():
            acc[...] = jnp.zeros_like(acc)

        for p, ar in enumerate(a_refs):
            acc[p * tp:(p + 1) * tp, :] += _dot_tn(ar[...], b_ref[...])

        @pl.when(s_ == pl.num_programs(1) - 1)
        def _():
            o_ref[...] = acc[...].astype(BF16)

    return pl.pallas_call(
        body, name=name, grid=(nm, S // TS),
        in_specs=[pl.BlockSpec((TS, tp), lambda m, s_, t=t: (s_, _table(t, m))) for t in tabs]
        + [pl.BlockSpec((TS, N), lambda m, s_: (s_, 0))],
        out_specs=pl.BlockSpec((tm, N), lambda m, s_: (m, 0)),
        out_shape=SDS((nm * tm, N), BF16),
        scratch_shapes=[pltpu.VMEM((tm, N), F32)],
        compiler_params=_cp(("arbitrary", "arbitrary")),
    )(*([a] * parts), b)


def _row_tile(rows, cols, limit=1 << 20):
    if rows * cols * 4 <= limit:
        return rows
    best = None
    for t in range(8, rows, 8):
        if rows % t == 0 and t * cols * 4 <= limit:
            best = t
    return best


def _adamw(w, g, m, v, name):
    R, C = w.shape
    tr = _row_tile(R, C)

    def body(w_ref, g_ref, m_ref, v_ref, d_ref, nm_ref, nv_ref):
        gg = g_ref[...]
        nm = ADAM_B1 * m_ref[...] + (1.0 - ADAM_B1) * gg
        nv = ADAM_B2 * v_ref[...] + (1.0 - ADAM_B2) * (gg * gg)
        m_hat = nm / (1.0 - ADAM_B1 ** ADAM_STEP)
        v_hat = nv / (1.0 - ADAM_B2 ** ADAM_STEP)
        d_ref[...] = -ADAM_LR * (m_hat / (jnp.sqrt(v_hat) + ADAM_EPS) + ADAM_WD * w_ref[...])
        nm_ref[...] = nm
        nv_ref[...] = nv

    spec = pl.BlockSpec((tr, C), lambda i: (i, 0))
    return pl.pallas_call(
        body, name=name, grid=(R // tr,), in_specs=[spec] * 4, out_specs=[spec] * 3,
        out_shape=[SDS((R, C), F32)] * 3, compiler_params=_cp(("arbitrary",)),
    )(w, g, m, v)


_FLIPS = ((1, 0), (0, 1), (1, 1))


def _place():
    x, y, c = lax.axis_index("x"), lax.axis_index("y"), lax.axis_index("c")
    return x, y, c


def _peer_chips(x, y):
    return [((x + fx) % 2, (y + fy) % 2) for fx, fy in _FLIPS]


def _gather_weights(shards):
    nw = len(shards)
    views = [s.reshape(2, s.shape[0] // 2, s.shape[1]) for s in shards]

    def body(*refs):
        ins = refs[:nw]
        outs = refs[nw:2 * nw]
        ici_send, ici_recv, d2d_send, d2d_recv, loc = refs[2 * nw:]
        x, y, c = _place()
        j = 2 * x + y
        chips = _peer_chips(x, y)
        copies = []
        for w in range(nw):
            cp = pltpu.make_async_copy(ins[w], outs[w].at[j], loc.at[w])
            cp.start()
            copies.append(cp)
        sends = []
        for w in range(nw):
            for k, (px, py) in enumerate(chips):
                cp = pltpu.make_async_remote_copy(
                    src_ref=ins[w].at[c], dst_ref=outs[w].at[j, c], send_sem=ici_send.at[w, k],
                    recv_sem=ici_recv.at[w, k], device_id=(px, py, c), device_id_type=MESH)
                cp.start()
                sends.append(cp)
        for w in range(nw):
            for k, (px, py) in enumerate(chips):
                jk = 2 * px + py
                land = outs[w].at[jk, c]
                pltpu.make_async_remote_copy(
                    src_ref=ins[w].at[c], dst_ref=land, send_sem=ici_send.at[w, k],
                    recv_sem=ici_recv.at[w, k], device_id=(px, py, c), device_id_type=MESH).wait_recv()
                cp = pltpu.make_async_remote_copy(
                    src_ref=land, dst_ref=land, send_sem=d2d_send.at[w, k],
                    recv_sem=d2d_recv.at[w, k], device_id=(x, y, 1 - c), device_id_type=MESH)
                cp.start()
                sends.append(cp)
        for w in range(nw):
            for k, (px, py) in enumerate(chips):
                jk = 2 * px + py
                land = outs[w].at[jk, 1 - c]
                pltpu.make_async_remote_copy(
                    src_ref=land, dst_ref=land, send_sem=d2d_send.at[w, k],
                    recv_sem=d2d_recv.at[w, k], device_id=(x, y, 1 - c), device_id_type=MESH).wait_recv()
        for cp in sends:
            cp.wait_send()
        for cp in copies:
            cp.wait()

    outs = pl.pallas_call(
        body, name="gather_weights",
        in_specs=[ANY] * nw, out_specs=[ANY] * nw,
        out_shape=[SDS((4,) + v.shape, BF16) for v in views],
        scratch_shapes=[pltpu.SemaphoreType.DMA((nw, 3)), pltpu.SemaphoreType.DMA((nw, 3)),
                        pltpu.SemaphoreType.DMA((nw, 3)), pltpu.SemaphoreType.DMA((nw, 3)),
                        pltpu.SemaphoreType.DMA((nw,))],
    )(*views)
    return [o.reshape(4 * s.shape[0], s.shape[1]) for o, s in zip(outs, shards)]


class _Rider:
    def __init__(self, ins, out_shape, scratch, start, finish, mid=None):
        self.ins, self.out_shape, self.scratch = list(ins), list(out_shape), list(scratch)
        self.start, self.finish, self.mid = start, finish, mid


def _ride(body, n_in, n_out, n_scr, rider, steps):
    if rider is None:
        return body, [], [], [], []
    ri, ro = len(rider.ins), len(rider.out_shape)

    def wrapped(*refs):
        ins, r_ins = refs[:n_in], refs[n_in:n_in + ri]
        o0 = n_in + ri
        outs, r_outs = refs[o0:o0 + n_out], refs[o0 + n_out:o0 + n_out + ro]
        s0 = o0 + n_out + ro
        scr, r_scr = refs[s0:s0 + n_scr], refs[s0 + n_scr:]
        i = pl.program_id(0)

        @pl.when(i == 0)
        def _():
            rider.start(r_ins, r_outs, r_scr)

        if rider.mid is not None:
            @pl.when(i == (3 * steps) // 4)
            def _():
                rider.mid(r_ins, r_outs, r_scr)

        body(*ins, *outs, *scr)

        @pl.when(i == steps - 1)
        def _():
            rider.finish(r_ins, r_outs, r_scr)

    return wrapped, [ANY] * ri, [ANY] * ro, rider.out_shape, rider.scratch


def _gather_rider(shards):
    nw = len(shards)
    views = [s.reshape(2, s.shape[0] // 2, s.shape[1]) for s in shards]

    def parts(ins, outs, sems):
        ici_send, ici_recv, d2d_send, d2d_recv, loc = sems
        x, y, c = _place()
        j = 2 * x + y
        local, ici, land_ici, fwd, land_fwd = [], [], [], [], []
        for w in range(nw):
            local.append(pltpu.make_async_copy(ins[w], outs[w].at[j], loc.at[w]))
            for k, (px, py) in enumerate(_peer_chips(x, y)):
                jk = 2 * px + py
                ici.append(pltpu.make_async_remote_copy(
                    src_ref=ins[w].at[c], dst_ref=outs[w].at[j, c], send_sem=ici_send.at[w, k],
                    recv_sem=ici_recv.at[w, k], device_id=(px, py, c), device_id_type=MESH))
                mine = outs[w].at[jk, c]
                land_ici.append(pltpu.make_async_remote_copy(
                    src_ref=ins[w].at[c], dst_ref=mine, send_sem=ici_send.at[w, k],
                    recv_sem=ici_recv.at[w, k], device_id=(px, py, c), device_id_type=MESH))
                fwd.append(pltpu.make_async_remote_copy(
                    src_ref=mine, dst_ref=mine, send_sem=d2d_send.at[w, k],
                    recv_sem=d2d_recv.at[w, k], device_id=(x, y, 1 - c), device_id_type=MESH))
                theirs = outs[w].at[jk, 1 - c]
                land_fwd.append(pltpu.make_async_remote_copy(
                    src_ref=theirs, dst_ref=theirs, send_sem=d2d_send.at[w, k],
                    recv_sem=d2d_recv.at[w, k], device_id=(x, y, 1 - c), device_id_type=MESH))
        return local, ici, land_ici, fwd, land_fwd

    def start(ins, outs, sems):
        local, ici, _, _, _ = parts(ins, outs, sems)
        for cp in local + ici:
            cp.start()

    def mid(ins, outs, sems):
        _, _, land_ici, fwd, _ = parts(ins, outs, sems)
        for landed, cp in zip(land_ici, fwd):
            landed.wait_recv()
            cp.start()

    def finish(ins, outs, sems):
        local, ici, _, fwd, land_fwd = parts(ins, outs, sems)
        for cp in land_fwd:
            cp.wait_recv()
        for cp in ici + fwd:
            cp.wait_send()
        for cp in local:
            cp.wait()

    sem = pltpu.SemaphoreType.DMA
    return _Rider(views, [SDS((4,) + v.shape, BF16) for v in views],
                  [sem((nw, 3)), sem((nw, 3)), sem((nw, 3)), sem((nw, 3)), sem((nw,))], start, finish, mid)


def _chip_exchange_rider(parts):
    nw = len(parts)

    def copies(ins, outs, sems):
        send, recv = sems
        x, y, c = _place()
        return [pltpu.make_async_remote_copy(
            src_ref=ins[w].at[2 * px + py], dst_ref=outs[w].at[k], send_sem=send.at[w, k],
            recv_sem=recv.at[w, k], device_id=(px, py, c), device_id_type=MESH)
            for w in range(nw) for k, (px, py) in enumerate(_peer_chips(x, y))]

    def start(ins, outs, sems):
        for cp in copies(ins, outs, sems):
            cp.start()

    def finish(ins, outs, sems):
        for cp in copies(ins, outs, sems):
            cp.wait()

    sem = pltpu.SemaphoreType.DMA
    return _Rider(parts, [SDS((3,) + p.shape[1:], BF16) for p in parts], [sem((nw, 3)), sem((nw, 3))], start, finish)


def _pair_exchange(grads, name):
    nw = len(grads)

    def body(*refs):
        ins = refs[:nw]
        outs = refs[nw:2 * nw]
        send, recv = refs[2 * nw:]
        x, y, c = _place()
        cps = []
        for w in range(nw):
            cp = pltpu.make_async_remote_copy(
                src_ref=ins[w].at[:, pl.ds(1 - c, 1)], dst_ref=outs[w], send_sem=send.at[w], recv_sem=recv.at[w],
                device_id=(x, y, 1 - c), device_id_type=MESH)
            cp.start()
            cps.append(cp)
        for cp in cps:
            cp.wait()

    return pl.pallas_call(
        body, name=name, in_specs=[ANY] * nw, out_specs=[ANY] * nw,
        out_shape=[SDS((4, 1) + g.shape[2:], BF16) for g in grads],
        scratch_shapes=[pltpu.SemaphoreType.DMA((nw,)), pltpu.SemaphoreType.DMA((nw,))],
    )(*grads)


def _half_tile(rh):
    best = 16
    for t in range(16, 545, 16):
        if rh % t == 0:
            best = t
    return best


def _pair_sum(c_arr, g, got, name):
    _, _, rh, n = g.shape
    tr = _half_tile(rh)

    def body(c_ref, a_ref, b_ref, o_ref):
        o_ref[...] = (a_ref[...].astype(F32) + b_ref[...].astype(F32)).astype(BF16)

    return pl.pallas_call(
        body, name=name,
        grid_spec=pltpu.PrefetchScalarGridSpec(
            num_scalar_prefetch=1, grid=(4, rh // tr),
            in_specs=[pl.BlockSpec((1, 1, tr, n), lambda s, i, c: (s, c[0], i, 0)),
                      pl.BlockSpec((1, 1, tr, n), lambda s, i, c: (s, 0, i, 0))],
            out_specs=pl.BlockSpec((1, 1, tr, n), lambda s, i, c: (s, 0, i, 0))),
        out_shape=SDS((4, 1, rh, n), BF16),
        compiler_params=_cp(("arbitrary", "arbitrary")),
    )(c_arr, g, got)


def _chip_sum(jc_arr, part, got, name):
    _, _, rh, n = part.shape
    tr = _half_tile(rh)

    def body(jc_ref, a_ref, b_ref, o_ref):
        acc = a_ref[0, 0].astype(F32)
        for k in range(3):
            acc = acc + b_ref[k, 0].astype(F32)
        o_ref[0] = acc

    return pl.pallas_call(
        body, name=name,
        grid_spec=pltpu.PrefetchScalarGridSpec(
            num_scalar_prefetch=1, grid=(rh // tr,),
            in_specs=[pl.BlockSpec((1, 1, tr, n), lambda i, jc: (jc[0], 0, i, 0)),
                      pl.BlockSpec((3, 1, tr, n), lambda i, jc: (0, 0, i, 0))],
            out_specs=pl.BlockSpec((1, tr, n), lambda i, jc: (jc[1], i, 0))),
        out_shape=SDS((2, rh, n), F32),
        compiler_params=_cp(("arbitrary",)),
    )(jc_arr, part, got)


def _half_swap(halves):
    nw = len(halves)

    def body(*refs):
        ins = refs[:nw]
        outs = refs[nw:2 * nw]
        send, recv = refs[2 * nw:]
        x, y, c = _place()
        cps = []
        for w in range(nw):
            cp = pltpu.make_async_remote_copy(
                src_ref=ins[w].at[c], dst_ref=outs[w].at[c], send_sem=send.at[w], recv_sem=recv.at[w],
                device_id=(x, y, 1 - c), device_id_type=MESH)
            cp.start()
            cps.append(cp)
        for cp in cps:
            cp.wait()

    return pl.pallas_call(
        body, name="grad_half_swap", in_specs=[ANY] * nw, out_specs=[ANY] * nw,
        out_shape=[SDS(h.shape, F32) for h in halves],
        input_output_aliases={w: w for w in range(nw)},
        scratch_shapes=[pltpu.SemaphoreType.DMA((nw,)), pltpu.SemaphoreType.DMA((nw,))],
    )(*halves)


def _all_sum_small(part, name):
    R = part.shape[0]

    def body(p_ref, o_ref, land, send, recv):
        x, y, c = _place()
        me = 4 * x + 2 * y + c
        cps = []
        for d in range(1, 8):
            t = (me + d) % 8
            cp = pltpu.make_async_remote_copy(
                src_ref=p_ref, dst_ref=land.at[me], send_sem=send.at[d - 1], recv_sem=recv.at[d - 1],
                device_id=(t // 4, (t // 2) % 2, t % 2), device_id_type=MESH)
            cp.start()
            cps.append(cp)
        land[me] = p_ref[...]
        for cp in cps:
            cp.wait()
        acc = land[0]
        for d in range(1, 8):
            acc = acc + land[d]
        o_ref[...] = acc

    return pl.pallas_call(
        body, name=name,
        in_specs=[pl.BlockSpec(memory_space=pltpu.VMEM)], out_specs=pl.BlockSpec(memory_space=pltpu.VMEM),
        out_shape=SDS((R, D), F32),
        scratch_shapes=[pltpu.VMEM((8, R, D), F32), pltpu.SemaphoreType.DMA((7,)), pltpu.SemaphoreType.DMA((7,))],
        compiler_params=pltpu.CompilerParams(vmem_limit_bytes=VMEM_LIMIT),
    )(part)


def _pad_rows(a, rows):
    return jnp.pad(a, ((0, rows - a.shape[0]), (0, 0)))


def _vec_pack(vs):
    return jnp.concatenate([_pad_rows(v, LANE_ROWS) for v in vs], axis=0)


def kernel(x, rel_bias_table, g_pre_mix, w_in, b_glu, w_dw, b_dw, g_conv_ln, b_conv_ln, w_conv_out, b_conv_out, w_attn_out, w_mix_out, g_post_mix, g_pre_ffn, w_ffn_in, w_ffn_out, g_post_ffn, loss_target, m_rel_bias_table, m_g_pre_mix, m_w_in, m_b_glu, m_w_dw, m_b_dw, m_g_conv_ln, m_b_conv_ln, m_w_conv_out, m_b_conv_out, m_w_attn_out, m_w_mix_out, m_g_post_mix, m_g_pre_ffn, m_w_ffn_in, m_w_ffn_out, m_g_post_ffn, v_rel_bias_table, v_g_pre_mix, v_w_in, v_b_glu, v_w_dw, v_b_dw, v_g_conv_ln, v_b_conv_ln, v_w_conv_out, v_b_conv_out, v_w_attn_out, v_w_mix_out, v_g_post_mix, v_g_pre_ffn, v_w_ffn_in, v_w_ffn_out, v_g_post_ffn):
    S = x.shape[1]
    xs = x.reshape(S, D)
    tgt = loss_target.reshape(S, D)
    cx, cy, cc = _place()
    chip = 2 * cx + cy

    shards = [w_in[0].T.astype(BF16),
              w_ffn_in[0].T.astype(BF16),
              w_attn_out[0].T.astype(BF16),
              w_conv_out[0].astype(BF16),
              w_mix_out[0].astype(BF16),
              w_ffn_out[0].astype(BF16)]
    (w_inT,) = _gather_weights(shards[:1])
    w_inN = w_inT.T

    buckets_np, valid_np = _bucket_tables()
    buckets = jnp.asarray(buckets_np)
    bias = _bias_expand(rel_bias_table, buckets, jnp.asarray(valid_np)).reshape(3, NH, QBLK, KBLK)
    bias2 = bias.reshape(3, NH // 2, 2 * QBLK, KBLK)
    bias_t = bias.reshape(3, NH // 2, 2, QBLK, KBLK).transpose(0, 1, 4, 2, 3).reshape(3, NH // 2, KBLK, 2 * QBLK)
    wdw32 = _pad_rows(w_dw[0], 32)
    wdw_full = _gather_small_cols(wdw32, chip)

    zrest, h, h_r4, h_r16, *gathered = _in_proj_rest(xs, g_pre_mix, w_inN[:, 3 * ATTN_COLS:], _gather_rider(shards[1:]))
    w_fiT, w_aoT, w_co, w_mx, w_fo = (t.reshape(4 * s.shape[0], s.shape[1]) for t, s in zip(gathered, shards[1:]))
    w_fiN, w_aoN = w_fiT.T, w_aoT.T
    w_coT, w_mxT, w_foT = w_co.T, w_mx.T, w_fo.T
    zq = _in_proj_qkv(h, w_inN[:, :3 * ATTN_COLS])
    og, lg = [], []
    for gi in range(3):
        o_g, l_g = _attn_fwd(zq[gi], bias2, gi)
        og.append(o_g)
        lg.append(l_g)
    cv, a = _conv_fwd(zrest, b_glu, wdw_full, b_dw, g_conv_ln, b_conv_ln)
    o, o_r4, o_r16, lse, lse_r4, lse_r16, ya, yc, mg, mm, x1 = _mix_fwd(
        og, lg, a, zrest, xs, w_aoN, w_co, b_conv_out, w_mx, g_post_mix)
    h2, gu, df, dx2, loss8, dg_post_ffn = _ffn_fwd(x1, tgt, g_pre_ffn, g_post_ffn, w_fiN, w_fo)

    c_arr = jnp.reshape(cc, (1,)).astype(jnp.int32)
    jc_arr = jnp.stack([chip, cc]).astype(jnp.int32)
    ident = lambda n: [list(range(n))]

    def pair_sums(partials, names, tag):
        views = [g.reshape(4, 2, g.shape[0] // 8, g.shape[1]) for g in partials]
        got = _pair_exchange(views, f"grad_pair_exchange_{tag}")
        return [_pair_sum(c_arr, v, r, f"pair_sum_{n}") for v, r, n in zip(views, got, names)]

    def chip_sums(pair, got, names):
        return [_chip_sum(jc_arr, p, r, f"chip_sum_{n}") for p, r, n in zip(pair, got, names)]

    dff, act = _ffn_bwd_act(df, gu, w_foT)
    g_fiT = _mm_tn(dff, h2, 512, [[2 * t if t < NFT else 2 * (t - NFT) + 1 for t in range(0, 22, 2)],
                                  [2 * t if t < NFT else 2 * (t - NFT) + 1 for t in range(1, 22, 2)]], "dw_ffn_in")
    g_fo = _mm_tn(act, df, FFN_H // 2, ident(2), "dw_ffn_out")
    names_a = ("w_ffn_in", "w_ffn_out")
    pair_a = pair_sums([g_fiT, g_fo], names_a, "ffn")
    dx1, dg_pre_ffn, *got_a = _ffn_bwd_in(dff, x1, dx2, g_pre_ffn, w_fiT, _chip_exchange_rider(pair_a))
    halves_a = chip_sums(pair_a, got_a, names_a)
    dmm, dya, dyc, do, do_r4, do_r16, da, dzg, dg_post_mix, db_conv_out = _mix_bwd(
        dx1, mm, ya, yc, zrest, g_post_mix, w_mxT, w_aoT, w_coT)
    dglu, db_glu_u, db_glu_g, dw_dw, dg_conv_ln, db_conv_ln, db_dw = _conv_bwd(da, cv, zrest, b_glu, wdw_full, g_conv_ln, b_conv_ln)
    first = lambda t: t.reshape(1, S, GW)
    dqkv, dbias = [], []
    for gi, (do_g, o_g, lse_g) in enumerate(((first(do), first(o), first(lse)), (do_r4, o_r4, lse_r4),
                                            (do_r16, o_r16, lse_r16))):
        d_g, db_g = _attn_bwd(zq[gi], do_g, o_g, lse_g, bias_t, gi)
        dqkv.append(d_g)
        dbias.append(db_g.reshape(NH // 2, KBLK, 2, QBLK).transpose(0, 2, 3, 1).reshape(NH, QBLK, KBLK))
    dtab = _bias_reduce(jnp.concatenate(dbias, axis=0), buckets)

    g_inT = _dw_in(dqkv, dglu, dzg, (h, h_r4, h_r16))
    g_aoT = _mm_tn(dya, o, 512, ident(2), "dw_attn_out")
    g_co = _mm_tn(a, dyc, 512, ident(2), "dw_conv_out")
    g_mx = _mm_tn(mg, dmm, 512, ident(2), "dw_mix_out")
    names_b = ("w_in", "w_attn_out", "w_conv_out", "w_mix_out")
    pair_b = pair_sums([g_inT, g_aoT, g_co, g_mx], names_b, "rest")
    grad_x, dg_pre_mix, *got_b = _in_bwd(dqkv, dglu, dzg, w_inT, xs, dx1, g_pre_mix, _chip_exchange_rider(pair_b))
    halves_b = chip_sums(pair_b, got_b, names_b)

    red = [t.reshape(t.shape[0] * t.shape[1], t.shape[2]) for t in _half_swap(halves_a + halves_b)]
    gw_ffn_in, gw_ffn_out, gw_in, gw_attn_out, gw_conv_out, gw_mix_out = (
        red[0].T, red[1], red[2].T, red[3].T, red[4], red[5])

    small = jnp.concatenate([loss8, dg_pre_mix, db_glu_u, db_glu_g, db_dw, dg_conv_ln, db_conv_ln, db_conv_out,
                             dg_post_mix, dg_pre_ffn, dg_post_ffn, dtab, dw_dw], axis=0)
    tot = _all_sum_small(small, "small_all_sum")
    row = lambda i: tot[LANE_ROWS * i:LANE_ROWS * i + 1]
    loss = tot[0, 0]
    g_g_pre_mix, g_b_glu = row(1), jnp.concatenate([row(2), row(3)], axis=1)
    g_b_dw, g_g_conv_ln, g_b_conv_ln, g_b_conv_out = row(4), row(5), row(6), row(7)
    g_g_post_mix, g_g_pre_ffn, g_g_post_ffn = row(8), row(9), row(10)
    g_tab = tot[88:112, 0:32].T
    g_w_dw = lax.dynamic_slice(tot[112:112 + CONV_W], (0, 256 * chip), (CONV_W, 256))

    vec_names = ["g_pre_mix", "b_dw", "g_conv_ln", "b_conv_ln", "b_conv_out", "g_post_mix", "g_pre_ffn", "g_post_ffn"]
    vec_w = [g_pre_mix, b_dw, g_conv_ln, b_conv_ln, b_conv_out, g_post_mix, g_pre_ffn, g_post_ffn]
    vec_m = [m_g_pre_mix, m_b_dw, m_g_conv_ln, m_b_conv_ln, m_b_conv_out, m_g_post_mix, m_g_pre_ffn, m_g_post_ffn]
    vec_v = [v_g_pre_mix, v_b_dw, v_g_conv_ln, v_b_conv_ln, v_b_conv_out, v_g_post_mix, v_g_pre_ffn, v_g_post_ffn]
    vec_g = [g_g_pre_mix, g_b_dw, g_g_conv_ln, g_b_conv_ln, g_b_conv_out, g_g_post_mix, g_g_pre_ffn, g_g_post_ffn]

    def pack(vs, glu, tab, dw):
        return jnp.concatenate([_vec_pack(vs), _pad_rows(glu.reshape(2, D), LANE_ROWS),
                                _pad_rows(jnp.pad(tab.T, ((0, 0), (0, D - 32))), 24),
                                _pad_rows(jnp.pad(dw, ((0, 0), (0, D - 256))), 32)], axis=0)

    sw = pack(vec_w, b_glu, rel_bias_table, w_dw[0])
    sg = pack(vec_g, g_b_glu, g_tab, g_w_dw)
    sm = pack(vec_m, m_b_glu, m_rel_bias_table, m_w_dw[0])
    sv = pack(vec_v, v_b_glu, v_rel_bias_table, v_w_dw[0])
    s_out = _adamw(sw, sg, sm, sv, "adamw_small")

    def unpack(t):
        vecs = {n: t[LANE_ROWS * i:LANE_ROWS * i + 1] for i, n in enumerate(vec_names)}
        vecs["b_glu"] = t[64:66].reshape(1, 2 * D)
        vecs["rel_bias_table"] = t[72:96, 0:32].T
        vecs["w_dw"] = t[96:96 + CONV_W, 0:256][None]
        return vecs

    small_out = [unpack(t) for t in s_out]
    big = {}
    for n, w, g, m, v in (("w_in", w_in, gw_in, m_w_in, v_w_in),
                          ("w_conv_out", w_conv_out, gw_conv_out, m_w_conv_out, v_w_conv_out),
                          ("w_attn_out", w_attn_out, gw_attn_out, m_w_attn_out, v_w_attn_out),
                          ("w_mix_out", w_mix_out, gw_mix_out, m_w_mix_out, v_w_mix_out),
                          ("w_ffn_in", w_ffn_in, gw_ffn_in, m_w_ffn_in, v_w_ffn_in),
                          ("w_ffn_out", w_ffn_out, gw_ffn_out, m_w_ffn_out, v_w_ffn_out)):
        big[n] = [t[None] for t in _adamw(w[0], g, m[0], v[0], f"adamw_{n}")]

    order = ["rel_bias_table", "g_pre_mix", "w_in", "b_glu", "w_dw", "b_dw", "g_conv_ln", "b_conv_ln", "w_conv_out",
             "b_conv_out", "w_attn_out", "w_mix_out", "g_post_mix", "g_pre_ffn", "w_ffn_in", "w_ffn_out", "g_post_ffn"]
    grads = {"rel_bias_table": g_tab, "g_pre_mix": g_g_pre_mix, "w_in": gw_in[None], "b_glu": g_b_glu,
             "w_dw": g_w_dw[None], "b_dw": g_b_dw, "g_conv_ln": g_g_conv_ln, "b_conv_ln": g_b_conv_ln,
             "w_conv_out": gw_conv_out[None], "b_conv_out": g_b_conv_out, "w_attn_out": gw_attn_out[None],
             "w_mix_out": gw_mix_out[None], "g_post_mix": g_g_post_mix, "g_pre_ffn": g_g_pre_ffn,
             "w_ffn_in": gw_ffn_in[None], "w_ffn_out": gw_ffn_out[None], "g_post_ffn": g_g_post_ffn}
    outs = [loss, grad_x.reshape(1, S, D)] + [grads[n] for n in order]
    for slot in range(3):
        outs += [big[n][slot] if n in big else small_out[slot][n] for n in order]
    return tuple(outs)


def _gather_small_cols(wdw32, chip):
    placed = lax.dynamic_update_slice(jnp.zeros((32, D), F32), wdw32, (0, 256 * chip))
    return _all_sum_small(placed, "conv_taps_gather") * 0.5
```

```python
import functools
import math

import numpy as np
import jax
import jax.numpy as jnp
from jax import lax
from jax.experimental import pallas as pl
from jax.experimental.pallas import tpu as pltpu

F32 = jnp.float32
BF16 = jnp.bfloat16
SDS = jax.ShapeDtypeStruct
MESH = pl.DeviceIdType.MESH
ANY = pl.BlockSpec(memory_space=pl.ANY)

D = 1024
HD = 64
NH = 8
GW = NH * HD
ATTN_COLS = 3 * GW
DILATIONS = (1, 4, 16)
SPAN = 128
QBLK = 128
KBLK = 2 * QBLK
CONV_W = 31
FFN_H = 2816
FFN_T = 256
NFT = FFN_H // FFN_T
RMS_EPS = 1e-6
LN_EPS = 1e-5
NEG_INF = -1e30
SCALE = HD ** -0.5
LANE_ROWS = 8
LANES = 128

ADAM_LR, ADAM_B1, ADAM_B2, ADAM_EPS, ADAM_WD, ADAM_STEP = 0.001, 0.9, 0.999, 1e-08, 0.01, 10

VMEM_LIMIT = 56 * 1024 * 1024


def _cp(sem):
    return pltpu.CompilerParams(dimension_semantics=sem, vmem_limit_bytes=VMEM_LIMIT)


def _dot(a, b):
    return jnp.dot(a, b, preferred_element_type=F32)


def _dot_nt(a, b):
    return lax.dot_general(a, b, (((1,), (1,)), ((), ())), preferred_element_type=F32)


def _dot_tn(a, b):
    return lax.dot_general(a, b, (((0,), (0,)), ((), ())), preferred_element_type=F32)


def _sigmoid(v):
    return 0.5 * jnp.tanh(0.5 * v) + 0.5


def _colsum8(v):
    s = jnp.sum(v, axis=0, keepdims=True)
    row = lax.broadcasted_iota(jnp.int32, (LANE_ROWS, v.shape[1]), 0)
    return jnp.where(row == 0, jnp.broadcast_to(s, (LANE_ROWS, v.shape[1])), 0.0)


def _col_scratch(n, width):
    return pltpu.VMEM((width // LANES, n, LANES), F32)


def _store_cols(scr, v):
    for lb in range(scr.shape[0]):
        scr[lb] = v[:, lb * LANES:(lb + 1) * LANES]


def _load_cols(scr):
    return jnp.concatenate([scr[lb] for lb in range(scr.shape[0])], axis=1)


def _split_residues(scr, dil, put):
    nb, n, _ = scr.shape
    for c in range(dil):
        put(c, jnp.concatenate([scr[lb, pl.ds(c, n // dil, stride=dil), :] for lb in range(nb)], axis=1))


def _merge_residues(scr, dil, get):
    nb, n, _ = scr.shape
    for c in range(dil):
        v = get(c)
        for lb in range(nb):
            scr[lb, pl.ds(c, n // dil, stride=dil), :] = v[:, lb * LANES:(lb + 1) * LANES]


def _residue_shape(S, dil, width):
    return (dil, S // dil, width)


def _residue_spec(TM, dil, width):
    return pl.BlockSpec((dil, TM // dil, width), lambda i: (0, i, 0))


def _in_proj_rest(x, g, w, rider):
    S = x.shape[0]
    N = w.shape[1]
    TM, TN = 512, 512

    def body(x_ref, g_ref, w_ref, zr_ref, h0_ref, h1_ref, h2_ref, hf_scr):
        xf = x_ref[...]
        r = lax.rsqrt(jnp.mean(xf * xf, axis=-1, keepdims=True) + RMS_EPS)
        hf = xf * r * g_ref[...]
        h0_ref[...] = hf.astype(BF16)
        _store_cols(hf_scr, hf)
        for dil, ref in ((DILATIONS[1], h1_ref), (DILATIONS[2], h2_ref)):
            def put(c, v, ref=ref):
                ref[c] = v.astype(BF16)
            _split_residues(hf_scr, dil, put)
        for j in range(N // TN):
            zr_ref[:, j * TN:(j + 1) * TN] = _dot(h0_ref[...], w_ref[:, j * TN:(j + 1) * TN]).astype(BF16)

    body, r_in, r_out, r_shape, r_scr = _ride(body, 3, 4, 1, rider, S // TM)
    return pl.pallas_call(
        body, name="in_proj_rest", grid=(S // TM,),
        in_specs=[pl.BlockSpec((TM, D), lambda i: (i, 0)),
                  pl.BlockSpec((1, D), lambda i: (0, 0)),
                  pl.BlockSpec((D, N), lambda i: (0, 0), pipeline_mode=pl.Buffered(1))] + r_in,
        out_specs=[pl.BlockSpec((TM, N), lambda i: (i, 0)), pl.BlockSpec((TM, D), lambda i: (i, 0)),
                   _residue_spec(TM, DILATIONS[1], D), _residue_spec(TM, DILATIONS[2], D)] + r_out,
        out_shape=[SDS((S, N), BF16), SDS((S, D), BF16),
                   SDS(_residue_shape(S, DILATIONS[1], D), BF16),
                   SDS(_residue_shape(S, DILATIONS[2], D), BF16)] + r_shape,
        scratch_shapes=[_col_scratch(TM, D)] + r_scr,
        compiler_params=_cp(("arbitrary",)),
    )(x, g, w, *rider.ins)


def _in_proj_qkv(h, w):
    S = h.shape[0]
    TM = 512

    def body(h_ref, w_ref, z0_ref, z1_ref, z2_ref, scr):
        outs = (z0_ref, z1_ref, z2_ref)
        for j in range(9):
            t, gi = j // 3, j % 3
            cols = slice(t * GW, (t + 1) * GW)
            zt = _dot(h_ref[...], w_ref[:, j * GW:(j + 1) * GW])
            if gi == 0:
                z0_ref[0, :, cols] = zt.astype(BF16)
            else:
                slot = scr.at[2 * t + gi - 1]
                _store_cols(slot, zt)

                def put(c, v, ref=outs[gi], cols=cols):
                    ref[c, :, cols] = v.astype(BF16)
                _split_residues(slot, DILATIONS[gi], put)

    return pl.pallas_call(
        body, name="in_proj_qkv", grid=(S // TM,),
        in_specs=[pl.BlockSpec((TM, D), lambda i: (i, 0)),
                  pl.BlockSpec(w.shape, lambda i: (0, 0), pipeline_mode=pl.Buffered(1))],
        out_specs=[_residue_spec(TM, d, 3 * GW) for d in DILATIONS],
        out_shape=[SDS(_residue_shape(S, d, 3 * GW), BF16) for d in DILATIONS],
        scratch_shapes=[pltpu.VMEM((6, GW // LANES, TM, LANES), F32)],
        compiler_params=_cp(("arbitrary",)),
    )(h, w)


def _bucket_tables():
    a = np.arange(QBLK, dtype=np.int32)[:, None]
    c = np.arange(KBLK, dtype=np.int32)[None, :]
    off = a - c + QBLK
    valid = ((off >= 0) & (off <= SPAN)).astype(np.float32)
    tabs = []
    for dil in DILATIONS:
        dist = np.maximum(off * dil, 0)
        df = np.maximum(dist, 1).astype(np.float32)
        large = 16 + (np.log(df / np.float32(16)) / np.float32(math.log(2048 / 16)) * np.float32(16)).astype(np.int32)
        large = np.minimum(large, 31)
        tabs.append(np.where(dist < 16, dist, large).astype(np.int32))
    return np.stack(tabs), valid


def _bias_expand(tab, buckets, valid):
    def body(tab_ref, b_ref, v_ref, o_ref):
        for gi in range(3):
            bk = b_ref[gi]
            for h in range(NH):
                acc = jnp.zeros((QBLK, KBLK), F32)
                for b in range(32):
                    acc = jnp.where(bk == b, tab_ref[b, gi * NH + h], acc)
                o_ref[gi * NH + h] = jnp.where(v_ref[...] > 0.5, acc, NEG_INF)

    return pl.pallas_call(
        body, name="bias_expand",
        in_specs=[pl.BlockSpec(memory_space=pltpu.SMEM),
                  pl.BlockSpec(memory_space=pltpu.VMEM), pl.BlockSpec(memory_space=pltpu.VMEM)],
        out_specs=pl.BlockSpec(memory_space=pltpu.VMEM),
        out_shape=SDS((3 * NH, QBLK, KBLK), F32),
    )(tab, buckets, valid)


def _bias_reduce(dbias, buckets):
    def body(d_ref, b_ref, o_ref):
        lane = lax.broadcasted_iota(jnp.int32, (1, D), 1)
        for gi in range(3):
            bk = b_ref[gi]
            for h in range(NH):
                dv = d_ref[gi * NH + h]
                row = jnp.zeros((1, D), F32)
                for b in range(32):
                    m = jnp.where(bk == b, dv, 0.0)
                    val = jnp.sum(jnp.sum(m, axis=0, keepdims=True), axis=1, keepdims=True)
                    row = jnp.where(lane == b, val, row)
                o_ref[gi * NH + h:gi * NH + h + 1, :] = row

    return pl.pallas_call(
        body, name="bias_reduce",
        in_specs=[pl.BlockSpec(memory_space=pltpu.VMEM), pl.BlockSpec(memory_space=pltpu.VMEM)],
        out_specs=pl.BlockSpec(memory_space=pltpu.VMEM),
        out_shape=SDS((3 * NH, D), F32),
    )(dbias, buckets)


def _attn_tile(S, dil):
    L = S // dil
    tq = min(512, L)
    return L, tq, tq // QBLK, L // tq


def _pair_stack(ref, rows, lanes, scale=None):
    blk = ref[rows, lanes]
    if scale is not None:
        blk = blk * scale
    lane = lax.broadcasted_iota(jnp.int32, blk.shape, 1)
    zero = jnp.zeros_like(blk)
    return jnp.concatenate([jnp.where(lane < HD, blk, zero), jnp.where(lane >= HD, blk, zero)], axis=0)


def _attn_fwd(zq, bias2, gi):
    dil, L, _ = zq.shape
    _, TQ, QB, ns = _attn_tile(L * dil, dil)
    NP = NH // 2

    def body(q_ref, kc_ref, kp_ref, vc_ref, vp_ref, b_ref, o_ref, l_ref, kext, vext):
        n = pl.program_id(1)
        kext[0:QBLK, :] = kp_ref[0]
        kext[QBLK:, :] = kc_ref[0]
        vext[0:QBLK, :] = vp_ref[0]
        vext[QBLK:, :] = vc_ref[0]
        col = lax.broadcasted_iota(jnp.int32, (2 * QBLK, KBLK), 1)
        no_prev = jnp.logical_and(n == 0, col < QBLK)
        lane = lax.broadcasted_iota(jnp.int32, (QBLK, LANES), 1)
        lanes_of = [slice(hp * LANES, (hp + 1) * LANES) for hp in range(NP)]
        for b in range(QB):
            rows = slice(b * QBLK, (b + 1) * QBLK)
            win = slice(b * QBLK, b * QBLK + KBLK)
            s = [_dot_nt(_pair_stack(q_ref.at[0], rows, pl_, SCALE), kext[win, pl_]) + b_ref[0, hp]
                 for hp, pl_ in enumerate(lanes_of)]
            if b == 0:
                s = [jnp.where(no_prev, NEG_INF, v) for v in s]
            m = [jnp.max(v, axis=-1, keepdims=True) for v in s]
            p = [jnp.exp(v - mv) for v, mv in zip(s, m)]
            l = [jnp.sum(v, axis=-1, keepdims=True) for v in p]
            o2 = [_dot(v.astype(BF16), vext[win, pl_]) / lv for v, lv, pl_ in zip(p, l, lanes_of)]
            for hp, pl_ in enumerate(lanes_of):
                lse2 = jnp.broadcast_to(m[hp] + jnp.log(l[hp]), (2 * QBLK, LANES))
                o_ref[0, rows, pl_] = jnp.where(lane < HD, o2[hp][0:QBLK], o2[hp][QBLK:]).astype(BF16)
                l_ref[0, rows, pl_] = jnp.where(lane < HD, lse2[0:QBLK], lse2[QBLK:])

    def prev(n):
        return jnp.maximum(n * QB - 1, 0)

    return pl.pallas_call(
        body, name=f"attn_fwd_g{gi}", grid=(dil, ns),
        in_specs=[pl.BlockSpec((1, TQ, GW), lambda c, n: (c, n, 0)),
                  pl.BlockSpec((1, TQ, GW), lambda c, n: (c, n, 1)),
                  pl.BlockSpec((1, QBLK, GW), lambda c, n: (c, prev(n), 1)),
                  pl.BlockSpec((1, TQ, GW), lambda c, n: (c, n, 2)),
                  pl.BlockSpec((1, QBLK, GW), lambda c, n: (c, prev(n), 2)),
                  pl.BlockSpec((1, NP, 2 * QBLK, KBLK), lambda c, n: (gi, 0, 0, 0))],
        out_specs=[pl.BlockSpec((1, TQ, GW), lambda c, n: (c, n, 0)),
                   pl.BlockSpec((1, TQ, GW), lambda c, n: (c, n, 0))],
        out_shape=[SDS((dil, L, GW), BF16), SDS((dil, L, GW), F32)],
        scratch_shapes=[pltpu.VMEM((QBLK + TQ, GW), BF16), pltpu.VMEM((QBLK + TQ, GW), BF16)],
        compiler_params=_cp(("arbitrary", "arbitrary")),
    )(zq, zq, zq, zq, zq, bias2)


CONV_TM = 256
SHIFT_PAD = 24


def _make_shifts(src, sh, n):
    for b in range(1, 8):
        sh[b - 1] = src[b:b + n + SHIFT_PAD, :]


def _shifted(src, sh, off, r0, n, lanes):
    a, b = divmod(off, 8)
    if b == 0:
        return src[8 * a + r0:8 * a + r0 + n, lanes]
    return sh[b - 1, 8 * a + r0:8 * a + r0 + n, lanes]


CONV_RC = 64


def _tap_blocks(TM):
    return [(r0, slice(l0, l0 + LANES)) for l0 in range(0, D, LANES) for r0 in range(0, TM, CONV_RC)]


def _conv_fwd(zrest, b_glu, wdw, b_dw, g_ln, b_ln):
    S = zrest.shape[0]
    TM = CONV_TM
    HALO = 32
    hb = TM // HALO

    def body(u_ref, g_ref, uh_ref, gh_ref, bg_ref, w_ref, bd_ref, gl_ref, bl_ref, cv_ref, a_ref, ext, sh):
        i = pl.program_id(0)
        bu = bg_ref[:, 0:D]
        bgt = bg_ref[:, D:2 * D]
        uh = (uh_ref[...].astype(F32) + bu) * _sigmoid(gh_ref[...].astype(F32) + bgt)
        ext[0:HALO, :] = jnp.where(i == 0, 0.0, uh)
        ext[HALO:, :] = (u_ref[...].astype(F32) + bu) * _sigmoid(g_ref[...].astype(F32) + bgt)
        _make_shifts(ext, sh, TM)
        acc = jnp.zeros((TM, D), F32)
        for j in range(CONV_W):
            acc = acc + _shifted(ext, sh, HALO - (CONV_W - 1) + j, 0, TM, slice(None)) * w_ref[j:j + 1, :]
        cv = (acc + bd_ref[...]).astype(BF16)
        cv_ref[...] = cv
        cf = cv.astype(F32)
        mu = jnp.mean(cf, axis=-1, keepdims=True)
        xc = cf - mu
        y = xc * lax.rsqrt(jnp.mean(xc * xc, axis=-1, keepdims=True) + LN_EPS) * gl_ref[...] + bl_ref[...]
        a_ref[...] = (y * _sigmoid(y)).astype(BF16)

    vec = pl.BlockSpec((1, D), lambda i: (0, 0))
    return pl.pallas_call(
        body, name="conv_fwd", grid=(S // TM,),
        in_specs=[pl.BlockSpec((TM, D), lambda i: (i, 0)), pl.BlockSpec((TM, D), lambda i: (i, 1)),
                  pl.BlockSpec((HALO, D), lambda i: (jnp.maximum(i * hb - 1, 0), 0)),
                  pl.BlockSpec((HALO, D), lambda i: (jnp.maximum(i * hb - 1, 0), 1)),
                  pl.BlockSpec((1, 2 * D), lambda i: (0, 0)),
                  pl.BlockSpec((32, D), lambda i: (0, 0)), vec, vec, vec],
        out_specs=[pl.BlockSpec((TM, D), lambda i: (i, 0)), pl.BlockSpec((TM, D), lambda i: (i, 0))],
        out_shape=[SDS((S, D), BF16), SDS((S, D), BF16)],
        scratch_shapes=[pltpu.VMEM((HALO + TM, D), F32), pltpu.VMEM((7, TM + SHIFT_PAD, D), F32)],
        compiler_params=_cp(("arbitrary",)),
    )(zrest, zrest, zrest, zrest, b_glu, wdw, b_dw, g_ln, b_ln)


def _mix_fwd(og, lg, a, zrest, x, w_ao, w_co, b_co, w_mx, g_pm):
    S = x.shape[0]
    TM = 512

    def body(o0, o1, o2, l0, l1, l2, a_ref, ga_ref, gc_ref, x_ref, wa_ref, wc_ref, bc_ref, wm_ref, g_ref,
             o_ref, oa_ref, ob_ref, lse_ref, lsea_ref, lseb_ref, ya_ref, yc_ref, mg_ref, mm_ref, x1_ref,
             so1, so2, sl1, sl2, so, sl):
        for dil, src, dst, cast in ((DILATIONS[1], o1, so1, True), (DILATIONS[2], o2, so2, True),
                                    (DILATIONS[1], l1, sl1, False), (DILATIONS[2], l2, sl2, False)):
            _merge_residues(dst, dil, (lambda c, src=src: src[c].astype(F32)) if cast else (lambda c, src=src: src[c]))
        la, lb, lc = l0[0], _load_cols(sl1), _load_cols(sl2)
        m = jnp.maximum(jnp.maximum(la, lb), lc)
        e0 = jnp.exp(la - m)
        e1 = jnp.exp(lb - m)
        e2 = jnp.exp(lc - m)
        den = e0 + e1 + e2
        of = (e0 * o0[0].astype(F32) + e1 * _load_cols(so1) + e2 * _load_cols(so2)) / den
        o = of.astype(BF16)
        o_ref[...] = o
        lse = m + jnp.log(den)
        lse_ref[...] = lse
        _store_cols(so, of)
        _store_cols(sl, lse)
        for dil, oref, lref in ((DILATIONS[1], oa_ref, lsea_ref), (DILATIONS[2], ob_ref, lseb_ref)):
            def put_o(c, v, oref=oref):
                oref[c] = v.astype(BF16)

            def put_l(c, v, lref=lref):
                lref[c] = v
            _split_residues(so, dil, put_o)
            _split_residues(sl, dil, put_l)
        ya = _dot(o, wa_ref[...]).astype(BF16)
        yc = (_dot(a_ref[...], wc_ref[...]) + bc_ref[...]).astype(BF16)
        ya_ref[...] = ya
        yc_ref[...] = yc
        mg = (_sigmoid(ga_ref[...].astype(F32)) * ya.astype(F32)
              + _sigmoid(gc_ref[...].astype(F32)) * yc.astype(F32)).astype(BF16)
        mg_ref[...] = mg
        mm = _dot(mg, wm_ref[...]).astype(BF16)
        mm_ref[...] = mm
        mf = mm.astype(F32)
        r = lax.rsqrt(jnp.mean(mf * mf, axis=-1, keepdims=True) + RMS_EPS)
        x1_ref[...] = x_ref[...] + mf * r * g_ref[...]

    row512 = pl.BlockSpec((TM, GW), lambda i: (i, 0))
    rowd = pl.BlockSpec((TM, D), lambda i: (i, 0))
    vec = pl.BlockSpec((1, D), lambda i: (0, 0))
    full = lambda r, c: pl.BlockSpec((r, c), lambda i: (0, 0))
    res = [_residue_spec(TM, d, GW) for d in DILATIONS]
    rshape = lambda d, t: SDS(_residue_shape(S, d, GW), t)
    scr = _col_scratch(TM, GW)
    return pl.pallas_call(
        body, name="mix_fwd", grid=(S // TM,),
        in_specs=res + res + [rowd, pl.BlockSpec((TM, D), lambda i: (i, 2)), pl.BlockSpec((TM, D), lambda i: (i, 3)),
                              rowd, full(GW, D), full(D, D), vec, full(D, D), vec],
        out_specs=[row512, res[1], res[2], row512, res[1], res[2], rowd, rowd, rowd, rowd, rowd],
        out_shape=[SDS((S, GW), BF16), rshape(DILATIONS[1], BF16), rshape(DILATIONS[2], BF16),
                   SDS((S, GW), F32), rshape(DILATIONS[1], F32), rshape(DILATIONS[2], F32),
                   SDS((S, D), BF16), SDS((S, D), BF16), SDS((S, D), BF16), SDS((S, D), BF16), SDS((S, D), F32)],
        scratch_shapes=[scr] * 6,
        compiler_params=_cp(("arbitrary",)),
    )(og[0], og[1], og[2], lg[0], lg[1], lg[2], a, zrest, zrest, x, w_ao, w_co, b_co, w_mx, g_pm)


def _ffn_fwd(x1, tgt, g_pre, g_post, w_fi, w_fo):
    S = x1.shape[0]
    TM = 512

    def body(x1_ref, t_ref, gp_ref, go_ref, wi_ref, wo_ref,
             h2_ref, gu_ref, df_ref, dx2_ref, loss_ref, dgo_ref):
        i = pl.program_id(0)

        @pl.when(i == 0)
        def _():
            loss_ref[...] = jnp.zeros_like(loss_ref)
            dgo_ref[...] = jnp.zeros_like(dgo_ref)

        xf = x1_ref[...]
        r = lax.rsqrt(jnp.mean(xf * xf, axis=-1, keepdims=True) + RMS_EPS)
        h2_ref[...] = (xf * r * gp_ref[...]).astype(BF16)
        for k in range(NFT):
            gu_ref[:, 2 * k * FFN_T:(2 * k + 1) * FFN_T] = _dot(
                h2_ref[...], wi_ref[:, k * FFN_T:(k + 1) * FFN_T]).astype(BF16)
            gu_ref[:, (2 * k + 1) * FFN_T:(2 * k + 2) * FFN_T] = _dot(
                h2_ref[...], wi_ref[:, FFN_H + k * FFN_T:FFN_H + (k + 1) * FFN_T]).astype(BF16)
        f = jnp.zeros((TM, D), F32)
        for k in range(NFT):
            gf = gu_ref[:, 2 * k * FFN_T:(2 * k + 1) * FFN_T].astype(F32)
            uf = gu_ref[:, (2 * k + 1) * FFN_T:(2 * k + 2) * FFN_T].astype(F32)
            act = (gf * _sigmoid(gf) * uf).astype(BF16)
            f = f + _dot(act, wo_ref[k * FFN_T:(k + 1) * FFN_T, :])
        r = lax.rsqrt(jnp.mean(f * f, axis=-1, keepdims=True) + RMS_EPS)
        nrm = f * r
        e = x1_ref[...] + nrm * go_ref[...] - t_ref[...]
        tot = jnp.sum(jnp.sum(e * e, axis=-1, keepdims=True), axis=0, keepdims=True) * (0.5 / D)
        corner = jnp.logical_and(lax.broadcasted_iota(jnp.int32, (LANE_ROWS, D), 0) == 0,
                                 lax.broadcasted_iota(jnp.int32, (LANE_ROWS, D), 1) == 0)
        loss_ref[...] += jnp.where(corner, tot, 0.0)
        dx2 = e * (1.0 / D)
        dx2_ref[...] = dx2
        dgo_ref[...] += _colsum8(dx2 * nrm)
        dn = dx2 * go_ref[...]
        df_ref[...] = (r * (dn - nrm * jnp.mean(dn * nrm, axis=-1, keepdims=True))).astype(BF16)

    rowd = pl.BlockSpec((TM, D), lambda i: (i, 0))
    vec = pl.BlockSpec((1, D), lambda i: (0, 0))
    acc8 = pl.BlockSpec((LANE_ROWS, D), lambda i: (0, 0))
    return pl.pallas_call(
        body, name="ffn_fwd", grid=(S // TM,),
        in_specs=[rowd, rowd, vec, vec,
                  pl.BlockSpec((D, 2 * FFN_H), lambda i: (0, 0), pipeline_mode=pl.Buffered(1)),
                  pl.BlockSpec((FFN_H, D), lambda i: (0, 0), pipeline_mode=pl.Buffered(1))],
        out_specs=[rowd, pl.BlockSpec((TM, 2 * FFN_H), lambda i: (i, 0)), rowd, rowd, acc8, acc8],
        out_shape=[SDS((S, D), BF16), SDS((S, 2 * FFN_H), BF16), SDS((S, D), BF16), SDS((S, D), F32),
                   SDS((LANE_ROWS, D), F32), SDS((LANE_ROWS, D), F32)],
        compiler_params=_cp(("arbitrary",)),
    )(x1, tgt, g_pre, g_post, w_fi, w_fo)


def _ffn_bwd_act(df, gu, w_foT):
    S = df.shape[0]
    TM = 512

    def body_act(df_ref, gu_ref, wo_ref, dff_ref, act_ref):
        dacts = [_dot(df_ref[...], wo_ref[:, k * FFN_T:(k + 1) * FFN_T]) for k in range(NFT)]
        for k in range(NFT):
            dact = dacts[k]
            g = gu_ref[:, 2 * k * FFN_T:(2 * k + 1) * FFN_T].astype(F32)
            u = gu_ref[:, (2 * k + 1) * FFN_T:(2 * k + 2) * FFN_T].astype(F32)
            sg = _sigmoid(g)
            sl = g * sg
            act_ref[:, k * FFN_T:(k + 1) * FFN_T] = (sl * u).astype(BF16)
            dff_ref[:, 2 * k * FFN_T:(2 * k + 1) * FFN_T] = (dact * u * (sg * (1.0 + g * (1.0 - sg)))).astype(BF16)
            dff_ref[:, (2 * k + 1) * FFN_T:(2 * k + 2) * FFN_T] = (dact * sl).astype(BF16)

    rowd = pl.BlockSpec((TM, D), lambda i: (i, 0))
    wide = pl.BlockSpec((TM, 2 * FFN_H), lambda i: (i, 0))
    return pl.pallas_call(
        body_act, name="ffn_bwd_act", grid=(S // TM,),
        in_specs=[rowd, wide, pl.BlockSpec((D, FFN_H), lambda i: (0, 0), pipeline_mode=pl.Buffered(1))],
        out_specs=[wide, pl.BlockSpec((TM, FFN_H), lambda i: (i, 0))],
        out_shape=[SDS((S, 2 * FFN_H), BF16), SDS((S, FFN_H), BF16)],
        compiler_params=_cp(("arbitrary",)),
    )(df, gu, w_foT)


def _ffn_bwd_in(dff, x1, dx2, g_pre, w_fiT, rider):
    S = x1.shape[0]
    TM = 512
    rowd = pl.BlockSpec((TM, D), lambda i: (i, 0))
    wide = pl.BlockSpec((TM, 2 * FFN_H), lambda i: (i, 0))
    KC = 512
    nkc = 2 * FFN_H // KC

    def body_in(dff_ref, x1_ref, dx2_ref, gp_ref, wi_ref, dx1_ref, dgp_ref):
        i = pl.program_id(0)

        @pl.when(i == 0)
        def _():
            dgp_ref[...] = jnp.zeros_like(dgp_ref)

        dh = jnp.zeros((TM, D), F32)
        for k in range(nkc):
            dh = dh + _dot(dff_ref[:, k * KC:k * KC + FFN_T], wi_ref[k * FFN_T:(k + 1) * FFN_T, :]) \
                + _dot(dff_ref[:, k * KC + FFN_T:(k + 1) * KC], wi_ref[FFN_H + k * FFN_T:FFN_H + (k + 1) * FFN_T, :])
        xf = x1_ref[...]
        r = lax.rsqrt(jnp.mean(xf * xf, axis=-1, keepdims=True) + RMS_EPS)
        nrm = xf * r
        dgp_ref[...] += _colsum8(dh * nrm)
        dn = dh * gp_ref[...]
        dx1_ref[...] = dx2_ref[...] + r * (dn - nrm * jnp.mean(dn * nrm, axis=-1, keepdims=True))

    body_in, r_in, r_out, r_shape, r_scr = _ride(body_in, 5, 2, 0, rider, S // TM)
    return pl.pallas_call(
        body_in, name="ffn_bwd_in", grid=(S // TM,),
        in_specs=[wide, rowd, rowd, pl.BlockSpec((1, D), lambda i: (0, 0)),
                  pl.BlockSpec((2 * FFN_H, D), lambda i: (0, 0), pipeline_mode=pl.Buffered(1))] + r_in,
        out_specs=[rowd, pl.BlockSpec((LANE_ROWS, D), lambda i: (0, 0))] + r_out,
        out_shape=[SDS((S, D), F32), SDS((LANE_ROWS, D), F32)] + r_shape,
        scratch_shapes=r_scr,
        compiler_params=_cp(("arbitrary",)),
    )(dff, x1, dx2, g_pre, w_fiT, *rider.ins)


def _mix_bwd(dx1, mm, ya, yc, zrest, g_pm, w_mxT, w_aoT, w_coT):
    S = dx1.shape[0]
    TM = 512

    def body(dx_ref, mm_ref, ya_ref, yc_ref, ga_ref, gc_ref, g_ref, wm_ref, wa_ref, wc_ref,
             dmm_ref, dya_ref, dyc_ref, do_ref, doa_ref, dob_ref, da_ref, dzg_ref, dgpm_ref, dbco_ref, sdo):
        i = pl.program_id(0)

        @pl.when(i == 0)
        def _():
            dgpm_ref[...] = jnp.zeros_like(dgpm_ref)
            dbco_ref[...] = jnp.zeros_like(dbco_ref)

        mf = mm_ref[...].astype(F32)
        r = lax.rsqrt(jnp.mean(mf * mf, axis=-1, keepdims=True) + RMS_EPS)
        nrm = mf * r
        dx = dx_ref[...]
        dgpm_ref[...] += _colsum8(dx * nrm)
        dn = dx * g_ref[...]
        dmm = (r * (dn - nrm * jnp.mean(dn * nrm, axis=-1, keepdims=True))).astype(BF16)
        dmm_ref[...] = dmm
        dmg = _dot(dmm, wm_ref[...])
        sa = _sigmoid(ga_ref[...].astype(F32))
        sc = _sigmoid(gc_ref[...].astype(F32))
        dya = (dmg * sa).astype(BF16)
        dyc = (dmg * sc).astype(BF16)
        dya_ref[...] = dya
        dyc_ref[...] = dyc
        dbco_ref[...] += _colsum8(dyc.astype(F32))
        dzg_ref[:, 0:D] = (dmg * ya_ref[...].astype(F32) * (sa * (1.0 - sa))).astype(BF16)
        dzg_ref[:, D:] = (dmg * yc_ref[...].astype(F32) * (sc * (1.0 - sc))).astype(BF16)
        dof = _dot(dya, wa_ref[...])
        do_ref[...] = dof.astype(BF16)
        _store_cols(sdo, dof)
        for dil, ref in ((DILATIONS[1], doa_ref), (DILATIONS[2], dob_ref)):
            def put(c, v, ref=ref):
                ref[c] = v.astype(BF16)
            _split_residues(sdo, dil, put)
        da_ref[...] = _dot(dyc, wc_ref[...]).astype(BF16)

    rowd = pl.BlockSpec((TM, D), lambda i: (i, 0))
    full = lambda r, c: pl.BlockSpec((r, c), lambda i: (0, 0))
    acc8 = pl.BlockSpec((LANE_ROWS, D), lambda i: (0, 0))
    return pl.pallas_call(
        body, name="mix_bwd", grid=(S // TM,),
        in_specs=[rowd, rowd, rowd, rowd, pl.BlockSpec((TM, D), lambda i: (i, 2)),
                  pl.BlockSpec((TM, D), lambda i: (i, 3)), full(1, D), full(D, D), full(D, GW), full(D, D)],
        out_specs=[rowd, rowd, rowd, pl.BlockSpec((TM, GW), lambda i: (i, 0)),
                   _residue_spec(TM, DILATIONS[1], GW), _residue_spec(TM, DILATIONS[2], GW), rowd,
                   pl.BlockSpec((TM, 2 * D), lambda i: (i, 0)), acc8, acc8],
        out_shape=[SDS((S, D), BF16), SDS((S, D), BF16), SDS((S, D), BF16), SDS((S, GW), BF16),
                   SDS(_residue_shape(S, DILATIONS[1], GW), BF16), SDS(_residue_shape(S, DILATIONS[2], GW), BF16),
                   SDS((S, D), BF16), SDS((S, 2 * D), BF16), SDS((LANE_ROWS, D), F32), SDS((LANE_ROWS, D), F32)],
        scratch_shapes=[_col_scratch(TM, GW)],
        compiler_params=_cp(("arbitrary",)),
    )(dx1, mm, ya, yc, zrest, zrest, g_pm, w_mxT, w_aoT, w_coT)


def _conv_bwd(da, cv, zrest, b_glu, wdw, g_ln, b_ln):
    S = da.shape[0]
    TM = CONV_TM
    HALO = 32
    hb = TM // HALO
    nh = S // HALO

    def body(da_ref, dan_ref, cv_ref, cvn_ref, u_ref, g_ref, uh_ref, gh_ref, bg_ref, w_ref, gl_ref, bl_ref,
             dglu_ref, dbu_ref, dbg_ref, dw_ref, dgl_ref, dbl_ref, dbd_ref, dext, uext, dsh, ush, du_scr, dw8):
        i = pl.program_id(0)
        last = i == pl.num_programs(0) - 1

        @pl.when(i == 0)
        def _():
            for ref in (dbu_ref, dbg_ref, dw8, dgl_ref, dbl_ref, dbd_ref):
                ref[...] = jnp.zeros_like(ref)

        def ln_bwd(da_v, cv_v):
            cf = cv_v.astype(F32)
            mu = jnp.mean(cf, axis=-1, keepdims=True)
            xc = cf - mu
            rstd = lax.rsqrt(jnp.mean(xc * xc, axis=-1, keepdims=True) + LN_EPS)
            xh = xc * rstd
            y = xh * gl_ref[...] + bl_ref[...]
            sy = _sigmoid(y)
            dy = da_v.astype(F32) * (sy * (1.0 + y * (1.0 - sy)))
            dxh = dy * gl_ref[...]
            dcv = rstd * (dxh - jnp.mean(dxh, axis=-1, keepdims=True)
                          - xh * jnp.mean(dxh * xh, axis=-1, keepdims=True))
            return dcv, dy, xh

        dcv, dy, xh = ln_bwd(da_ref[...], cv_ref[...])
        dgl_ref[...] += _colsum8(dy * xh)
        dbl_ref[...] += _colsum8(dy)
        dbd_ref[...] += _colsum8(dcv)
        dcvn, _, _ = ln_bwd(dan_ref[...], cvn_ref[...])
        dext[0:TM, :] = dcv
        dext[TM:, :] = jnp.where(last, 0.0, dcvn)

        bu = bg_ref[:, 0:D]
        bgt = bg_ref[:, D:2 * D]
        upre = u_ref[...].astype(F32) + bu
        sg = _sigmoid(g_ref[...].astype(F32) + bgt)
        uh = (uh_ref[...].astype(F32) + bu) * _sigmoid(gh_ref[...].astype(F32) + bgt)
        uext[0:HALO, :] = jnp.where(i == 0, 0.0, uh)
        uext[HALO:, :] = upre * sg

        _make_shifts(dext, dsh, TM)
        _make_shifts(uext, ush, TM)
        for r0, lanes in _tap_blocks(TM):
            acc = jnp.zeros((CONV_RC, LANES), F32)
            for j in range(CONV_W):
                acc = acc + _shifted(dext, dsh, CONV_W - 1 - j, r0, CONV_RC, lanes) * w_ref[j:j + 1, lanes]
            du_scr[r0:r0 + CONV_RC, lanes] = acc
        for l0 in range(0, D, LANES):
            lanes = slice(l0, l0 + LANES)
            accs = [jnp.zeros((LANE_ROWS, LANES), F32)] * CONV_W
            for r0 in range(0, TM, CONV_RC):
                dc = dext[r0:r0 + CONV_RC, lanes]
                for j in range(CONV_W):
                    prod = dc * _shifted(uext, ush, HALO - (CONV_W - 1) + j, r0, CONV_RC, lanes)
                    accs[j] = accs[j] + jnp.sum(prod.reshape(CONV_RC // LANE_ROWS, LANE_ROWS, LANES), axis=0)
            for j in range(CONV_W):
                dw8[j, :, lanes] += accs[j]

        @pl.when(last)
        def _():
            for j in range(CONV_W):
                dw_ref[j:j + 1, :] = jnp.sum(dw8[j], axis=0, keepdims=True)
            dw_ref[CONV_W:, :] = jnp.zeros((32 - CONV_W, D), F32)

        du = du_scr[...]
        dup = du * sg
        dgp = du * upre * (sg * (1.0 - sg))
        dglu_ref[:, 0:D] = dup.astype(BF16)
        dglu_ref[:, D:] = dgp.astype(BF16)
        dbu_ref[...] += _colsum8(dup.astype(BF16).astype(F32))
        dbg_ref[...] += _colsum8(dgp.astype(BF16).astype(F32))

    rowd = pl.BlockSpec((TM, D), lambda i: (i, 0))
    nxt = pl.BlockSpec((HALO, D), lambda i: (jnp.minimum((i + 1) * hb, nh - 1), 0))
    vec = pl.BlockSpec((1, D), lambda i: (0, 0))
    acc8 = pl.BlockSpec((LANE_ROWS, D), lambda i: (0, 0))
    return pl.pallas_call(
        body, name="conv_bwd", grid=(S // TM,),
        in_specs=[rowd, nxt, rowd, nxt,
                  pl.BlockSpec((TM, D), lambda i: (i, 0)), pl.BlockSpec((TM, D), lambda i: (i, 1)),
                  pl.BlockSpec((HALO, D), lambda i: (jnp.maximum(i * hb - 1, 0), 0)),
                  pl.BlockSpec((HALO, D), lambda i: (jnp.maximum(i * hb - 1, 0), 1)),
                  pl.BlockSpec((1, 2 * D), lambda i: (0, 0)), pl.BlockSpec((32, D), lambda i: (0, 0)), vec, vec],
        out_specs=[pl.BlockSpec((TM, 2 * D), lambda i: (i, 0)), acc8, acc8,
                   pl.BlockSpec((32, D), lambda i: (0, 0)), acc8, acc8, acc8],
        out_shape=[SDS((S, 2 * D), BF16), SDS((LANE_ROWS, D), F32), SDS((LANE_ROWS, D), F32), SDS((32, D), F32),
                   SDS((LANE_ROWS, D), F32), SDS((LANE_ROWS, D), F32), SDS((LANE_ROWS, D), F32)],
        scratch_shapes=[pltpu.VMEM((TM + HALO, D), F32), pltpu.VMEM((HALO + TM, D), F32),
                        pltpu.VMEM((7, TM + SHIFT_PAD, D), F32), pltpu.VMEM((7, TM + SHIFT_PAD, D), F32),
                        pltpu.VMEM((TM, D), F32), pltpu.VMEM((32, LANE_ROWS, D), F32)],
        compiler_params=_cp(("arbitrary",)),
    )(da, da, cv, cv, zrest, zrest, zrest, zrest, b_glu, wdw, g_ln, b_ln)


def _attn_bwd(zq, do, o, lse, bias_t, gi):
    dil, L, _ = zq.shape
    _, TQ, QB, ns = _attn_tile(L * dil, dil)
    NP = NH // 2

    def body(q3, kc3, kp3, vc3, vp3, do3, o3, l3, b_ref,
             out3, db_ref, kext, vext, dkx, dvx, dqn, dqc, dkc, dvc):
        q_ref, kc_ref, kp_ref, vc_ref, vp_ref, do_ref, o_ref, l_ref, out_ref = (
            r.at[0] for r in (q3, kc3, kp3, vc3, vp3, do3, o3, l3, out3))
        t = pl.program_id(0)
        n = lax.rem(cur(t), ns)

        @pl.when(t == 0)
        def _():
            db_ref[...] = jnp.zeros_like(db_ref)

        @pl.when(t < T - 1)
        def _():
            kext[0:QBLK, :] = kp_ref[...]
            kext[QBLK:, :] = kc_ref[...]
            vext[0:QBLK, :] = vp_ref[...]
            vext[QBLK:, :] = vc_ref[...]
            krow = lax.broadcasted_iota(jnp.int32, (KBLK, 2 * QBLK), 0)
            no_prev = jnp.logical_and(n == 0, krow < QBLK)
            lane = lax.broadcasted_iota(jnp.int32, (QBLK, LANES), 1)

            def overlap_add(parts):
                segs = [parts[0][0:QBLK]]
                for b in range(1, QB):
                    segs.append(parts[b - 1][QBLK:] + parts[b][0:QBLK])
                segs.append(parts[QB - 1][QBLK:])
                return jnp.concatenate(segs, axis=0)

            lanes_of = [slice(hp * LANES, (hp + 1) * LANES) for hp in range(NP)]
            dv_parts = [[] for _ in range(NP)]
            dk_parts = [[] for _ in range(NP)]
            dbsum = [None] * NP
            for b in range(QB):
                rows = slice(b * QBLK, (b + 1) * QBLK)
                win = slice(b * QBLK, b * QBLK + KBLK)
                q2 = [_pair_stack(q_ref, rows, pl_, SCALE) for pl_ in lanes_of]
                do2 = [_pair_stack(do_ref, rows, pl_) for pl_ in lanes_of]
                st = [_dot_nt(kext[win, pl_], q2[hp]) + b_ref[0, hp] for hp, pl_ in enumerate(lanes_of)]
                dpt = [_dot_nt(vext[win, pl_], do2[hp]) for hp, pl_ in enumerate(lanes_of)]
                lse_t = l_ref[rows, :].T
                prod_t = (do_ref[rows, :].astype(F32) * o_ref[rows, :].astype(F32)).T
                dst = []
                for hp in range(NP):
                    lo = hp * LANES
                    lse_row = jnp.concatenate([lse_t[lo:lo + 1], lse_t[lo + HD:lo + HD + 1]], axis=1)
                    delta_row = jnp.concatenate([jnp.sum(prod_t[lo:lo + HD], axis=0, keepdims=True),
                                                 jnp.sum(prod_t[lo + HD:lo + LANES], axis=0, keepdims=True)], axis=1)
                    s_hp = jnp.where(no_prev, NEG_INF, st[hp]) if b == 0 else st[hp]
                    pt = jnp.exp(s_hp - lse_row)
                    d = pt * (dpt[hp] - delta_row)
                    dbsum[hp] = d if dbsum[hp] is None else dbsum[hp] + d
                    dst.append(d.astype(BF16))
                    dv_parts[hp].append(_dot(pt.astype(BF16), do2[hp]))
                for hp, pl_ in enumerate(lanes_of):
                    dk_parts[hp].append(_dot(dst[hp], q2[hp]))
                    dq2 = _dot_tn(dst[hp], kext[win, pl_])
                    dqn[rows, pl_] = jnp.where(lane < HD, dq2[0:QBLK], dq2[QBLK:]) * SCALE
            for hp, pl_ in enumerate(lanes_of):
                db_ref[hp] += dbsum[hp]
                dvx[:, pl_] = overlap_add(dv_parts[hp])
                dkx[:, pl_] = overlap_add(dk_parts[hp])

        @pl.when(t > 0)
        def _():
            out_ref[:, 0:GW] = dqc[...].astype(BF16)
            out_ref[:, GW:2 * GW] = dkc[...].astype(BF16)
            out_ref[:, 2 * GW:] = dvc[...].astype(BF16)

        @pl.when(jnp.logical_and(t > 0, t < T - 1))
        def _():
            out_ref[TQ - QBLK:, GW:2 * GW] = (dkc[TQ - QBLK:, :] + dkx[0:QBLK, :]).astype(BF16)
            out_ref[TQ - QBLK:, 2 * GW:] = (dvc[TQ - QBLK:, :] + dvx[0:QBLK, :]).astype(BF16)

        @pl.when(t < T - 1)
        def _():
            dqc[...] = dqn[...]
            dkc[...] = dkx[QBLK:, :]
            dvc[...] = dvx[QBLK:, :]

    T = dil * ns + 1

    def cur(t):
        return jnp.minimum(t, T - 2)

    def blk(t, col):
        return (lax.div(cur(t), ns), lax.rem(cur(t), ns), col)

    def prev(t, col):
        return (lax.div(cur(t), ns), jnp.maximum(lax.rem(cur(t), ns) * QB - 1, 0), col)

    def late(t):
        tp = jnp.maximum(t - 1, 0)
        return (lax.div(tp, ns), lax.rem(tp, ns), 0)

    rows = lambda t: blk(t, 0)
    return pl.pallas_call(
        body, name=f"attn_bwd_g{gi}", grid=(T,),
        in_specs=[pl.BlockSpec((1, TQ, GW), lambda t: blk(t, 0)),
                  pl.BlockSpec((1, TQ, GW), lambda t: blk(t, 1)),
                  pl.BlockSpec((1, QBLK, GW), lambda t: prev(t, 1)),
                  pl.BlockSpec((1, TQ, GW), lambda t: blk(t, 2)),
                  pl.BlockSpec((1, QBLK, GW), lambda t: prev(t, 2)),
                  pl.BlockSpec((1, TQ, GW), rows), pl.BlockSpec((1, TQ, GW), rows), pl.BlockSpec((1, TQ, GW), rows),
                  pl.BlockSpec((1, NP, KBLK, 2 * QBLK), lambda t: (gi, 0, 0, 0))],
        out_specs=[pl.BlockSpec((1, TQ, 3 * GW), late),
                   pl.BlockSpec((NP, KBLK, 2 * QBLK), lambda t: (0, 0, 0))],
        out_shape=[SDS((dil, L, 3 * GW), BF16), SDS((NP, KBLK, 2 * QBLK), F32)],
        scratch_shapes=[pltpu.VMEM((QBLK + TQ, GW), BF16), pltpu.VMEM((QBLK + TQ, GW), BF16),
                        pltpu.VMEM((QBLK + TQ, GW), F32), pltpu.VMEM((QBLK + TQ, GW), F32),
                        pltpu.VMEM((TQ, GW), F32), pltpu.VMEM((TQ, GW), F32),
                        pltpu.VMEM((TQ, GW), F32), pltpu.VMEM((TQ, GW), F32)],
        compiler_params=_cp(("arbitrary",)),
    )(zq, zq, zq, zq, zq, do, o, lse, bias_t)


def _dz_block(k):
    if k < 9:
        return k % 3, k // 3
    if k < 13:
        return 3, k - 9
    return 4, k - 13


_DZ_SRC = np.array([_dz_block(k)[0] for k in range(17)], np.int32)


def _dz_hold(s):
    uses = [(k, _dz_block(k)[1]) for k in range(17) if _dz_block(k)[0] == s]
    hold = []
    for k in range(17):
        nxt = [b for kk, b in uses if kk >= k]
        hold.append(nxt[0] if nxt else uses[-1][1])
    return np.array(hold, np.int32)


def _table(tab, k):
    out = jnp.int32(int(tab[0]))
    for idx in range(1, len(tab)):
        out = jnp.where(k == idx, jnp.int32(int(tab[idx])), out)
    return out


def _w_in_tile(s, blk):
    return blk * 3 + s if s < 3 else (9 if s == 3 else 13) + blk


def _in_bwd(dqkv, dglu, dzg, w_inT, x, dx1, g, rider):
    S = x.shape[0]
    TM = 512

    def body(d0, d1, d2, d3, d4, w_ref, x_ref, dx1_ref, g_ref, gx_ref, dg_ref, scr):
        i = pl.program_id(0)

        @pl.when(i == 0)
        def _():
            dg_ref[...] = jnp.zeros_like(dg_ref)

        def rows(s, blk):
            k = _w_in_tile(s, blk)
            return w_ref[k * GW:(k + 1) * GW, :]

        dh = jnp.zeros((TM, D), F32)
        for blk in range(3):
            dh = dh + _dot(d0[0, :, blk * GW:(blk + 1) * GW], rows(0, blk))
        for s, ref in ((3, d3), (4, d4)):
            for blk in range(4):
                dh = dh + _dot(ref[:, blk * GW:(blk + 1) * GW], rows(s, blk))
        for s, ref in ((1, d1), (2, d2)):
            dil = DILATIONS[s]
            part = jnp.zeros((TM, D), F32)
            for blk in range(3):
                part = part + _dot(ref[:, :, blk * GW:(blk + 1) * GW].reshape(TM, GW), rows(s, blk))
            _merge_residues(scr, dil, lambda c, part=part, dil=dil: part[c * (TM // dil):(c + 1) * (TM // dil)])
            dh = dh + _load_cols(scr)
        xf = x_ref[...]
        r = lax.rsqrt(jnp.mean(xf * xf, axis=-1, keepdims=True) + RMS_EPS)
        nrm = xf * r
        dg_ref[...] += _colsum8(dh * nrm)
        dn = dh * g_ref[...]
        gx_ref[...] = dx1_ref[...] + r * (dn - nrm * jnp.mean(dn * nrm, axis=-1, keepdims=True))

    rowd = pl.BlockSpec((TM, D), lambda i: (i, 0))
    wide = pl.BlockSpec((TM, 2 * D), lambda i: (i, 0))
    body, r_in, r_out, r_shape, r_scr = _ride(body, 9, 2, 1, rider, S // TM)
    return pl.pallas_call(
        body, name="in_bwd", grid=(S // TM,),
        in_specs=[_residue_spec(TM, d, 3 * GW) for d in DILATIONS] + [wide, wide]
        + [pl.BlockSpec(w_inT.shape, lambda i: (0, 0), pipeline_mode=pl.Buffered(1)), rowd, rowd,
           pl.BlockSpec((1, D), lambda i: (0, 0))] + r_in,
        out_specs=[rowd, pl.BlockSpec((LANE_ROWS, D), lambda i: (0, 0))] + r_out,
        out_shape=[SDS((S, D), F32), SDS((LANE_ROWS, D), F32)] + r_shape,
        scratch_shapes=[_col_scratch(TM, D)] + r_scr,
        compiler_params=_cp(("arbitrary",)),
    )(*dqkv, dglu, dzg, w_inT, x, dx1, g, *rider.ins)


def _dw_in(dqkv, dglu, dzg, hs):
    S = hs[0].shape[0]
    TS = min(2048, S)
    nk = 17
    holds = [_dz_hold(s) for s in range(5)]
    h_of = (0, 1, 2, 0, 0)

    def body(d0, d1, d2, d3, d4, h0, h1, h2, o_ref, acc):
        m = pl.program_id(0)
        s_ = pl.program_id(1)

        @pl.when(s_ == 0)
        def _():
            acc[...] = jnp.zeros_like(acc)

        src = _table(_DZ_SRC, m)
        pairs = ((d0, h0), (d1, h1), (d2, h2), (d3, h0), (d4, h0))
        for s, (dref, href) in enumerate(pairs):
            @pl.when(src == s)
            def _(dref=dref, href=href):
                acc[...] += _dot_tn(dref[...].reshape(TS, GW), href[...].reshape(TS, D))

        @pl.when(s_ == pl.num_programs(1) - 1)
        def _():
            o_ref[...] = acc[...].astype(BF16)

    def row(s, m, s_):
        return jnp.where(_table(_DZ_SRC, m) == s, s_, 0)

    def dspec(s):
        if s < 3:
            dil = DILATIONS[s]
            return pl.BlockSpec((dil, TS // dil, GW), lambda m, s_: (0, row(s, m, s_), _table(holds[s], m)))
        return pl.BlockSpec((TS, GW), lambda m, s_: (row(s, m, s_), _table(holds[s], m)))

    def hrow(j, m, s_):
        used = _table(np.array([int(h_of[_dz_block(k)[0]] == j) for k in range(nk)], np.int32), m)
        return jnp.where(used == 1, s_, 0)

    hspecs = [pl.BlockSpec((TS, D), lambda m, s_: (hrow(0, m, s_), 0))] + [
        pl.BlockSpec((DILATIONS[j], TS // DILATIONS[j], D), lambda m, s_, j=j: (0, hrow(j, m, s_), 0)) for j in (1, 2)]
    return pl.pallas_call(
        body, name="dw_in", grid=(nk, S // TS),
        in_specs=[dspec(s) for s in range(5)] + hspecs,
        out_specs=pl.BlockSpec((GW, D), lambda m, s_: (m, 0)),
        out_shape=SDS((nk * GW, D), BF16),
        scratch_shapes=[pltpu.VMEM((GW, D), F32)],
        compiler_params=_cp(("arbitrary", "arbitrary")),
    )(*dqkv, dglu, dzg, *hs)


def _mm_tn(a, b, tm, a_maps, name):
    S, N = b.shape
    parts = len(a_maps)
    tp = tm // parts
    nm = len(a_maps[0])
    TS = min(2048, S)
    tabs = [np.array(t, np.int32) for t in a_maps]

    def body(*refs):
        a_refs = refs[:parts]
        b_ref, o_ref, acc = refs[parts:]
        s_ = pl.program_id(1)

        @pl.when(s_ == 0)
        def _():
            acc[...] = jnp.zeros_like(acc)

        for p, ar in enumerate(a_refs):
            acc[p * tp:(p + 1) * tp, :] += _dot_tn(ar[...], b_ref[...])

        @pl.when(s_ == pl.num_programs(1) - 1)
        def _():
            o_ref[...] = acc[...].astype(BF16)

    return pl.pallas_call(
        body, name=name, grid=(nm, S // TS),
        in_specs=[pl.BlockSpec((TS, tp), lambda m, s_, t=t: (s_, _table(t, m))) for t in tabs]
        + [pl.BlockSpec((TS, N), lambda m, s_: (s_, 0))],
        out_specs=pl.BlockSpec((tm, N), lambda m, s_: (m, 0)),
        out_shape=SDS((nm * tm, N), BF16),
        scratch_shapes=[pltpu.VMEM((tm, N), F32)],
        compiler_params=_cp(("arbitrary", "arbitrary")),
    )(*([a] * parts), b)


def _row_tile(rows, cols, limit=1 << 20):
    if rows * cols * 4 <= limit:
        return rows
    best = None
    for t in range(8, rows, 8):
        if rows % t == 0 and t * cols * 4 <= limit:
            best = t
    return best


def _adamw(w, g, m, v, name):
    R, C = w.shape
    tr = _row_tile(R, C)

    def body(w_ref, g_ref, m_ref, v_ref, d_ref, nm_ref, nv_ref):
        gg = g_ref[...]
        nm = ADAM_B1 * m_ref[...] + (1.0 - ADAM_B1) * gg
        nv = ADAM_B2 * v_ref[...] + (1.0 - ADAM_B2) * (gg * gg)
        m_hat = nm / (1.0 - ADAM_B1 ** ADAM_STEP)
        v_hat = nv / (1.0 - ADAM_B2 ** ADAM_STEP)
        d_ref[...] = -ADAM_LR * (m_hat / (jnp.sqrt(v_hat) + ADAM_EPS) + ADAM_WD * w_ref[...])
        nm_ref[...] = nm
        nv_ref[...] = nv

    spec = pl.BlockSpec((tr, C), lambda i: (i, 0))
    return pl.pallas_call(
        body, name=name, grid=(R // tr,), in_specs=[spec] * 4, out_specs=[spec] * 3,
        out_shape=[SDS((R, C), F32)] * 3, compiler_params=_cp(("arbitrary",)),
    )(w, g, m, v)


_FLIPS = ((1, 0), (0, 1), (1, 1))


def _place():
    x, y, c = lax.axis_index("x"), lax.axis_index("y"), lax.axis_index("c")
    return x, y, c


def _peer_chips(x, y):
    return [((x + fx) % 2, (y + fy) % 2) for fx, fy in _FLIPS]


def _gather_weights(shards):
    nw = len(shards)
    views = [s.reshape(2, s.shape[0] // 2, s.shape[1]) for s in shards]

    def body(*refs):
        ins = refs[:nw]
        outs = refs[nw:2 * nw]
        ici_send, ici_recv, d2d_send, d2d_recv, loc = refs[2 * nw:]
        x, y, c = _place()
        j = 2 * x + y
        chips = _peer_chips(x, y)
        copies = []
        for w in range(nw):
            cp = pltpu.make_async_copy(ins[w], outs[w].at[j], loc.at[w])
            cp.start()
            copies.append(cp)
        sends = []
        for w in range(nw):
            for k, (px, py) in enumerate(chips):
                cp = pltpu.make_async_remote_copy(
                    src_ref=ins[w].at[c], dst_ref=outs[w].at[j, c], send_sem=ici_send.at[w, k],
                    recv_sem=ici_recv.at[w, k], device_id=(px, py, c), device_id_type=MESH)
                cp.start()
                sends.append(cp)
        for w in range(nw):
            for k, (px, py) in enumerate(chips):
                jk = 2 * px + py
                land = outs[w].at[jk, c]
                pltpu.make_async_remote_copy(
                    src_ref=ins[w].at[c], dst_ref=land, send_sem=ici_send.at[w, k],
                    recv_sem=ici_recv.at[w, k], device_id=(px, py, c), device_id_type=MESH).wait_recv()
                cp = pltpu.make_async_remote_copy(
                    src_ref=land, dst_ref=land, send_sem=d2d_send.at[w, k],
                    recv_sem=d2d_recv.at[w, k], device_id=(x, y, 1 - c), device_id_type=MESH)
                cp.start()
                sends.append(cp)
        for w in range(nw):
            for k, (px, py) in enumerate(chips):
                jk = 2 * px + py
                land = outs[w].at[jk, 1 - c]
                pltpu.make_async_remote_copy(
                    src_ref=land, dst_ref=land, send_sem=d2d_send.at[w, k],
                    recv_sem=d2d_recv.at[w, k], device_id=(x, y, 1 - c), device_id_type=MESH).wait_recv()
        for cp in sends:
            cp.wait_send()
        for cp in copies:
            cp.wait()

    outs = pl.pallas_call(
        body, name="gather_weights",
        in_specs=[ANY] * nw, out_specs=[ANY] * nw,
        out_shape=[SDS((4,) + v.shape, BF16) for v in views],
        scratch_shapes=[pltpu.SemaphoreType.DMA((nw, 3)), pltpu.SemaphoreType.DMA((nw, 3)),
                        pltpu.SemaphoreType.DMA((nw, 3)), pltpu.SemaphoreType.DMA((nw, 3)),
                        pltpu.SemaphoreType.DMA((nw,))],
    )(*views)
    return [o.reshape(4 * s.shape[0], s.shape[1]) for o, s in zip(outs, shards)]


class _Rider:
    def __init__(self, ins, out_shape, scratch, start, finish, mid=None):
        self.ins, self.out_shape, self.scratch = list(ins), list(out_shape), list(scratch)
        self.start, self.finish, self.mid = start, finish, mid


def _ride(body, n_in, n_out, n_scr, rider, steps):
    if rider is None:
        return body, [], [], [], []
    ri, ro = len(rider.ins), len(rider.out_shape)

    def wrapped(*refs):
        ins, r_ins = refs[:n_in], refs[n_in:n_in + ri]
        o0 = n_in + ri
        outs, r_outs = refs[o0:o0 + n_out], refs[o0 + n_out:o0 + n_out + ro]
        s0 = o0 + n_out + ro
        scr, r_scr = refs[s0:s0 + n_scr], refs[s0 + n_scr:]
        i = pl.program_id(0)

        @pl.when(i == 0)
        def _():
            rider.start(r_ins, r_outs, r_scr)

        if rider.mid is not None:
            @pl.when(i == (3 * steps) // 4)
            def _():
                rider.mid(r_ins, r_outs, r_scr)

        body(*ins, *outs, *scr)

        @pl.when(i == steps - 1)
        def _():
            rider.finish(r_ins, r_outs, r_scr)

    return wrapped, [ANY] * ri, [ANY] * ro, rider.out_shape, rider.scratch


def _gather_rider(shards):
    nw = len(shards)
    views = [s.reshape(2, s.shape[0] // 2, s.shape[1]) for s in shards]

    def parts(ins, outs, sems):
        ici_send, ici_recv, d2d_send, d2d_recv, loc = sems
        x, y, c = _place()
        j = 2 * x + y
        local, ici, land_ici, fwd, land_fwd = [], [], [], [], []
        for w in range(nw):
            local.append(pltpu.make_async_copy(ins[w], outs[w].at[j], loc.at[w]))
            for k, (px, py) in enumerate(_peer_chips(x, y)):
                jk = 2 * px + py
                ici.append(pltpu.make_async_remote_copy(
                    src_ref=ins[w].at[c], dst_ref=outs[w].at[j, c], send_sem=ici_send.at[w, k],
                    recv_sem=ici_recv.at[w, k], device_id=(px, py, c), device_id_type=MESH))
                mine = outs[w].at[jk, c]
                land_ici.append(pltpu.make_async_remote_copy(
                    src_ref=ins[w].at[c], dst_ref=mine, send_sem=ici_send.at[w, k],
                    recv_sem=ici_recv.at[w, k], device_id=(px, py, c), device_id_type=MESH))
                fwd.append(pltpu.make_async_remote_copy(
                    src_ref=mine, dst_ref=mine, send_sem=d2d_send.at[w, k],
                    recv_sem=d2d_recv.at[w, k], device_id=(x, y, 1 - c), device_id_type=MESH))
                theirs = outs[w].at[jk, 1 - c]
                land_fwd.append(pltpu.make_async_remote_copy(
                    src_ref=theirs, dst_ref=theirs, send_sem=d2d_send.at[w, k],
                    recv_sem=d2d_recv.at[w, k], device_id=(x, y, 1 - c), device_id_type=MESH))
        return local, ici, land_ici, fwd, land_fwd

    def start(ins, outs, sems):
        local, ici, _, _, _ = parts(ins, outs, sems)
        for cp in local + ici:
            cp.start()

    def mid(ins, outs, sems):
        _, _, land_ici, fwd, _ = parts(ins, outs, sems)
        for landed, cp in zip(land_ici, fwd):
            landed.wait_recv()
            cp.start()

    def finish(ins, outs, sems):
        local, ici, _, fwd, land_fwd = parts(ins, outs, sems)
        for cp in land_fwd:
            cp.wait_recv()
        for cp in ici + fwd:
            cp.wait_send()
        for cp in local:
            cp.wait()

    sem = pltpu.SemaphoreType.DMA
    return _Rider(views, [SDS((4,) + v.shape, BF16) for v in views],
                  [sem((nw, 3)), sem((nw, 3)), sem((nw, 3)), sem((nw, 3)), sem((nw,))], start, finish, mid)


def _chip_exchange_rider(parts):
    nw = len(parts)

    def copies(ins, outs, sems):
        send, recv = sems
        x, y, c = _place()
        return [pltpu.make_async_remote_copy(
            src_ref=ins[w].at[2 * px + py], dst_ref=outs[w].at[k], send_sem=send.at[w, k],
            recv_sem=recv.at[w, k], device_id=(px, py, c), device_id_type=MESH)
            for w in range(nw) for k, (px, py) in enumerate(_peer_chips(x, y))]

    def start(ins, outs, sems):
        for cp in copies(ins, outs, sems):
            cp.start()

    def finish(ins, outs, sems):
        for cp in copies(ins, outs, sems):
            cp.wait()

    sem = pltpu.SemaphoreType.DMA
    return _Rider(parts, [SDS((3,) + p.shape[1:], BF16) for p in parts], [sem((nw, 3)), sem((nw, 3))], start, finish)


def _pair_exchange(grads, name):
    nw = len(grads)

    def body(*refs):
        ins = refs[:nw]
        outs = refs[nw:2 * nw]
        send, recv = refs[2 * nw:]
        x, y, c = _place()
        cps = []
        for w in range(nw):
            cp = pltpu.make_async_remote_copy(
                src_ref=ins[w].at[:, pl.ds(1 - c, 1)], dst_ref=outs[w], send_sem=send.at[w], recv_sem=recv.at[w],
                device_id=(x, y, 1 - c), device_id_type=MESH)
            cp.start()
            cps.append(cp)
        for cp in cps:
            cp.wait()

    return pl.pallas_call(
        body, name=name, in_specs=[ANY] * nw, out_specs=[ANY] * nw,
        out_shape=[SDS((4, 1) + g.shape[2:], BF16) for g in grads],
        scratch_shapes=[pltpu.SemaphoreType.DMA((nw,)), pltpu.SemaphoreType.DMA((nw,))],
    )(*grads)


def _half_tile(rh):
    best = 16
    for t in range(16, 545, 16):
        if rh % t == 0:
            best = t
    return best


def _pair_sum(c_arr, g, got, name):
    _, _, rh, n = g.shape
    tr = _half_tile(rh)

    def body(c_ref, a_ref, b_ref, o_ref):
        o_ref[...] = (a_ref[...].astype(F32) + b_ref[...].astype(F32)).astype(BF16)

    return pl.pallas_call(
        body, name=name,
        grid_spec=pltpu.PrefetchScalarGridSpec(
            num_scalar_prefetch=1, grid=(4, rh // tr),
            in_specs=[pl.BlockSpec((1, 1, tr, n), lambda s, i, c: (s, c[0], i, 0)),
                      pl.BlockSpec((1, 1, tr, n), lambda s, i, c: (s, 0, i, 0))],
            out_specs=pl.BlockSpec((1, 1, tr, n), lambda s, i, c: (s, 0, i, 0))),
        out_shape=SDS((4, 1, rh, n), BF16),
        compiler_params=_cp(("arbitrary", "arbitrary")),
    )(c_arr, g, got)


def _chip_sum(jc_arr, part, got, name):
    _, _, rh, n = part.shape
    tr = _half_tile(rh)

    def body(jc_ref, a_ref, b_ref, o_ref):
        acc = a_ref[0, 0].astype(F32)
        for k in range(3):
            acc = acc + b_ref[k, 0].astype(F32)
        o_ref[0] = acc

    return pl.pallas_call(
        body, name=name,
        grid_spec=pltpu.PrefetchScalarGridSpec(
            num_scalar_prefetch=1, grid=(rh // tr,),
            in_specs=[pl.BlockSpec((1, 1, tr, n), lambda i, jc: (jc[0], 0, i, 0)),
                      pl.BlockSpec((3, 1, tr, n), lambda i, jc: (0, 0, i, 0))],
            out_specs=pl.BlockSpec((1, tr, n), lambda i, jc: (jc[1], i, 0))),
        out_shape=SDS((2, rh, n), F32),
        compiler_params=_cp(("arbitrary",)),
    )(jc_arr, part, got)


def _half_swap(halves):
    nw = len(halves)

    def body(*refs):
        ins = refs[:nw]
        outs = refs[nw:2 * nw]
        send, recv = refs[2 * nw:]
        x, y, c = _place()
        cps = []
        for w in range(nw):
            cp = pltpu.make_async_remote_copy(
                src_ref=ins[w].at[c], dst_ref=outs[w].at[c], send_sem=send.at[w], recv_sem=recv.at[w],
                device_id=(x, y, 1 - c), device_id_type=MESH)
            cp.start()
            cps.append(cp)
        for cp in cps:
            cp.wait()

    return pl.pallas_call(
        body, name="grad_half_swap", in_specs=[ANY] * nw, out_specs=[ANY] * nw,
        out_shape=[SDS(h.shape, F32) for h in halves],
        input_output_aliases={w: w for w in range(nw)},
        scratch_shapes=[pltpu.SemaphoreType.DMA((nw,)), pltpu.SemaphoreType.DMA((nw,))],
    )(*halves)


def _all_sum_small(part, name):
    R = part.shape[0]

    def body(p_ref, o_ref, land, send, recv):
        x, y, c = _place()
        me = 4 * x + 2 * y + c
        cps = []
        for d in range(1, 8):
            t = (me + d) % 8
            cp = pltpu.make_async_remote_copy(
                src_ref=p_ref, dst_ref=land.at[me], send_sem=send.at[d - 1], recv_sem=recv.at[d - 1],
                device_id=(t // 4, (t // 2) % 2, t % 2), device_id_type=MESH)
            cp.start()
            cps.append(cp)
        land[me] = p_ref[...]
        for cp in cps:
            cp.wait()
        acc = land[0]
        for d in range(1, 8):
            acc = acc + land[d]
        o_ref[...] = acc

    return pl.pallas_call(
        body, name=name,
        in_specs=[pl.BlockSpec(memory_space=pltpu.VMEM)], out_specs=pl.BlockSpec(memory_space=pltpu.VMEM),
        out_shape=SDS((R, D), F32),
        scratch_shapes=[pltpu.VMEM((8, R, D), F32), pltpu.SemaphoreType.DMA((7,)), pltpu.SemaphoreType.DMA((7,))],
        compiler_params=pltpu.CompilerParams(vmem_limit_bytes=VMEM_LIMIT),
    )(part)


def _pad_rows(a, rows):
    return jnp.pad(a, ((0, rows - a.shape[0]), (0, 0)))


def _vec_pack(vs):
    return jnp.concatenate([_pad_rows(v, LANE_ROWS) for v in vs], axis=0)


def kernel(x, rel_bias_table, g_pre_mix, w_in, b_glu, w_dw, b_dw, g_conv_ln, b_conv_ln, w_conv_out, b_conv_out, w_attn_out, w_mix_out, g_post_mix, g_pre_ffn, w_ffn_in, w_ffn_out, g_post_ffn, loss_target, m_rel_bias_table, m_g_pre_mix, m_w_in, m_b_glu, m_w_dw, m_b_dw, m_g_conv_ln, m_b_conv_ln, m_w_conv_out, m_b_conv_out, m_w_attn_out, m_w_mix_out, m_g_post_mix, m_g_pre_ffn, m_w_ffn_in, m_w_ffn_out, m_g_post_ffn, v_rel_bias_table, v_g_pre_mix, v_w_in, v_b_glu, v_w_dw, v_b_dw, v_g_conv_ln, v_b_conv_ln, v_w_conv_out, v_b_conv_out, v_w_attn_out, v_w_mix_out, v_g_post_mix, v_g_pre_ffn, v_w_ffn_in, v_w_ffn_out, v_g_post_ffn):
    S = x.shape[1]
    xs = x.reshape(S, D)
    tgt = loss_target.reshape(S, D)
    cx, cy, cc = _place()
    chip = 2 * cx + cy

    shards = [w_in[0].T.astype(BF16),
              w_ffn_in[0].T.astype(BF16),
              w_attn_out[0].T.astype(BF16),
              w_conv_out[0].astype(BF16),
              w_mix_out[0].astype(BF16),
              w_ffn_out[0].astype(BF16)]
    (w_inT,) = _gather_weights(shards[:1])
    w_inN = w_inT.T

    buckets_np, valid_np = _bucket_tables()
    buckets = jnp.asarray(buckets_np)
    bias = _bias_expand(rel_bias_table, buckets, jnp.asarray(valid_np)).reshape(3, NH, QBLK, KBLK)
    bias2 = bias.reshape(3, NH // 2, 2 * QBLK, KBLK)
    bias_t = bias.reshape(3, NH // 2, 2, QBLK, KBLK).transpose(0, 1, 4, 2, 3).reshape(3, NH // 2, KBLK, 2 * QBLK)
    wdw32 = _pad_rows(w_dw[0], 32)
    wdw_full = _gather_small_cols(wdw32, chip)

    zrest, h, h_r4, h_r16, *gathered = _in_proj_rest(xs, g_pre_mix, w_inN[:, 3 * ATTN_COLS:], _gather_rider(shards[1:]))
    w_fiT, w_aoT, w_co, w_mx, w_fo = (t.reshape(4 * s.shape[0], s.shape[1]) for t, s in zip(gathered, shards[1:]))
    w_fiN, w_aoN = w_fiT.T, w_aoT.T
    w_coT, w_mxT, w_foT = w_co.T, w_mx.T, w_fo.T
    zq = _in_proj_qkv(h, w_inN[:, :3 * ATTN_COLS])
    og, lg = [], []
    for gi in range(3):
        o_g, l_g = _attn_fwd(zq[gi], bias2, gi)
        og.append(o_g)
        lg.append(l_g)
    cv, a = _conv_fwd(zrest, b_glu, wdw_full, b_dw, g_conv_ln, b_conv_ln)
    o, o_r4, o_r16, lse, lse_r4, lse_r16, ya, yc, mg, mm, x1 = _mix_fwd(
        og, lg, a, zrest, xs, w_aoN, w_co, b_conv_out, w_mx, g_post_mix)
    h2, gu, df, dx2, loss8, dg_post_ffn = _ffn_fwd(x1, tgt, g_pre_ffn, g_post_ffn, w_fiN, w_fo)

    c_arr = jnp.reshape(cc, (1,)).astype(jnp.int32)
    jc_arr = jnp.stack([chip, cc]).astype(jnp.int32)
    ident = lambda n: [list(range(n))]

    def pair_sums(partials, names, tag):
        views = [g.reshape(4, 2, g.shape[0] // 8, g.shape[1]) for g in partials]
        got = _pair_exchange(views, f"grad_pair_exchange_{tag}")
        return [_pair_sum(c_arr, v, r, f"pair_sum_{n}") for v, r, n in zip(views, got, names)]

    def chip_sums(pair, got, names):
        return [_chip_sum(jc_arr, p, r, f"chip_sum_{n}") for p, r, n in zip(pair, got, names)]

    dff, act = _ffn_bwd_act(df, gu, w_foT)
    g_fiT = _mm_tn(dff, h2, 512, [[2 * t if t < NFT else 2 * (t - NFT) + 1 for t in range(0, 22, 2)],
                                  [2 * t if t < NFT else 2 * (t - NFT) + 1 for t in range(1, 22, 2)]], "dw_ffn_in")
    g_fo = _mm_tn(act, df, FFN_H // 2, ident(2), "dw_ffn_out")
    names_a = ("w_ffn_in", "w_ffn_out")
    pair_a = pair_sums([g_fiT, g_fo], names_a, "ffn")
    dx1, dg_pre_ffn, *got_a = _ffn_bwd_in(dff, x1, dx2, g_pre_ffn, w_fiT, _chip_exchange_rider(pair_a))
    halves_a = chip_sums(pair_a, got_a, names_a)
    dmm, dya, dyc, do, do_r4, do_r16, da, dzg, dg_post_mix, db_conv_out = _mix_bwd(
        dx1, mm, ya, yc, zrest, g_post_mix, w_mxT, w_aoT, w_coT)
    dglu, db_glu_u, db_glu_g, dw_dw, dg_conv_ln, db_conv_ln, db_dw = _conv_bwd(da, cv, zrest, b_glu, wdw_full, g_conv_ln, b_conv_ln)
    first = lambda t: t.reshape(1, S, GW)
    dqkv, dbias = [], []
    for gi, (do_g, o_g, lse_g) in enumerate(((first(do), first(o), first(lse)), (do_r4, o_r4, lse_r4),
                                            (do_r16, o_r16, lse_r16))):
        d_g, db_g = _attn_bwd(zq[gi], do_g, o_g, lse_g, bias_t, gi)
        dqkv.append(d_g)
        dbias.append(db_g.reshape(NH // 2, KBLK, 2, QBLK).transpose(0, 2, 3, 1).reshape(NH, QBLK, KBLK))
    dtab = _bias_reduce(jnp.concatenate(dbias, axis=0), buckets)

    g_inT = _dw_in(dqkv, dglu, dzg, (h, h_r4, h_r16))
    g_aoT = _mm_tn(dya, o, 512, ident(2), "dw_attn_out")
    g_co = _mm_tn(a, dyc, 512, ident(2), "dw_conv_out")
    g_mx = _mm_tn(mg, dmm, 512, ident(2), "dw_mix_out")
    names_b = ("w_in", "w_attn_out", "w_conv_out", "w_mix_out")
    pair_b = pair_sums([g_inT, g_aoT, g_co, g_mx], names_b, "rest")
    grad_x, dg_pre_mix, *got_b = _in_bwd(dqkv, dglu, dzg, w_inT, xs, dx1, g_pre_mix, _chip_exchange_rider(pair_b))
    halves_b = chip_sums(pair_b, got_b, names_b)

    red = [t.reshape(t.shape[0] * t.shape[1], t.shape[2]) for t in _half_swap(halves_a + halves_b)]
    gw_ffn_in, gw_ffn_out, gw_in, gw_attn_out, gw_conv_out, gw_mix_out = (
        red[0].T, red[1], red[2].T, red[3].T, red[4], red[5])

    small = jnp.concatenate([loss8, dg_pre_mix, db_glu_u, db_glu_g, db_dw, dg_conv_ln, db_conv_ln, db_conv_out,
                             dg_post_mix, dg_pre_ffn, dg_post_ffn, dtab, dw_dw], axis=0)
    tot = _all_sum_small(small, "small_all_sum")
    row = lambda i: tot[LANE_ROWS * i:LANE_ROWS * i + 1]
    loss = tot[0, 0]
    g_g_pre_mix, g_b_glu = row(1), jnp.concatenate([row(2), row(3)], axis=1)
    g_b_dw, g_g_conv_ln, g_b_conv_ln, g_b_conv_out = row(4), row(5), row(6), row(7)
    g_g_post_mix, g_g_pre_ffn, g_g_post_ffn = row(8), row(9), row(10)
    g_tab = tot[88:112, 0:32].T
    g_w_dw = lax.dynamic_slice(tot[112:112 + CONV_W], (0, 256 * chip), (CONV_W, 256))

    vec_names = ["g_pre_mix", "b_dw", "g_conv_ln", "b_conv_ln", "b_conv_out", "g_post_mix", "g_pre_ffn", "g_post_ffn"]
    vec_w = [g_pre_mix, b_dw, g_conv_ln, b_conv_ln, b_conv_out, g_post_mix, g_pre_ffn, g_post_ffn]
    vec_m = [m_g_pre_mix, m_b_dw, m_g_conv_ln, m_b_conv_ln, m_b_conv_out, m_g_post_mix, m_g_pre_ffn, m_g_post_ffn]
    vec_v = [v_g_pre_mix, v_b_dw, v_g_conv_ln, v_b_conv_ln, v_b_conv_out, v_g_post_mix, v_g_pre_ffn, v_g_post_ffn]
    vec_g = [g_g_pre_mix, g_b_dw, g_g_conv_ln, g_b_conv_ln, g_b_conv_out, g_g_post_mix, g_g_pre_ffn, g_g_post_ffn]

    def pack(vs, glu, tab, dw):
        return jnp.concatenate([_vec_pack(vs), _pad_rows(glu.reshape(2, D), LANE_ROWS),
                                _pad_rows(jnp.pad(tab.T, ((0, 0), (0, D - 32))), 24),
                                _pad_rows(jnp.pad(dw, ((0, 0), (0, D - 256))), 32)], axis=0)

    sw = pack(vec_w, b_glu, rel_bias_table, w_dw[0])
    sg = pack(vec_g, g_b_glu, g_tab, g_w_dw)
    sm = pack(vec_m, m_b_glu, m_rel_bias_table, m_w_dw[0])
    sv = pack(vec_v, v_b_glu, v_rel_bias_table, v_w_dw[0])
    s_out = _adamw(sw, sg, sm, sv, "adamw_small")

    def unpack(t):
        vecs = {n: t[LANE_ROWS * i:LANE_ROWS * i + 1] for i, n in enumerate(vec_names)}
        vecs["b_glu"] = t[64:66].reshape(1, 2 * D)
        vecs["rel_bias_table"] = t[72:96, 0:32].T
        vecs["w_dw"] = t[96:96 + CONV_W, 0:256][None]
        return vecs

    small_out = [unpack(t) for t in s_out]
    big = {}
    for n, w, g, m, v in (("w_in", w_in, gw_in, m_w_in, v_w_in),
                          ("w_conv_out", w_conv_out, gw_conv_out, m_w_conv_out, v_w_conv_out),
                          ("w_attn_out", w_attn_out, gw_attn_out, m_w_attn_out, v_w_attn_out),
                          ("w_mix_out", w_mix_out, gw_mix_out, m_w_mix_out, v_w_mix_out),
                          ("w_ffn_in", w_ffn_in, gw_ffn_in, m_w_ffn_in, v_w_ffn_in),
                          ("w_ffn_out", w_ffn_out, gw_ffn_out, m_w_ffn_out, v_w_ffn_out)):
        big[n] = [t[None] for t in _adamw(w[0], g, m[0], v[0], f"adamw_{n}")]

    order = ["rel_bias_table", "g_pre_mix", "w_in", "b_glu", "w_dw", "b_dw", "g_conv_ln", "b_conv_ln", "w_conv_out",
             "b_conv_out", "w_attn_out", "w_mix_out", "g_post_mix", "g_pre_ffn", "w_ffn_in", "w_ffn_out", "g_post_ffn"]
    grads = {"rel_bias_table": g_tab, "g_pre_mix": g_g_pre_mix, "w_in": gw_in[None], "b_glu": g_b_glu,
             "w_dw": g_w_dw[None], "b_dw": g_b_dw, "g_conv_ln": g_g_conv_ln, "b_conv_ln": g_b_conv_ln,
             "w_conv_out": gw_conv_out[None], "b_conv_out": g_b_conv_out, "w_attn_out": gw_attn_out[None],
             "w_mix_out": gw_mix_out[None], "g_post_mix": g_g_post_mix, "g_pre_ffn": g_g_pre_ffn,
             "w_ffn_in": gw_ffn_in[None], "w_ffn_out": gw_ffn_out[None], "g_post_ffn": g_g_post_ffn}
    outs = [loss, grad_x.reshape(1, S, D)] + [grads[n] for n in order]
    for slot in range(3):
        outs += [big[n][slot] if n in big else small_out[slot][n] for n in order]
    return tuple(outs)


def _gather_small_cols(wdw32, chip):
    placed = lax.dynamic_update_slice(jnp.zeros((32, D), F32), wdw32, (0, 256 * chip))
    return _all_sum_small(placed, "conv_taps_gather") * 0.5
```

```python
import functools
import math

import numpy as np
import jax
import jax.numpy as jnp
from jax import lax
from jax.experimental import pallas as pl
from jax.experimental.pallas import tpu as pltpu

F32 = jnp.float32
BF16 = jnp.bfloat16
SDS = jax.ShapeDtypeStruct
MESH = pl.DeviceIdType.MESH
ANY = pl.BlockSpec(memory_space=pl.ANY)

D = 1024
HD = 64
NH = 8
GW = NH * HD
ATTN_COLS = 3 * GW
DILATIONS = (1, 4, 16)
SPAN = 128
QBLK = 128
KBLK = 2 * QBLK
CONV_W = 31
FFN_H = 2816
FFN_T = 256
NFT = FFN_H // FFN_T
RMS_EPS = 1e-6
LN_EPS = 1e-5
NEG_INF = -1e30
SCALE = HD ** -0.5
LANE_ROWS = 8
LANES = 128

ADAM_LR, ADAM_B1, ADAM_B2, ADAM_EPS, ADAM_WD, ADAM_STEP = 0.001, 0.9, 0.999, 1e-08, 0.01, 10

VMEM_LIMIT = 56 * 1024 * 1024


def _cp(sem):
    return pltpu.CompilerParams(dimension_semantics=sem, vmem_limit_bytes=VMEM_LIMIT)


def _dot(a, b):
    return jnp.dot(a, b, preferred_element_type=F32)


def _dot_nt(a, b):
    return lax.dot_general(a, b, (((1,), (1,)), ((), ())), preferred_element_type=F32)


def _dot_tn(a, b):
    return lax.dot_general(a, b, (((0,), (0,)), ((), ())), preferred_element_type=F32)


def _sigmoid(v):
    return 0.5 * jnp.tanh(0.5 * v) + 0.5


def _colsum8(v):
    s = jnp.sum(v, axis=0, keepdims=True)
    row = lax.broadcasted_iota(jnp.int32, (LANE_ROWS, v.shape[1]), 0)
    return jnp.where(row == 0, jnp.broadcast_to(s, (LANE_ROWS, v.shape[1])), 0.0)


def _col_scratch(n, width):
    return pltpu.VMEM((width // LANES, n, LANES), F32)


def _store_cols(scr, v):
    for lb in range(scr.shape[0]):
        scr[lb] = v[:, lb * LANES:(lb + 1) * LANES]


def _load_cols(scr):
    return jnp.concatenate([scr[lb] for lb in range(scr.shape[0])], axis=1)


def _split_residues(scr, dil, put):
    nb, n, _ = scr.shape
    for c in range(dil):
        put(c, jnp.concatenate([scr[lb, pl.ds(c, n // dil, stride=dil), :] for lb in range(nb)], axis=1))


def _merge_residues(scr, dil, get):
    nb, n, _ = scr.shape
    for c in range(dil):
        v = get(c)
        for lb in range(nb):
            scr[lb, pl.ds(c, n // dil, stride=dil), :] = v[:, lb * LANES:(lb + 1) * LANES]


def _residue_shape(S, dil, width):
    return (dil, S // dil, width)


def _residue_spec(TM, dil, width):
    return pl.BlockSpec((dil, TM // dil, width), lambda i: (0, i, 0))


def _in_proj_rest(x, g, w, rider):
    S = x.shape[0]
    N = w.shape[1]
    TM, TN = 512, 512

    def body(x_ref, g_ref, w_ref, zr_ref, h0_ref, h1_ref, h2_ref, hf_scr):
        xf = x_ref[...]
        r = lax.rsqrt(jnp.mean(xf * xf, axis=-1, keepdims=True) + RMS_EPS)
        hf = xf * r * g_ref[...]
        h0_ref[...] = hf.astype(BF16)
        _store_cols(hf_scr, hf)
        for dil, ref in ((DILATIONS[1], h1_ref), (DILATIONS[2], h2_ref)):
            def put(c, v, ref=ref):
                ref[c] = v.astype(BF16)
            _split_residues(hf_scr, dil, put)
        for j in range(N // TN):
            zr_ref[:, j * TN:(j + 1) * TN] = _dot(h0_ref[...], w_ref[:, j * TN:(j + 1) * TN]).astype(BF16)

    body, r_in, r_out, r_shape, r_scr = _ride(body, 3, 4, 1, rider, S // TM)
    return pl.pallas_call(
        body, name="in_proj_rest", grid=(S // TM,),
        in_specs=[pl.BlockSpec((TM, D), lambda i: (i, 0)),
                  pl.BlockSpec((1, D), lambda i: (0, 0)),
                  pl.BlockSpec((D, N), lambda i: (0, 0), pipeline_mode=pl.Buffered(1))] + r_in,
        out_specs=[pl.BlockSpec((TM, N), lambda i: (i, 0)), pl.BlockSpec((TM, D), lambda i: (i, 0)),
                   _residue_spec(TM, DILATIONS[1], D), _residue_spec(TM, DILATIONS[2], D)] + r_out,
        out_shape=[SDS((S, N), BF16), SDS((S, D), BF16),
                   SDS(_residue_shape(S, DILATIONS[1], D), BF16),
                   SDS(_residue_shape(S, DILATIONS[2], D), BF16)] + r_shape,
        scratch_shapes=[_col_scratch(TM, D)] + r_scr,
        compiler_params=_cp(("arbitrary",)),
    )(x, g, w, *rider.ins)


def _in_proj_qkv(h, w):
    S = h.shape[0]
    TM = 512

    def body(h_ref, w_ref, z0_ref, z1_ref, z2_ref, scr):
        outs = (z0_ref, z1_ref, z2_ref)
        for j in range(9):
            t, gi = j // 3, j % 3
            cols = slice(t * GW, (t + 1) * GW)
            zt = _dot(h_ref[...], w_ref[:, j * GW:(j + 1) * GW])
            if gi == 0:
                z0_ref[0, :, cols] = zt.astype(BF16)
            else:
                slot = scr.at[2 * t + gi - 1]
                _store_cols(slot, zt)

                def put(c, v, ref=outs[gi], cols=cols):
                    ref[c, :, cols] = v.astype(BF16)
                _split_residues(slot, DILATIONS[gi], put)

    return pl.pallas_call(
        body, name="in_proj_qkv", grid=(S // TM,),
        in_specs=[pl.BlockSpec((TM, D), lambda i: (i, 0)),
                  pl.BlockSpec(w.shape, lambda i: (0, 0), pipeline_mode=pl.Buffered(1))],
        out_specs=[_residue_spec(TM, d, 3 * GW) for d in DILATIONS],
        out_shape=[SDS(_residue_shape(S, d, 3 * GW), BF16) for d in DILATIONS],
        scratch_shapes=[pltpu.VMEM((6, GW // LANES, TM, LANES), F32)],
        compiler_params=_cp(("arbitrary",)),
    )(h, w)


def _bucket_tables():
    a = np.arange(QBLK, dtype=np.int32)[:, None]
    c = np.arange(KBLK, dtype=np.int32)[None, :]
    off = a - c + QBLK
    valid = ((off >= 0) & (off <= SPAN)).astype(np.float32)
    tabs = []
    for dil in DILATIONS:
        dist = np.maximum(off * dil, 0)
        df = np.maximum(dist, 1).astype(np.float32)
        large = 16 + (np.log(df / np.float32(16)) / np.float32(math.log(2048 / 16)) * np.float32(16)).astype(np.int32)
        large = np.minimum(large, 31)
        tabs.append(np.where(dist < 16, dist, large).astype(np.int32))
    return np.stack(tabs), valid


def _bias_expand(tab, buckets, valid):
    def body(tab_ref, b_ref, v_ref, o_ref):
        for gi in range(3):
            bk = b_ref[gi]
            for h in range(NH):
                acc = jnp.zeros((QBLK, KBLK), F32)
                for b in range(32):
                    acc = jnp.where(bk == b, tab_ref[b, gi * NH + h], acc)
                o_ref[gi * NH + h] = jnp.where(v_ref[...] > 0.5, acc, NEG_INF)

    return pl.pallas_call(
        body, name="bias_expand",
        in_specs=[pl.BlockSpec(memory_space=pltpu.SMEM),
                  pl.BlockSpec(memory_space=pltpu.VMEM), pl.BlockSpec(memory_space=pltpu.VMEM)],
        out_specs=pl.BlockSpec(memory_space=pltpu.VMEM),
        out_shape=SDS((3 * NH, QBLK, KBLK), F32),
    )(tab, buckets, valid)


def _bias_reduce(dbias, buckets):
    def body(d_ref, b_ref, o_ref):
        lane = lax.broadcasted_iota(jnp.int32, (1, D), 1)
        for gi in range(3):
            bk = b_ref[gi]
            for h in range(NH):
                dv = d_ref[gi * NH + h]
                row = jnp.zeros((1, D), F32)
                for b in range(32):
                    m = jnp.where(bk == b, dv, 0.0)
                    val = jnp.sum(jnp.sum(m, axis=0, keepdims=True), axis=1, keepdims=True)
                    row = jnp.where(lane == b, val, row)
                o_ref[gi * NH + h:gi * NH + h + 1, :] = row

    return pl.pallas_call(
        body, name="bias_reduce",
        in_specs=[pl.BlockSpec(memory_space=pltpu.VMEM), pl.BlockSpec(memory_space=pltpu.VMEM)],
        out_specs=pl.BlockSpec(memory_space=pltpu.VMEM),
        out_shape=SDS((3 * NH, D), F32),
    )(dbias, buckets)


def _attn_tile(S, dil):
    L = S // dil
    tq = min(512, L)
    return L, tq, tq // QBLK, L // tq


def _pair_stack(ref, rows, lanes, scale=None):
    blk = ref[rows, lanes]
    if scale is not None:
        blk = blk * scale
    lane = lax.broadcasted_iota(jnp.int32, blk.shape, 1)
    zero = jnp.zeros_like(blk)
    return jnp.concatenate([jnp.where(lane < HD, blk, zero), jnp.where(lane >= HD, blk, zero)], axis=0)


def _attn_fwd(zq, bias2, gi):
    dil, L, _ = zq.shape
    _, TQ, QB, ns = _attn_tile(L * dil, dil)
    NP = NH // 2

    def body(q_ref, kc_ref, kp_ref, vc_ref, vp_ref, b_ref, o_ref, l_ref, kext, vext):
        n = pl.program_id(1)
        kext[0:QBLK, :] = kp_ref[0]
        kext[QBLK:, :] = kc_ref[0]
        vext[0:QBLK, :] = vp_ref[0]
        vext[QBLK:, :] = vc_ref[0]
        col = lax.broadcasted_iota(jnp.int32, (2 * QBLK, KBLK), 1)
        no_prev = jnp.logical_and(n == 0, col < QBLK)
        lane = lax.broadcasted_iota(jnp.int32, (QBLK, LANES), 1)
        lanes_of = [slice(hp * LANES, (hp + 1) * LANES) for hp in range(NP)]
        for b in range(QB):
            rows = slice(b * QBLK, (b + 1) * QBLK)
            win = slice(b * QBLK, b * QBLK + KBLK)
            s = [_dot_nt(_pair_stack(q_ref.at[0], rows, pl_, SCALE), kext[win, pl_]) + b_ref[0, hp]
                 for hp, pl_ in enumerate(lanes_of)]
            if b == 0:
                s = [jnp.where(no_prev, NEG_INF, v) for v in s]
            m = [jnp.max(v, axis=-1, keepdims=True) for v in s]
            p = [jnp.exp(v - mv) for v, mv in zip(s, m)]
            l = [jnp.sum(v, axis=-1, keepdims=True) for v in p]
            o2 = [_dot(v.astype(BF16), vext[win, pl_]) / lv for v, lv, pl_ in zip(p, l, lanes_of)]
            for hp, pl_ in enumerate(lanes_of):
                lse2 = jnp.broadcast_to(m[hp] + jnp.log(l[hp]), (2 * QBLK, LANES))
                o_ref[0, rows, pl_] = jnp.where(lane < HD, o2[hp][0:QBLK], o2[hp][QBLK:]).astype(BF16)
                l_ref[0, rows, pl_] = jnp.where(lane < HD, lse2[0:QBLK], lse2[QBLK:])

    def prev(n):
        return jnp.maximum(n * QB - 1, 0)

    return pl.pallas_call(
        body, name=f"attn_fwd_g{gi}", grid=(dil, ns),
        in_specs=[pl.BlockSpec((1, TQ, GW), lambda c, n: (c, n, 0)),
                  pl.BlockSpec((1, TQ, GW), lambda c, n: (c, n, 1)),
                  pl.BlockSpec((1, QBLK, GW), lambda c, n: (c, prev(n), 1)),
                  pl.BlockSpec((1, TQ, GW), lambda c, n: (c, n, 2)),
                  pl.BlockSpec((1, QBLK, GW), lambda c, n: (c, prev(n), 2)),
                  pl.BlockSpec((1, NP, 2 * QBLK, KBLK), lambda c, n: (gi, 0, 0, 0))],
        out_specs=[pl.BlockSpec((1, TQ, GW), lambda c, n: (c, n, 0)),
                   pl.BlockSpec((1, TQ, GW), lambda c, n: (c, n, 0))],
        out_shape=[SDS((dil, L, GW), BF16), SDS((dil, L, GW), F32)],
        scratch_shapes=[pltpu.VMEM((QBLK + TQ, GW), BF16), pltpu.VMEM((QBLK + TQ, GW), BF16)],
        compiler_params=_cp(("arbitrary", "arbitrary")),
    )(zq, zq, zq, zq, zq, bias2)


CONV_TM = 256
SHIFT_PAD = 24


def _make_shifts(src, sh, n):
    for b in range(1, 8):
        sh[b - 1] = src[b:b + n + SHIFT_PAD, :]


def _shifted(src, sh, off, r0, n, lanes):
    a, b = divmod(off, 8)
    if b == 0:
        return src[8 * a + r0:8 * a + r0 + n, lanes]
    return sh[b - 1, 8 * a + r0:8 * a + r0 + n, lanes]


CONV_RC = 32


def _tap_blocks(TM):
    return [(r0, slice(l0, l0 + LANES)) for l0 in range(0, D, LANES) for r0 in range(0, TM, CONV_RC)]


def _conv_fwd(zrest, b_glu, wdw, b_dw, g_ln, b_ln):
    S = zrest.shape[0]
    TM = CONV_TM
    HALO = 32
    hb = TM // HALO

    def body(u_ref, g_ref, uh_ref, gh_ref, bg_ref, w_ref, bd_ref, gl_ref, bl_ref, cv_ref, a_ref, ext, sh):
        i = pl.program_id(0)
        bu = bg_ref[:, 0:D]
        bgt = bg_ref[:, D:2 * D]
        uh = (uh_ref[...].astype(F32) + bu) * _sigmoid(gh_ref[...].astype(F32) + bgt)
        ext[0:HALO, :] = jnp.where(i == 0, 0.0, uh)
        ext[HALO:, :] = (u_ref[...].astype(F32) + bu) * _sigmoid(g_ref[...].astype(F32) + bgt)
        _make_shifts(ext, sh, TM)
        acc = jnp.zeros((TM, D), F32)
        for j in range(CONV_W):
            acc = acc + _shifted(ext, sh, HALO - (CONV_W - 1) + j, 0, TM, slice(None)) * w_ref[j:j + 1, :]
        cv = (acc + bd_ref[...]).astype(BF16)
        cv_ref[...] = cv
        cf = cv.astype(F32)
        mu = jnp.mean(cf, axis=-1, keepdims=True)
        xc = cf - mu
        y = xc * lax.rsqrt(jnp.mean(xc * xc, axis=-1, keepdims=True) + LN_EPS) * gl_ref[...] + bl_ref[...]
        a_ref[...] = (y * _sigmoid(y)).astype(BF16)

    vec = pl.BlockSpec((1, D), lambda i: (0, 0))
    return pl.pallas_call(
        body, name="conv_fwd", grid=(S // TM,),
        in_specs=[pl.BlockSpec((TM, D), lambda i: (i, 0)), pl.BlockSpec((TM, D), lambda i: (i, 1)),
                  pl.BlockSpec((HALO, D), lambda i: (jnp.maximum(i * hb - 1, 0), 0)),
                  pl.BlockSpec((HALO, D), lambda i: (jnp.maximum(i * hb - 1, 0), 1)),
                  pl.BlockSpec((1, 2 * D), lambda i: (0, 0)),
                  pl.BlockSpec((32, D), lambda i: (0, 0)), vec, vec, vec],
        out_specs=[pl.BlockSpec((TM, D), lambda i: (i, 0)), pl.BlockSpec((TM, D), lambda i: (i, 0))],
        out_shape=[SDS((S, D), BF16), SDS((S, D), BF16)],
        scratch_shapes=[pltpu.VMEM((HALO + TM, D), F32), pltpu.VMEM((7, TM + SHIFT_PAD, D), F32)],
        compiler_params=_cp(("arbitrary",)),
    )(zrest, zrest, zrest, zrest, b_glu, wdw, b_dw, g_ln, b_ln)


def _mix_fwd(og, lg, a, zrest, x, w_ao, w_co, b_co, w_mx, g_pm):
    S = x.shape[0]
    TM = 512

    def body(o0, o1, o2, l0, l1, l2, a_ref, ga_ref, gc_ref, x_ref, wa_ref, wc_ref, bc_ref, wm_ref, g_ref,
             o_ref, oa_ref, ob_ref, lse_ref, lsea_ref, lseb_ref, ya_ref, yc_ref, mg_ref, mm_ref, x1_ref,
             so1, so2, sl1, sl2, so, sl):
        for dil, src, dst, cast in ((DILATIONS[1], o1, so1, True), (DILATIONS[2], o2, so2, True),
                                    (DILATIONS[1], l1, sl1, False), (DILATIONS[2], l2, sl2, False)):
            _merge_residues(dst, dil, (lambda c, src=src: src[c].astype(F32)) if cast else (lambda c, src=src: src[c]))
        la, lb, lc = l0[0], _load_cols(sl1), _load_cols(sl2)
        m = jnp.maximum(jnp.maximum(la, lb), lc)
        e0 = jnp.exp(la - m)
        e1 = jnp.exp(lb - m)
        e2 = jnp.exp(lc - m)
        den = e0 + e1 + e2
        of = (e0 * o0[0].astype(F32) + e1 * _load_cols(so1) + e2 * _load_cols(so2)) / den
        o = of.astype(BF16)
        o_ref[...] = o
        lse = m + jnp.log(den)
        lse_ref[...] = lse
        _store_cols(so, of)
        _store_cols(sl, lse)
        for dil, oref, lref in ((DILATIONS[1], oa_ref, lsea_ref), (DILATIONS[2], ob_ref, lseb_ref)):
            def put_o(c, v, oref=oref):
                oref[c] = v.astype(BF16)

            def put_l(c, v, lref=lref):
                lref[c] = v
            _split_residues(so, dil, put_o)
            _split_residues(sl, dil, put_l)
        ya = _dot(o, wa_ref[...]).astype(BF16)
        yc = (_dot(a_ref[...], wc_ref[...]) + bc_ref[...]).astype(BF16)
        ya_ref[...] = ya
        yc_ref[...] = yc
        mg = (_sigmoid(ga_ref[...].astype(F32)) * ya.astype(F32)
              + _sigmoid(gc_ref[...].astype(F32)) * yc.astype(F32)).astype(BF16)
        mg_ref[...] = mg
        mm = _dot(mg, wm_ref[...]).astype(BF16)
        mm_ref[...] = mm
        mf = mm.astype(F32)
        r = lax.rsqrt(jnp.mean(mf * mf, axis=-1, keepdims=True) + RMS_EPS)
        x1_ref[...] = x_ref[...] + mf * r * g_ref[...]

    row512 = pl.BlockSpec((TM, GW), lambda i: (i, 0))
    rowd = pl.BlockSpec((TM, D), lambda i: (i, 0))
    vec = pl.BlockSpec((1, D), lambda i: (0, 0))
    full = lambda r, c: pl.BlockSpec((r, c), lambda i: (0, 0))
    res = [_residue_spec(TM, d, GW) for d in DILATIONS]
    rshape = lambda d, t: SDS(_residue_shape(S, d, GW), t)
    scr = _col_scratch(TM, GW)
    return pl.pallas_call(
        body, name="mix_fwd", grid=(S // TM,),
        in_specs=res + res + [rowd, pl.BlockSpec((TM, D), lambda i: (i, 2)), pl.BlockSpec((TM, D), lambda i: (i, 3)),
                              rowd, full(GW, D), full(D, D), vec, full(D, D), vec],
        out_specs=[row512, res[1], res[2], row512, res[1], res[2], rowd, rowd, rowd, rowd, rowd],
        out_shape=[SDS((S, GW), BF16), rshape(DILATIONS[1], BF16), rshape(DILATIONS[2], BF16),
                   SDS((S, GW), F32), rshape(DILATIONS[1], F32), rshape(DILATIONS[2], F32),
                   SDS((S, D), BF16), SDS((S, D), BF16), SDS((S, D), BF16), SDS((S, D), BF16), SDS((S, D), F32)],
        scratch_shapes=[scr] * 6,
        compiler_params=_cp(("arbitrary",)),
    )(og[0], og[1], og[2], lg[0], lg[1], lg[2], a, zrest, zrest, x, w_ao, w_co, b_co, w_mx, g_pm)


def _ffn_fwd(x1, tgt, g_pre, g_post, w_fi, w_fo):
    S = x1.shape[0]
    TM = 512

    def body(x1_ref, t_ref, gp_ref, go_ref, wi_ref, wo_ref,
             h2_ref, gu_ref, df_ref, dx2_ref, loss_ref, dgo_ref):
        i = pl.program_id(0)

        @pl.when(i == 0)
        def _():
            loss_ref[...] = jnp.zeros_like(loss_ref)
            dgo_ref[...] = jnp.zeros_like(dgo_ref)

        xf = x1_ref[...]
        r = lax.rsqrt(jnp.mean(xf * xf, axis=-1, keepdims=True) + RMS_EPS)
        h2_ref[...] = (xf * r * gp_ref[...]).astype(BF16)
        for k in range(NFT):
            gu_ref[:, 2 * k * FFN_T:(2 * k + 1) * FFN_T] = _dot(
                h2_ref[...], wi_ref[:, k * FFN_T:(k + 1) * FFN_T]).astype(BF16)
            gu_ref[:, (2 * k + 1) * FFN_T:(2 * k + 2) * FFN_T] = _dot(
                h2_ref[...], wi_ref[:, FFN_H + k * FFN_T:FFN_H + (k + 1) * FFN_T]).astype(BF16)
        f = jnp.zeros((TM, D), F32)
        for k in range(NFT):
            gf = gu_ref[:, 2 * k * FFN_T:(2 * k + 1) * FFN_T].astype(F32)
            uf = gu_ref[:, (2 * k + 1) * FFN_T:(2 * k + 2) * FFN_T].astype(F32)
            act = (gf * _sigmoid(gf) * uf).astype(BF16)
            f = f + _dot(act, wo_ref[k * FFN_T:(k + 1) * FFN_T, :])
        r = lax.rsqrt(jnp.mean(f * f, axis=-1, keepdims=True) + RMS_EPS)
        nrm = f * r
        e = x1_ref[...] + nrm * go_ref[...] - t_ref[...]
        tot = jnp.sum(jnp.sum(e * e, axis=-1, keepdims=True), axis=0, keepdims=True) * (0.5 / D)
        corner = jnp.logical_and(lax.broadcasted_iota(jnp.int32, (LANE_ROWS, D), 0) == 0,
                                 lax.broadcasted_iota(jnp.int32, (LANE_ROWS, D), 1) == 0)
        loss_ref[...] += jnp.where(corner, tot, 0.0)
        dx2 = e * (1.0 / D)
        dx2_ref[...] = dx2
        dgo_ref[...] += _colsum8(dx2 * nrm)
        dn = dx2 * go_ref[...]
        df_ref[...] = (r * (dn - nrm * jnp.mean(dn * nrm, axis=-1, keepdims=True))).astype(BF16)

    rowd = pl.BlockSpec((TM, D), lambda i: (i, 0))
    vec = pl.BlockSpec((1, D), lambda i: (0, 0))
    acc8 = pl.BlockSpec((LANE_ROWS, D), lambda i: (0, 0))
    return pl.pallas_call(
        body, name="ffn_fwd", grid=(S // TM,),
        in_specs=[rowd, rowd, vec, vec,
                  pl.BlockSpec((D, 2 * FFN_H), lambda i: (0, 0), pipeline_mode=pl.Buffered(1)),
                  pl.BlockSpec((FFN_H, D), lambda i: (0, 0), pipeline_mode=pl.Buffered(1))],
        out_specs=[rowd, pl.BlockSpec((TM, 2 * FFN_H), lambda i: (i, 0)), rowd, rowd, acc8, acc8],
        out_shape=[SDS((S, D), BF16), SDS((S, 2 * FFN_H), BF16), SDS((S, D), BF16), SDS((S, D), F32),
                   SDS((LANE_ROWS, D), F32), SDS((LANE_ROWS, D), F32)],
        compiler_params=_cp(("arbitrary",)),
    )(x1, tgt, g_pre, g_post, w_fi, w_fo)


def _ffn_bwd_act(df, gu, w_foT):
    S = df.shape[0]
    TM = 512

    def body_act(df_ref, gu_ref, wo_ref, dff_ref, act_ref):
        dacts = [_dot(df_ref[...], wo_ref[:, k * FFN_T:(k + 1) * FFN_T]) for k in range(NFT)]
        for k in range(NFT):
            dact = dacts[k]
            g = gu_ref[:, 2 * k * FFN_T:(2 * k + 1) * FFN_T].astype(F32)
            u = gu_ref[:, (2 * k + 1) * FFN_T:(2 * k + 2) * FFN_T].astype(F32)
            sg = _sigmoid(g)
            sl = g * sg
            act_ref[:, k * FFN_T:(k + 1) * FFN_T] = (sl * u).astype(BF16)
            dff_ref[:, 2 * k * FFN_T:(2 * k + 1) * FFN_T] = (dact * u * (sg * (1.0 + g * (1.0 - sg)))).astype(BF16)
            dff_ref[:, (2 * k + 1) * FFN_T:(2 * k + 2) * FFN_T] = (dact * sl).astype(BF16)

    rowd = pl.BlockSpec((TM, D), lambda i: (i, 0))
    wide = pl.BlockSpec((TM, 2 * FFN_H), lambda i: (i, 0))
    return pl.pallas_call(
        body_act, name="ffn_bwd_act", grid=(S // TM,),
        in_specs=[rowd, wide, pl.BlockSpec((D, FFN_H), lambda i: (0, 0), pipeline_mode=pl.Buffered(1))],
        out_specs=[wide, pl.BlockSpec((TM, FFN_H), lambda i: (i, 0))],
        out_shape=[SDS((S, 2 * FFN_H), BF16), SDS((S, FFN_H), BF16)],
        compiler_params=_cp(("arbitrary",)),
    )(df, gu, w_foT)


def _ffn_bwd_in(dff, x1, dx2, g_pre, w_fiT, rider):
    S = x1.shape[0]
    TM = 512
    rowd = pl.BlockSpec((TM, D), lambda i: (i, 0))
    wide = pl.BlockSpec((TM, 2 * FFN_H), lambda i: (i, 0))
    KC = 512
    nkc = 2 * FFN_H // KC

    def body_in(dff_ref, x1_ref, dx2_ref, gp_ref, wi_ref, dx1_ref, dgp_ref):
        i = pl.program_id(0)

        @pl.when(i == 0)
        def _():
            dgp_ref[...] = jnp.zeros_like(dgp_ref)

        dh = jnp.zeros((TM, D), F32)
        for k in range(nkc):
            dh = dh + _dot(dff_ref[:, k * KC:k * KC + FFN_T], wi_ref[k * FFN_T:(k + 1) * FFN_T, :]) \
                + _dot(dff_ref[:, k * KC + FFN_T:(k + 1) * KC], wi_ref[FFN_H + k * FFN_T:FFN_H + (k + 1) * FFN_T, :])
        xf = x1_ref[...]
        r = lax.rsqrt(jnp.mean(xf * xf, axis=-1, keepdims=True) + RMS_EPS)
        nrm = xf * r
        dgp_ref[...] += _colsum8(dh * nrm)
        dn = dh * gp_ref[...]
        dx1_ref[...] = dx2_ref[...] + r * (dn - nrm * jnp.mean(dn * nrm, axis=-1, keepdims=True))

    body_in, r_in, r_out, r_shape, r_scr = _ride(body_in, 5, 2, 0, rider, S // TM)
    return pl.pallas_call(
        body_in, name="ffn_bwd_in", grid=(S // TM,),
        in_specs=[wide, rowd, rowd, pl.BlockSpec((1, D), lambda i: (0, 0)),
                  pl.BlockSpec((2 * FFN_H, D), lambda i: (0, 0), pipeline_mode=pl.Buffered(1))] + r_in,
        out_specs=[rowd, pl.BlockSpec((LANE_ROWS, D), lambda i: (0, 0))] + r_out,
        out_shape=[SDS((S, D), F32), SDS((LANE_ROWS, D), F32)] + r_shape,
        scratch_shapes=r_scr,
        compiler_params=_cp(("arbitrary",)),
    )(dff, x1, dx2, g_pre, w_fiT, *rider.ins)


def _mix_bwd(dx1, mm, ya, yc, zrest, g_pm, w_mxT, w_aoT, w_coT):
    S = dx1.shape[0]
    TM = 512

    def body(dx_ref, mm_ref, ya_ref, yc_ref, ga_ref, gc_ref, g_ref, wm_ref, wa_ref, wc_ref,
             dmm_ref, dya_ref, dyc_ref, do_ref, doa_ref, dob_ref, da_ref, dzg_ref, dgpm_ref, dbco_ref, sdo):
        i = pl.program_id(0)

        @pl.when(i == 0)
        def _():
            dgpm_ref[...] = jnp.zeros_like(dgpm_ref)
            dbco_ref[...] = jnp.zeros_like(dbco_ref)

        mf = mm_ref[...].astype(F32)
        r = lax.rsqrt(jnp.mean(mf * mf, axis=-1, keepdims=True) + RMS_EPS)
        nrm = mf * r
        dx = dx_ref[...]
        dgpm_ref[...] += _colsum8(dx * nrm)
        dn = dx * g_ref[...]
        dmm = (r * (dn - nrm * jnp.mean(dn * nrm, axis=-1, keepdims=True))).astype(BF16)
        dmm_ref[...] = dmm
        dmg = _dot(dmm, wm_ref[...])
        sa = _sigmoid(ga_ref[...].astype(F32))
        sc = _sigmoid(gc_ref[...].astype(F32))
        dya = (dmg * sa).astype(BF16)
        dyc = (dmg * sc).astype(BF16)
        dya_ref[...] = dya
        dyc_ref[...] = dyc
        dbco_ref[...] += _colsum8(dyc.astype(F32))
        dzg_ref[:, 0:D] = (dmg * ya_ref[...].astype(F32) * (sa * (1.0 - sa))).astype(BF16)
        dzg_ref[:, D:] = (dmg * yc_ref[...].astype(F32) * (sc * (1.0 - sc))).astype(BF16)
        dof = _dot(dya, wa_ref[...])
        do_ref[...] = dof.astype(BF16)
        _store_cols(sdo, dof)
        for dil, ref in ((DILATIONS[1], doa_ref), (DILATIONS[2], dob_ref)):
            def put(c, v, ref=ref):
                ref[c] = v.astype(BF16)
            _split_residues(sdo, dil, put)
        da_ref[...] = _dot(dyc, wc_ref[...]).astype(BF16)

    rowd = pl.BlockSpec((TM, D), lambda i: (i, 0))
    full = lambda r, c: pl.BlockSpec((r, c), lambda i: (0, 0))
    acc8 = pl.BlockSpec((LANE_ROWS, D), lambda i: (0, 0))
    return pl.pallas_call(
        body, name="mix_bwd", grid=(S // TM,),
        in_specs=[rowd, rowd, rowd, rowd, pl.BlockSpec((TM, D), lambda i: (i, 2)),
                  pl.BlockSpec((TM, D), lambda i: (i, 3)), full(1, D), full(D, D), full(D, GW), full(D, D)],
        out_specs=[rowd, rowd, rowd, pl.BlockSpec((TM, GW), lambda i: (i, 0)),
                   _residue_spec(TM, DILATIONS[1], GW), _residue_spec(TM, DILATIONS[2], GW), rowd,
                   pl.BlockSpec((TM, 2 * D), lambda i: (i, 0)), acc8, acc8],
        out_shape=[SDS((S, D), BF16), SDS((S, D), BF16), SDS((S, D), BF16), SDS((S, GW), BF16),
                   SDS(_residue_shape(S, DILATIONS[1], GW), BF16), SDS(_residue_shape(S, DILATIONS[2], GW), BF16),
                   SDS((S, D), BF16), SDS((S, 2 * D), BF16), SDS((LANE_ROWS, D), F32), SDS((LANE_ROWS, D), F32)],
        scratch_shapes=[_col_scratch(TM, GW)],
        compiler_params=_cp(("arbitrary",)),
    )(dx1, mm, ya, yc, zrest, zrest, g_pm, w_mxT, w_aoT, w_coT)


def _conv_bwd(da, cv, zrest, b_glu, wdw, g_ln, b_ln):
    S = da.shape[0]
    TM = CONV_TM
    HALO = 32
    hb = TM // HALO
    nh = S // HALO

    def body(da_ref, dan_ref, cv_ref, cvn_ref, u_ref, g_ref, uh_ref, gh_ref, bg_ref, w_ref, gl_ref, bl_ref,
             dglu_ref, dbu_ref, dbg_ref, dw_ref, dgl_ref, dbl_ref, dbd_ref, dext, uext, dsh, ush, du_scr, dw8):
        i = pl.program_id(0)
        last = i == pl.num_programs(0) - 1

        @pl.when(i == 0)
        def _():
            for ref in (dbu_ref, dbg_ref, dw8, dgl_ref, dbl_ref, dbd_ref):
                ref[...] = jnp.zeros_like(ref)

        def ln_bwd(da_v, cv_v):
            cf = cv_v.astype(F32)
            mu = jnp.mean(cf, axis=-1, keepdims=True)
            xc = cf - mu
            rstd = lax.rsqrt(jnp.mean(xc * xc, axis=-1, keepdims=True) + LN_EPS)
            xh = xc * rstd
            y = xh * gl_ref[...] + bl_ref[...]
            sy = _sigmoid(y)
            dy = da_v.astype(F32) * (sy * (1.0 + y * (1.0 - sy)))
            dxh = dy * gl_ref[...]
            dcv = rstd * (dxh - jnp.mean(dxh, axis=-1, keepdims=True)
                          - xh * jnp.mean(dxh * xh, axis=-1, keepdims=True))
            return dcv, dy, xh

        dcv, dy, xh = ln_bwd(da_ref[...], cv_ref[...])
        dgl_ref[...] += _colsum8(dy * xh)
        dbl_ref[...] += _colsum8(dy)
        dbd_ref[...] += _colsum8(dcv)
        dcvn, _, _ = ln_bwd(dan_ref[...], cvn_ref[...])
        dext[0:TM, :] = dcv
        dext[TM:, :] = jnp.where(last, 0.0, dcvn)

        bu = bg_ref[:, 0:D]
        bgt = bg_ref[:, D:2 * D]
        upre = u_ref[...].astype(F32) + bu
        sg = _sigmoid(g_ref[...].astype(F32) + bgt)
        uh = (uh_ref[...].astype(F32) + bu) * _sigmoid(gh_ref[...].astype(F32) + bgt)
        uext[0:HALO, :] = jnp.where(i == 0, 0.0, uh)
        uext[HALO:, :] = upre * sg

        _make_shifts(dext, dsh, TM)
        _make_shifts(uext, ush, TM)
        for r0, lanes in _tap_blocks(TM):
            acc = jnp.zeros((CONV_RC, LANES), F32)
            for j in range(CONV_W):
                acc = acc + _shifted(dext, dsh, CONV_W - 1 - j, r0, CONV_RC, lanes) * w_ref[j:j + 1, lanes]
            du_scr[r0:r0 + CONV_RC, lanes] = acc
        for l0 in range(0, D, LANES):
            lanes = slice(l0, l0 + LANES)
            accs = [jnp.zeros((LANE_ROWS, LANES), F32)] * CONV_W
            for r0 in range(0, TM, CONV_RC):
                dc = dext[r0:r0 + CONV_RC, lanes]
                for j in range(CONV_W):
                    prod = dc * _shifted(uext, ush, HALO - (CONV_W - 1) + j, r0, CONV_RC, lanes)
                    accs[j] = accs[j] + jnp.sum(prod.reshape(CONV_RC // LANE_ROWS, LANE_ROWS, LANES), axis=0)
            for j in range(CONV_W):
                dw8[j, :, lanes] += accs[j]

        @pl.when(last)
        def _():
            for j in range(CONV_W):
                dw_ref[j:j + 1, :] = jnp.sum(dw8[j], axis=0, keepdims=True)
            dw_ref[CONV_W:, :] = jnp.zeros((32 - CONV_W, D), F32)

        du = du_scr[...]
        dup = du * sg
        dgp = du * upre * (sg * (1.0 - sg))
        dglu_ref[:, 0:D] = dup.astype(BF16)
        dglu_ref[:, D:] = dgp.astype(BF16)
        dbu_ref[...] += _colsum8(dup.astype(BF16).astype(F32))
        dbg_ref[...] += _colsum8(dgp.astype(BF16).astype(F32))

    rowd = pl.BlockSpec((TM, D), lambda i: (i, 0))
    nxt = pl.BlockSpec((HALO, D), lambda i: (jnp.minimum((i + 1) * hb, nh - 1), 0))
    vec = pl.BlockSpec((1, D), lambda i: (0, 0))
    acc8 = pl.BlockSpec((LANE_ROWS, D), lambda i: (0, 0))
    return pl.pallas_call(
        body, name="conv_bwd", grid=(S // TM,),
        in_specs=[rowd, nxt, rowd, nxt,
                  pl.BlockSpec((TM, D), lambda i: (i, 0)), pl.BlockSpec((TM, D), lambda i: (i, 1)),
                  pl.BlockSpec((HALO, D), lambda i: (jnp.maximum(i * hb - 1, 0), 0)),
                  pl.BlockSpec((HALO, D), lambda i: (jnp.maximum(i * hb - 1, 0), 1)),
                  pl.BlockSpec((1, 2 * D), lambda i: (0, 0)), pl.BlockSpec((32, D), lambda i: (0, 0)), vec, vec],
        out_specs=[pl.BlockSpec((TM, 2 * D), lambda i: (i, 0)), acc8, acc8,
                   pl.BlockSpec((32, D), lambda i: (0, 0)), acc8, acc8, acc8],
        out_shape=[SDS((S, 2 * D), BF16), SDS((LANE_ROWS, D), F32), SDS((LANE_ROWS, D), F32), SDS((32, D), F32),
                   SDS((LANE_ROWS, D), F32), SDS((LANE_ROWS, D), F32), SDS((LANE_ROWS, D), F32)],
        scratch_shapes=[pltpu.VMEM((TM + HALO, D), F32), pltpu.VMEM((HALO + TM, D), F32),
                        pltpu.VMEM((7, TM + SHIFT_PAD, D), F32), pltpu.VMEM((7, TM + SHIFT_PAD, D), F32),
                        pltpu.VMEM((TM, D), F32), pltpu.VMEM((32, LANE_ROWS, D), F32)],
        compiler_params=_cp(("arbitrary",)),
    )(da, da, cv, cv, zrest, zrest, zrest, zrest, b_glu, wdw, g_ln, b_ln)


def _attn_bwd(zq, do, o, lse, bias_t, gi):
    dil, L, _ = zq.shape
    _, TQ, QB, ns = _attn_tile(L * dil, dil)
    NP = NH // 2

    def body(q3, kc3, kp3, vc3, vp3, do3, o3, l3, b_ref,
             out3, db_ref, kext, vext, dkx, dvx, dqn, dqc, dkc, dvc):
        q_ref, kc_ref, kp_ref, vc_ref, vp_ref, do_ref, o_ref, l_ref, out_ref = (
            r.at[0] for r in (q3, kc3, kp3, vc3, vp3, do3, o3, l3, out3))
        t = pl.program_id(0)
        n = lax.rem(cur(t), ns)

        @pl.when(t == 0)
        def _():
            db_ref[...] = jnp.zeros_like(db_ref)

        @pl.when(t < T - 1)
        def _():
            kext[0:QBLK, :] = kp_ref[...]
            kext[QBLK:, :] = kc_ref[...]
            vext[0:QBLK, :] = vp_ref[...]
            vext[QBLK:, :] = vc_ref[...]
            krow = lax.broadcasted_iota(jnp.int32, (KBLK, 2 * QBLK), 0)
            no_prev = jnp.logical_and(n == 0, krow < QBLK)
            lane = lax.broadcasted_iota(jnp.int32, (QBLK, LANES), 1)

            def overlap_add(parts):
                segs = [parts[0][0:QBLK]]
                for b in range(1, QB):
                    segs.append(parts[b - 1][QBLK:] + parts[b][0:QBLK])
                segs.append(parts[QB - 1][QBLK:])
                return jnp.concatenate(segs, axis=0)

            lanes_of = [slice(hp * LANES, (hp + 1) * LANES) for hp in range(NP)]
            dv_parts = [[] for _ in range(NP)]
            dk_parts = [[] for _ in range(NP)]
            dbsum = [None] * NP
            for b in range(QB):
                rows = slice(b * QBLK, (b + 1) * QBLK)
                win = slice(b * QBLK, b * QBLK + KBLK)
                q2 = [_pair_stack(q_ref, rows, pl_, SCALE) for pl_ in lanes_of]
                do2 = [_pair_stack(do_ref, rows, pl_) for pl_ in lanes_of]
                st = [_dot_nt(kext[win, pl_], q2[hp]) + b_ref[0, hp] for hp, pl_ in enumerate(lanes_of)]
                dpt = [_dot_nt(vext[win, pl_], do2[hp]) for hp, pl_ in enumerate(lanes_of)]
                lse_t = l_ref[rows, :].T
                prod_t = (do_ref[rows, :].astype(F32) * o_ref[rows, :].astype(F32)).T
                dst = []
                for hp in range(NP):
                    lo = hp * LANES
                    lse_row = jnp.concatenate([lse_t[lo:lo + 1], lse_t[lo + HD:lo + HD + 1]], axis=1)
                    delta_row = jnp.concatenate([jnp.sum(prod_t[lo:lo + HD], axis=0, keepdims=True),
                                                 jnp.sum(prod_t[lo + HD:lo + LANES], axis=0, keepdims=True)], axis=1)
                    s_hp = jnp.where(no_prev, NEG_INF, st[hp]) if b == 0 else st[hp]
                    pt = jnp.exp(s_hp - lse_row)
                    d = pt * (dpt[hp] - delta_row)
                    dbsum[hp] = d if dbsum[hp] is None else dbsum[hp] + d
                    dst.append(d.astype(BF16))
                    dv_parts[hp].append(_dot(pt.astype(BF16), do2[hp]))
                for hp, pl_ in enumerate(lanes_of):
                    dk_parts[hp].append(_dot(dst[hp], q2[hp]))
                    dq2 = _dot_tn(dst[hp], kext[win, pl_])
                    dqn[rows, pl_] = jnp.where(lane < HD, dq2[0:QBLK], dq2[QBLK:]) * SCALE
            for hp, pl_ in enumerate(lanes_of):
                db_ref[hp] += dbsum[hp]
                dvx[:, pl_] = overlap_add(dv_parts[hp])
                dkx[:, pl_] = overlap_add(dk_parts[hp])

        @pl.when(t > 0)
        def _():
            out_ref[:, 0:GW] = dqc[...].astype(BF16)
            out_ref[:, GW:2 * GW] = dkc[...].astype(BF16)
            out_ref[:, 2 * GW:] = dvc[...].astype(BF16)

        @pl.when(jnp.logical_and(t > 0, t < T - 1))
        def _():
            out_ref[TQ - QBLK:, GW:2 * GW] = (dkc[TQ - QBLK:, :] + dkx[0:QBLK, :]).astype(BF16)
            out_ref[TQ - QBLK:, 2 * GW:] = (dvc[TQ - QBLK:, :] + dvx[0:QBLK, :]).astype(BF16)

        @pl.when(t < T - 1)
        def _():
            dqc[...] = dqn[...]
            dkc[...] = dkx[QBLK:, :]
            dvc[...] = dvx[QBLK:, :]

    T = dil * ns + 1

    def cur(t):
        return jnp.minimum(t, T - 2)

    def blk(t, col):
        return (lax.div(cur(t), ns), lax.rem(cur(t), ns), col)

    def prev(t, col):
        return (lax.div(cur(t), ns), jnp.maximum(lax.rem(cur(t), ns) * QB - 1, 0), col)

    def late(t):
        tp = jnp.maximum(t - 1, 0)
        return (lax.div(tp, ns), lax.rem(tp, ns), 0)

    rows = lambda t: blk(t, 0)
    return pl.pallas_call(
        body, name=f"attn_bwd_g{gi}", grid=(T,),
        in_specs=[pl.BlockSpec((1, TQ, GW), lambda t: blk(t, 0)),
                  pl.BlockSpec((1, TQ, GW), lambda t: blk(t, 1)),
                  pl.BlockSpec((1, QBLK, GW), lambda t: prev(t, 1)),
                  pl.BlockSpec((1, TQ, GW), lambda t: blk(t, 2)),
                  pl.BlockSpec((1, QBLK, GW), lambda t: prev(t, 2)),
                  pl.BlockSpec((1, TQ, GW), rows), pl.BlockSpec((1, TQ, GW), rows), pl.BlockSpec((1, TQ, GW), rows),
                  pl.BlockSpec((1, NP, KBLK, 2 * QBLK), lambda t: (gi, 0, 0, 0))],
        out_specs=[pl.BlockSpec((1, TQ, 3 * GW), late),
                   pl.BlockSpec((NP, KBLK, 2 * QBLK), lambda t: (0, 0, 0))],
        out_shape=[SDS((dil, L, 3 * GW), BF16), SDS((NP, KBLK, 2 * QBLK), F32)],
        scratch_shapes=[pltpu.VMEM((QBLK + TQ, GW), BF16), pltpu.VMEM((QBLK + TQ, GW), BF16),
                        pltpu.VMEM((QBLK + TQ, GW), F32), pltpu.VMEM((QBLK + TQ, GW), F32),
                        pltpu.VMEM((TQ, GW), F32), pltpu.VMEM((TQ, GW), F32),
                        pltpu.VMEM((TQ, GW), F32), pltpu.VMEM((TQ, GW), F32)],
        compiler_params=_cp(("arbitrary",)),
    )(zq, zq, zq, zq, zq, do, o, lse, bias_t)


def _dz_block(k):
    if k < 9:
        return k % 3, k // 3
    if k < 13:
        return 3, k - 9
    return 4, k - 13


_DZ_SRC = np.array([_dz_block(k)[0] for k in range(17)], np.int32)


def _dz_hold(s):
    uses = [(k, _dz_block(k)[1]) for k in range(17) if _dz_block(k)[0] == s]
    hold = []
    for k in range(17):
        nxt = [b for kk, b in uses if kk >= k]
        hold.append(nxt[0] if nxt else uses[-1][1])
    return np.array(hold, np.int32)


def _table(tab, k):
    out = jnp.int32(int(tab[0]))
    for idx in range(1, len(tab)):
        out = jnp.where(k == idx, jnp.int32(int(tab[idx])), out)
    return out


def _w_in_tile(s, blk):
    return blk * 3 + s if s < 3 else (9 if s == 3 else 13) + blk


def _in_bwd(dqkv, dglu, dzg, w_inT, x, dx1, g, rider):
    S = x.shape[0]
    TM = 512

    def body(d0, d1, d2, d3, d4, w_ref, x_ref, dx1_ref, g_ref, gx_ref, dg_ref, scr):
        i = pl.program_id(0)

        @pl.when(i == 0)
        def _():
            dg_ref[...] = jnp.zeros_like(dg_ref)

        def rows(s, blk):
            k = _w_in_tile(s, blk)
            return w_ref[k * GW:(k + 1) * GW, :]

        dh = jnp.zeros((TM, D), F32)
        for blk in range(3):
            dh = dh + _dot(d0[0, :, blk * GW:(blk + 1) * GW], rows(0, blk))
        for s, ref in ((3, d3), (4, d4)):
            for blk in range(4):
                dh = dh + _dot(ref[:, blk * GW:(blk + 1) * GW], rows(s, blk))
        for s, ref in ((1, d1), (2, d2)):
            dil = DILATIONS[s]
            part = jnp.zeros((TM, D), F32)
            for blk in range(3):
                part = part + _dot(ref[:, :, blk * GW:(blk + 1) * GW].reshape(TM, GW), rows(s, blk))
            _merge_residues(scr, dil, lambda c, part=part, dil=dil: part[c * (TM // dil):(c + 1) * (TM // dil)])
            dh = dh + _load_cols(scr)
        xf = x_ref[...]
        r = lax.rsqrt(jnp.mean(xf * xf, axis=-1, keepdims=True) + RMS_EPS)
        nrm = xf * r
        dg_ref[...] += _colsum8(dh * nrm)
        dn = dh * g_ref[...]
        gx_ref[...] = dx1_ref[...] + r * (dn - nrm * jnp.mean(dn * nrm, axis=-1, keepdims=True))

    rowd = pl.BlockSpec((TM, D), lambda i: (i, 0))
    wide = pl.BlockSpec((TM, 2 * D), lambda i: (i, 0))
    body, r_in, r_out, r_shape, r_scr = _ride(body, 9, 2, 1, rider, S // TM)
    return pl.pallas_call(
        body, name="in_bwd", grid=(S // TM,),
        in_specs=[_residue_spec(TM, d, 3 * GW) for d in DILATIONS] + [wide, wide]
        + [pl.BlockSpec(w_inT.shape, lambda i: (0, 0), pipeline_mode=pl.Buffered(1)), rowd, rowd,
           pl.BlockSpec((1, D), lambda i: (0, 0))] + r_in,
        out_specs=[rowd, pl.BlockSpec((LANE_ROWS, D), lambda i: (0, 0))] + r_out,
        out_shape=[SDS((S, D), F32), SDS((LANE_ROWS, D), F32)] + r_shape,
        scratch_shapes=[_col_scratch(TM, D)] + r_scr,
        compiler_params=_cp(("arbitrary",)),
    )(*dqkv, dglu, dzg, w_inT, x, dx1, g, *rider.ins)


def _dw_in(dqkv, dglu, dzg, hs):
    S = hs[0].shape[0]
    TS = min(2048, S)
    nk = 17
    holds = [_dz_hold(s) for s in range(5)]
    h_of = (0, 1, 2, 0, 0)

    def body(d0, d1, d2, d3, d4, h0, h1, h2, o_ref, acc):
        m = pl.program_id(0)
        s_ = pl.program_id(1)

        @pl.when(s_ == 0)
        def _():
            acc[...] = jnp.zeros_like(acc)

        src = _table(_DZ_SRC, m)
        pairs = ((d0, h0), (d1, h1), (d2, h2), (d3, h0), (d4, h0))
        for s, (dref, href) in enumerate(pairs):
            @pl.when(src == s)
            def _(dref=dref, href=href):
                acc[...] += _dot_tn(dref[...].reshape(TS, GW), href[...].reshape(TS, D))

        @pl.when(s_ == pl.num_programs(1) - 1)
        def _():
            o_ref[...] = acc[...].astype(BF16)

    def row(s, m, s_):
        return jnp.where(_table(_DZ_SRC, m) == s, s_, 0)

    def dspec(s):
        if s < 3:
            dil = DILATIONS[s]
            return pl.BlockSpec((dil, TS // dil, GW), lambda m, s_: (0, row(s, m, s_), _table(holds[s], m)))
        return pl.BlockSpec((TS, GW), lambda m, s_: (row(s, m, s_), _table(holds[s], m)))

    def hrow(j, m, s_):
        used = _table(np.array([int(h_of[_dz_block(k)[0]] == j) for k in range(nk)], np.int32), m)
        return jnp.where(used == 1, s_, 0)

    hspecs = [pl.BlockSpec((TS, D), lambda m, s_: (hrow(0, m, s_), 0))] + [
        pl.BlockSpec((DILATIONS[j], TS // DILATIONS[j], D), lambda m, s_, j=j: (0, hrow(j, m, s_), 0)) for j in (1, 2)]
    return pl.pallas_call(
        body, name="dw_in", grid=(nk, S // TS),
        in_specs=[dspec(s) for s in range(5)] + hspecs,
        out_specs=pl.BlockSpec((GW, D), lambda m, s_: (m, 0)),
        out_shape=SDS((nk * GW, D), BF16),
        scratch_shapes=[pltpu.VMEM((GW, D), F32)],
        compiler_params=_cp(("arbitrary", "arbitrary")),
    )(*dqkv, dglu, dzg, *hs)


def _mm_tn(a, b, tm, a_maps, name):
    S, N = b.shape
    parts = len(a_maps)
    tp = tm // parts
    nm = len(a_maps[0])
    TS = min(4096 if tm <= 512 else 2048, S)
    tabs = [np.array(t, np.int32) for t in a_maps]

    def body(*refs):
        a_refs = refs[:parts]
        b_ref, o_ref, acc = refs[parts:]
        s_ = pl.program_id(1)

        @pl.when(s_ == 0)
        def _():
            acc[...] = jnp.zeros_like(acc)

        for p, ar in enumerate(a_refs):
            acc[p * tp:(p + 1) * tp, :] += _dot_tn(ar[...], b_ref[...])

        @pl.when(s_ == pl.num_programs(1) - 1)
        def _():
            o_ref[...] = acc[...].astype(BF16)

    return pl.pallas_call(
        body, name=name, grid=(nm, S // TS),
        in_specs=[pl.BlockSpec((TS, tp), lambda m, s_, t=t: (s_, _table(t, m))) for t in tabs]
        + [pl.BlockSpec((TS, N), lambda m, s_: (s_, 0))],
        out_specs=pl.BlockSpec((tm, N), lambda m, s_: (m, 0)),
        out_shape=SDS((nm * tm, N), BF16),
        scratch_shapes=[pltpu.VMEM((tm, N), F32)],
        compiler_params=_cp(("arbitrary", "arbitrary")),
    )(*([a] * parts), b)


def _row_tile(rows, cols, limit=1 << 20):
    if rows * cols * 4 <= limit:
        return rows
    best = None
    for t in range(8, rows, 8):
        if rows % t == 0 and t * cols * 4 <= limit:
            best = t
    return best


def _adamw(w, g, m, v, name):
    R, C = w.shape
    tr = _row_tile(R, C)

    def body(w_ref, g_ref, m_ref, v_ref, d_ref, nm_ref, nv_ref):
        gg = g_ref[...]
        nm = ADAM_B1 * m_ref[...] + (1.0 - ADAM_B1) * gg
        nv = ADAM_B2 * v_ref[...] + (1.0 - ADAM_B2) * (gg * gg)
        m_hat = nm / (1.0 - ADAM_B1 ** ADAM_STEP)
        v_hat = nv / (1.0 - ADAM_B2 ** ADAM_STEP)
        d_ref[...] = -ADAM_LR * (m_hat / (jnp.sqrt(v_hat) + ADAM_EPS) + ADAM_WD * w_ref[...])
        nm_ref[...] = nm
        nv_ref[...] = nv

    spec = pl.BlockSpec((tr, C), lambda i: (i, 0))
    return pl.pallas_call(
        body, name=name, grid=(R // tr,), in_specs=[spec] * 4, out_specs=[spec] * 3,
        out_shape=[SDS((R, C), F32)] * 3, compiler_params=_cp(("arbitrary",)),
    )(w, g, m, v)


_FLIPS = ((1, 0), (0, 1), (1, 1))


def _place():
    x, y, c = lax.axis_index("x"), lax.axis_index("y"), lax.axis_index("c")
    return x, y, c


def _peer_chips(x, y):
    return [((x + fx) % 2, (y + fy) % 2) for fx, fy in _FLIPS]


def _gather_weights(shards):
    nw = len(shards)
    views = [s.reshape(2, s.shape[0] // 2, s.shape[1]) for s in shards]

    def body(*refs):
        ins = refs[:nw]
        outs = refs[nw:2 * nw]
        ici_send, ici_recv, d2d_send, d2d_recv, loc = refs[2 * nw:]
        x, y, c = _place()
        j = 2 * x + y
        chips = _peer_chips(x, y)
        copies = []
        for w in range(nw):
            cp = pltpu.make_async_copy(ins[w], outs[w].at[j], loc.at[w])
            cp.start()
            copies.append(cp)
        sends = []
        for w in range(nw):
            for k, (px, py) in enumerate(chips):
                cp = pltpu.make_async_remote_copy(
                    src_ref=ins[w].at[c], dst_ref=outs[w].at[j, c], send_sem=ici_send.at[w, k],
                    recv_sem=ici_recv.at[w, k], device_id=(px, py, c), device_id_type=MESH)
                cp.start()
                sends.append(cp)
        for w in range(nw):
            for k, (px, py) in enumerate(chips):
                jk = 2 * px + py
                land = outs[w].at[jk, c]
                pltpu.make_async_remote_copy(
                    src_ref=ins[w].at[c], dst_ref=land, send_sem=ici_send.at[w, k],
                    recv_sem=ici_recv.at[w, k], device_id=(px, py, c), device_id_type=MESH).wait_recv()
                cp = pltpu.make_async_remote_copy(
                    src_ref=land, dst_ref=land, send_sem=d2d_send.at[w, k],
                    recv_sem=d2d_recv.at[w, k], device_id=(x, y, 1 - c), device_id_type=MESH)
                cp.start()
                sends.append(cp)
        for w in range(nw):
            for k, (px, py) in enumerate(chips):
                jk = 2 * px + py
                land = outs[w].at[jk, 1 - c]
                pltpu.make_async_remote_copy(
                    src_ref=land, dst_ref=land, send_sem=d2d_send.at[w, k],
                    recv_sem=d2d_recv.at[w, k], device_id=(x, y, 1 - c), device_id_type=MESH).wait_recv()
        for cp in sends:
            cp.wait_send()
        for cp in copies:
            cp.wait()

    outs = pl.pallas_call(
        body, name="gather_weights",
        in_specs=[ANY] * nw, out_specs=[ANY] * nw,
        out_shape=[SDS((4,) + v.shape, BF16) for v in views],
        scratch_shapes=[pltpu.SemaphoreType.DMA((nw, 3)), pltpu.SemaphoreType.DMA((nw, 3)),
                        pltpu.SemaphoreType.DMA((nw, 3)), pltpu.SemaphoreType.DMA((nw, 3)),
                        pltpu.SemaphoreType.DMA((nw,))],
    )(*views)
    return [o.reshape(4 * s.shape[0], s.shape[1]) for o, s in zip(outs, shards)]


class _Rider:
    def __init__(self, ins, out_shape, scratch, start, finish, mid=None):
        self.ins, self.out_shape, self.scratch = list(ins), list(out_shape), list(scratch)
        self.start, self.finish, self.mid = start, finish, mid


def _ride(body, n_in, n_out, n_scr, rider, steps):
    if rider is None:
        return body, [], [], [], []
    ri, ro = len(rider.ins), len(rider.out_shape)

    def wrapped(*refs):
        ins, r_ins = refs[:n_in], refs[n_in:n_in + ri]
        o0 = n_in + ri
        outs, r_outs = refs[o0:o0 + n_out], refs[o0 + n_out:o0 + n_out + ro]
        s0 = o0 + n_out + ro
        scr, r_scr = refs[s0:s0 + n_scr], refs[s0 + n_scr:]
        i = pl.program_id(0)

        @pl.when(i == 0)
        def _():
            rider.start(r_ins, r_outs, r_scr)

        if rider.mid is not None:
            @pl.when(i == (3 * steps) // 4)
            def _():
                rider.mid(r_ins, r_outs, r_scr)

        body(*ins, *outs, *scr)

        @pl.when(i == steps - 1)
        def _():
            rider.finish(r_ins, r_outs, r_scr)

    return wrapped, [ANY] * ri, [ANY] * ro, rider.out_shape, rider.scratch


def _gather_rider(shards):
    nw = len(shards)
    views = [s.reshape(2, s.shape[0] // 2, s.shape[1]) for s in shards]

    def parts(ins, outs, sems):
        ici_send, ici_recv, d2d_send, d2d_recv, loc = sems
        x, y, c = _place()
        j = 2 * x + y
        local, ici, land_ici, fwd, land_fwd = [], [], [], [], []
        for w in range(nw):
            local.append(pltpu.make_async_copy(ins[w], outs[w].at[j], loc.at[w]))
            for k, (px, py) in enumerate(_peer_chips(x, y)):
                jk = 2 * px + py
                ici.append(pltpu.make_async_remote_copy(
                    src_ref=ins[w].at[c], dst_ref=outs[w].at[j, c], send_sem=ici_send.at[w, k],
                    recv_sem=ici_recv.at[w, k], device_id=(px, py, c), device_id_type=MESH))
                mine = outs[w].at[jk, c]
                land_ici.append(pltpu.make_async_remote_copy(
                    src_ref=ins[w].at[c], dst_ref=mine, send_sem=ici_send.at[w, k],
                    recv_sem=ici_recv.at[w, k], device_id=(px, py, c), device_id_type=MESH))
                fwd.append(pltpu.make_async_remote_copy(
                    src_ref=mine, dst_ref=mine, send_sem=d2d_send.at[w, k],
                    recv_sem=d2d_recv.at[w, k], device_id=(x, y, 1 - c), device_id_type=MESH))
                theirs = outs[w].at[jk, 1 - c]
                land_fwd.append(pltpu.make_async_remote_copy(
                    src_ref=theirs, dst_ref=theirs, send_sem=d2d_send.at[w, k],
                    recv_sem=d2d_recv.at[w, k], device_id=(x, y, 1 - c), device_id_type=MESH))
        return local, ici, land_ici, fwd, land_fwd

    def start(ins, outs, sems):
        local, ici, _, _, _ = parts(ins, outs, sems)
        for cp in local + ici:
            cp.start()

    def mid(ins, outs, sems):
        _, _, land_ici, fwd, _ = parts(ins, outs, sems)
        for landed, cp in zip(land_ici, fwd):
            landed.wait_recv()
            cp.start()

    def finish(ins, outs, sems):
        local, ici, _, fwd, land_fwd = parts(ins, outs, sems)
        for cp in land_fwd:
            cp.wait_recv()
        for cp in ici + fwd:
            cp.wait_send()
        for cp in local:
            cp.wait()

    sem = pltpu.SemaphoreType.DMA
    return _Rider(views, [SDS((4,) + v.shape, BF16) for v in views],
                  [sem((nw, 3)), sem((nw, 3)), sem((nw, 3)), sem((nw, 3)), sem((nw,))], start, finish, mid)


def _chip_exchange_rider(parts):
    nw = len(parts)

    def copies(ins, outs, sems):
        send, recv = sems
        x, y, c = _place()
        return [pltpu.make_async_remote_copy(
            src_ref=ins[w].at[2 * px + py], dst_ref=outs[w].at[k], send_sem=send.at[w, k],
            recv_sem=recv.at[w, k], device_id=(px, py, c), device_id_type=MESH)
            for w in range(nw) for k, (px, py) in enumerate(_peer_chips(x, y))]

    def start(ins, outs, sems):
        for cp in copies(ins, outs, sems):
            cp.start()

    def finish(ins, outs, sems):
        for cp in copies(ins, outs, sems):
            cp.wait()

    sem = pltpu.SemaphoreType.DMA
    return _Rider(parts, [SDS((3,) + p.shape[1:], BF16) for p in parts], [sem((nw, 3)), sem((nw, 3))], start, finish)


def _pair_exchange(grads, name):
    nw = len(grads)

    def body(*refs):
        ins = refs[:nw]
        outs = refs[nw:2 * nw]
        send, recv = refs[2 * nw:]
        x, y, c = _place()
        cps = []
        for w in range(nw):
            cp = pltpu.make_async_remote_copy(
                src_ref=ins[w].at[:, pl.ds(1 - c, 1)], dst_ref=outs[w], send_sem=send.at[w], recv_sem=recv.at[w],
                device_id=(x, y, 1 - c), device_id_type=MESH)
            cp.start()
            cps.append(cp)
        for cp in cps:
            cp.wait()

    return pl.pallas_call(
        body, name=name, in_specs=[ANY] * nw, out_specs=[ANY] * nw,
        out_shape=[SDS((4, 1) + g.shape[2:], BF16) for g in grads],
        scratch_shapes=[pltpu.SemaphoreType.DMA((nw,)), pltpu.SemaphoreType.DMA((nw,))],
    )(*grads)


def _half_tile(rh):
    best = 16
    for t in range(16, 545, 16):
        if rh % t == 0:
            best = t
    return best


def _pair_sum(c_arr, g, got, name):
    _, _, rh, n = g.shape
    tr = _half_tile(rh)

    def body(c_ref, a_ref, b_ref, o_ref):
        o_ref[...] = (a_ref[...].astype(F32) + b_ref[...].astype(F32)).astype(BF16)

    return pl.pallas_call(
        body, name=name,
        grid_spec=pltpu.PrefetchScalarGridSpec(
            num_scalar_prefetch=1, grid=(4, rh // tr),
            in_specs=[pl.BlockSpec((1, 1, tr, n), lambda s, i, c: (s, c[0], i, 0)),
                      pl.BlockSpec((1, 1, tr, n), lambda s, i, c: (s, 0, i, 0))],
            out_specs=pl.BlockSpec((1, 1, tr, n), lambda s, i, c: (s, 0, i, 0))),
        out_shape=SDS((4, 1, rh, n), BF16),
        compiler_params=_cp(("arbitrary", "arbitrary")),
    )(c_arr, g, got)


def _chip_sum(jc_arr, part, got, name):
    _, _, rh, n = part.shape
    tr = _half_tile(rh)

    def body(jc_ref, a_ref, b_ref, o_ref):
        acc = a_ref[0, 0].astype(F32)
        for k in range(3):
            acc = acc + b_ref[k, 0].astype(F32)
        o_ref[0] = acc

    return pl.pallas_call(
        body, name=name,
        grid_spec=pltpu.PrefetchScalarGridSpec(
            num_scalar_prefetch=1, grid=(rh // tr,),
            in_specs=[pl.BlockSpec((1, 1, tr, n), lambda i, jc: (jc[0], 0, i, 0)),
                      pl.BlockSpec((3, 1, tr, n), lambda i, jc: (0, 0, i, 0))],
            out_specs=pl.BlockSpec((1, tr, n), lambda i, jc: (jc[1], i, 0))),
        out_shape=SDS((2, rh, n), F32),
        compiler_params=_cp(("arbitrary",)),
    )(jc_arr, part, got)


def _half_swap(halves):
    nw = len(halves)

    def body(*refs):
        ins = refs[:nw]
        outs = refs[nw:2 * nw]
        send, recv = refs[2 * nw:]
        x, y, c = _place()
        cps = []
        for w in range(nw):
            cp = pltpu.make_async_remote_copy(
                src_ref=ins[w].at[c], dst_ref=outs[w].at[c], send_sem=send.at[w], recv_sem=recv.at[w],
                device_id=(x, y, 1 - c), device_id_type=MESH)
            cp.start()
            cps.append(cp)
        for cp in cps:
            cp.wait()

    return pl.pallas_call(
        body, name="grad_half_swap", in_specs=[ANY] * nw, out_specs=[ANY] * nw,
        out_shape=[SDS(h.shape, F32) for h in halves],
        input_output_aliases={w: w for w in range(nw)},
        scratch_shapes=[pltpu.SemaphoreType.DMA((nw,)), pltpu.SemaphoreType.DMA((nw,))],
    )(*halves)


def _all_sum_small(part, name):
    R = part.shape[0]

    def body(p_ref, o_ref, land, send, recv):
        x, y, c = _place()
        me = 4 * x + 2 * y + c
        cps = []
        for d in range(1, 8):
            t = (me + d) % 8
            cp = pltpu.make_async_remote_copy(
                src_ref=p_ref, dst_ref=land.at[me], send_sem=send.at[d - 1], recv_sem=recv.at[d - 1],
                device_id=(t // 4, (t // 2) % 2, t % 2), device_id_type=MESH)
            cp.start()
            cps.append(cp)
        land[me] = p_ref[...]
        for cp in cps:
            cp.wait()
        acc = land[0]
        for d in range(1, 8):
            acc = acc + land[d]
        o_ref[...] = acc

    return pl.pallas_call(
        body, name=name,
        in_specs=[pl.BlockSpec(memory_space=pltpu.VMEM)], out_specs=pl.BlockSpec(memory_space=pltpu.VMEM),
        out_shape=SDS((R, D), F32),
        scratch_shapes=[pltpu.VMEM((8, R, D), F32), pltpu.SemaphoreType.DMA((7,)), pltpu.SemaphoreType.DMA((7,))],
        compiler_params=pltpu.CompilerParams(vmem_limit_bytes=VMEM_LIMIT),
    )(part)


def _pad_rows(a, rows):
    return jnp.pad(a, ((0, rows - a.shape[0]), (0, 0)))


def _vec_pack(vs):
    return jnp.concatenate([_pad_rows(v, LANE_ROWS) for v in vs], axis=0)


def kernel(x, rel_bias_table, g_pre_mix, w_in, b_glu, w_dw, b_dw, g_conv_ln, b_conv_ln, w_conv_out, b_conv_out, w_attn_out, w_mix_out, g_post_mix, g_pre_ffn, w_ffn_in, w_ffn_out, g_post_ffn, loss_target, m_rel_bias_table, m_g_pre_mix, m_w_in, m_b_glu, m_w_dw, m_b_dw, m_g_conv_ln, m_b_conv_ln, m_w_conv_out, m_b_conv_out, m_w_attn_out, m_w_mix_out, m_g_post_mix, m_g_pre_ffn, m_w_ffn_in, m_w_ffn_out, m_g_post_ffn, v_rel_bias_table, v_g_pre_mix, v_w_in, v_b_glu, v_w_dw, v_b_dw, v_g_conv_ln, v_b_conv_ln, v_w_conv_out, v_b_conv_out, v_w_attn_out, v_w_mix_out, v_g_post_mix, v_g_pre_ffn, v_w_ffn_in, v_w_ffn_out, v_g_post_ffn):
    S = x.shape[1]
    xs = x.reshape(S, D)
    tgt = loss_target.reshape(S, D)
    cx, cy, cc = _place()
    chip = 2 * cx + cy

    shards = [w_in[0].T.astype(BF16),
              w_ffn_in[0].T.astype(BF16),
              w_attn_out[0].T.astype(BF16),
              w_conv_out[0].astype(BF16),
              w_mix_out[0].astype(BF16),
              w_ffn_out[0].astype(BF16)]
    (w_inT,) = _gather_weights(shards[:1])
    w_inN = w_inT.T

    buckets_np, valid_np = _bucket_tables()
    buckets = jnp.asarray(buckets_np)
    bias = _bias_expand(rel_bias_table, buckets, jnp.asarray(valid_np)).reshape(3, NH, QBLK, KBLK)
    bias2 = bias.reshape(3, NH // 2, 2 * QBLK, KBLK)
    bias_t = bias.reshape(3, NH // 2, 2, QBLK, KBLK).transpose(0, 1, 4, 2, 3).reshape(3, NH // 2, KBLK, 2 * QBLK)
    wdw32 = _pad_rows(w_dw[0], 32)
    wdw_full = _gather_small_cols(wdw32, chip)

    zrest, h, h_r4, h_r16, *gathered = _in_proj_rest(xs, g_pre_mix, w_inN[:, 3 * ATTN_COLS:], _gather_rider(shards[1:]))
    w_fiT, w_aoT, w_co, w_mx, w_fo = (t.reshape(4 * s.shape[0], s.shape[1]) for t, s in zip(gathered, shards[1:]))
    w_fiN, w_aoN = w_fiT.T, w_aoT.T
    w_coT, w_mxT, w_foT = w_co.T, w_mx.T, w_fo.T
    zq = _in_proj_qkv(h, w_inN[:, :3 * ATTN_COLS])
    og, lg = [], []
    for gi in range(3):
        o_g, l_g = _attn_fwd(zq[gi], bias2, gi)
        og.append(o_g)
        lg.append(l_g)
    cv, a = _conv_fwd(zrest, b_glu, wdw_full, b_dw, g_conv_ln, b_conv_ln)
    o, o_r4, o_r16, lse, lse_r4, lse_r16, ya, yc, mg, mm, x1 = _mix_fwd(
        og, lg, a, zrest, xs, w_aoN, w_co, b_conv_out, w_mx, g_post_mix)
    h2, gu, df, dx2, loss8, dg_post_ffn = _ffn_fwd(x1, tgt, g_pre_ffn, g_post_ffn, w_fiN, w_fo)

    c_arr = jnp.reshape(cc, (1,)).astype(jnp.int32)
    jc_arr = jnp.stack([chip, cc]).astype(jnp.int32)
    ident = lambda n: [list(range(n))]

    def pair_sums(partials, names, tag):
        views = [g.reshape(4, 2, g.shape[0] // 8, g.shape[1]) for g in partials]
        got = _pair_exchange(views, f"grad_pair_exchange_{tag}")
        return [_pair_sum(c_arr, v, r, f"pair_sum_{n}") for v, r, n in zip(views, got, names)]

    def chip_sums(pair, got, names):
        return [_chip_sum(jc_arr, p, r, f"chip_sum_{n}") for p, r, n in zip(pair, got, names)]

    dff, act = _ffn_bwd_act(df, gu, w_foT)
    g_fiT = _mm_tn(dff, h2, 512, [[2 * t if t < NFT else 2 * (t - NFT) + 1 for t in range(0, 22, 2)],
                                  [2 * t if t < NFT else 2 * (t - NFT) + 1 for t in range(1, 22, 2)]], "dw_ffn_in")
    g_fo = _mm_tn(act, df, FFN_H // 2, ident(2), "dw_ffn_out")
    names_a = ("w_ffn_in", "w_ffn_out")
    pair_a = pair_sums([g_fiT, g_fo], names_a, "ffn")
    dx1, dg_pre_ffn, *got_a = _ffn_bwd_in(dff, x1, dx2, g_pre_ffn, w_fiT, _chip_exchange_rider(pair_a))
    halves_a = chip_sums(pair_a, got_a, names_a)
    dmm, dya, dyc, do, do_r4, do_r16, da, dzg, dg_post_mix, db_conv_out = _mix_bwd(
        dx1, mm, ya, yc, zrest, g_post_mix, w_mxT, w_aoT, w_coT)
    dglu, db_glu_u, db_glu_g, dw_dw, dg_conv_ln, db_conv_ln, db_dw = _conv_bwd(da, cv, zrest, b_glu, wdw_full, g_conv_ln, b_conv_ln)
    first = lambda t: t.reshape(1, S, GW)
    dqkv, dbias = [], []
    for gi, (do_g, o_g, lse_g) in enumerate(((first(do), first(o), first(lse)), (do_r4, o_r4, lse_r4),
                                            (do_r16, o_r16, lse_r16))):
        d_g, db_g = _attn_bwd(zq[gi], do_g, o_g, lse_g, bias_t, gi)
        dqkv.append(d_g)
        dbias.append(db_g.reshape(NH // 2, KBLK, 2, QBLK).transpose(0, 2, 3, 1).reshape(NH, QBLK, KBLK))
    dtab = _bias_reduce(jnp.concatenate(dbias, axis=0), buckets)

    g_inT = _dw_in(dqkv, dglu, dzg, (h, h_r4, h_r16))
    g_aoT = _mm_tn(dya, o, 512, ident(2), "dw_attn_out")
    g_co = _mm_tn(a, dyc, 512, ident(2), "dw_conv_out")
    g_mx = _mm_tn(mg, dmm, 512, ident(2), "dw_mix_out")
    names_b = ("w_in", "w_attn_out", "w_conv_out", "w_mix_out")
    pair_b = pair_sums([g_inT, g_aoT, g_co, g_mx], names_b, "rest")
    grad_x, dg_pre_mix, *got_b = _in_bwd(dqkv, dglu, dzg, w_inT, xs, dx1, g_pre_mix, _chip_exchange_rider(pair_b))
    halves_b = chip_sums(pair_b, got_b, names_b)

    red = [t.reshape(t.shape[0] * t.shape[1], t.shape[2]) for t in _half_swap(halves_a + halves_b)]
    gw_ffn_in, gw_ffn_out, gw_in, gw_attn_out, gw_conv_out, gw_mix_out = (
        red[0].T, red[1], red[2].T, red[3].T, red[4], red[5])

    small = jnp.concatenate([loss8, dg_pre_mix, db_glu_u, db_glu_g, db_dw, dg_conv_ln, db_conv_ln, db_conv_out,
                             dg_post_mix, dg_pre_ffn, dg_post_ffn, dtab, dw_dw], axis=0)
    tot = _all_sum_small(small, "small_all_sum")
    row = lambda i: tot[LANE_ROWS * i:LANE_ROWS * i + 1]
    loss = tot[0, 0]
    g_g_pre_mix, g_b_glu = row(1), jnp.concatenate([row(2), row(3)], axis=1)
    g_b_dw, g_g_conv_ln, g_b_conv_ln, g_b_conv_out = row(4), row(5), row(6), row(7)
    g_g_post_mix, g_g_pre_ffn, g_g_post_ffn = row(8), row(9), row(10)
    g_tab = tot[88:112, 0:32].T
    g_w_dw = lax.dynamic_slice(tot[112:112 + CONV_W], (0, 256 * chip), (CONV_W, 256))

    vec_names = ["g_pre_mix", "b_dw", "g_conv_ln", "b_conv_ln", "b_conv_out", "g_post_mix", "g_pre_ffn", "g_post_ffn"]
    vec_w = [g_pre_mix, b_dw, g_conv_ln, b_conv_ln, b_conv_out, g_post_mix, g_pre_ffn, g_post_ffn]
    vec_m = [m_g_pre_mix, m_b_dw, m_g_conv_ln, m_b_conv_ln, m_b_conv_out, m_g_post_mix, m_g_pre_ffn, m_g_post_ffn]
    vec_v = [v_g_pre_mix, v_b_dw, v_g_conv_ln, v_b_conv_ln, v_b_conv_out, v_g_post_mix, v_g_pre_ffn, v_g_post_ffn]
    vec_g = [g_g_pre_mix, g_b_dw, g_g_conv_ln, g_b_conv_ln, g_b_conv_out, g_g_post_mix, g_g_pre_ffn, g_g_post_ffn]

    def pack(vs, glu, tab, dw):
        return jnp.concatenate([_vec_pack(vs), _pad_rows(glu.reshape(2, D), LANE_ROWS),
                                _pad_rows(jnp.pad(tab.T, ((0, 0), (0, D - 32))), 24),
                                _pad_rows(jnp.pad(dw, ((0, 0), (0, D - 256))), 32)], axis=0)

    sw = pack(vec_w, b_glu, rel_bias_table, w_dw[0])
    sg = pack(vec_g, g_b_glu, g_tab, g_w_dw)
    sm = pack(vec_m, m_b_glu, m_rel_bias_table, m_w_dw[0])
    sv = pack(vec_v, v_b_glu, v_rel_bias_table, v_w_dw[0])
    s_out = _adamw(sw, sg, sm, sv, "adamw_small")

    def unpack(t):
        vecs = {n: t[LANE_ROWS * i:LANE_ROWS * i + 1] for i, n in enumerate(vec_names)}
        vecs["b_glu"] = t[64:66].reshape(1, 2 * D)
        vecs["rel_bias_table"] = t[72:96, 0:32].T
        vecs["w_dw"] = t[96:96 + CONV_W, 0:256][None]
        return vecs

    small_out = [unpack(t) for t in s_out]
    big = {}
    for n, w, g, m, v in (("w_in", w_in, gw_in, m_w_in, v_w_in),
                          ("w_conv_out", w_conv_out, gw_conv_out, m_w_conv_out, v_w_conv_out),
                          ("w_attn_out", w_attn_out, gw_attn_out, m_w_attn_out, v_w_attn_out),
                          ("w_mix_out", w_mix_out, gw_mix_out, m_w_mix_out, v_w_mix_out),
                          ("w_ffn_in", w_ffn_in, gw_ffn_in, m_w_ffn_in, v_w_ffn_in),
                          ("w_ffn_out", w_ffn_out, gw_ffn_out, m_w_ffn_out, v_w_ffn_out)):
        big[n] = [t[None] for t in _adamw(w[0], g, m[0], v[0], f"adamw_{n}")]

    order = ["rel_bias_table", "g_pre_mix", "w_in", "b_glu", "w_dw", "b_dw", "g_conv_ln", "b_conv_ln", "w_conv_out",
             "b_conv_out", "w_attn_out", "w_mix_out", "g_post_mix", "g_pre_ffn", "w_ffn_in", "w_ffn_out", "g_post_ffn"]
    grads = {"rel_bias_table": g_tab, "g_pre_mix": g_g_pre_mix, "w_in": gw_in[None], "b_glu": g_b_glu,
             "w_dw": g_w_dw[None], "b_dw": g_b_dw, "g_conv_ln": g_g_conv_ln, "b_conv_ln": g_b_conv_ln,
             "w_conv_out": gw_conv_out[None], "b_conv_out": g_b_conv_out, "w_attn_out": gw_attn_out[None],
             "w_mix_out": gw_mix_out[None], "g_post_mix": g_g_post_mix, "g_pre_ffn": g_g_pre_ffn,
             "w_ffn_in": gw_ffn_in[None], "w_ffn_out": gw_ffn_out[None], "g_post_ffn": g_g_post_ffn}
    outs = [loss, grad_x.reshape(1, S, D)] + [grads[n] for n in order]
    for slot in range(3):
        outs += [big[n][slot] if n in big else small_out[slot][n] for n in order]
    return tuple(outs)


def _gather_small_cols(wdw32, chip):
    placed = lax.dynamic_update_slice(jnp.zeros((32, D), F32), wdw32, (0, 256 * chip))
    return _all_sum_small(placed, "conv_taps_gather") * 0.5
```

```python
import functools
import math

import numpy as np
import jax
import jax.numpy as jnp
from jax import lax
from jax.experimental import pallas as pl
from jax.experimental.pallas import tpu as pltpu

F32 = jnp.float32
BF16 = jnp.bfloat16
SDS = jax.ShapeDtypeStruct
MESH = pl.DeviceIdType.MESH
ANY = pl.BlockSpec(memory_space=pl.ANY)

D = 1024
HD = 64
NH = 8
GW = NH * HD
ATTN_COLS = 3 * GW
DILATIONS = (1, 4, 16)
SPAN = 128
QBLK = 128
KBLK = 2 * QBLK
CONV_W = 31
FFN_H = 2816
FFN_T = 256
NFT = FFN_H // FFN_T
RMS_EPS = 1e-6
LN_EPS = 1e-5
NEG_INF = -1e30
SCALE = HD ** -0.5
LANE_ROWS = 8
LANES = 128

ADAM_LR, ADAM_B1, ADAM_B2, ADAM_EPS, ADAM_WD, ADAM_STEP = 0.001, 0.9, 0.999, 1e-08, 0.01, 10

VMEM_LIMIT = 56 * 1024 * 1024


def _cp(sem):
    return pltpu.CompilerParams(dimension_semantics=sem, vmem_limit_bytes=VMEM_LIMIT)


def _dot(a, b):
    return jnp.dot(a, b, preferred_element_type=F32)


def _dot_nt(a, b):
    return lax.dot_general(a, b, (((1,), (1,)), ((), ())), preferred_element_type=F32)


def _dot_tn(a, b):
    return lax.dot_general(a, b, (((0,), (0,)), ((), ())), preferred_element_type=F32)


def _sigmoid(v):
    return 0.5 * jnp.tanh(0.5 * v) + 0.5


def _colsum8(v):
    s = jnp.sum(v, axis=0, keepdims=True)
    row = lax.broadcasted_iota(jnp.int32, (LANE_ROWS, v.shape[1]), 0)
    return jnp.where(row == 0, jnp.broadcast_to(s, (LANE_ROWS, v.shape[1])), 0.0)


def _col_scratch(n, width):
    return pltpu.VMEM((width // LANES, n, LANES), F32)


def _store_cols(scr, v):
    for lb in range(scr.shape[0]):
        scr[lb] = v[:, lb * LANES:(lb + 1) * LANES]


def _load_cols(scr):
    return jnp.concatenate([scr[lb] for lb in range(scr.shape[0])], axis=1)


def _split_residues(scr, dil, put):
    nb, n, _ = scr.shape
    for c in range(dil):
        put(c, jnp.concatenate([scr[lb, pl.ds(c, n // dil, stride=dil), :] for lb in range(nb)], axis=1))


def _merge_residues(scr, dil, get):
    nb, n, _ = scr.shape
    for c in range(dil):
        v = get(c)
        for lb in range(nb):
            scr[lb, pl.ds(c, n // dil, stride=dil), :] = v[:, lb * LANES:(lb + 1) * LANES]


def _residue_shape(S, dil, width):
    return (dil, S // dil, width)


def _residue_spec(TM, dil, width):
    return pl.BlockSpec((dil, TM // dil, width), lambda i: (0, i, 0))


def _in_proj_rest(x, g, w, rider):
    S = x.shape[0]
    N = w.shape[1]
    TM, TN = 512, 512

    def body(x_ref, g_ref, w_ref, zr_ref, h0_ref, h1_ref, h2_ref, hf_scr):
        xf = x_ref[...]
        r = lax.rsqrt(jnp.mean(xf * xf, axis=-1, keepdims=True) + RMS_EPS)
        hf = xf * r * g_ref[...]
        h0_ref[...] = hf.astype(BF16)
        _store_cols(hf_scr, hf)
        for dil, ref in ((DILATIONS[1], h1_ref), (DILATIONS[2], h2_ref)):
            def put(c, v, ref=ref):
                ref[c] = v.astype(BF16)
            _split_residues(hf_scr, dil, put)
        for j in range(N // TN):
            zr_ref[:, j * TN:(j + 1) * TN] = _dot(h0_ref[...], w_ref[:, j * TN:(j + 1) * TN]).astype(BF16)

    body, r_in, r_out, r_shape, r_scr = _ride(body, 3, 4, 1, rider, S // TM)
    return pl.pallas_call(
        body, name="in_proj_rest", grid=(S // TM,),
        in_specs=[pl.BlockSpec((TM, D), lambda i: (i, 0)),
                  pl.BlockSpec((1, D), lambda i: (0, 0)),
                  pl.BlockSpec((D, N), lambda i: (0, 0), pipeline_mode=pl.Buffered(1))] + r_in,
        out_specs=[pl.BlockSpec((TM, N), lambda i: (i, 0)), pl.BlockSpec((TM, D), lambda i: (i, 0)),
                   _residue_spec(TM, DILATIONS[1], D), _residue_spec(TM, DILATIONS[2], D)] + r_out,
        out_shape=[SDS((S, N), BF16), SDS((S, D), BF16),
                   SDS(_residue_shape(S, DILATIONS[1], D), BF16),
                   SDS(_residue_shape(S, DILATIONS[2], D), BF16)] + r_shape,
        scratch_shapes=[_col_scratch(TM, D)] + r_scr,
        compiler_params=_cp(("arbitrary",)),
    )(x, g, w, *rider.ins)


def _in_proj_qkv(h, w):
    S = h.shape[0]
    TM = 512

    def body(h_ref, w_ref, z0_ref, z1_ref, z2_ref, scr):
        outs = (z0_ref, z1_ref, z2_ref)
        for j in range(9):
            t, gi = j // 3, j % 3
            cols = slice(t * GW, (t + 1) * GW)
            zt = _dot(h_ref[...], w_ref[:, j * GW:(j + 1) * GW])
            if gi == 0:
                z0_ref[0, :, cols] = zt.astype(BF16)
            else:
                slot = scr.at[2 * t + gi - 1]
                _store_cols(slot, zt)

                def put(c, v, ref=outs[gi], cols=cols):
                    ref[c, :, cols] = v.astype(BF16)
                _split_residues(slot, DILATIONS[gi], put)

    return pl.pallas_call(
        body, name="in_proj_qkv", grid=(S // TM,),
        in_specs=[pl.BlockSpec((TM, D), lambda i: (i, 0)),
                  pl.BlockSpec(w.shape, lambda i: (0, 0), pipeline_mode=pl.Buffered(1))],
        out_specs=[_residue_spec(TM, d, 3 * GW) for d in DILATIONS],
        out_shape=[SDS(_residue_shape(S, d, 3 * GW), BF16) for d in DILATIONS],
        scratch_shapes=[pltpu.VMEM((6, GW // LANES, TM, LANES), F32)],
        compiler_params=_cp(("arbitrary",)),
    )(h, w)


def _bucket_tables():
    a = np.arange(QBLK, dtype=np.int32)[:, None]
    c = np.arange(KBLK, dtype=np.int32)[None, :]
    off = a - c + QBLK
    valid = ((off >= 0) & (off <= SPAN)).astype(np.float32)
    tabs = []
    for dil in DILATIONS:
        dist = np.maximum(off * dil, 0)
        df = np.maximum(dist, 1).astype(np.float32)
        large = 16 + (np.log(df / np.float32(16)) / np.float32(math.log(2048 / 16)) * np.float32(16)).astype(np.int32)
        large = np.minimum(large, 31)
        tabs.append(np.where(dist < 16, dist, large).astype(np.int32))
    return np.stack(tabs), valid


def _bias_expand(tab, buckets, valid):
    def body(tab_ref, b_ref, v_ref, o_ref):
        for gi in range(3):
            bk = b_ref[gi]
            for h in range(NH):
                acc = jnp.zeros((QBLK, KBLK), F32)
                for b in range(32):
                    acc = jnp.where(bk == b, tab_ref[b, gi * NH + h], acc)
                o_ref[gi * NH + h] = jnp.where(v_ref[...] > 0.5, acc, NEG_INF)

    return pl.pallas_call(
        body, name="bias_expand",
        in_specs=[pl.BlockSpec(memory_space=pltpu.SMEM),
                  pl.BlockSpec(memory_space=pltpu.VMEM), pl.BlockSpec(memory_space=pltpu.VMEM)],
        out_specs=pl.BlockSpec(memory_space=pltpu.VMEM),
        out_shape=SDS((3 * NH, QBLK, KBLK), F32),
    )(tab, buckets, valid)


def _bias_reduce(dbias, buckets):
    def body(d_ref, b_ref, o_ref):
        lane = lax.broadcasted_iota(jnp.int32, (1, D), 1)
        for gi in range(3):
            bk = b_ref[gi]
            for h in range(NH):
                dv = d_ref[gi * NH + h]
                row = jnp.zeros((1, D), F32)
                for b in range(32):
                    m = jnp.where(bk == b, dv, 0.0)
                    val = jnp.sum(jnp.sum(m, axis=0, keepdims=True), axis=1, keepdims=True)
                    row = jnp.where(lane == b, val, row)
                o_ref[gi * NH + h:gi * NH + h + 1, :] = row

    return pl.pallas_call(
        body, name="bias_reduce",
        in_specs=[pl.BlockSpec(memory_space=pltpu.VMEM), pl.BlockSpec(memory_space=pltpu.VMEM)],
        out_specs=pl.BlockSpec(memory_space=pltpu.VMEM),
        out_shape=SDS((3 * NH, D), F32),
    )(dbias, buckets)


def _attn_tile(S, dil):
    L = S // dil
    tq = min(512, L)
    return L, tq, tq // QBLK, L // tq


def _pair_stack(ref, rows, lanes, scale=None):
    blk = ref[rows, lanes]
    if scale is not None:
        blk = blk * scale
    lane = lax.broadcasted_iota(jnp.int32, blk.shape, 1)
    zero = jnp.zeros_like(blk)
    return jnp.concatenate([jnp.where(lane < HD, blk, zero), jnp.where(lane >= HD, blk, zero)], axis=0)


def _attn_fwd(zq, bias2, gi):
    dil, L, _ = zq.shape
    _, TQ, QB, ns = _attn_tile(L * dil, dil)
    NP = NH // 2

    def body(q_ref, kc_ref, kp_ref, vc_ref, vp_ref, b_ref, o_ref, l_ref, kext, vext):
        n = pl.program_id(1)
        kext[0:QBLK, :] = kp_ref[0]
        kext[QBLK:, :] = kc_ref[0]
        vext[0:QBLK, :] = vp_ref[0]
        vext[QBLK:, :] = vc_ref[0]
        col = lax.broadcasted_iota(jnp.int32, (2 * QBLK, KBLK), 1)
        no_prev = jnp.logical_and(n == 0, col < QBLK)
        lane = lax.broadcasted_iota(jnp.int32, (QBLK, LANES), 1)
        lanes_of = [slice(hp * LANES, (hp + 1) * LANES) for hp in range(NP)]
        for b in range(QB):
            rows = slice(b * QBLK, (b + 1) * QBLK)
            win = slice(b * QBLK, b * QBLK + KBLK)
            s = [_dot_nt(_pair_stack(q_ref.at[0], rows, pl_, SCALE), kext[win, pl_]) + b_ref[0, hp]
                 for hp, pl_ in enumerate(lanes_of)]
            if b == 0:
                s = [jnp.where(no_prev, NEG_INF, v) for v in s]
            m = [jnp.max(v, axis=-1, keepdims=True) for v in s]
            p = [jnp.exp(v - mv) for v, mv in zip(s, m)]
            l = [jnp.sum(v, axis=-1, keepdims=True) for v in p]
            o2 = [_dot(v.astype(BF16), vext[win, pl_]) / lv for v, lv, pl_ in zip(p, l, lanes_of)]
            for hp, pl_ in enumerate(lanes_of):
                lse2 = jnp.broadcast_to(m[hp] + jnp.log(l[hp]), (2 * QBLK, LANES))
                o_ref[0, rows, pl_] = jnp.where(lane < HD, o2[hp][0:QBLK], o2[hp][QBLK:]).astype(BF16)
                l_ref[0, rows, pl_] = jnp.where(lane < HD, lse2[0:QBLK], lse2[QBLK:])

    def prev(n):
        return jnp.maximum(n * QB - 1, 0)

    return pl.pallas_call(
        body, name=f"attn_fwd_g{gi}", grid=(dil, ns),
        in_specs=[pl.BlockSpec((1, TQ, GW), lambda c, n: (c, n, 0)),
                  pl.BlockSpec((1, TQ, GW), lambda c, n: (c, n, 1)),
                  pl.BlockSpec((1, QBLK, GW), lambda c, n: (c, prev(n), 1)),
                  pl.BlockSpec((1, TQ, GW), lambda c, n: (c, n, 2)),
                  pl.BlockSpec((1, QBLK, GW), lambda c, n: (c, prev(n), 2)),
                  pl.BlockSpec((1, NP, 2 * QBLK, KBLK), lambda c, n: (gi, 0, 0, 0))],
        out_specs=[pl.BlockSpec((1, TQ, GW), lambda c, n: (c, n, 0)),
                   pl.BlockSpec((1, TQ, GW), lambda c, n: (c, n, 0))],
        out_shape=[SDS((dil, L, GW), BF16), SDS((dil, L, GW), F32)],
        scratch_shapes=[pltpu.VMEM((QBLK + TQ, GW), BF16), pltpu.VMEM((QBLK + TQ, GW), BF16)],
        compiler_params=_cp(("arbitrary", "arbitrary")),
    )(zq, zq, zq, zq, zq, bias2)


CONV_TM = 256
SHIFT_PAD = 24


def _make_shifts(src, sh, n):
    for b in range(1, 8):
        sh[b - 1] = src[b:b + n + SHIFT_PAD, :]


def _shifted(src, sh, off, r0, n, lanes):
    a, b = divmod(off, 8)
    if b == 0:
        return src[8 * a + r0:8 * a + r0 + n, lanes]
    return sh[b - 1, 8 * a + r0:8 * a + r0 + n, lanes]


CONV_RC = 32


def _tap_blocks(TM):
    return [(r0, slice(l0, l0 + LANES)) for l0 in range(0, D, LANES) for r0 in range(0, TM, CONV_RC)]


def _conv_fwd(zrest, b_glu, wdw, b_dw, g_ln, b_ln):
    S = zrest.shape[0]
    TM = CONV_TM
    HALO = 32
    hb = TM // HALO

    def body(u_ref, g_ref, uh_ref, gh_ref, bg_ref, w_ref, bd_ref, gl_ref, bl_ref, cv_ref, a_ref, ext, sh):
        i = pl.program_id(0)
        bu = bg_ref[:, 0:D]
        bgt = bg_ref[:, D:2 * D]
        uh = (uh_ref[...].astype(F32) + bu) * _sigmoid(gh_ref[...].astype(F32) + bgt)
        ext[0:HALO, :] = jnp.where(i == 0, 0.0, uh)
        ext[HALO:, :] = (u_ref[...].astype(F32) + bu) * _sigmoid(g_ref[...].astype(F32) + bgt)
        _make_shifts(ext, sh, TM)
        acc = jnp.zeros((TM, D), F32)
        for j in range(CONV_W):
            acc = acc + _shifted(ext, sh, HALO - (CONV_W - 1) + j, 0, TM, slice(None)) * w_ref[j:j + 1, :]
        cv = (acc + bd_ref[...]).astype(BF16)
        cv_ref[...] = cv
        cf = cv.astype(F32)
        mu = jnp.mean(cf, axis=-1, keepdims=True)
        xc = cf - mu
        y = xc * lax.rsqrt(jnp.mean(xc * xc, axis=-1, keepdims=True) + LN_EPS) * gl_ref[...] + bl_ref[...]
        a_ref[...] = (y * _sigmoid(y)).astype(BF16)

    vec = pl.BlockSpec((1, D), lambda i: (0, 0))
    return pl.pallas_call(
        body, name="conv_fwd", grid=(S // TM,),
        in_specs=[pl.BlockSpec((TM, D), lambda i: (i, 0)), pl.BlockSpec((TM, D), lambda i: (i, 1)),
                  pl.BlockSpec((HALO, D), lambda i: (jnp.maximum(i * hb - 1, 0), 0)),
                  pl.BlockSpec((HALO, D), lambda i: (jnp.maximum(i * hb - 1, 0), 1)),
                  pl.BlockSpec((1, 2 * D), lambda i: (0, 0)),
                  pl.BlockSpec((32, D), lambda i: (0, 0)), vec, vec, vec],
        out_specs=[pl.BlockSpec((TM, D), lambda i: (i, 0)), pl.BlockSpec((TM, D), lambda i: (i, 0))],
        out_shape=[SDS((S, D), BF16), SDS((S, D), BF16)],
        scratch_shapes=[pltpu.VMEM((HALO + TM, D), F32), pltpu.VMEM((7, TM + SHIFT_PAD, D), F32)],
        compiler_params=_cp(("arbitrary",)),
    )(zrest, zrest, zrest, zrest, b_glu, wdw, b_dw, g_ln, b_ln)


def _mix_fwd(og, lg, a, zrest, x, w_ao, w_co, b_co, w_mx, g_pm):
    S = x.shape[0]
    TM = 512

    def body(o0, o1, o2, l0, l1, l2, a_ref, ga_ref, gc_ref, x_ref, wa_ref, wc_ref, bc_ref, wm_ref, g_ref,
             o_ref, oa_ref, ob_ref, lse_ref, lsea_ref, lseb_ref, ya_ref, yc_ref, mg_ref, mm_ref, x1_ref,
             so1, so2, sl1, sl2, so, sl):
        for dil, src, dst, cast in ((DILATIONS[1], o1, so1, True), (DILATIONS[2], o2, so2, True),
                                    (DILATIONS[1], l1, sl1, False), (DILATIONS[2], l2, sl2, False)):
            _merge_residues(dst, dil, (lambda c, src=src: src[c].astype(F32)) if cast else (lambda c, src=src: src[c]))
        la, lb, lc = l0[0], _load_cols(sl1), _load_cols(sl2)
        m = jnp.maximum(jnp.maximum(la, lb), lc)
        e0 = jnp.exp(la - m)
        e1 = jnp.exp(lb - m)
        e2 = jnp.exp(lc - m)
        den = e0 + e1 + e2
        of = (e0 * o0[0].astype(F32) + e1 * _load_cols(so1) + e2 * _load_cols(so2)) / den
        o = of.astype(BF16)
        o_ref[...] = o
        lse = m + jnp.log(den)
        lse_ref[...] = lse
        _store_cols(so, of)
        _store_cols(sl, lse)
        for dil, oref, lref in ((DILATIONS[1], oa_ref, lsea_ref), (DILATIONS[2], ob_ref, lseb_ref)):
            def put_o(c, v, oref=oref):
                oref[c] = v.astype(BF16)

            def put_l(c, v, lref=lref):
                lref[c] = v
            _split_residues(so, dil, put_o)
            _split_residues(sl, dil, put_l)
        ya = _dot(o, wa_ref[...]).astype(BF16)
        yc = (_dot(a_ref[...], wc_ref[...]) + bc_ref[...]).astype(BF16)
        ya_ref[...] = ya
        yc_ref[...] = yc
        mg = (_sigmoid(ga_ref[...].astype(F32)) * ya.astype(F32)
              + _sigmoid(gc_ref[...].astype(F32)) * yc.astype(F32)).astype(BF16)
        mg_ref[...] = mg
        mm = _dot(mg, wm_ref[...]).astype(BF16)
        mm_ref[...] = mm
        mf = mm.astype(F32)
        r = lax.rsqrt(jnp.mean(mf * mf, axis=-1, keepdims=True) + RMS_EPS)
        x1_ref[...] = x_ref[...] + mf * r * g_ref[...]

    row512 = pl.BlockSpec((TM, GW), lambda i: (i, 0))
    rowd = pl.BlockSpec((TM, D), lambda i: (i, 0))
    vec = pl.BlockSpec((1, D), lambda i: (0, 0))
    full = lambda r, c: pl.BlockSpec((r, c), lambda i: (0, 0))
    res = [_residue_spec(TM, d, GW) for d in DILATIONS]
    rshape = lambda d, t: SDS(_residue_shape(S, d, GW), t)
    scr = _col_scratch(TM, GW)
    return pl.pallas_call(
        body, name="mix_fwd", grid=(S // TM,),
        in_specs=res + res + [rowd, pl.BlockSpec((TM, D), lambda i: (i, 2)), pl.BlockSpec((TM, D), lambda i: (i, 3)),
                              rowd, full(GW, D), full(D, D), vec, full(D, D), vec],
        out_specs=[row512, res[1], res[2], row512, res[1], res[2], rowd, rowd, rowd, rowd, rowd],
        out_shape=[SDS((S, GW), BF16), rshape(DILATIONS[1], BF16), rshape(DILATIONS[2], BF16),
                   SDS((S, GW), F32), rshape(DILATIONS[1], F32), rshape(DILATIONS[2], F32),
                   SDS((S, D), BF16), SDS((S, D), BF16), SDS((S, D), BF16), SDS((S, D), BF16), SDS((S, D), F32)],
        scratch_shapes=[scr] * 6,
        compiler_params=_cp(("arbitrary",)),
    )(og[0], og[1], og[2], lg[0], lg[1], lg[2], a, zrest, zrest, x, w_ao, w_co, b_co, w_mx, g_pm)


def _ffn_fwd(x1, tgt, g_pre, g_post, w_fi, w_fo):
    S = x1.shape[0]
    TM = 512

    def body(x1_ref, t_ref, gp_ref, go_ref, wi_ref, wo_ref,
             h2_ref, gu_ref, df_ref, dx2_ref, loss_ref, dgo_ref):
        i = pl.program_id(0)

        @pl.when(i == 0)
        def _():
            loss_ref[...] = jnp.zeros_like(loss_ref)
            dgo_ref[...] = jnp.zeros_like(dgo_ref)

        xf = x1_ref[...]
        r = lax.rsqrt(jnp.mean(xf * xf, axis=-1, keepdims=True) + RMS_EPS)
        h2_ref[...] = (xf * r * gp_ref[...]).astype(BF16)
        for k in range(NFT):
            gu_ref[:, 2 * k * FFN_T:(2 * k + 1) * FFN_T] = _dot(
                h2_ref[...], wi_ref[:, k * FFN_T:(k + 1) * FFN_T]).astype(BF16)
            gu_ref[:, (2 * k + 1) * FFN_T:(2 * k + 2) * FFN_T] = _dot(
                h2_ref[...], wi_ref[:, FFN_H + k * FFN_T:FFN_H + (k + 1) * FFN_T]).astype(BF16)
        f = jnp.zeros((TM, D), F32)
        for k in range(NFT):
            gf = gu_ref[:, 2 * k * FFN_T:(2 * k + 1) * FFN_T].astype(F32)
            uf = gu_ref[:, (2 * k + 1) * FFN_T:(2 * k + 2) * FFN_T].astype(F32)
            act = (gf * _sigmoid(gf) * uf).astype(BF16)
            f = f + _dot(act, wo_ref[k * FFN_T:(k + 1) * FFN_T, :])
        r = lax.rsqrt(jnp.mean(f * f, axis=-1, keepdims=True) + RMS_EPS)
        nrm = f * r
        e = x1_ref[...] + nrm * go_ref[...] - t_ref[...]
        tot = jnp.sum(jnp.sum(e * e, axis=-1, keepdims=True), axis=0, keepdims=True) * (0.5 / D)
        corner = jnp.logical_and(lax.broadcasted_iota(jnp.int32, (LANE_ROWS, D), 0) == 0,
                                 lax.broadcasted_iota(jnp.int32, (LANE_ROWS, D), 1) == 0)
        loss_ref[...] += jnp.where(corner, tot, 0.0)
        dx2 = e * (1.0 / D)
        dx2_ref[...] = dx2
        dgo_ref[...] += _colsum8(dx2 * nrm)
        dn = dx2 * go_ref[...]
        df_ref[...] = (r * (dn - nrm * jnp.mean(dn * nrm, axis=-1, keepdims=True))).astype(BF16)

    rowd = pl.BlockSpec((TM, D), lambda i: (i, 0))
    vec = pl.BlockSpec((1, D), lambda i: (0, 0))
    acc8 = pl.BlockSpec((LANE_ROWS, D), lambda i: (0, 0))
    return pl.pallas_call(
        body, name="ffn_fwd", grid=(S // TM,),
        in_specs=[rowd, rowd, vec, vec,
                  pl.BlockSpec((D, 2 * FFN_H), lambda i: (0, 0), pipeline_mode=pl.Buffered(1)),
                  pl.BlockSpec((FFN_H, D), lambda i: (0, 0), pipeline_mode=pl.Buffered(1))],
        out_specs=[rowd, pl.BlockSpec((TM, 2 * FFN_H), lambda i: (i, 0)), rowd, rowd, acc8, acc8],
        out_shape=[SDS((S, D), BF16), SDS((S, 2 * FFN_H), BF16), SDS((S, D), BF16), SDS((S, D), F32),
                   SDS((LANE_ROWS, D), F32), SDS((LANE_ROWS, D), F32)],
        compiler_params=_cp(("arbitrary",)),
    )(x1, tgt, g_pre, g_post, w_fi, w_fo)


def _ffn_bwd_act(df, gu, w_foT):
    S = df.shape[0]
    TM = 512

    def body_act(df_ref, gu_ref, wo_ref, dff_ref, act_ref):
        dacts = [_dot(df_ref[...], wo_ref[:, k * FFN_T:(k + 1) * FFN_T]) for k in range(NFT)]
        for k in range(NFT):
            dact = dacts[k]
            g = gu_ref[:, 2 * k * FFN_T:(2 * k + 1) * FFN_T].astype(F32)
            u = gu_ref[:, (2 * k + 1) * FFN_T:(2 * k + 2) * FFN_T].astype(F32)
            sg = _sigmoid(g)
            sl = g * sg
            act_ref[:, k * FFN_T:(k + 1) * FFN_T] = (sl * u).astype(BF16)
            dff_ref[:, 2 * k * FFN_T:(2 * k + 1) * FFN_T] = (dact * u * (sg * (1.0 + g * (1.0 - sg)))).astype(BF16)
            dff_ref[:, (2 * k + 1) * FFN_T:(2 * k + 2) * FFN_T] = (dact * sl).astype(BF16)

    rowd = pl.BlockSpec((TM, D), lambda i: (i, 0))
    wide = pl.BlockSpec((TM, 2 * FFN_H), lambda i: (i, 0))
    return pl.pallas_call(
        body_act, name="ffn_bwd_act", grid=(S // TM,),
        in_specs=[rowd, wide, pl.BlockSpec((D, FFN_H), lambda i: (0, 0), pipeline_mode=pl.Buffered(1))],
        out_specs=[wide, pl.BlockSpec((TM, FFN_H), lambda i: (i, 0))],
        out_shape=[SDS((S, 2 * FFN_H), BF16), SDS((S, FFN_H), BF16)],
        compiler_params=_cp(("arbitrary",)),
    )(df, gu, w_foT)


def _ffn_bwd_in(dff, x1, dx2, g_pre, w_fiT, rider):
    S = x1.shape[0]
    TM = 512
    rowd = pl.BlockSpec((TM, D), lambda i: (i, 0))
    wide = pl.BlockSpec((TM, 2 * FFN_H), lambda i: (i, 0))
    KC = 512
    nkc = 2 * FFN_H // KC

    def body_in(dff_ref, x1_ref, dx2_ref, gp_ref, wi_ref, dx1_ref, dgp_ref):
        i = pl.program_id(0)

        @pl.when(i == 0)
        def _():
            dgp_ref[...] = jnp.zeros_like(dgp_ref)

        dh = jnp.zeros((TM, D), F32)
        for k in range(nkc):
            dh = dh + _dot(dff_ref[:, k * KC:k * KC + FFN_T], wi_ref[k * FFN_T:(k + 1) * FFN_T, :]) \
                + _dot(dff_ref[:, k * KC + FFN_T:(k + 1) * KC], wi_ref[FFN_H + k * FFN_T:FFN_H + (k + 1) * FFN_T, :])
        xf = x1_ref[...]
        r = lax.rsqrt(jnp.mean(xf * xf, axis=-1, keepdims=True) + RMS_EPS)
        nrm = xf * r
        dgp_ref[...] += _colsum8(dh * nrm)
        dn = dh * gp_ref[...]
        dx1_ref[...] = dx2_ref[...] + r * (dn - nrm * jnp.mean(dn * nrm, axis=-1, keepdims=True))

    body_in, r_in, r_out, r_shape, r_scr = _ride(body_in, 5, 2, 0, rider, S // TM)
    return pl.pallas_call(
        body_in, name="ffn_bwd_in", grid=(S // TM,),
        in_specs=[wide, rowd, rowd, pl.BlockSpec((1, D), lambda i: (0, 0)),
                  pl.BlockSpec((2 * FFN_H, D), lambda i: (0, 0), pipeline_mode=pl.Buffered(1))] + r_in,
        out_specs=[rowd, pl.BlockSpec((LANE_ROWS, D), lambda i: (0, 0))] + r_out,
        out_shape=[SDS((S, D), F32), SDS((LANE_ROWS, D), F32)] + r_shape,
        scratch_shapes=r_scr,
        compiler_params=_cp(("arbitrary",)),
    )(dff, x1, dx2, g_pre, w_fiT, *rider.ins)


def _mix_bwd(dx1, mm, ya, yc, zrest, g_pm, w_mxT, w_aoT, w_coT):
    S = dx1.shape[0]
    TM = 512

    def body(dx_ref, mm_ref, ya_ref, yc_ref, ga_ref, gc_ref, g_ref, wm_ref, wa_ref, wc_ref,
             dmm_ref, dya_ref, dyc_ref, do_ref, doa_ref, dob_ref, da_ref, dzg_ref, dgpm_ref, dbco_ref, sdo):
        i = pl.program_id(0)

        @pl.when(i == 0)
        def _():
            dgpm_ref[...] = jnp.zeros_like(dgpm_ref)
            dbco_ref[...] = jnp.zeros_like(dbco_ref)

        mf = mm_ref[...].astype(F32)
        r = lax.rsqrt(jnp.mean(mf * mf, axis=-1, keepdims=True) + RMS_EPS)
        nrm = mf * r
        dx = dx_ref[...]
        dgpm_ref[...] += _colsum8(dx * nrm)
        dn = dx * g_ref[...]
        dmm = (r * (dn - nrm * jnp.mean(dn * nrm, axis=-1, keepdims=True))).astype(BF16)
        dmm_ref[...] = dmm
        dmg = _dot(dmm, wm_ref[...])
        sa = _sigmoid(ga_ref[...].astype(F32))
        sc = _sigmoid(gc_ref[...].astype(F32))
        dya = (dmg * sa).astype(BF16)
        dyc = (dmg * sc).astype(BF16)
        dya_ref[...] = dya
        dyc_ref[...] = dyc
        dbco_ref[...] += _colsum8(dyc.astype(F32))
        dzg_ref[:, 0:D] = (dmg * ya_ref[...].astype(F32) * (sa * (1.0 - sa))).astype(BF16)
        dzg_ref[:, D:] = (dmg * yc_ref[...].astype(F32) * (sc * (1.0 - sc))).astype(BF16)
        dof = _dot(dya, wa_ref[...])
        do_ref[...] = dof.astype(BF16)
        _store_cols(sdo, dof)
        for dil, ref in ((DILATIONS[1], doa_ref), (DILATIONS[2], dob_ref)):
            def put(c, v, ref=ref):
                ref[c] = v.astype(BF16)
            _split_residues(sdo, dil, put)
        da_ref[...] = _dot(dyc, wc_ref[...]).astype(BF16)

    rowd = pl.BlockSpec((TM, D), lambda i: (i, 0))
    full = lambda r, c: pl.BlockSpec((r, c), lambda i: (0, 0))
    acc8 = pl.BlockSpec((LANE_ROWS, D), lambda i: (0, 0))
    return pl.pallas_call(
        body, name="mix_bwd", grid=(S // TM,),
        in_specs=[rowd, rowd, rowd, rowd, pl.BlockSpec((TM, D), lambda i: (i, 2)),
                  pl.BlockSpec((TM, D), lambda i: (i, 3)), full(1, D), full(D, D), full(D, GW), full(D, D)],
        out_specs=[rowd, rowd, rowd, pl.BlockSpec((TM, GW), lambda i: (i, 0)),
                   _residue_spec(TM, DILATIONS[1], GW), _residue_spec(TM, DILATIONS[2], GW), rowd,
                   pl.BlockSpec((TM, 2 * D), lambda i: (i, 0)), acc8, acc8],
        out_shape=[SDS((S, D), BF16), SDS((S, D), BF16), SDS((S, D), BF16), SDS((S, GW), BF16),
                   SDS(_residue_shape(S, DILATIONS[1], GW), BF16), SDS(_residue_shape(S, DILATIONS[2], GW), BF16),
                   SDS((S, D), BF16), SDS((S, 2 * D), BF16), SDS((LANE_ROWS, D), F32), SDS((LANE_ROWS, D), F32)],
        scratch_shapes=[_col_scratch(TM, GW)],
        compiler_params=_cp(("arbitrary",)),
    )(dx1, mm, ya, yc, zrest, zrest, g_pm, w_mxT, w_aoT, w_coT)


def _conv_bwd(da, cv, zrest, b_glu, wdw, g_ln, b_ln):
    S = da.shape[0]
    TM = CONV_TM
    HALO = 32
    hb = TM // HALO
    nh = S // HALO

    def body(da_ref, dan_ref, cv_ref, cvn_ref, u_ref, g_ref, uh_ref, gh_ref, bg_ref, w_ref, gl_ref, bl_ref,
             dglu_ref, dbu_ref, dbg_ref, dw_ref, dgl_ref, dbl_ref, dbd_ref, dext, uext, dsh, ush, du_scr, dw8):
        i = pl.program_id(0)
        last = i == pl.num_programs(0) - 1

        @pl.when(i == 0)
        def _():
            for ref in (dbu_ref, dbg_ref, dw8, dgl_ref, dbl_ref, dbd_ref):
                ref[...] = jnp.zeros_like(ref)

        def ln_bwd(da_v, cv_v):
            cf = cv_v.astype(F32)
            mu = jnp.mean(cf, axis=-1, keepdims=True)
            xc = cf - mu
            rstd = lax.rsqrt(jnp.mean(xc * xc, axis=-1, keepdims=True) + LN_EPS)
            xh = xc * rstd
            y = xh * gl_ref[...] + bl_ref[...]
            sy = _sigmoid(y)
            dy = da_v.astype(F32) * (sy * (1.0 + y * (1.0 - sy)))
            dxh = dy * gl_ref[...]
            dcv = rstd * (dxh - jnp.mean(dxh, axis=-1, keepdims=True)
                          - xh * jnp.mean(dxh * xh, axis=-1, keepdims=True))
            return dcv, dy, xh

        dcv, dy, xh = ln_bwd(da_ref[...], cv_ref[...])
        dgl_ref[...] += _colsum8(dy * xh)
        dbl_ref[...] += _colsum8(dy)
        dbd_ref[...] += _colsum8(dcv)
        dcvn, _, _ = ln_bwd(dan_ref[...], cvn_ref[...])
        dext[0:TM, :] = dcv
        dext[TM:, :] = jnp.where(last, 0.0, dcvn)

        bu = bg_ref[:, 0:D]
        bgt = bg_ref[:, D:2 * D]
        upre = u_ref[...].astype(F32) + bu
        sg = _sigmoid(g_ref[...].astype(F32) + bgt)
        uh = (uh_ref[...].astype(F32) + bu) * _sigmoid(gh_ref[...].astype(F32) + bgt)
        uext[0:HALO, :] = jnp.where(i == 0, 0.0, uh)
        uext[HALO:, :] = upre * sg

        _make_shifts(dext, dsh, TM)
        _make_shifts(uext, ush, TM)
        for r0, lanes in _tap_blocks(TM):
            acc = jnp.zeros((CONV_RC, LANES), F32)
            for j in range(CONV_W):
                acc = acc + _shifted(dext, dsh, CONV_W - 1 - j, r0, CONV_RC, lanes) * w_ref[j:j + 1, lanes]
            du_scr[r0:r0 + CONV_RC, lanes] = acc
        for l0 in range(0, D, LANES):
            lanes = slice(l0, l0 + LANES)
            accs = [jnp.zeros((LANE_ROWS, LANES), F32)] * CONV_W
            for r0 in range(0, TM, CONV_RC):
                dc = dext[r0:r0 + CONV_RC, lanes]
                for j in range(CONV_W):
                    prod = dc * _shifted(uext, ush, HALO - (CONV_W - 1) + j, r0, CONV_RC, lanes)
                    accs[j] = accs[j] + jnp.sum(prod.reshape(CONV_RC // LANE_ROWS, LANE_ROWS, LANES), axis=0)
            for j in range(CONV_W):
                dw8[j, :, lanes] += accs[j]

        @pl.when(last)
        def _():
            for j in range(CONV_W):
                dw_ref[j:j + 1, :] = jnp.sum(dw8[j], axis=0, keepdims=True)
            dw_ref[CONV_W:, :] = jnp.zeros((32 - CONV_W, D), F32)

        du = du_scr[...]
        dup = du * sg
        dgp = du * upre * (sg * (1.0 - sg))
        dglu_ref[:, 0:D] = dup.astype(BF16)
        dglu_ref[:, D:] = dgp.astype(BF16)
        dbu_ref[...] += _colsum8(dup.astype(BF16).astype(F32))
        dbg_ref[...] += _colsum8(dgp.astype(BF16).astype(F32))

    rowd = pl.BlockSpec((TM, D), lambda i: (i, 0))
    nxt = pl.BlockSpec((HALO, D), lambda i: (jnp.minimum((i + 1) * hb, nh - 1), 0))
    vec = pl.BlockSpec((1, D), lambda i: (0, 0))
    acc8 = pl.BlockSpec((LANE_ROWS, D), lambda i: (0, 0))
    return pl.pallas_call(
        body, name="conv_bwd", grid=(S // TM,),
        in_specs=[rowd, nxt, rowd, nxt,
                  pl.BlockSpec((TM, D), lambda i: (i, 0)), pl.BlockSpec((TM, D), lambda i: (i, 1)),
                  pl.BlockSpec((HALO, D), lambda i: (jnp.maximum(i * hb - 1, 0), 0)),
                  pl.BlockSpec((HALO, D), lambda i: (jnp.maximum(i * hb - 1, 0), 1)),
                  pl.BlockSpec((1, 2 * D), lambda i: (0, 0)), pl.BlockSpec((32, D), lambda i: (0, 0)), vec, vec],
        out_specs=[pl.BlockSpec((TM, 2 * D), lambda i: (i, 0)), acc8, acc8,
                   pl.BlockSpec((32, D), lambda i: (0, 0)), acc8, acc8, acc8],
        out_shape=[SDS((S, 2 * D), BF16), SDS((LANE_ROWS, D), F32), SDS((LANE_ROWS, D), F32), SDS((32, D), F32),
                   SDS((LANE_ROWS, D), F32), SDS((LANE_ROWS, D), F32), SDS((LANE_ROWS, D), F32)],
        scratch_shapes=[pltpu.VMEM((TM + HALO, D), F32), pltpu.VMEM((HALO + TM, D), F32),
                        pltpu.VMEM((7, TM + SHIFT_PAD, D), F32), pltpu.VMEM((7, TM + SHIFT_PAD, D), F32),
                        pltpu.VMEM((TM, D), F32), pltpu.VMEM((32, LANE_ROWS, D), F32)],
        compiler_params=_cp(("arbitrary",)),
    )(da, da, cv, cv, zrest, zrest, zrest, zrest, b_glu, wdw, g_ln, b_ln)


def _attn_bwd(zq, do, o, lse, bias_t, gi):
    dil, L, _ = zq.shape
    _, TQ, QB, ns = _attn_tile(L * dil, dil)
    NP = NH // 2

    def body(q3, kc3, kp3, vc3, vp3, do3, o3, l3, b_ref,
             out3, db_ref, kext, vext, dkx, dvx, dqn, dqc, dkc, dvc):
        q_ref, kc_ref, kp_ref, vc_ref, vp_ref, do_ref, o_ref, l_ref, out_ref = (
            r.at[0] for r in (q3, kc3, kp3, vc3, vp3, do3, o3, l3, out3))
        t = pl.program_id(0)
        n = lax.rem(cur(t), ns)

        @pl.when(t == 0)
        def _():
            db_ref[...] = jnp.zeros_like(db_ref)

        @pl.when(t < T - 1)
        def _():
            kext[0:QBLK, :] = kp_ref[...]
            kext[QBLK:, :] = kc_ref[...]
            vext[0:QBLK, :] = vp_ref[...]
            vext[QBLK:, :] = vc_ref[...]
            krow = lax.broadcasted_iota(jnp.int32, (KBLK, 2 * QBLK), 0)
            no_prev = jnp.logical_and(n == 0, krow < QBLK)
            lane = lax.broadcasted_iota(jnp.int32, (QBLK, LANES), 1)

            def overlap_add(parts):
                segs = [parts[0][0:QBLK]]
                for b in range(1, QB):
                    segs.append(parts[b - 1][QBLK:] + parts[b][0:QBLK])
                segs.append(parts[QB - 1][QBLK:])
                return jnp.concatenate(segs, axis=0)

            lanes_of = [slice(hp * LANES, (hp + 1) * LANES) for hp in range(NP)]
            dv_parts = [[] for _ in range(NP)]
            dk_parts = [[] for _ in range(NP)]
            dbsum = [None] * NP
            for b in range(QB):
                rows = slice(b * QBLK, (b + 1) * QBLK)
                win = slice(b * QBLK, b * QBLK + KBLK)
                q2 = [_pair_stack(q_ref, rows, pl_, SCALE) for pl_ in lanes_of]
                do2 = [_pair_stack(do_ref, rows, pl_) for pl_ in lanes_of]
                st = [_dot_nt(kext[win, pl_], q2[hp]) + b_ref[0, hp] for hp, pl_ in enumerate(lanes_of)]
                dpt = [_dot_nt(vext[win, pl_], do2[hp]) for hp, pl_ in enumerate(lanes_of)]
                lse_t = l_ref[rows, :].T
                prod_t = (do_ref[rows, :].astype(F32) * o_ref[rows, :].astype(F32)).T
                dst = []
                for hp in range(NP):
                    lo = hp * LANES
                    lse_row = jnp.concatenate([lse_t[lo:lo + 1], lse_t[lo + HD:lo + HD + 1]], axis=1)
                    delta_row = jnp.concatenate([jnp.sum(prod_t[lo:lo + HD], axis=0, keepdims=True),
                                                 jnp.sum(prod_t[lo + HD:lo + LANES], axis=0, keepdims=True)], axis=1)
                    s_hp = jnp.where(no_prev, NEG_INF, st[hp]) if b == 0 else st[hp]
                    pt = jnp.exp(s_hp - lse_row)
                    d = pt * (dpt[hp] - delta_row)
                    dbsum[hp] = d if dbsum[hp] is None else dbsum[hp] + d
                    dst.append(d.astype(BF16))
                    dv_parts[hp].append(_dot(pt.astype(BF16), do2[hp]))
                for hp, pl_ in enumerate(lanes_of):
                    dk_parts[hp].append(_dot(dst[hp], q2[hp]))
                    dq2 = _dot_tn(dst[hp], kext[win, pl_])
                    dqn[rows, pl_] = jnp.where(lane < HD, dq2[0:QBLK], dq2[QBLK:]) * SCALE
            for hp, pl_ in enumerate(lanes_of):
                db_ref[hp] += dbsum[hp]
                dvx[:, pl_] = overlap_add(dv_parts[hp])
                dkx[:, pl_] = overlap_add(dk_parts[hp])

        @pl.when(t > 0)
        def _():
            out_ref[:, 0:GW] = dqc[...].astype(BF16)
            out_ref[:, GW:2 * GW] = dkc[...].astype(BF16)
            out_ref[:, 2 * GW:] = dvc[...].astype(BF16)

        @pl.when(jnp.logical_and(t > 0, t < T - 1))
        def _():
            out_ref[TQ - QBLK:, GW:2 * GW] = (dkc[TQ - QBLK:, :] + dkx[0:QBLK, :]).astype(BF16)
            out_ref[TQ - QBLK:, 2 * GW:] = (dvc[TQ - QBLK:, :] + dvx[0:QBLK, :]).astype(BF16)

        @pl.when(t < T - 1)
        def _():
            dqc[...] = dqn[...]
            dkc[...] = dkx[QBLK:, :]
            dvc[...] = dvx[QBLK:, :]

    T = dil * ns + 1

    def cur(t):
        return jnp.minimum(t, T - 2)

    def blk(t, col):
        return (lax.div(cur(t), ns), lax.rem(cur(t), ns), col)

    def prev(t, col):
        return (lax.div(cur(t), ns), jnp.maximum(lax.rem(cur(t), ns) * QB - 1, 0), col)

    def late(t):
        tp = jnp.maximum(t - 1, 0)
        return (lax.div(tp, ns), lax.rem(tp, ns), 0)

    rows = lambda t: blk(t, 0)
    return pl.pallas_call(
        body, name=f"attn_bwd_g{gi}", grid=(T,),
        in_specs=[pl.BlockSpec((1, TQ, GW), lambda t: blk(t, 0)),
                  pl.BlockSpec((1, TQ, GW), lambda t: blk(t, 1)),
                  pl.BlockSpec((1, QBLK, GW), lambda t: prev(t, 1)),
                  pl.BlockSpec((1, TQ, GW), lambda t: blk(t, 2)),
                  pl.BlockSpec((1, QBLK, GW), lambda t: prev(t, 2)),
                  pl.BlockSpec((1, TQ, GW), rows), pl.BlockSpec((1, TQ, GW), rows), pl.BlockSpec((1, TQ, GW), rows),
                  pl.BlockSpec((1, NP, KBLK, 2 * QBLK), lambda t: (gi, 0, 0, 0))],
        out_specs=[pl.BlockSpec((1, TQ, 3 * GW), late),
                   pl.BlockSpec((NP, KBLK, 2 * QBLK), lambda t: (0, 0, 0))],
        out_shape=[SDS((dil, L, 3 * GW), BF16), SDS((NP, KBLK, 2 * QBLK), F32)],
        scratch_shapes=[pltpu.VMEM((QBLK + TQ, GW), BF16), pltpu.VMEM((QBLK + TQ, GW), BF16),
                        pltpu.VMEM((QBLK + TQ, GW), F32), pltpu.VMEM((QBLK + TQ, GW), F32),
                        pltpu.VMEM((TQ, GW), F32), pltpu.VMEM((TQ, GW), F32),
                        pltpu.VMEM((TQ, GW), F32), pltpu.VMEM((TQ, GW), F32)],
        compiler_params=_cp(("arbitrary",)),
    )(zq, zq, zq, zq, zq, do, o, lse, bias_t)


def _dz_block(k):
    if k < 9:
        return k % 3, k // 3
    if k < 13:
        return 3, k - 9
    return 4, k - 13


_DZ_SRC = np.array([_dz_block(k)[0] for k in range(17)], np.int32)


def _dz_hold(s):
    uses = [(k, _dz_block(k)[1]) for k in range(17) if _dz_block(k)[0] == s]
    hold = []
    for k in range(17):
        nxt = [b for kk, b in uses if kk >= k]
        hold.append(nxt[0] if nxt else uses[-1][1])
    return np.array(hold, np.int32)


def _table(tab, k):
    out = jnp.int32(int(tab[0]))
    for idx in range(1, len(tab)):
        out = jnp.where(k == idx, jnp.int32(int(tab[idx])), out)
    return out


def _w_in_tile(s, blk):
    return blk * 3 + s if s < 3 else (9 if s == 3 else 13) + blk


def _in_bwd(dqkv, dglu, dzg, w_inT, x, dx1, g, rider):
    S = x.shape[0]
    TM = 512

    def body(d0, d1, d2, d3, d4, w_ref, x_ref, dx1_ref, g_ref, gx_ref, dg_ref, scr):
        i = pl.program_id(0)

        @pl.when(i == 0)
        def _():
            dg_ref[...] = jnp.zeros_like(dg_ref)

        def rows(s, blk):
            k = _w_in_tile(s, blk)
            return w_ref[k * GW:(k + 1) * GW, :]

        dh = jnp.zeros((TM, D), F32)
        for blk in range(3):
            dh = dh + _dot(d0[0, :, blk * GW:(blk + 1) * GW], rows(0, blk))
        for s, ref in ((3, d3), (4, d4)):
            for blk in range(4):
                dh = dh + _dot(ref[:, blk * GW:(blk + 1) * GW], rows(s, blk))
        for s, ref in ((1, d1), (2, d2)):
            dil = DILATIONS[s]
            part = jnp.zeros((TM, D), F32)
            for blk in range(3):
                part = part + _dot(ref[:, :, blk * GW:(blk + 1) * GW].reshape(TM, GW), rows(s, blk))
            _merge_residues(scr, dil, lambda c, part=part, dil=dil: part[c * (TM // dil):(c + 1) * (TM // dil)])
            dh = dh + _load_cols(scr)
        xf = x_ref[...]
        r = lax.rsqrt(jnp.mean(xf * xf, axis=-1, keepdims=True) + RMS_EPS)
        nrm = xf * r
        dg_ref[...] += _colsum8(dh * nrm)
        dn = dh * g_ref[...]
        gx_ref[...] = dx1_ref[...] + r * (dn - nrm * jnp.mean(dn * nrm, axis=-1, keepdims=True))

    rowd = pl.BlockSpec((TM, D), lambda i: (i, 0))
    wide = pl.BlockSpec((TM, 2 * D), lambda i: (i, 0))
    body, r_in, r_out, r_shape, r_scr = _ride(body, 9, 2, 1, rider, S // TM)
    return pl.pallas_call(
        body, name="in_bwd", grid=(S // TM,),
        in_specs=[_residue_spec(TM, d, 3 * GW) for d in DILATIONS] + [wide, wide]
        + [pl.BlockSpec(w_inT.shape, lambda i: (0, 0), pipeline_mode=pl.Buffered(1)), rowd, rowd,
           pl.BlockSpec((1, D), lambda i: (0, 0))] + r_in,
        out_specs=[rowd, pl.BlockSpec((LANE_ROWS, D), lambda i: (0, 0))] + r_out,
        out_shape=[SDS((S, D), F32), SDS((LANE_ROWS, D), F32)] + r_shape,
        scratch_shapes=[_col_scratch(TM, D)] + r_scr,
        compiler_params=_cp(("arbitrary",)),
    )(*dqkv, dglu, dzg, w_inT, x, dx1, g, *rider.ins)


def _dw_in(dqkv, dglu, dzg, hs):
    S = hs[0].shape[0]
    TS = min(2048, S)
    nk = 17
    holds = [_dz_hold(s) for s in range(5)]
    h_of = (0, 1, 2, 0, 0)

    def body(d0, d1, d2, d3, d4, h0, h1, h2, o_ref, acc):
        m = pl.program_id(0)
        s_ = pl.program_id(1)

        @pl.when(s_ == 0)
        def _():
            acc[...] = jnp.zeros_like(acc)

        src = _table(_DZ_SRC, m)
        pairs = ((d0, h0), (d1, h1), (d2, h2), (d3, h0), (d4, h0))
        for s, (dref, href) in enumerate(pairs):
            @pl.when(src == s)
            def _(dref=dref, href=href):
                acc[...] += _dot_tn(dref[...].reshape(TS, GW), href[...].reshape(TS, D))

        @pl.when(s_ == pl.num_programs(1) - 1)
        def _():
            o_ref[...] = acc[...].astype(BF16)

    def row(s, m, s_):
        return jnp.where(_table(_DZ_SRC, m) == s, s_, 0)

    def dspec(s):
        if s < 3:
            dil = DILATIONS[s]
            return pl.BlockSpec((dil, TS // dil, GW), lambda m, s_: (0, row(s, m, s_), _table(holds[s], m)))
        return pl.BlockSpec((TS, GW), lambda m, s_: (row(s, m, s_), _table(holds[s], m)))

    def hrow(j, m, s_):
        used = _table(np.array([int(h_of[_dz_block(k)[0]] == j) for k in range(nk)], np.int32), m)
        return jnp.where(used == 1, s_, 0)

    hspecs = [pl.BlockSpec((TS, D), lambda m, s_: (hrow(0, m, s_), 0))] + [
        pl.BlockSpec((DILATIONS[j], TS // DILATIONS[j], D), lambda m, s_, j=j: (0, hrow(j, m, s_), 0)) for j in (1, 2)]
    return pl.pallas_call(
        body, name="dw_in", grid=(nk, S // TS),
        in_specs=[dspec(s) for s in range(5)] + hspecs,
        out_specs=pl.BlockSpec((GW, D), lambda m, s_: (m, 0)),
        out_shape=SDS((nk * GW, D), BF16),
        scratch_shapes=[pltpu.VMEM((GW, D), F32)],
        compiler_params=_cp(("arbitrary", "arbitrary")),
    )(*dqkv, dglu, dzg, *hs)


def _mm_tn(a, b, tm, a_maps, name):
    S, N = b.shape
    parts = len(a_maps)
    tp = tm // parts
    nm = len(a_maps[0])
    TS = min(4096 if tm <= 512 else 2048, S)
    tabs = [np.array(t, np.int32) for t in a_maps]

    def body(*refs):
        a_refs = refs[:parts]
        b_ref, o_ref, acc = refs[parts:]
        s_ = pl.program_id(1)

        @pl.when(s_ == 0)
        def _():
            acc[...] = jnp.zeros_like(acc)

        for p, ar in enumerate(a_refs):
            acc[p * tp:(p + 1) * tp, :] += _dot_tn(ar[...], b_ref[...])

        @pl.when(s_ == pl.num_programs(1) - 1)
        def _():
            o_ref[...] = acc[...].astype(BF16)

    return pl.pallas_call(
        body, name=name, grid=(nm, S // TS),
        in_specs=[pl.BlockSpec((TS, tp), lambda m, s_, t=t: (s_, _table(t, m))) for t in tabs]
        + [pl.BlockSpec((TS, N), lambda m, s_: (s_, 0))],
        out_specs=pl.BlockSpec((tm, N), lambda m, s_: (m, 0)),
        out_shape=SDS((nm * tm, N), BF16),
        scratch_shapes=[pltpu.VMEM((tm, N), F32)],
        compiler_params=_cp(("arbitrary", "arbitrary")),
    )(*([a] * parts), b)


def _row_tile(rows, cols, limit=1 << 20):
    if rows * cols * 4 <= limit:
        return rows
    best = None
    for t in range(8, rows, 8):
        if rows % t == 0 and t * cols * 4 <= limit:
            best = t
    return best


def _adamw(w, g, m, v, name):
    R, C = w.shape
    tr = _row_tile(R, C)

    def body(w_ref, g_ref, m_ref, v_ref, d_ref, nm_ref, nv_ref):
        gg = g_ref[...]
        nm = ADAM_B1 * m_ref[...] + (1.0 - ADAM_B1) * gg
        nv = ADAM_B2 * v_ref[...] + (1.0 - ADAM_B2) * (gg * gg)
        m_hat = nm / (1.0 - ADAM_B1 ** ADAM_STEP)
        v_hat = nv / (1.0 - ADAM_B2 ** ADAM_STEP)
        d_ref[...] = -ADAM_LR * (m_hat / (jnp.sqrt(v_hat) + ADAM_EPS) + ADAM_WD * w_ref[...])
        nm_ref[...] = nm
        nv_ref[...] = nv

    spec = pl.BlockSpec((tr, C), lambda i: (i, 0))
    return pl.pallas_call(
        body, name=name, grid=(R // tr,), in_specs=[spec] * 4, out_specs=[spec] * 3,
        out_shape=[SDS((R, C), F32)] * 3, compiler_params=_cp(("arbitrary",)),
    )(w, g, m, v)


_FLIPS = ((1, 0), (0, 1), (1, 1))


def _place():
    x, y, c = lax.axis_index("x"), lax.axis_index("y"), lax.axis_index("c")
    return x, y, c


def _peer_chips(x, y):
    return [((x + fx) % 2, (y + fy) % 2) for fx, fy in _FLIPS]


def _gather_weights(shards):
    nw = len(shards)
    views = [s.reshape(2, s.shape[0] // 2, s.shape[1]) for s in shards]

    def body(*refs):
        ins = refs[:nw]
        outs = refs[nw:2 * nw]
        ici_send, ici_recv, d2d_send, d2d_recv, loc = refs[2 * nw:]
        x, y, c = _place()
        j = 2 * x + y
        chips = _peer_chips(x, y)
        copies = []
        for w in range(nw):
            cp = pltpu.make_async_copy(ins[w], outs[w].at[j], loc.at[w])
            cp.start()
            copies.append(cp)
        sends = []
        for w in range(nw):
            for k, (px, py) in enumerate(chips):
                cp = pltpu.make_async_remote_copy(
                    src_ref=ins[w].at[c], dst_ref=outs[w].at[j, c], send_sem=ici_send.at[w, k],
                    recv_sem=ici_recv.at[w, k], device_id=(px, py, c), device_id_type=MESH)
                cp.start()
                sends.append(cp)
        for w in range(nw):
            for k, (px, py) in enumerate(chips):
                jk = 2 * px + py
                land = outs[w].at[jk, c]
                pltpu.make_async_remote_copy(
                    src_ref=ins[w].at[c], dst_ref=land, send_sem=ici_send.at[w, k],
                    recv_sem=ici_recv.at[w, k], device_id=(px, py, c), device_id_type=MESH).wait_recv()
                cp = pltpu.make_async_remote_copy(
                    src_ref=land, dst_ref=land, send_sem=d2d_send.at[w, k],
                    recv_sem=d2d_recv.at[w, k], device_id=(x, y, 1 - c), device_id_type=MESH)
                cp.start()
                sends.append(cp)
        for w in range(nw):
            for k, (px, py) in enumerate(chips):
                jk = 2 * px + py
                land = outs[w].at[jk, 1 - c]
                pltpu.make_async_remote_copy(
                    src_ref=land, dst_ref=land, send_sem=d2d_send.at[w, k],
                    recv_sem=d2d_recv.at[w, k], device_id=(x, y, 1 - c), device_id_type=MESH).wait_recv()
        for cp in sends:
            cp.wait_send()
        for cp in copies:
            cp.wait()

    outs = pl.pallas_call(
        body, name="gather_weights",
        in_specs=[ANY] * nw, out_specs=[ANY] * nw,
        out_shape=[SDS((4,) + v.shape, BF16) for v in views],
        scratch_shapes=[pltpu.SemaphoreType.DMA((nw, 3)), pltpu.SemaphoreType.DMA((nw, 3)),
                        pltpu.SemaphoreType.DMA((nw, 3)), pltpu.SemaphoreType.DMA((nw, 3)),
                        pltpu.SemaphoreType.DMA((nw,))],
    )(*views)
    return [o.reshape(4 * s.shape[0], s.shape[1]) for o, s in zip(outs, shards)]


class _Rider:
    def __init__(self, ins, out_shape, scratch, start, finish, mid=None):
        self.ins, self.out_shape, self.scratch = list(ins), list(out_shape), list(scratch)
        self.start, self.finish, self.mid = start, finish, mid


def _ride(body, n_in, n_out, n_scr, rider, steps):
    if rider is None:
        return body, [], [], [], []
    ri, ro = len(rider.ins), len(rider.out_shape)

    def wrapped(*refs):
        ins, r_ins = refs[:n_in], refs[n_in:n_in + ri]
        o0 = n_in + ri
        outs, r_outs = refs[o0:o0 + n_out], refs[o0 + n_out:o0 + n_out + ro]
        s0 = o0 + n_out + ro
        scr, r_scr = refs[s0:s0 + n_scr], refs[s0 + n_scr:]
        i = pl.program_id(0)

        @pl.when(i == 0)
        def _():
            rider.start(r_ins, r_outs, r_scr)

        if rider.mid is not None:
            @pl.when(i == (3 * steps) // 4)
            def _():
                rider.mid(r_ins, r_outs, r_scr)

        body(*ins, *outs, *scr)

        @pl.when(i == steps - 1)
        def _():
            rider.finish(r_ins, r_outs, r_scr)

    return wrapped, [ANY] * ri, [ANY] * ro, rider.out_shape, rider.scratch


def _gather_rider(shards):
    nw = len(shards)
    views = [s.reshape(2, s.shape[0] // 2, s.shape[1]) for s in shards]

    def parts(ins, outs, sems):
        ici_send, ici_recv, d2d_send, d2d_recv, loc = sems
        x, y, c = _place()
        j = 2 * x + y
        local, ici, land_ici, fwd, land_fwd = [], [], [], [], []
        for w in range(nw):
            local.append(pltpu.make_async_copy(ins[w], outs[w].at[j], loc.at[w]))
            for k, (px, py) in enumerate(_peer_chips(x, y)):
                jk = 2 * px + py
                ici.append(pltpu.make_async_remote_copy(
                    src_ref=ins[w].at[c], dst_ref=outs[w].at[j, c], send_sem=ici_send.at[w, k],
                    recv_sem=ici_recv.at[w, k], device_id=(px, py, c), device_id_type=MESH))
                mine = outs[w].at[jk, c]
                land_ici.append(pltpu.make_async_remote_copy(
                    src_ref=ins[w].at[c], dst_ref=mine, send_sem=ici_send.at[w, k],
                    recv_sem=ici_recv.at[w, k], device_id=(px, py, c), device_id_type=MESH))
                fwd.append(pltpu.make_async_remote_copy(
                    src_ref=mine, dst_ref=mine, send_sem=d2d_send.at[w, k],
                    recv_sem=d2d_recv.at[w, k], device_id=(x, y, 1 - c), device_id_type=MESH))
                theirs = outs[w].at[jk, 1 - c]
                land_fwd.append(pltpu.make_async_remote_copy(
                    src_ref=theirs, dst_ref=theirs, send_sem=d2d_send.at[w, k],
                    recv_sem=d2d_recv.at[w, k], device_id=(x, y, 1 - c), device_id_type=MESH))
        return local, ici, land_ici, fwd, land_fwd

    def start(ins, outs, sems):
        local, ici, _, _, _ = parts(ins, outs, sems)
        for cp in local + ici:
            cp.start()

    def mid(ins, outs, sems):
        _, _, land_ici, fwd, _ = parts(ins, outs, sems)
        for landed, cp in zip(land_ici, fwd):
            landed.wait_recv()
            cp.start()

    def finish(ins, outs, sems):
        local, ici, _, fwd, land_fwd = parts(ins, outs, sems)
        for cp in land_fwd:
            cp.wait_recv()
        for cp in ici + fwd:
            cp.wait_send()
        for cp in local:
            cp.wait()

    sem = pltpu.SemaphoreType.DMA
    return _Rider(views, [SDS((4,) + v.shape, BF16) for v in views],
                  [sem((nw, 3)), sem((nw, 3)), sem((nw, 3)), sem((nw, 3)), sem((nw,))], start, finish, mid)


def _chip_exchange_rider(parts):
    nw = len(parts)

    def copies(ins, outs, sems):
        send, recv = sems
        x, y, c = _place()
        return [pltpu.make_async_remote_copy(
            src_ref=ins[w].at[2 * px + py], dst_ref=outs[w].at[k], send_sem=send.at[w, k],
            recv_sem=recv.at[w, k], device_id=(px, py, c), device_id_type=MESH)
            for w in range(nw) for k, (px, py) in enumerate(_peer_chips(x, y))]

    def start(ins, outs, sems):
        for cp in copies(ins, outs, sems):
            cp.start()

    def finish(ins, outs, sems):
        for cp in copies(ins, outs, sems):
            cp.wait()

    sem = pltpu.SemaphoreType.DMA
    return _Rider(parts, [SDS((3,) + p.shape[1:], BF16) for p in parts], [sem((nw, 3)), sem((nw, 3))], start, finish)


def _pair_exchange(grads, name):
    nw = len(grads)

    def body(*refs):
        ins = refs[:nw]
        outs = refs[nw:2 * nw]
        send, recv = refs[2 * nw:]
        x, y, c = _place()
        cps = []
        for w in range(nw):
            cp = pltpu.make_async_remote_copy(
                src_ref=ins[w].at[:, pl.ds(1 - c, 1)], dst_ref=outs[w], send_sem=send.at[w], recv_sem=recv.at[w],
                device_id=(x, y, 1 - c), device_id_type=MESH)
            cp.start()
            cps.append(cp)
        for cp in cps:
            cp.wait()

    return pl.pallas_call(
        body, name=name, in_specs=[ANY] * nw, out_specs=[ANY] * nw,
        out_shape=[SDS((4, 1) + g.shape[2:], BF16) for g in grads],
        scratch_shapes=[pltpu.SemaphoreType.DMA((nw,)), pltpu.SemaphoreType.DMA((nw,))],
    )(*grads)


def _half_tile(rh):
    best = 16
    for t in range(16, 545, 16):
        if rh % t == 0:
            best = t
    return best


def _pair_sum(g, got, name):
    _, _, rh, n = g.shape
    tr = _half_tile(rh)

    def body(a_ref, b_ref, o_ref):
        o_ref[...] = (a_ref[...].astype(F32) + b_ref[...].astype(F32)).astype(BF16)

    return pl.pallas_call(
        body, name=name, grid=(4, rh // tr),
        in_specs=[pl.BlockSpec((1, 1, tr, n), lambda s, i: (s, lax.axis_index("c"), i, 0)),
                  pl.BlockSpec((1, 1, tr, n), lambda s, i: (s, 0, i, 0))],
        out_specs=pl.BlockSpec((1, 1, tr, n), lambda s, i: (s, 0, i, 0)),
        out_shape=SDS((4, 1, rh, n), BF16),
        compiler_params=_cp(("arbitrary", "arbitrary")),
    )(g, got)


def _chip_sum(part, got, name):
    _, _, rh, n = part.shape
    tr = _half_tile(rh)

    def body(a_ref, b_ref, o_ref):
        acc = a_ref[0, 0].astype(F32)
        for k in range(3):
            acc = acc + b_ref[k, 0].astype(F32)
        o_ref[0] = acc

    return pl.pallas_call(
        body, name=name, grid=(rh // tr,),
        in_specs=[pl.BlockSpec((1, 1, tr, n), lambda i: (2 * lax.axis_index("x") + lax.axis_index("y"), 0, i, 0)),
                  pl.BlockSpec((3, 1, tr, n), lambda i: (0, 0, i, 0))],
        out_specs=pl.BlockSpec((1, tr, n), lambda i: (lax.axis_index("c"), i, 0)),
        out_shape=SDS((2, rh, n), F32),
        compiler_params=_cp(("arbitrary",)),
    )(part, got)


def _half_swap(halves):
    nw = len(halves)

    def body(*refs):
        ins = refs[:nw]
        outs = refs[nw:2 * nw]
        send, recv = refs[2 * nw:]
        x, y, c = _place()
        cps = []
        for w in range(nw):
            cp = pltpu.make_async_remote_copy(
                src_ref=ins[w].at[c], dst_ref=outs[w].at[c], send_sem=send.at[w], recv_sem=recv.at[w],
                device_id=(x, y, 1 - c), device_id_type=MESH)
            cp.start()
            cps.append(cp)
        for cp in cps:
            cp.wait()

    return pl.pallas_call(
        body, name="grad_half_swap", in_specs=[ANY] * nw, out_specs=[ANY] * nw,
        out_shape=[SDS(h.shape, F32) for h in halves],
        input_output_aliases={w: w for w in range(nw)},
        scratch_shapes=[pltpu.SemaphoreType.DMA((nw,)), pltpu.SemaphoreType.DMA((nw,))],
    )(*halves)


def _all_sum_small(part, name):
    R = part.shape[0]

    def body(p_ref, o_ref, land, send, recv):
        x, y, c = _place()
        me = 4 * x + 2 * y + c
        cps = []
        for d in range(1, 8):
            t = (me + d) % 8
            cp = pltpu.make_async_remote_copy(
                src_ref=p_ref, dst_ref=land.at[me], send_sem=send.at[d - 1], recv_sem=recv.at[d - 1],
                device_id=(t // 4, (t // 2) % 2, t % 2), device_id_type=MESH)
            cp.start()
            cps.append(cp)
        land[me] = p_ref[...]
        for cp in cps:
            cp.wait()
        acc = land[0]
        for d in range(1, 8):
            acc = acc + land[d]
        o_ref[...] = acc

    return pl.pallas_call(
        body, name=name,
        in_specs=[pl.BlockSpec(memory_space=pltpu.VMEM)], out_specs=pl.BlockSpec(memory_space=pltpu.VMEM),
        out_shape=SDS((R, D), F32),
        scratch_shapes=[pltpu.VMEM((8, R, D), F32), pltpu.SemaphoreType.DMA((7,)), pltpu.SemaphoreType.DMA((7,))],
        compiler_params=pltpu.CompilerParams(vmem_limit_bytes=VMEM_LIMIT),
    )(part)


def _pad_rows(a, rows):
    return jnp.pad(a, ((0, rows - a.shape[0]), (0, 0)))


def _vec_pack(vs):
    return jnp.concatenate([_pad_rows(v, LANE_ROWS) for v in vs], axis=0)


def kernel(x, rel_bias_table, g_pre_mix, w_in, b_glu, w_dw, b_dw, g_conv_ln, b_conv_ln, w_conv_out, b_conv_out, w_attn_out, w_mix_out, g_post_mix, g_pre_ffn, w_ffn_in, w_ffn_out, g_post_ffn, loss_target, m_rel_bias_table, m_g_pre_mix, m_w_in, m_b_glu, m_w_dw, m_b_dw, m_g_conv_ln, m_b_conv_ln, m_w_conv_out, m_b_conv_out, m_w_attn_out, m_w_mix_out, m_g_post_mix, m_g_pre_ffn, m_w_ffn_in, m_w_ffn_out, m_g_post_ffn, v_rel_bias_table, v_g_pre_mix, v_w_in, v_b_glu, v_w_dw, v_b_dw, v_g_conv_ln, v_b_conv_ln, v_w_conv_out, v_b_conv_out, v_w_attn_out, v_w_mix_out, v_g_post_mix, v_g_pre_ffn, v_w_ffn_in, v_w_ffn_out, v_g_post_ffn):
    S = x.shape[1]
    xs = x.reshape(S, D)
    tgt = loss_target.reshape(S, D)
    cx, cy, cc = _place()
    chip = 2 * cx + cy

    shards = [w_in[0].T.astype(BF16),
              w_ffn_in[0].T.astype(BF16),
              w_attn_out[0].T.astype(BF16),
              w_conv_out[0].astype(BF16),
              w_mix_out[0].astype(BF16),
              w_ffn_out[0].astype(BF16)]
    (w_inT,) = _gather_weights(shards[:1])
    w_inN = w_inT.T

    buckets_np, valid_np = _bucket_tables()
    buckets = jnp.asarray(buckets_np)
    bias = _bias_expand(rel_bias_table, buckets, jnp.asarray(valid_np)).reshape(3, NH, QBLK, KBLK)
    bias2 = bias.reshape(3, NH // 2, 2 * QBLK, KBLK)
    bias_t = bias.reshape(3, NH // 2, 2, QBLK, KBLK).transpose(0, 1, 4, 2, 3).reshape(3, NH // 2, KBLK, 2 * QBLK)
    wdw32 = _pad_rows(w_dw[0], 32)
    wdw_full = _gather_small_cols(wdw32, chip)

    zrest, h, h_r4, h_r16, *gathered = _in_proj_rest(xs, g_pre_mix, w_inN[:, 3 * ATTN_COLS:], _gather_rider(shards[1:]))
    w_fiT, w_aoT, w_co, w_mx, w_fo = (t.reshape(4 * s.shape[0], s.shape[1]) for t, s in zip(gathered, shards[1:]))
    w_fiN, w_aoN = w_fiT.T, w_aoT.T
    w_coT, w_mxT, w_foT = w_co.T, w_mx.T, w_fo.T
    zq = _in_proj_qkv(h, w_inN[:, :3 * ATTN_COLS])
    og, lg = [], []
    for gi in range(3):
        o_g, l_g = _attn_fwd(zq[gi], bias2, gi)
        og.append(o_g)
        lg.append(l_g)
    cv, a = _conv_fwd(zrest, b_glu, wdw_full, b_dw, g_conv_ln, b_conv_ln)
    o, o_r4, o_r16, lse, lse_r4, lse_r16, ya, yc, mg, mm, x1 = _mix_fwd(
        og, lg, a, zrest, xs, w_aoN, w_co, b_conv_out, w_mx, g_post_mix)
    h2, gu, df, dx2, loss8, dg_post_ffn = _ffn_fwd(x1, tgt, g_pre_ffn, g_post_ffn, w_fiN, w_fo)

    ident = lambda n: [list(range(n))]

    def pair_sums(partials, names, tag):
        views = [g.reshape(4, 2, g.shape[0] // 8, g.shape[1]) for g in partials]
        got = _pair_exchange(views, f"grad_pair_exchange_{tag}")
        return [_pair_sum(v, r, f"pair_sum_{n}") for v, r, n in zip(views, got, names)]

    def chip_sums(pair, got, names):
        return [_chip_sum(p, r, f"chip_sum_{n}") for p, r, n in zip(pair, got, names)]

    dff, act = _ffn_bwd_act(df, gu, w_foT)
    g_fiT = _mm_tn(dff, h2, 512, [[2 * t if t < NFT else 2 * (t - NFT) + 1 for t in range(0, 22, 2)],
                                  [2 * t if t < NFT else 2 * (t - NFT) + 1 for t in range(1, 22, 2)]], "dw_ffn_in")
    g_fo = _mm_tn(act, df, FFN_H // 2, ident(2), "dw_ffn_out")
    names_a = ("w_ffn_in", "w_ffn_out")
    pair_a = pair_sums([g_fiT, g_fo], names_a, "ffn")
    dx1, dg_pre_ffn, *got_a = _ffn_bwd_in(dff, x1, dx2, g_pre_ffn, w_fiT, _chip_exchange_rider(pair_a))
    halves_a = chip_sums(pair_a, got_a, names_a)
    dmm, dya, dyc, do, do_r4, do_r16, da, dzg, dg_post_mix, db_conv_out = _mix_bwd(
        dx1, mm, ya, yc, zrest, g_post_mix, w_mxT, w_aoT, w_coT)
    dglu, db_glu_u, db_glu_g, dw_dw, dg_conv_ln, db_conv_ln, db_dw = _conv_bwd(da, cv, zrest, b_glu, wdw_full, g_conv_ln, b_conv_ln)
    first = lambda t: t.reshape(1, S, GW)
    dqkv, dbias = [], []
    for gi, (do_g, o_g, lse_g) in enumerate(((first(do), first(o), first(lse)), (do_r4, o_r4, lse_r4),
                                            (do_r16, o_r16, lse_r16))):
        d_g, db_g = _attn_bwd(zq[gi], do_g, o_g, lse_g, bias_t, gi)
        dqkv.append(d_g)
        dbias.append(db_g.reshape(NH // 2, KBLK, 2, QBLK).transpose(0, 2, 3, 1).reshape(NH, QBLK, KBLK))
    dtab = _bias_reduce(jnp.concatenate(dbias, axis=0), buckets)

    g_inT = _dw_in(dqkv, dglu, dzg, (h, h_r4, h_r16))
    g_aoT = _mm_tn(dya, o, 512, ident(2), "dw_attn_out")
    g_co = _mm_tn(a, dyc, 512, ident(2), "dw_conv_out")
    g_mx = _mm_tn(mg, dmm, 512, ident(2), "dw_mix_out")
    names_b = ("w_in", "w_attn_out", "w_conv_out", "w_mix_out")
    pair_b = pair_sums([g_inT, g_aoT, g_co, g_mx], names_b, "rest")
    grad_x, dg_pre_mix, *got_b = _in_bwd(dqkv, dglu, dzg, w_inT, xs, dx1, g_pre_mix, _chip_exchange_rider(pair_b))
    halves_b = chip_sums(pair_b, got_b, names_b)

    red = [t.reshape(t.shape[0] * t.shape[1], t.shape[2]) for t in _half_swap(halves_a + halves_b)]
    gw_ffn_in, gw_ffn_out, gw_in, gw_attn_out, gw_conv_out, gw_mix_out = (
        red[0].T, red[1], red[2].T, red[3].T, red[4], red[5])

    small = jnp.concatenate([loss8, dg_pre_mix, db_glu_u, db_glu_g, db_dw, dg_conv_ln, db_conv_ln, db_conv_out,
                             dg_post_mix, dg_pre_ffn, dg_post_ffn, dtab, dw_dw], axis=0)
    tot = _all_sum_small(small, "small_all_sum")
    row = lambda i: tot[LANE_ROWS * i:LANE_ROWS * i + 1]
    loss = tot[0, 0]
    g_g_pre_mix, g_b_glu = row(1), jnp.concatenate([row(2), row(3)], axis=1)
    g_b_dw, g_g_conv_ln, g_b_conv_ln, g_b_conv_out = row(4), row(5), row(6), row(7)
    g_g_post_mix, g_g_pre_ffn, g_g_post_ffn = row(8), row(9), row(10)
    g_tab = tot[88:112, 0:32].T
    g_w_dw = lax.dynamic_slice(tot[112:112 + CONV_W], (0, 256 * chip), (CONV_W, 256))

    vec_names = ["g_pre_mix", "b_dw", "g_conv_ln", "b_conv_ln", "b_conv_out", "g_post_mix", "g_pre_ffn", "g_post_ffn"]
    vec_w = [g_pre_mix, b_dw, g_conv_ln, b_conv_ln, b_conv_out, g_post_mix, g_pre_ffn, g_post_ffn]
    vec_m = [m_g_pre_mix, m_b_dw, m_g_conv_ln, m_b_conv_ln, m_b_conv_out, m_g_post_mix, m_g_pre_ffn, m_g_post_ffn]
    vec_v = [v_g_pre_mix, v_b_dw, v_g_conv_ln, v_b_conv_ln, v_b_conv_out, v_g_post_mix, v_g_pre_ffn, v_g_post_ffn]
    vec_g = [g_g_pre_mix, g_b_dw, g_g_conv_ln, g_b_conv_ln, g_b_conv_out, g_g_post_mix, g_g_pre_ffn, g_g_post_ffn]

    def pack(vs, glu, tab, dw):
        return jnp.concatenate([_vec_pack(vs), _pad_rows(glu.reshape(2, D), LANE_ROWS),
                                _pad_rows(jnp.pad(tab.T, ((0, 0), (0, D - 32))), 24),
                                _pad_rows(jnp.pad(dw, ((0, 0), (0, D - 256))), 32)], axis=0)

    sw = pack(vec_w, b_glu, rel_bias_table, w_dw[0])
    sg = pack(vec_g, g_b_glu, g_tab, g_w_dw)
    sm = pack(vec_m, m_b_glu, m_rel_bias_table, m_w_dw[0])
    sv = pack(vec_v, v_b_glu, v_rel_bias_table, v_w_dw[0])
    s_out = _adamw(sw, sg, sm, sv, "adamw_small")

    def unpack(t):
        vecs = {n: t[LANE_ROWS * i:LANE_ROWS * i + 1] for i, n in enumerate(vec_names)}
        vecs["b_glu"] = t[64:66].reshape(1, 2 * D)
        vecs["rel_bias_table"] = t[72:96, 0:32].T
        vecs["w_dw"] = t[96:96 + CONV_W, 0:256][None]
        return vecs

    small_out = [unpack(t) for t in s_out]
    big = {}
    for n, w, g, m, v in (("w_in", w_in, gw_in, m_w_in, v_w_in),
                          ("w_conv_out", w_conv_out, gw_conv_out, m_w_conv_out, v_w_conv_out),
                          ("w_attn_out", w_attn_out, gw_attn_out, m_w_attn_out, v_w_attn_out),
                          ("w_mix_out", w_mix_out, gw_mix_out, m_w_mix_out, v_w_mix_out),
                          ("w_ffn_in", w_ffn_in, gw_ffn_in, m_w_ffn_in, v_w_ffn_in),
                          ("w_ffn_out", w_ffn_out, gw_ffn_out, m_w_ffn_out, v_w_ffn_out)):
        big[n] = [t[None] for t in _adamw(w[0], g, m[0], v[0], f"adamw_{n}")]

    order = ["rel_bias_table", "g_pre_mix", "w_in", "b_glu", "w_dw", "b_dw", "g_conv_ln", "b_conv_ln", "w_conv_out",
             "b_conv_out", "w_attn_out", "w_mix_out", "g_post_mix", "g_pre_ffn", "w_ffn_in", "w_ffn_out", "g_post_ffn"]
    grads = {"rel_bias_table": g_tab, "g_pre_mix": g_g_pre_mix, "w_in": gw_in[None], "b_glu": g_b_glu,
             "w_dw": g_w_dw[None], "b_dw": g_b_dw, "g_conv_ln": g_g_conv_ln, "b_conv_ln": g_b_conv_ln,
             "w_conv_out": gw_conv_out[None], "b_conv_out": g_b_conv_out, "w_attn_out": gw_attn_out[None],
             "w_mix_out": gw_mix_out[None], "g_post_mix": g_g_post_mix, "g_pre_ffn": g_g_pre_ffn,
             "w_ffn_in": gw_ffn_in[None], "w_ffn_out": gw_ffn_out[None], "g_post_ffn": g_g_post_ffn}
    outs = [loss, grad_x.reshape(1, S, D)] + [grads[n] for n in order]
    for slot in range(3):
        outs += [big[n][slot] if n in big else small_out[slot][n] for n in order]
    return tuple(outs)


def _gather_small_cols(wdw32, chip):
    placed = lax.dynamic_update_slice(jnp.zeros((32, D), F32), wdw32, (0, 256 * chip))
    return _all_sum_small(placed, "conv_taps_gather") * 0.5
```

```python
import functools
import math

import numpy as np
import jax
import jax.numpy as jnp
from jax import lax
from jax.experimental import pallas as pl
from jax.experimental.pallas import tpu as pltpu

F32 = jnp.float32
BF16 = jnp.bfloat16
SDS = jax.ShapeDtypeStruct
MESH = pl.DeviceIdType.MESH
ANY = pl.BlockSpec(memory_space=pl.ANY)

D = 1024
HD = 64
NH = 8
GW = NH * HD
ATTN_COLS = 3 * GW
DILATIONS = (1, 4, 16)
SPAN = 128
QBLK = 128
KBLK = 2 * QBLK
CONV_W = 31
FFN_H = 2816
FFN_T = 256
NFT = FFN_H // FFN_T
RMS_EPS = 1e-6
LN_EPS = 1e-5
NEG_INF = -1e30
SCALE = HD ** -0.5
LANE_ROWS = 8
LANES = 128

ADAM_LR, ADAM_B1, ADAM_B2, ADAM_EPS, ADAM_WD, ADAM_STEP = 0.001, 0.9, 0.999, 1e-08, 0.01, 10

VMEM_LIMIT = 56 * 1024 * 1024


def _cp(sem):
    return pltpu.CompilerParams(dimension_semantics=sem, vmem_limit_bytes=VMEM_LIMIT)


def _dot(a, b):
    return jnp.dot(a, b, preferred_element_type=F32)


def _dot_nt(a, b):
    return lax.dot_general(a, b, (((1,), (1,)), ((), ())), preferred_element_type=F32)


def _dot_tn(a, b):
    return lax.dot_general(a, b, (((0,), (0,)), ((), ())), preferred_element_type=F32)


def _sigmoid(v):
    return 0.5 * jnp.tanh(0.5 * v) + 0.5


def _colsum8(v):
    s = jnp.sum(v, axis=0, keepdims=True)
    row = lax.broadcasted_iota(jnp.int32, (LANE_ROWS, v.shape[1]), 0)
    return jnp.where(row == 0, jnp.broadcast_to(s, (LANE_ROWS, v.shape[1])), 0.0)


def _col_scratch(n, width):
    return pltpu.VMEM((width // LANES, n, LANES), F32)


def _store_cols(scr, v):
    for lb in range(scr.shape[0]):
        scr[lb] = v[:, lb * LANES:(lb + 1) * LANES]


def _load_cols(scr):
    return jnp.concatenate([scr[lb] for lb in range(scr.shape[0])], axis=1)


def _split_residues(scr, dil, put):
    nb, n, _ = scr.shape
    for c in range(dil):
        put(c, jnp.concatenate([scr[lb, pl.ds(c, n // dil, stride=dil), :] for lb in range(nb)], axis=1))


def _merge_residues(scr, dil, get):
    nb, n, _ = scr.shape
    for c in range(dil):
        v = get(c)
        for lb in range(nb):
            scr[lb, pl.ds(c, n // dil, stride=dil), :] = v[:, lb * LANES:(lb + 1) * LANES]


def _residue_shape(S, dil, width):
    return (dil, S // dil, width)


def _residue_spec(TM, dil, width):
    return pl.BlockSpec((dil, TM // dil, width), lambda i: (0, i, 0))


def _in_proj_rest(x, g, w, rider):
    S = x.shape[0]
    N = w.shape[1]
    TM, TN = 512, 512

    def body(x_ref, g_ref, w_ref, zr_ref, h0_ref, h1_ref, h2_ref, hf_scr):
        xf = x_ref[...]
        r = lax.rsqrt(jnp.mean(xf * xf, axis=-1, keepdims=True) + RMS_EPS)
        hf = xf * r * g_ref[...]
        h0_ref[...] = hf.astype(BF16)
        _store_cols(hf_scr, hf)
        for dil, ref in ((DILATIONS[1], h1_ref), (DILATIONS[2], h2_ref)):
            def put(c, v, ref=ref):
                ref[c] = v.astype(BF16)
            _split_residues(hf_scr, dil, put)
        for j in range(N // TN):
            zr_ref[:, j * TN:(j + 1) * TN] = _dot(h0_ref[...], w_ref[:, j * TN:(j + 1) * TN]).astype(BF16)

    body, r_in, r_out, r_shape, r_scr = _ride(body, 3, 4, 1, rider, S // TM)
    return pl.pallas_call(
        body, name="in_proj_rest", grid=(S // TM,),
        in_specs=[pl.BlockSpec((TM, D), lambda i: (i, 0)),
                  pl.BlockSpec((1, D), lambda i: (0, 0)),
                  pl.BlockSpec((D, N), lambda i: (0, 0), pipeline_mode=pl.Buffered(1))] + r_in,
        out_specs=[pl.BlockSpec((TM, N), lambda i: (i, 0)), pl.BlockSpec((TM, D), lambda i: (i, 0)),
                   _residue_spec(TM, DILATIONS[1], D), _residue_spec(TM, DILATIONS[2], D)] + r_out,
        out_shape=[SDS((S, N), BF16), SDS((S, D), BF16),
                   SDS(_residue_shape(S, DILATIONS[1], D), BF16),
                   SDS(_residue_shape(S, DILATIONS[2], D), BF16)] + r_shape,
        scratch_shapes=[_col_scratch(TM, D)] + r_scr,
        compiler_params=_cp(("arbitrary",)),
    )(x, g, w, *rider.ins)


def _in_proj_qkv(h, w):
    S = h.shape[0]
    TM = 512

    def body(h_ref, w_ref, z0_ref, z1_ref, z2_ref, scr):
        outs = (z0_ref, z1_ref, z2_ref)
        for j in range(9):
            t, gi = j // 3, j % 3
            cols = slice(t * GW, (t + 1) * GW)
            zt = _dot(h_ref[...], w_ref[:, j * GW:(j + 1) * GW])
            if gi == 0:
                z0_ref[0, :, cols] = zt.astype(BF16)
            else:
                slot = scr.at[2 * t + gi - 1]
                _store_cols(slot, zt)

                def put(c, v, ref=outs[gi], cols=cols):
                    ref[c, :, cols] = v.astype(BF16)
                _split_residues(slot, DILATIONS[gi], put)

    return pl.pallas_call(
        body, name="in_proj_qkv", grid=(S // TM,),
        in_specs=[pl.BlockSpec((TM, D), lambda i: (i, 0)),
                  pl.BlockSpec(w.shape, lambda i: (0, 0), pipeline_mode=pl.Buffered(1))],
        out_specs=[_residue_spec(TM, d, 3 * GW) for d in DILATIONS],
        out_shape=[SDS(_residue_shape(S, d, 3 * GW), BF16) for d in DILATIONS],
        scratch_shapes=[pltpu.VMEM((6, GW // LANES, TM, LANES), F32)],
        compiler_params=_cp(("arbitrary",)),
    )(h, w)


def _bucket_tables():
    a = np.arange(QBLK, dtype=np.int32)[:, None]
    c = np.arange(KBLK, dtype=np.int32)[None, :]
    off = a - c + QBLK
    valid = ((off >= 0) & (off <= SPAN)).astype(np.float32)
    tabs = []
    for dil in DILATIONS:
        dist = np.maximum(off * dil, 0)
        df = np.maximum(dist, 1).astype(np.float32)
        large = 16 + (np.log(df / np.float32(16)) / np.float32(math.log(2048 / 16)) * np.float32(16)).astype(np.int32)
        large = np.minimum(large, 31)
        tabs.append(np.where(dist < 16, dist, large).astype(np.int32))
    return np.stack(tabs), valid


def _bias_expand(tab, buckets, valid):
    def body(tab_ref, b_ref, v_ref, o_ref):
        for gi in range(3):
            bk = b_ref[gi]
            for h in range(NH):
                acc = jnp.zeros((QBLK, KBLK), F32)
                for b in range(32):
                    acc = jnp.where(bk == b, tab_ref[b, gi * NH + h], acc)
                o_ref[gi * NH + h] = jnp.where(v_ref[...] > 0.5, acc, NEG_INF)

    return pl.pallas_call(
        body, name="bias_expand",
        in_specs=[pl.BlockSpec(memory_space=pltpu.SMEM),
                  pl.BlockSpec(memory_space=pltpu.VMEM), pl.BlockSpec(memory_space=pltpu.VMEM)],
        out_specs=pl.BlockSpec(memory_space=pltpu.VMEM),
        out_shape=SDS((3 * NH, QBLK, KBLK), F32),
    )(tab, buckets, valid)


def _bias_reduce(dbias, buckets):
    def body(d_ref, b_ref, o_ref):
        lane = lax.broadcasted_iota(jnp.int32, (1, D), 1)
        for gi in range(3):
            bk = b_ref[gi]
            for h in range(NH):
                dv = d_ref[gi * NH + h]
                row = jnp.zeros((1, D), F32)
                for b in range(32):
                    m = jnp.where(bk == b, dv, 0.0)
                    val = jnp.sum(jnp.sum(m, axis=0, keepdims=True), axis=1, keepdims=True)
                    row = jnp.where(lane == b, val, row)
                o_ref[gi * NH + h:gi * NH + h + 1, :] = row

    return pl.pallas_call(
        body, name="bias_reduce",
        in_specs=[pl.BlockSpec(memory_space=pltpu.VMEM), pl.BlockSpec(memory_space=pltpu.VMEM)],
        out_specs=pl.BlockSpec(memory_space=pltpu.VMEM),
        out_shape=SDS((3 * NH, D), F32),
    )(dbias, buckets)


def _attn_tile(S, dil):
    L = S // dil
    tq = min(512, L)
    return L, tq, tq // QBLK, L // tq


def _pair_stack(ref, rows, lanes, scale=None):
    blk = ref[rows, lanes]
    if scale is not None:
        blk = blk * scale
    lane = lax.broadcasted_iota(jnp.int32, blk.shape, 1)
    zero = jnp.zeros_like(blk)
    return jnp.concatenate([jnp.where(lane < HD, blk, zero), jnp.where(lane >= HD, blk, zero)], axis=0)


def _attn_fwd(zq, bias2, gi):
    dil, L, _ = zq.shape
    _, TQ, QB, ns = _attn_tile(L * dil, dil)
    NP = NH // 2

    def body(q_ref, kc_ref, kp_ref, vc_ref, vp_ref, b_ref, o_ref, l_ref, kext, vext):
        n = pl.program_id(1)
        kext[0:QBLK, :] = kp_ref[0]
        kext[QBLK:, :] = kc_ref[0]
        vext[0:QBLK, :] = vp_ref[0]
        vext[QBLK:, :] = vc_ref[0]
        col = lax.broadcasted_iota(jnp.int32, (2 * QBLK, KBLK), 1)
        no_prev = jnp.logical_and(n == 0, col < QBLK)
        lane = lax.broadcasted_iota(jnp.int32, (QBLK, LANES), 1)
        lanes_of = [slice(hp * LANES, (hp + 1) * LANES) for hp in range(NP)]
        for b in range(QB):
            rows = slice(b * QBLK, (b + 1) * QBLK)
            win = slice(b * QBLK, b * QBLK + KBLK)
            s = [_dot_nt(_pair_stack(q_ref.at[0], rows, pl_, SCALE), kext[win, pl_]) + b_ref[0, hp]
                 for hp, pl_ in enumerate(lanes_of)]
            if b == 0:
                s = [jnp.where(no_prev, NEG_INF, v) for v in s]
            m = [jnp.max(v, axis=-1, keepdims=True) for v in s]
            p = [jnp.exp(v - mv) for v, mv in zip(s, m)]
            l = [jnp.sum(v, axis=-1, keepdims=True) for v in p]
            o2 = [_dot(v.astype(BF16), vext[win, pl_]) / lv for v, lv, pl_ in zip(p, l, lanes_of)]
            for hp, pl_ in enumerate(lanes_of):
                lse2 = jnp.broadcast_to(m[hp] + jnp.log(l[hp]), (2 * QBLK, LANES))
                o_ref[0, rows, pl_] = jnp.where(lane < HD, o2[hp][0:QBLK], o2[hp][QBLK:]).astype(BF16)
                l_ref[0, rows, pl_] = jnp.where(lane < HD, lse2[0:QBLK], lse2[QBLK:])

    def prev(n):
        return jnp.maximum(n * QB - 1, 0)

    return pl.pallas_call(
        body, name=f"attn_fwd_g{gi}", grid=(dil, ns),
        in_specs=[pl.BlockSpec((1, TQ, GW), lambda c, n: (c, n, 0)),
                  pl.BlockSpec((1, TQ, GW), lambda c, n: (c, n, 1)),
                  pl.BlockSpec((1, QBLK, GW), lambda c, n: (c, prev(n), 1)),
                  pl.BlockSpec((1, TQ, GW), lambda c, n: (c, n, 2)),
                  pl.BlockSpec((1, QBLK, GW), lambda c, n: (c, prev(n), 2)),
                  pl.BlockSpec((1, NP, 2 * QBLK, KBLK), lambda c, n: (gi, 0, 0, 0))],
        out_specs=[pl.BlockSpec((1, TQ, GW), lambda c, n: (c, n, 0)),
                   pl.BlockSpec((1, TQ, GW), lambda c, n: (c, n, 0))],
        out_shape=[SDS((dil, L, GW), BF16), SDS((dil, L, GW), F32)],
        scratch_shapes=[pltpu.VMEM((QBLK + TQ, GW), BF16), pltpu.VMEM((QBLK + TQ, GW), BF16)],
        compiler_params=_cp(("arbitrary", "arbitrary")),
    )(zq, zq, zq, zq, zq, bias2)


CONV_TM = 256
SHIFT_PAD = 24


def _make_shifts(src, sh, n):
    for b in range(1, 8):
        sh[b - 1] = src[b:b + n + SHIFT_PAD, :]


def _shifted(src, sh, off, r0, n, lanes):
    a, b = divmod(off, 8)
    if b == 0:
        return src[8 * a + r0:8 * a + r0 + n, lanes]
    return sh[b - 1, 8 * a + r0:8 * a + r0 + n, lanes]


CONV_RC = 32


def _tap_blocks(TM):
    return [(r0, slice(l0, l0 + LANES)) for l0 in range(0, D, LANES) for r0 in range(0, TM, CONV_RC)]


def _conv_fwd(zrest, b_glu, wdw, b_dw, g_ln, b_ln):
    S = zrest.shape[0]
    TM = CONV_TM
    HALO = 32
    hb = TM // HALO

    def body(u_ref, g_ref, uh_ref, gh_ref, bg_ref, w_ref, bd_ref, gl_ref, bl_ref, cv_ref, a_ref, ext, sh):
        i = pl.program_id(0)
        bu = bg_ref[:, 0:D]
        bgt = bg_ref[:, D:2 * D]
        uh = (uh_ref[...].astype(F32) + bu) * _sigmoid(gh_ref[...].astype(F32) + bgt)
        ext[0:HALO, :] = jnp.where(i == 0, 0.0, uh)
        ext[HALO:, :] = (u_ref[...].astype(F32) + bu) * _sigmoid(g_ref[...].astype(F32) + bgt)
        _make_shifts(ext, sh, TM)
        acc = jnp.zeros((TM, D), F32)
        for j in range(CONV_W):
            acc = acc + _shifted(ext, sh, HALO - (CONV_W - 1) + j, 0, TM, slice(None)) * w_ref[j:j + 1, :]
        cv = (acc + bd_ref[...]).astype(BF16)
        cv_ref[...] = cv
        cf = cv.astype(F32)
        mu = jnp.mean(cf, axis=-1, keepdims=True)
        xc = cf - mu
        y = xc * lax.rsqrt(jnp.mean(xc * xc, axis=-1, keepdims=True) + LN_EPS) * gl_ref[...] + bl_ref[...]
        a_ref[...] = (y * _sigmoid(y)).astype(BF16)

    vec = pl.BlockSpec((1, D), lambda i: (0, 0))
    return pl.pallas_call(
        body, name="conv_fwd", grid=(S // TM,),
        in_specs=[pl.BlockSpec((TM, D), lambda i: (i, 0)), pl.BlockSpec((TM, D), lambda i: (i, 1)),
                  pl.BlockSpec((HALO, D), lambda i: (jnp.maximum(i * hb - 1, 0), 0)),
                  pl.BlockSpec((HALO, D), lambda i: (jnp.maximum(i * hb - 1, 0), 1)),
                  pl.BlockSpec((1, 2 * D), lambda i: (0, 0)),
                  pl.BlockSpec((32, D), lambda i: (0, 0)), vec, vec, vec],
        out_specs=[pl.BlockSpec((TM, D), lambda i: (i, 0)), pl.BlockSpec((TM, D), lambda i: (i, 0))],
        out_shape=[SDS((S, D), BF16), SDS((S, D), BF16)],
        scratch_shapes=[pltpu.VMEM((HALO + TM, D), F32), pltpu.VMEM((7, TM + SHIFT_PAD, D), F32)],
        compiler_params=_cp(("arbitrary",)),
    )(zrest, zrest, zrest, zrest, b_glu, wdw, b_dw, g_ln, b_ln)


def _mix_fwd(og, lg, a, zrest, x, w_ao, w_co, b_co, w_mx, g_pm):
    S = x.shape[0]
    TM = 512

    def body(o0, o1, o2, l0, l1, l2, a_ref, ga_ref, gc_ref, x_ref, wa_ref, wc_ref, bc_ref, wm_ref, g_ref,
             o_ref, oa_ref, ob_ref, lse_ref, lsea_ref, lseb_ref, ya_ref, yc_ref, mg_ref, mm_ref, x1_ref,
             so1, so2, sl1, sl2, so, sl):
        for dil, src, dst, cast in ((DILATIONS[1], o1, so1, True), (DILATIONS[2], o2, so2, True),
                                    (DILATIONS[1], l1, sl1, False), (DILATIONS[2], l2, sl2, False)):
            _merge_residues(dst, dil, (lambda c, src=src: src[c].astype(F32)) if cast else (lambda c, src=src: src[c]))
        la, lb, lc = l0[0], _load_cols(sl1), _load_cols(sl2)
        m = jnp.maximum(jnp.maximum(la, lb), lc)
        e0 = jnp.exp(la - m)
        e1 = jnp.exp(lb - m)
        e2 = jnp.exp(lc - m)
        den = e0 + e1 + e2
        of = (e0 * o0[0].astype(F32) + e1 * _load_cols(so1) + e2 * _load_cols(so2)) / den
        o = of.astype(BF16)
        o_ref[...] = o
        lse = m + jnp.log(den)
        lse_ref[...] = lse
        _store_cols(so, of)
        _store_cols(sl, lse)
        for dil, oref, lref in ((DILATIONS[1], oa_ref, lsea_ref), (DILATIONS[2], ob_ref, lseb_ref)):
            def put_o(c, v, oref=oref):
                oref[c] = v.astype(BF16)

            def put_l(c, v, lref=lref):
                lref[c] = v
            _split_residues(so, dil, put_o)
            _split_residues(sl, dil, put_l)
        ya = _dot(o, wa_ref[...]).astype(BF16)
        yc = (_dot(a_ref[...], wc_ref[...]) + bc_ref[...]).astype(BF16)
        ya_ref[...] = ya
        yc_ref[...] = yc
        mg = (_sigmoid(ga_ref[...].astype(F32)) * ya.astype(F32)
              + _sigmoid(gc_ref[...].astype(F32)) * yc.astype(F32)).astype(BF16)
        mg_ref[...] = mg
        mm = _dot(mg, wm_ref[...]).astype(BF16)
        mm_ref[...] = mm
        mf = mm.astype(F32)
        r = lax.rsqrt(jnp.mean(mf * mf, axis=-1, keepdims=True) + RMS_EPS)
        x1_ref[...] = x_ref[...] + mf * r * g_ref[...]

    row512 = pl.BlockSpec((TM, GW), lambda i: (i, 0))
    rowd = pl.BlockSpec((TM, D), lambda i: (i, 0))
    vec = pl.BlockSpec((1, D), lambda i: (0, 0))
    full = lambda r, c: pl.BlockSpec((r, c), lambda i: (0, 0))
    res = [_residue_spec(TM, d, GW) for d in DILATIONS]
    rshape = lambda d, t: SDS(_residue_shape(S, d, GW), t)
    scr = _col_scratch(TM, GW)
    return pl.pallas_call(
        body, name="mix_fwd", grid=(S // TM,),
        in_specs=res + res + [rowd, pl.BlockSpec((TM, D), lambda i: (i, 2)), pl.BlockSpec((TM, D), lambda i: (i, 3)),
                              rowd, full(GW, D), full(D, D), vec, full(D, D), vec],
        out_specs=[row512, res[1], res[2], row512, res[1], res[2], rowd, rowd, rowd, rowd, rowd],
        out_shape=[SDS((S, GW), BF16), rshape(DILATIONS[1], BF16), rshape(DILATIONS[2], BF16),
                   SDS((S, GW), F32), rshape(DILATIONS[1], F32), rshape(DILATIONS[2], F32),
                   SDS((S, D), BF16), SDS((S, D), BF16), SDS((S, D), BF16), SDS((S, D), BF16), SDS((S, D), F32)],
        scratch_shapes=[scr] * 6,
        compiler_params=_cp(("arbitrary",)),
    )(og[0], og[1], og[2], lg[0], lg[1], lg[2], a, zrest, zrest, x, w_ao, w_co, b_co, w_mx, g_pm)


def _ffn_fwd(x1, tgt, g_pre, g_post, w_fi, w_fo):
    S = x1.shape[0]
    TM = 512

    def body(x1_ref, t_ref, gp_ref, go_ref, wi_ref, wo_ref,
             h2_ref, gu_ref, df_ref, dx2_ref, loss_ref, dgo_ref):
        i = pl.program_id(0)

        @pl.when(i == 0)
        def _():
            loss_ref[...] = jnp.zeros_like(loss_ref)
            dgo_ref[...] = jnp.zeros_like(dgo_ref)

        xf = x1_ref[...]
        r = lax.rsqrt(jnp.mean(xf * xf, axis=-1, keepdims=True) + RMS_EPS)
        h2_ref[...] = (xf * r * gp_ref[...]).astype(BF16)
        for k in range(NFT):
            gu_ref[:, 2 * k * FFN_T:(2 * k + 1) * FFN_T] = _dot(
                h2_ref[...], wi_ref[:, k * FFN_T:(k + 1) * FFN_T]).astype(BF16)
            gu_ref[:, (2 * k + 1) * FFN_T:(2 * k + 2) * FFN_T] = _dot(
                h2_ref[...], wi_ref[:, FFN_H + k * FFN_T:FFN_H + (k + 1) * FFN_T]).astype(BF16)
        f = jnp.zeros((TM, D), F32)
        for k in range(NFT):
            gf = gu_ref[:, 2 * k * FFN_T:(2 * k + 1) * FFN_T].astype(F32)
            uf = gu_ref[:, (2 * k + 1) * FFN_T:(2 * k + 2) * FFN_T].astype(F32)
            act = (gf * _sigmoid(gf) * uf).astype(BF16)
            f = f + _dot(act, wo_ref[k * FFN_T:(k + 1) * FFN_T, :])
        r = lax.rsqrt(jnp.mean(f * f, axis=-1, keepdims=True) + RMS_EPS)
        nrm = f * r
        e = x1_ref[...] + nrm * go_ref[...] - t_ref[...]
        tot = jnp.sum(jnp.sum(e * e, axis=-1, keepdims=True), axis=0, keepdims=True) * (0.5 / D)
        corner = jnp.logical_and(lax.broadcasted_iota(jnp.int32, (LANE_ROWS, D), 0) == 0,
                                 lax.broadcasted_iota(jnp.int32, (LANE_ROWS, D), 1) == 0)
        loss_ref[...] += jnp.where(corner, tot, 0.0)
        dx2 = e * (1.0 / D)
        dx2_ref[...] = dx2
        dgo_ref[...] += _colsum8(dx2 * nrm)
        dn = dx2 * go_ref[...]
        df_ref[...] = (r * (dn - nrm * jnp.mean(dn * nrm, axis=-1, keepdims=True))).astype(BF16)

    rowd = pl.BlockSpec((TM, D), lambda i: (i, 0))
    vec = pl.BlockSpec((1, D), lambda i: (0, 0))
    acc8 = pl.BlockSpec((LANE_ROWS, D), lambda i: (0, 0))
    return pl.pallas_call(
        body, name="ffn_fwd", grid=(S // TM,),
        in_specs=[rowd, rowd, vec, vec,
                  pl.BlockSpec((D, 2 * FFN_H), lambda i: (0, 0), pipeline_mode=pl.Buffered(1)),
                  pl.BlockSpec((FFN_H, D), lambda i: (0, 0), pipeline_mode=pl.Buffered(1))],
        out_specs=[rowd, pl.BlockSpec((TM, 2 * FFN_H), lambda i: (i, 0)), rowd, rowd, acc8, acc8],
        out_shape=[SDS((S, D), BF16), SDS((S, 2 * FFN_H), BF16), SDS((S, D), BF16), SDS((S, D), F32),
                   SDS((LANE_ROWS, D), F32), SDS((LANE_ROWS, D), F32)],
        compiler_params=_cp(("arbitrary",)),
    )(x1, tgt, g_pre, g_post, w_fi, w_fo)


def _ffn_bwd_act(df, gu, w_foT):
    S = df.shape[0]
    TM = 512

    def body_act(df_ref, gu_ref, wo_ref, dff_ref, act_ref):
        dacts = [_dot(df_ref[...], wo_ref[:, k * FFN_T:(k + 1) * FFN_T]) for k in range(NFT)]
        for k in range(NFT):
            dact = dacts[k]
            g = gu_ref[:, 2 * k * FFN_T:(2 * k + 1) * FFN_T].astype(F32)
            u = gu_ref[:, (2 * k + 1) * FFN_T:(2 * k + 2) * FFN_T].astype(F32)
            sg = _sigmoid(g)
            sl = g * sg
            act_ref[:, k * FFN_T:(k + 1) * FFN_T] = (sl * u).astype(BF16)
            dff_ref[:, 2 * k * FFN_T:(2 * k + 1) * FFN_T] = (dact * u * (sg * (1.0 + g * (1.0 - sg)))).astype(BF16)
            dff_ref[:, (2 * k + 1) * FFN_T:(2 * k + 2) * FFN_T] = (dact * sl).astype(BF16)

    rowd = pl.BlockSpec((TM, D), lambda i: (i, 0))
    wide = pl.BlockSpec((TM, 2 * FFN_H), lambda i: (i, 0))
    return pl.pallas_call(
        body_act, name="ffn_bwd_act", grid=(S // TM,),
        in_specs=[rowd, wide, pl.BlockSpec((D, FFN_H), lambda i: (0, 0), pipeline_mode=pl.Buffered(1))],
        out_specs=[wide, pl.BlockSpec((TM, FFN_H), lambda i: (i, 0))],
        out_shape=[SDS((S, 2 * FFN_H), BF16), SDS((S, FFN_H), BF16)],
        compiler_params=_cp(("arbitrary",)),
    )(df, gu, w_foT)


def _ffn_bwd_in(dff, x1, dx2, g_pre, w_fiT, rider):
    S = x1.shape[0]
    TM = 512
    rowd = pl.BlockSpec((TM, D), lambda i: (i, 0))
    wide = pl.BlockSpec((TM, 2 * FFN_H), lambda i: (i, 0))
    KC = 512
    nkc = 2 * FFN_H // KC

    def body_in(dff_ref, x1_ref, dx2_ref, gp_ref, wi_ref, dx1_ref, dgp_ref):
        i = pl.program_id(0)

        @pl.when(i == 0)
        def _():
            dgp_ref[...] = jnp.zeros_like(dgp_ref)

        dh = jnp.zeros((TM, D), F32)
        for k in range(nkc):
            dh = dh + _dot(dff_ref[:, k * KC:k * KC + FFN_T], wi_ref[k * FFN_T:(k + 1) * FFN_T, :]) \
                + _dot(dff_ref[:, k * KC + FFN_T:(k + 1) * KC], wi_ref[FFN_H + k * FFN_T:FFN_H + (k + 1) * FFN_T, :])
        xf = x1_ref[...]
        r = lax.rsqrt(jnp.mean(xf * xf, axis=-1, keepdims=True) + RMS_EPS)
        nrm = xf * r
        dgp_ref[...] += _colsum8(dh * nrm)
        dn = dh * gp_ref[...]
        dx1_ref[...] = dx2_ref[...] + r * (dn - nrm * jnp.mean(dn * nrm, axis=-1, keepdims=True))

    body_in, r_in, r_out, r_shape, r_scr = _ride(body_in, 5, 2, 0, rider, S // TM)
    return pl.pallas_call(
        body_in, name="ffn_bwd_in", grid=(S // TM,),
        in_specs=[wide, rowd, rowd, pl.BlockSpec((1, D), lambda i: (0, 0)),
                  pl.BlockSpec((2 * FFN_H, D), lambda i: (0, 0), pipeline_mode=pl.Buffered(1))] + r_in,
        out_specs=[rowd, pl.BlockSpec((LANE_ROWS, D), lambda i: (0, 0))] + r_out,
        out_shape=[SDS((S, D), F32), SDS((LANE_ROWS, D), F32)] + r_shape,
        scratch_shapes=r_scr,
        compiler_params=_cp(("arbitrary",)),
    )(dff, x1, dx2, g_pre, w_fiT, *rider.ins)


def _mix_bwd(dx1, mm, ya, yc, zrest, g_pm, w_mxT, w_aoT, w_coT):
    S = dx1.shape[0]
    TM = 512

    def body(dx_ref, mm_ref, ya_ref, yc_ref, ga_ref, gc_ref, g_ref, wm_ref, wa_ref, wc_ref,
             dmm_ref, dya_ref, dyc_ref, do_ref, doa_ref, dob_ref, da_ref, dzg_ref, dgpm_ref, dbco_ref, sdo):
        i = pl.program_id(0)

        @pl.when(i == 0)
        def _():
            dgpm_ref[...] = jnp.zeros_like(dgpm_ref)
            dbco_ref[...] = jnp.zeros_like(dbco_ref)

        mf = mm_ref[...].astype(F32)
        r = lax.rsqrt(jnp.mean(mf * mf, axis=-1, keepdims=True) + RMS_EPS)
        nrm = mf * r
        dx = dx_ref[...]
        dgpm_ref[...] += _colsum8(dx * nrm)
        dn = dx * g_ref[...]
        dmm = (r * (dn - nrm * jnp.mean(dn * nrm, axis=-1, keepdims=True))).astype(BF16)
        dmm_ref[...] = dmm
        dmg = _dot(dmm, wm_ref[...])
        sa = _sigmoid(ga_ref[...].astype(F32))
        sc = _sigmoid(gc_ref[...].astype(F32))
        dya = (dmg * sa).astype(BF16)
        dyc = (dmg * sc).astype(BF16)
        dya_ref[...] = dya
        dyc_ref[...] = dyc
        dbco_ref[...] += _colsum8(dyc.astype(F32))
        dzg_ref[:, 0:D] = (dmg * ya_ref[...].astype(F32) * (sa * (1.0 - sa))).astype(BF16)
        dzg_ref[:, D:] = (dmg * yc_ref[...].astype(F32) * (sc * (1.0 - sc))).astype(BF16)
        dof = _dot(dya, wa_ref[...])
        do_ref[...] = dof.astype(BF16)
        _store_cols(sdo, dof)
        for dil, ref in ((DILATIONS[1], doa_ref), (DILATIONS[2], dob_ref)):
            def put(c, v, ref=ref):
                ref[c] = v.astype(BF16)
            _split_residues(sdo, dil, put)
        da_ref[...] = _dot(dyc, wc_ref[...]).astype(BF16)

    rowd = pl.BlockSpec((TM, D), lambda i: (i, 0))
    full = lambda r, c: pl.BlockSpec((r, c), lambda i: (0, 0))
    acc8 = pl.BlockSpec((LANE_ROWS, D), lambda i: (0, 0))
    return pl.pallas_call(
        body, name="mix_bwd", grid=(S // TM,),
        in_specs=[rowd, rowd, rowd, rowd, pl.BlockSpec((TM, D), lambda i: (i, 2)),
                  pl.BlockSpec((TM, D), lambda i: (i, 3)), full(1, D), full(D, D), full(D, GW), full(D, D)],
        out_specs=[rowd, rowd, rowd, pl.BlockSpec((TM, GW), lambda i: (i, 0)),
                   _residue_spec(TM, DILATIONS[1], GW), _residue_spec(TM, DILATIONS[2], GW), rowd,
                   pl.BlockSpec((TM, 2 * D), lambda i: (i, 0)), acc8, acc8],
        out_shape=[SDS((S, D), BF16), SDS((S, D), BF16), SDS((S, D), BF16), SDS((S, GW), BF16),
                   SDS(_residue_shape(S, DILATIONS[1], GW), BF16), SDS(_residue_shape(S, DILATIONS[2], GW), BF16),
                   SDS((S, D), BF16), SDS((S, 2 * D), BF16), SDS((LANE_ROWS, D), F32), SDS((LANE_ROWS, D), F32)],
        scratch_shapes=[_col_scratch(TM, GW)],
        compiler_params=_cp(("arbitrary",)),
    )(dx1, mm, ya, yc, zrest, zrest, g_pm, w_mxT, w_aoT, w_coT)


def _conv_bwd(da, cv, zrest, b_glu, wdw, g_ln, b_ln):
    S = da.shape[0]
    TM = CONV_TM
    HALO = 32
    hb = TM // HALO
    nh = S // HALO

    def body(da_ref, dan_ref, cv_ref, cvn_ref, u_ref, g_ref, uh_ref, gh_ref, bg_ref, w_ref, gl_ref, bl_ref,
             dglu_ref, dbu_ref, dbg_ref, dw_ref, dgl_ref, dbl_ref, dbd_ref, dext, uext, dsh, ush, du_scr, dw8):
        i = pl.program_id(0)
        last = i == pl.num_programs(0) - 1

        @pl.when(i == 0)
        def _():
            for ref in (dbu_ref, dbg_ref, dw8, dgl_ref, dbl_ref, dbd_ref):
                ref[...] = jnp.zeros_like(ref)

        def ln_bwd(da_v, cv_v):
            cf = cv_v.astype(F32)
            mu = jnp.mean(cf, axis=-1, keepdims=True)
            xc = cf - mu
            rstd = lax.rsqrt(jnp.mean(xc * xc, axis=-1, keepdims=True) + LN_EPS)
            xh = xc * rstd
            y = xh * gl_ref[...] + bl_ref[...]
            sy = _sigmoid(y)
            dy = da_v.astype(F32) * (sy * (1.0 + y * (1.0 - sy)))
            dxh = dy * gl_ref[...]
            dcv = rstd * (dxh - jnp.mean(dxh, axis=-1, keepdims=True)
                          - xh * jnp.mean(dxh * xh, axis=-1, keepdims=True))
            return dcv, dy, xh

        dcv, dy, xh = ln_bwd(da_ref[...], cv_ref[...])
        dgl_ref[...] += _colsum8(dy * xh)
        dbl_ref[...] += _colsum8(dy)
        dbd_ref[...] += _colsum8(dcv)
        dcvn, _, _ = ln_bwd(dan_ref[...], cvn_ref[...])
        dext[0:TM, :] = dcv
        dext[TM:, :] = jnp.where(last, 0.0, dcvn)

        bu = bg_ref[:, 0:D]
        bgt = bg_ref[:, D:2 * D]
        upre = u_ref[...].astype(F32) + bu
        sg = _sigmoid(g_ref[...].astype(F32) + bgt)
        uh = (uh_ref[...].astype(F32) + bu) * _sigmoid(gh_ref[...].astype(F32) + bgt)
        uext[0:HALO, :] = jnp.where(i == 0, 0.0, uh)
        uext[HALO:, :] = upre * sg

        _make_shifts(dext, dsh, TM)
        _make_shifts(uext, ush, TM)
        for r0, lanes in _tap_blocks(TM):
            acc = jnp.zeros((CONV_RC, LANES), F32)
            for j in range(CONV_W):
                acc = acc + _shifted(dext, dsh, CONV_W - 1 - j, r0, CONV_RC, lanes) * w_ref[j:j + 1, lanes]
            du_scr[r0:r0 + CONV_RC, lanes] = acc
        for l0 in range(0, D, LANES):
            lanes = slice(l0, l0 + LANES)
            accs = [jnp.zeros((LANE_ROWS, LANES), F32)] * CONV_W
            for r0 in range(0, TM, CONV_RC):
                dc = dext[r0:r0 + CONV_RC, lanes]
                for j in range(CONV_W):
                    prod = dc * _shifted(uext, ush, HALO - (CONV_W - 1) + j, r0, CONV_RC, lanes)
                    accs[j] = accs[j] + jnp.sum(prod.reshape(CONV_RC // LANE_ROWS, LANE_ROWS, LANES), axis=0)
            for j in range(CONV_W):
                dw8[j, :, lanes] += accs[j]

        @pl.when(last)
        def _():
            for j in range(CONV_W):
                dw_ref[j:j + 1, :] = jnp.sum(dw8[j], axis=0, keepdims=True)
            dw_ref[CONV_W:, :] = jnp.zeros((32 - CONV_W, D), F32)

        du = du_scr[...]
        dup = du * sg
        dgp = du * upre * (sg * (1.0 - sg))
        dglu_ref[:, 0:D] = dup.astype(BF16)
        dglu_ref[:, D:] = dgp.astype(BF16)
        dbu_ref[...] += _colsum8(dup.astype(BF16).astype(F32))
        dbg_ref[...] += _colsum8(dgp.astype(BF16).astype(F32))

    rowd = pl.BlockSpec((TM, D), lambda i: (i, 0))
    nxt = pl.BlockSpec((HALO, D), lambda i: (jnp.minimum((i + 1) * hb, nh - 1), 0))
    vec = pl.BlockSpec((1, D), lambda i: (0, 0))
    acc8 = pl.BlockSpec((LANE_ROWS, D), lambda i: (0, 0))
    return pl.pallas_call(
        body, name="conv_bwd", grid=(S // TM,),
        in_specs=[rowd, nxt, rowd, nxt,
                  pl.BlockSpec((TM, D), lambda i: (i, 0)), pl.BlockSpec((TM, D), lambda i: (i, 1)),
                  pl.BlockSpec((HALO, D), lambda i: (jnp.maximum(i * hb - 1, 0), 0)),
                  pl.BlockSpec((HALO, D), lambda i: (jnp.maximum(i * hb - 1, 0), 1)),
                  pl.BlockSpec((1, 2 * D), lambda i: (0, 0)), pl.BlockSpec((32, D), lambda i: (0, 0)), vec, vec],
        out_specs=[pl.BlockSpec((TM, 2 * D), lambda i: (i, 0)), acc8, acc8,
                   pl.BlockSpec((32, D), lambda i: (0, 0)), acc8, acc8, acc8],
        out_shape=[SDS((S, 2 * D), BF16), SDS((LANE_ROWS, D), F32), SDS((LANE_ROWS, D), F32), SDS((32, D), F32),
                   SDS((LANE_ROWS, D), F32), SDS((LANE_ROWS, D), F32), SDS((LANE_ROWS, D), F32)],
        scratch_shapes=[pltpu.VMEM((TM + HALO, D), F32), pltpu.VMEM((HALO + TM, D), F32),
                        pltpu.VMEM((7, TM + SHIFT_PAD, D), F32), pltpu.VMEM((7, TM + SHIFT_PAD, D), F32),
                        pltpu.VMEM((TM, D), F32), pltpu.VMEM((32, LANE_ROWS, D), F32)],
        compiler_params=_cp(("arbitrary",)),
    )(da, da, cv, cv, zrest, zrest, zrest, zrest, b_glu, wdw, g_ln, b_ln)


def _attn_bwd(zq, do, o, lse, bias_t, gi):
    dil, L, _ = zq.shape
    _, TQ, QB, ns = _attn_tile(L * dil, dil)
    NP = NH // 2

    def body(q3, kc3, kp3, vc3, vp3, do3, o3, l3, b_ref,
             out3, db_ref, kext, vext, dkx, dvx, dqn, dqc, dkc, dvc):
        q_ref, kc_ref, kp_ref, vc_ref, vp_ref, do_ref, o_ref, l_ref, out_ref = (
            r.at[0] for r in (q3, kc3, kp3, vc3, vp3, do3, o3, l3, out3))
        t = pl.program_id(0)
        n = lax.rem(cur(t), ns)

        @pl.when(t == 0)
        def _():
            db_ref[...] = jnp.zeros_like(db_ref)

        @pl.when(t < T - 1)
        def _():
            kext[0:QBLK, :] = kp_ref[...]
            kext[QBLK:, :] = kc_ref[...]
            vext[0:QBLK, :] = vp_ref[...]
            vext[QBLK:, :] = vc_ref[...]
            krow = lax.broadcasted_iota(jnp.int32, (KBLK, 2 * QBLK), 0)
            no_prev = jnp.logical_and(n == 0, krow < QBLK)
            lane = lax.broadcasted_iota(jnp.int32, (QBLK, LANES), 1)

            def overlap_add(parts):
                segs = [parts[0][0:QBLK]]
                for b in range(1, QB):
                    segs.append(parts[b - 1][QBLK:] + parts[b][0:QBLK])
                segs.append(parts[QB - 1][QBLK:])
                return jnp.concatenate(segs, axis=0)

            lanes_of = [slice(hp * LANES, (hp + 1) * LANES) for hp in range(NP)]
            dv_parts = [[] for _ in range(NP)]
            dk_parts = [[] for _ in range(NP)]
            dbsum = [None] * NP
            for b in range(QB):
                rows = slice(b * QBLK, (b + 1) * QBLK)
                win = slice(b * QBLK, b * QBLK + KBLK)
                q2 = [_pair_stack(q_ref, rows, pl_, SCALE) for pl_ in lanes_of]
                do2 = [_pair_stack(do_ref, rows, pl_) for pl_ in lanes_of]
                st = [_dot_nt(kext[win, pl_], q2[hp]) + b_ref[0, hp] for hp, pl_ in enumerate(lanes_of)]
                dpt = [_dot_nt(vext[win, pl_], do2[hp]) for hp, pl_ in enumerate(lanes_of)]
                lse_t = l_ref[rows, :].T
                prod_t = (do_ref[rows, :].astype(F32) * o_ref[rows, :].astype(F32)).T
                dst = []
                for hp in range(NP):
                    lo = hp * LANES
                    lse_row = jnp.concatenate([lse_t[lo:lo + 1], lse_t[lo + HD:lo + HD + 1]], axis=1)
                    delta_row = jnp.concatenate([jnp.sum(prod_t[lo:lo + HD], axis=0, keepdims=True),
                                                 jnp.sum(prod_t[lo + HD:lo + LANES], axis=0, keepdims=True)], axis=1)
                    s_hp = jnp.where(no_prev, NEG_INF, st[hp]) if b == 0 else st[hp]
                    pt = jnp.exp(s_hp - lse_row)
                    d = pt * (dpt[hp] - delta_row)
                    dbsum[hp] = d if dbsum[hp] is None else dbsum[hp] + d
                    dst.append(d.astype(BF16))
                    dv_parts[hp].append(_dot(pt.astype(BF16), do2[hp]))
                for hp, pl_ in enumerate(lanes_of):
                    dk_parts[hp].append(_dot(dst[hp], q2[hp]))
                    dq2 = _dot_tn(dst[hp], kext[win, pl_])
                    dqn[rows, pl_] = jnp.where(lane < HD, dq2[0:QBLK], dq2[QBLK:]) * SCALE
            for hp, pl_ in enumerate(lanes_of):
                db_ref[hp] += dbsum[hp]
                dvx[:, pl_] = overlap_add(dv_parts[hp])
                dkx[:, pl_] = overlap_add(dk_parts[hp])

        @pl.when(t > 0)
        def _():
            out_ref[:, 0:GW] = dqc[...].astype(BF16)
            out_ref[:, GW:2 * GW] = dkc[...].astype(BF16)
            out_ref[:, 2 * GW:] = dvc[...].astype(BF16)

        @pl.when(jnp.logical_and(t > 0, t < T - 1))
        def _():
            out_ref[TQ - QBLK:, GW:2 * GW] = (dkc[TQ - QBLK:, :] + dkx[0:QBLK, :]).astype(BF16)
            out_ref[TQ - QBLK:, 2 * GW:] = (dvc[TQ - QBLK:, :] + dvx[0:QBLK, :]).astype(BF16)

        @pl.when(t < T - 1)
        def _():
            dqc[...] = dqn[...]
            dkc[...] = dkx[QBLK:, :]
            dvc[...] = dvx[QBLK:, :]

    T = dil * ns + 1

    def cur(t):
        return jnp.minimum(t, T - 2)

    def blk(t, col):
        return (lax.div(cur(t), ns), lax.rem(cur(t), ns), col)

    def prev(t, col):
        return (lax.div(cur(t), ns), jnp.maximum(lax.rem(cur(t), ns) * QB - 1, 0), col)

    def late(t):
        tp = jnp.maximum(t - 1, 0)
        return (lax.div(tp, ns), lax.rem(tp, ns), 0)

    rows = lambda t: blk(t, 0)
    return pl.pallas_call(
        body, name=f"attn_bwd_g{gi}", grid=(T,),
        in_specs=[pl.BlockSpec((1, TQ, GW), lambda t: blk(t, 0)),
                  pl.BlockSpec((1, TQ, GW), lambda t: blk(t, 1)),
                  pl.BlockSpec((1, QBLK, GW), lambda t: prev(t, 1)),
                  pl.BlockSpec((1, TQ, GW), lambda t: blk(t, 2)),
                  pl.BlockSpec((1, QBLK, GW), lambda t: prev(t, 2)),
                  pl.BlockSpec((1, TQ, GW), rows), pl.BlockSpec((1, TQ, GW), rows), pl.BlockSpec((1, TQ, GW), rows),
                  pl.BlockSpec((1, NP, KBLK, 2 * QBLK), lambda t: (gi, 0, 0, 0))],
        out_specs=[pl.BlockSpec((1, TQ, 3 * GW), late),
                   pl.BlockSpec((NP, KBLK, 2 * QBLK), lambda t: (0, 0, 0))],
        out_shape=[SDS((dil, L, 3 * GW), BF16), SDS((NP, KBLK, 2 * QBLK), F32)],
        scratch_shapes=[pltpu.VMEM((QBLK + TQ, GW), BF16), pltpu.VMEM((QBLK + TQ, GW), BF16),
                        pltpu.VMEM((QBLK + TQ, GW), F32), pltpu.VMEM((QBLK + TQ, GW), F32),
                        pltpu.VMEM((TQ, GW), F32), pltpu.VMEM((TQ, GW), F32),
                        pltpu.VMEM((TQ, GW), F32), pltpu.VMEM((TQ, GW), F32)],
        compiler_params=_cp(("arbitrary",)),
    )(zq, zq, zq, zq, zq, do, o, lse, bias_t)


def _dz_block(k):
    if k < 9:
        return k % 3, k // 3
    if k < 13:
        return 3, k - 9
    return 4, k - 13


_DZ_SRC = np.array([_dz_block(k)[0] for k in range(17)], np.int32)


def _dz_hold(s):
    uses = [(k, _dz_block(k)[1]) for k in range(17) if _dz_block(k)[0] == s]
    hold = []
    for k in range(17):
        nxt = [b for kk, b in uses if kk >= k]
        hold.append(nxt[0] if nxt else uses[-1][1])
    return np.array(hold, np.int32)


def _table(tab, k):
    out = jnp.int32(int(tab[0]))
    for idx in range(1, len(tab)):
        out = jnp.where(k == idx, jnp.int32(int(tab[idx])), out)
    return out


def _w_in_tile(s, blk):
    return blk * 3 + s if s < 3 else (9 if s == 3 else 13) + blk


def _in_bwd(dqkv, dglu, dzg, w_inT, x, dx1, g, rider):
    S = x.shape[0]
    TM = 512

    def body(d0, d1, d2, d3, d4, w_ref, x_ref, dx1_ref, g_ref, gx_ref, dg_ref, scr):
        i = pl.program_id(0)

        @pl.when(i == 0)
        def _():
            dg_ref[...] = jnp.zeros_like(dg_ref)

        def rows(s, blk):
            k = _w_in_tile(s, blk)
            return w_ref[k * GW:(k + 1) * GW, :]

        dh = jnp.zeros((TM, D), F32)
        for blk in range(3):
            dh = dh + _dot(d0[0, :, blk * GW:(blk + 1) * GW], rows(0, blk))
        for s, ref in ((3, d3), (4, d4)):
            for blk in range(4):
                dh = dh + _dot(ref[:, blk * GW:(blk + 1) * GW], rows(s, blk))
        for s, ref in ((1, d1), (2, d2)):
            dil = DILATIONS[s]
            part = jnp.zeros((TM, D), F32)
            for blk in range(3):
                part = part + _dot(ref[:, :, blk * GW:(blk + 1) * GW].reshape(TM, GW), rows(s, blk))
            _merge_residues(scr, dil, lambda c, part=part, dil=dil: part[c * (TM // dil):(c + 1) * (TM // dil)])
            dh = dh + _load_cols(scr)
        xf = x_ref[...]
        r = lax.rsqrt(jnp.mean(xf * xf, axis=-1, keepdims=True) + RMS_EPS)
        nrm = xf * r
        dg_ref[...] += _colsum8(dh * nrm)
        dn = dh * g_ref[...]
        gx_ref[...] = dx1_ref[...] + r * (dn - nrm * jnp.mean(dn * nrm, axis=-1, keepdims=True))

    rowd = pl.BlockSpec((TM, D), lambda i: (i, 0))
    wide = pl.BlockSpec((TM, 2 * D), lambda i: (i, 0))
    body, r_in, r_out, r_shape, r_scr = _ride(body, 9, 2, 1, rider, S // TM)
    return pl.pallas_call(
        body, name="in_bwd", grid=(S // TM,),
        in_specs=[_residue_spec(TM, d, 3 * GW) for d in DILATIONS] + [wide, wide]
        + [pl.BlockSpec(w_inT.shape, lambda i: (0, 0), pipeline_mode=pl.Buffered(1)), rowd, rowd,
           pl.BlockSpec((1, D), lambda i: (0, 0))] + r_in,
        out_specs=[rowd, pl.BlockSpec((LANE_ROWS, D), lambda i: (0, 0))] + r_out,
        out_shape=[SDS((S, D), F32), SDS((LANE_ROWS, D), F32)] + r_shape,
        scratch_shapes=[_col_scratch(TM, D)] + r_scr,
        compiler_params=_cp(("arbitrary",)),
    )(*dqkv, dglu, dzg, w_inT, x, dx1, g, *rider.ins)


def _dw_in(dqkv, dglu, dzg, hs):
    S = hs[0].shape[0]
    TS = min(2048, S)
    nk = 17
    holds = [_dz_hold(s) for s in range(5)]
    h_of = (0, 1, 2, 0, 0)

    def body(d0, d1, d2, d3, d4, h0, h1, h2, o_ref, acc):
        m = pl.program_id(0)
        s_ = pl.program_id(1)

        @pl.when(s_ == 0)
        def _():
            acc[...] = jnp.zeros_like(acc)

        src = _table(_DZ_SRC, m)
        pairs = ((d0, h0), (d1, h1), (d2, h2), (d3, h0), (d4, h0))
        for s, (dref, href) in enumerate(pairs):
            @pl.when(src == s)
            def _(dref=dref, href=href):
                acc[...] += _dot_tn(dref[...].reshape(TS, GW), href[...].reshape(TS, D))

        @pl.when(s_ == pl.num_programs(1) - 1)
        def _():
            o_ref[...] = acc[...].astype(BF16)

    def row(s, m, s_):
        return jnp.where(_table(_DZ_SRC, m) == s, s_, 0)

    def dspec(s):
        if s < 3:
            dil = DILATIONS[s]
            return pl.BlockSpec((dil, TS // dil, GW), lambda m, s_: (0, row(s, m, s_), _table(holds[s], m)))
        return pl.BlockSpec((TS, GW), lambda m, s_: (row(s, m, s_), _table(holds[s], m)))

    def hrow(j, m, s_):
        used = _table(np.array([int(h_of[_dz_block(k)[0]] == j) for k in range(nk)], np.int32), m)
        return jnp.where(used == 1, s_, 0)

    hspecs = [pl.BlockSpec((TS, D), lambda m, s_: (hrow(0, m, s_), 0))] + [
        pl.BlockSpec((DILATIONS[j], TS // DILATIONS[j], D), lambda m, s_, j=j: (0, hrow(j, m, s_), 0)) for j in (1, 2)]
    return pl.pallas_call(
        body, name="dw_in", grid=(nk, S // TS),
        in_specs=[dspec(s) for s in range(5)] + hspecs,
        out_specs=pl.BlockSpec((GW, D), lambda m, s_: (m, 0)),
        out_shape=SDS((nk * GW, D), BF16),
        scratch_shapes=[pltpu.VMEM((GW, D), F32)],
        compiler_params=_cp(("arbitrary", "arbitrary")),
    )(*dqkv, dglu, dzg, *hs)


def _mm_tn(a, b, tm, a_maps, name):
    S, N = b.shape
    parts = len(a_maps)
    tp = tm // parts
    nm = len(a_maps[0])
    TS = min(4096 if tm <= 512 else 2048, S)
    tabs = [np.array(t, np.int32) for t in a_maps]

    def body(*refs):
        a_refs = refs[:parts]
        b_ref, o_ref, acc = refs[parts:]
        s_ = pl.program_id(1)

        @pl.when(s_ == 0)
        def _():
            acc[...] = jnp.zeros_like(acc)

        for p, ar in enumerate(a_refs):
            acc[p * tp:(p + 1) * tp, :] += _dot_tn(ar[...], b_ref[...])

        @pl.when(s_ == pl.num_programs(1) - 1)
        def _():
            o_ref[...] = acc[...].astype(BF16)

    return pl.pallas_call(
        body, name=name, grid=(nm, S // TS),
        in_specs=[pl.BlockSpec((TS, tp), lambda m, s_, t=t: (s_, _table(t, m))) for t in tabs]
        + [pl.BlockSpec((TS, N), lambda m, s_: (s_, 0))],
        out_specs=pl.BlockSpec((tm, N), lambda m, s_: (m, 0)),
        out_shape=SDS((nm * tm, N), BF16),
        scratch_shapes=[pltpu.VMEM((tm, N), F32)],
        compiler_params=_cp(("arbitrary", "arbitrary")),
    )(*([a] * parts), b)


def _row_tile(rows, cols, limit=1 << 20):
    if rows * cols * 4 <= limit:
        return rows
    best = None
    for t in range(8, rows, 8):
        if rows % t == 0 and t * cols * 4 <= limit:
            best = t
    return best


def _adamw(w, g, m, v, name):
    R, C = w.shape
    tr = _row_tile(R, C)

    def body(w_ref, g_ref, m_ref, v_ref, d_ref, nm_ref, nv_ref):
        gg = g_ref[...]
        nm = ADAM_B1 * m_ref[...] + (1.0 - ADAM_B1) * gg
        nv = ADAM_B2 * v_ref[...] + (1.0 - ADAM_B2) * (gg * gg)
        m_hat = nm / (1.0 - ADAM_B1 ** ADAM_STEP)
        v_hat = nv / (1.0 - ADAM_B2 ** ADAM_STEP)
        d_ref[...] = -ADAM_LR * (m_hat / (jnp.sqrt(v_hat) + ADAM_EPS) + ADAM_WD * w_ref[...])
        nm_ref[...] = nm
        nv_ref[...] = nv

    spec = pl.BlockSpec((tr, C), lambda i: (i, 0))
    return pl.pallas_call(
        body, name=name, grid=(R // tr,), in_specs=[spec] * 4, out_specs=[spec] * 3,
        out_shape=[SDS((R, C), F32)] * 3, compiler_params=_cp(("arbitrary",)),
    )(w, g, m, v)


_FLIPS = ((1, 0), (0, 1), (1, 1))


def _place():
    x, y, c = lax.axis_index("x"), lax.axis_index("y"), lax.axis_index("c")
    return x, y, c


def _peer_chips(x, y):
    return [((x + fx) % 2, (y + fy) % 2) for fx, fy in _FLIPS]


def _gather_weights(shards):
    nw = len(shards)
    views = [s.reshape(2, s.shape[0] // 2, s.shape[1]) for s in shards]

    def body(*refs):
        ins = refs[:nw]
        outs = refs[nw:2 * nw]
        ici_send, ici_recv, d2d_send, d2d_recv, loc = refs[2 * nw:]
        x, y, c = _place()
        j = 2 * x + y
        chips = _peer_chips(x, y)
        copies = []
        for w in range(nw):
            cp = pltpu.make_async_copy(ins[w], outs[w].at[j], loc.at[w])
            cp.start()
            copies.append(cp)
        sends = []
        for w in range(nw):
            for k, (px, py) in enumerate(chips):
                cp = pltpu.make_async_remote_copy(
                    src_ref=ins[w].at[c], dst_ref=outs[w].at[j, c], send_sem=ici_send.at[w, k],
                    recv_sem=ici_recv.at[w, k], device_id=(px, py, c), device_id_type=MESH)
                cp.start()
                sends.append(cp)
        for w in range(nw):
            for k, (px, py) in enumerate(chips):
                jk = 2 * px + py
                land = outs[w].at[jk, c]
                pltpu.make_async_remote_copy(
                    src_ref=ins[w].at[c], dst_ref=land, send_sem=ici_send.at[w, k],
                    recv_sem=ici_recv.at[w, k], device_id=(px, py, c), device_id_type=MESH).wait_recv()
                cp = pltpu.make_async_remote_copy(
                    src_ref=land, dst_ref=land, send_sem=d2d_send.at[w, k],
                    recv_sem=d2d_recv.at[w, k], device_id=(x, y, 1 - c), device_id_type=MESH)
                cp.start()
                sends.append(cp)
        for w in range(nw):
            for k, (px, py) in enumerate(chips):
                jk = 2 * px + py
                land = outs[w].at[jk, 1 - c]
                pltpu.make_async_remote_copy(
                    src_ref=land, dst_ref=land, send_sem=d2d_send.at[w, k],
                    recv_sem=d2d_recv.at[w, k], device_id=(x, y, 1 - c), device_id_type=MESH).wait_recv()
        for cp in sends:
            cp.wait_send()
        for cp in copies:
            cp.wait()

    outs = pl.pallas_call(
        body, name="gather_weights",
        in_specs=[ANY] * nw, out_specs=[ANY] * nw,
        out_shape=[SDS((4,) + v.shape, BF16) for v in views],
        scratch_shapes=[pltpu.SemaphoreType.DMA((nw, 3)), pltpu.SemaphoreType.DMA((nw, 3)),
                        pltpu.SemaphoreType.DMA((nw, 3)), pltpu.SemaphoreType.DMA((nw, 3)),
                        pltpu.SemaphoreType.DMA((nw,))],
    )(*views)
    return [o.reshape(4 * s.shape[0], s.shape[1]) for o, s in zip(outs, shards)]


def _gather_two_routes(shard):
    R, N = shard.shape
    view = shard.reshape(2, 2, R // 4, N)

    def body(in_ref, out_ref, isend, irecv, dsend, drecv, loc):
        x, y, c = _place()
        j = 2 * x + y
        xn, yn = (1 - x, y), (x, 1 - y)
        jx, jy, jd = 2 * (1 - x) + y, 2 * x + (1 - y), 2 * (1 - x) + (1 - y)
        where = {0: (jx, 0), 1: (jx, 1), 2: (jd, 1), 3: (jy, 1), 4: (jy, 0), 5: (jd, 0)}

        def ici(slot, src, dst, to):
            return pltpu.make_async_remote_copy(src_ref=src, dst_ref=dst, send_sem=isend.at[slot],
                                                recv_sem=irecv.at[slot], device_id=(to[0], to[1], c),
                                                device_id_type=MESH)

        def land(slot, half):
            chip_of, route = where[slot]
            return out_ref.at[chip_of, half, route]

        def arrived(slot):
            ici(slot, in_ref.at[c, 0], land(slot, c), xn).wait_recv()

        local = pltpu.make_async_copy(in_ref, out_ref.at[j], loc)
        local.start()
        sends = [ici(0, in_ref.at[c, 0], out_ref.at[j, c, 0], xn), ici(1, in_ref.at[c, 1], out_ref.at[j, c, 1], xn),
                 ici(3, in_ref.at[c, 1], out_ref.at[j, c, 1], yn), ici(4, in_ref.at[c, 0], out_ref.at[j, c, 0], yn)]
        for cp in sends:
            cp.start()
        arrived(0)
        sends.append(ici(5, land(0, c), land(0, c), yn))
        sends[-1].start()
        arrived(3)
        sends.append(ici(2, land(3, c), land(3, c), xn))
        sends[-1].start()
        for slot in (0, 3, 1, 4, 2, 5):
            if slot not in (0, 3):
                arrived(slot)
            cp = pltpu.make_async_remote_copy(src_ref=land(slot, c), dst_ref=land(slot, c), send_sem=dsend.at[slot],
                                              recv_sem=drecv.at[slot], device_id=(x, y, 1 - c), device_id_type=MESH)
            cp.start()
            sends.append(cp)
        for slot in range(6):
            pltpu.make_async_remote_copy(src_ref=land(slot, 1 - c), dst_ref=land(slot, 1 - c), send_sem=dsend.at[slot],
                                         recv_sem=drecv.at[slot], device_id=(x, y, 1 - c),
                                         device_id_type=MESH).wait_recv()
        for cp in sends:
            cp.wait_send()
        local.wait()

    sem = pltpu.SemaphoreType.DMA
    out = pl.pallas_call(
        body, name="gather_w_in", in_specs=[ANY], out_specs=ANY,
        out_shape=SDS((4,) + view.shape, BF16),
        scratch_shapes=[sem((6,)), sem((6,)), sem((6,)), sem((6,)), sem(())],
    )(view)
    return out.reshape(4 * R, N)


class _Rider:
    def __init__(self, ins, out_shape, scratch, start, finish, mid=None):
        self.ins, self.out_shape, self.scratch = list(ins), list(out_shape), list(scratch)
        self.start, self.finish, self.mid = start, finish, mid


def _ride(body, n_in, n_out, n_scr, rider, steps):
    if rider is None:
        return body, [], [], [], []
    ri, ro = len(rider.ins), len(rider.out_shape)

    def wrapped(*refs):
        ins, r_ins = refs[:n_in], refs[n_in:n_in + ri]
        o0 = n_in + ri
        outs, r_outs = refs[o0:o0 + n_out], refs[o0 + n_out:o0 + n_out + ro]
        s0 = o0 + n_out + ro
        scr, r_scr = refs[s0:s0 + n_scr], refs[s0 + n_scr:]
        i = pl.program_id(0)

        @pl.when(i == 0)
        def _():
            rider.start(r_ins, r_outs, r_scr)

        if rider.mid is not None:
            @pl.when(i == (3 * steps) // 4)
            def _():
                rider.mid(r_ins, r_outs, r_scr)

        body(*ins, *outs, *scr)

        @pl.when(i == steps - 1)
        def _():
            rider.finish(r_ins, r_outs, r_scr)

    return wrapped, [ANY] * ri, [ANY] * ro, rider.out_shape, rider.scratch


def _gather_rider(shards):
    nw = len(shards)
    views = [s.reshape(2, s.shape[0] // 2, s.shape[1]) for s in shards]

    def parts(ins, outs, sems):
        ici_send, ici_recv, d2d_send, d2d_recv, loc = sems
        x, y, c = _place()
        j = 2 * x + y
        local, ici, land_ici, fwd, land_fwd = [], [], [], [], []
        for w in range(nw):
            local.append(pltpu.make_async_copy(ins[w], outs[w].at[j], loc.at[w]))
            for k, (px, py) in enumerate(_peer_chips(x, y)):
                jk = 2 * px + py
                ici.append(pltpu.make_async_remote_copy(
                    src_ref=ins[w].at[c], dst_ref=outs[w].at[j, c], send_sem=ici_send.at[w, k],
                    recv_sem=ici_recv.at[w, k], device_id=(px, py, c), device_id_type=MESH))
                mine = outs[w].at[jk, c]
                land_ici.append(pltpu.make_async_remote_copy(
                    src_ref=ins[w].at[c], dst_ref=mine, send_sem=ici_send.at[w, k],
                    recv_sem=ici_recv.at[w, k], device_id=(px, py, c), device_id_type=MESH))
                fwd.append(pltpu.make_async_remote_copy(
                    src_ref=mine, dst_ref=mine, send_sem=d2d_send.at[w, k],
                    recv_sem=d2d_recv.at[w, k], device_id=(x, y, 1 - c), device_id_type=MESH))
                theirs = outs[w].at[jk, 1 - c]
                land_fwd.append(pltpu.make_async_remote_copy(
                    src_ref=theirs, dst_ref=theirs, send_sem=d2d_send.at[w, k],
                    recv_sem=d2d_recv.at[w, k], device_id=(x, y, 1 - c), device_id_type=MESH))
        return local, ici, land_ici, fwd, land_fwd

    def start(ins, outs, sems):
        local, ici, _, _, _ = parts(ins, outs, sems)
        for cp in local + ici:
            cp.start()

    def mid(ins, outs, sems):
        _, _, land_ici, fwd, _ = parts(ins, outs, sems)
        for landed, cp in zip(land_ici, fwd):
            landed.wait_recv()
            cp.start()

    def finish(ins, outs, sems):
        local, ici, _, fwd, land_fwd = parts(ins, outs, sems)
        for cp in land_fwd:
            cp.wait_recv()
        for cp in ici + fwd:
            cp.wait_send()
        for cp in local:
            cp.wait()

    sem = pltpu.SemaphoreType.DMA
    return _Rider(views, [SDS((4,) + v.shape, BF16) for v in views],
                  [sem((nw, 3)), sem((nw, 3)), sem((nw, 3)), sem((nw, 3)), sem((nw,))], start, finish, mid)


def _chip_exchange_rider(parts):
    nw = len(parts)

    def copies(ins, outs, sems):
        send, recv = sems
        x, y, c = _place()
        return [pltpu.make_async_remote_copy(
            src_ref=ins[w].at[2 * px + py], dst_ref=outs[w].at[k], send_sem=send.at[w, k],
            recv_sem=recv.at[w, k], device_id=(px, py, c), device_id_type=MESH)
            for w in range(nw) for k, (px, py) in enumerate(_peer_chips(x, y))]

    def start(ins, outs, sems):
        for cp in copies(ins, outs, sems):
            cp.start()

    def finish(ins, outs, sems):
        for cp in copies(ins, outs, sems):
            cp.wait()

    sem = pltpu.SemaphoreType.DMA
    return _Rider(parts, [SDS((3,) + p.shape[1:], BF16) for p in parts], [sem((nw, 3)), sem((nw, 3))], start, finish)


def _pair_exchange(grads, name):
    nw = len(grads)

    def body(*refs):
        ins = refs[:nw]
        outs = refs[nw:2 * nw]
        send, recv = refs[2 * nw:]
        x, y, c = _place()
        cps = []
        for w in range(nw):
            cp = pltpu.make_async_remote_copy(
                src_ref=ins[w].at[:, pl.ds(1 - c, 1)], dst_ref=outs[w], send_sem=send.at[w], recv_sem=recv.at[w],
                device_id=(x, y, 1 - c), device_id_type=MESH)
            cp.start()
            cps.append(cp)
        for cp in cps:
            cp.wait()

    return pl.pallas_call(
        body, name=name, in_specs=[ANY] * nw, out_specs=[ANY] * nw,
        out_shape=[SDS((4, 1) + g.shape[2:], BF16) for g in grads],
        scratch_shapes=[pltpu.SemaphoreType.DMA((nw,)), pltpu.SemaphoreType.DMA((nw,))],
    )(*grads)


def _half_tile(rh):
    best = 16
    for t in range(16, 545, 16):
        if rh % t == 0:
            best = t
    return best


def _pair_sum(g, got, name):
    _, _, rh, n = g.shape
    tr = _half_tile(rh)

    def body(a_ref, b_ref, o_ref):
        o_ref[...] = (a_ref[...].astype(F32) + b_ref[...].astype(F32)).astype(BF16)

    return pl.pallas_call(
        body, name=name, grid=(4, rh // tr),
        in_specs=[pl.BlockSpec((1, 1, tr, n), lambda s, i: (s, lax.axis_index("c"), i, 0)),
                  pl.BlockSpec((1, 1, tr, n), lambda s, i: (s, 0, i, 0))],
        out_specs=pl.BlockSpec((1, 1, tr, n), lambda s, i: (s, 0, i, 0)),
        out_shape=SDS((4, 1, rh, n), BF16),
        compiler_params=_cp(("arbitrary", "arbitrary")),
    )(g, got)


def _chip_sum(part, got, name):
    _, _, rh, n = part.shape
    tr = _half_tile(rh)

    def body(a_ref, b_ref, o_ref):
        acc = a_ref[0, 0].astype(F32)
        for k in range(3):
            acc = acc + b_ref[k, 0].astype(F32)
        o_ref[0] = acc

    return pl.pallas_call(
        body, name=name, grid=(rh // tr,),
        in_specs=[pl.BlockSpec((1, 1, tr, n), lambda i: (2 * lax.axis_index("x") + lax.axis_index("y"), 0, i, 0)),
                  pl.BlockSpec((3, 1, tr, n), lambda i: (0, 0, i, 0))],
        out_specs=pl.BlockSpec((1, tr, n), lambda i: (lax.axis_index("c"), i, 0)),
        out_shape=SDS((2, rh, n), F32),
        compiler_params=_cp(("arbitrary",)),
    )(part, got)


def _half_swap(halves):
    nw = len(halves)

    def body(*refs):
        ins = refs[:nw]
        outs = refs[nw:2 * nw]
        send, recv = refs[2 * nw:]
        x, y, c = _place()
        cps = []
        for w in range(nw):
            cp = pltpu.make_async_remote_copy(
                src_ref=ins[w].at[c], dst_ref=outs[w].at[c], send_sem=send.at[w], recv_sem=recv.at[w],
                device_id=(x, y, 1 - c), device_id_type=MESH)
            cp.start()
            cps.append(cp)
        for cp in cps:
            cp.wait()

    return pl.pallas_call(
        body, name="grad_half_swap", in_specs=[ANY] * nw, out_specs=[ANY] * nw,
        out_shape=[SDS(h.shape, F32) for h in halves],
        input_output_aliases={w: w for w in range(nw)},
        scratch_shapes=[pltpu.SemaphoreType.DMA((nw,)), pltpu.SemaphoreType.DMA((nw,))],
    )(*halves)


def _all_sum_small(part, name):
    R = part.shape[0]

    def body(p_ref, o_ref, land, send, recv):
        x, y, c = _place()
        me = 4 * x + 2 * y + c
        cps = []
        for d in range(1, 8):
            t = (me + d) % 8
            cp = pltpu.make_async_remote_copy(
                src_ref=p_ref, dst_ref=land.at[me], send_sem=send.at[d - 1], recv_sem=recv.at[d - 1],
                device_id=(t // 4, (t // 2) % 2, t % 2), device_id_type=MESH)
            cp.start()
            cps.append(cp)
        land[me] = p_ref[...]
        for cp in cps:
            cp.wait()
        acc = land[0]
        for d in range(1, 8):
            acc = acc + land[d]
        o_ref[...] = acc

    return pl.pallas_call(
        body, name=name,
        in_specs=[pl.BlockSpec(memory_space=pltpu.VMEM)], out_specs=pl.BlockSpec(memory_space=pltpu.VMEM),
        out_shape=SDS((R, D), F32),
        scratch_shapes=[pltpu.VMEM((8, R, D), F32), pltpu.SemaphoreType.DMA((7,)), pltpu.SemaphoreType.DMA((7,))],
        compiler_params=pltpu.CompilerParams(vmem_limit_bytes=VMEM_LIMIT),
    )(part)


def _pad_rows(a, rows):
    return jnp.pad(a, ((0, rows - a.shape[0]), (0, 0)))


def _vec_pack(vs):
    return jnp.concatenate([_pad_rows(v, LANE_ROWS) for v in vs], axis=0)


def kernel(x, rel_bias_table, g_pre_mix, w_in, b_glu, w_dw, b_dw, g_conv_ln, b_conv_ln, w_conv_out, b_conv_out, w_attn_out, w_mix_out, g_post_mix, g_pre_ffn, w_ffn_in, w_ffn_out, g_post_ffn, loss_target, m_rel_bias_table, m_g_pre_mix, m_w_in, m_b_glu, m_w_dw, m_b_dw, m_g_conv_ln, m_b_conv_ln, m_w_conv_out, m_b_conv_out, m_w_attn_out, m_w_mix_out, m_g_post_mix, m_g_pre_ffn, m_w_ffn_in, m_w_ffn_out, m_g_post_ffn, v_rel_bias_table, v_g_pre_mix, v_w_in, v_b_glu, v_w_dw, v_b_dw, v_g_conv_ln, v_b_conv_ln, v_w_conv_out, v_b_conv_out, v_w_attn_out, v_w_mix_out, v_g_post_mix, v_g_pre_ffn, v_w_ffn_in, v_w_ffn_out, v_g_post_ffn):
    S = x.shape[1]
    xs = x.reshape(S, D)
    tgt = loss_target.reshape(S, D)
    cx, cy, cc = _place()
    chip = 2 * cx + cy

    shards = [w_in[0].T.astype(BF16),
              w_ffn_in[0].T.astype(BF16),
              w_attn_out[0].T.astype(BF16),
              w_conv_out[0].astype(BF16),
              w_mix_out[0].astype(BF16),
              w_ffn_out[0].astype(BF16)]
    w_inT = _gather_two_routes(shards[0])
    w_inN = w_inT.T

    buckets_np, valid_np = _bucket_tables()
    buckets = jnp.asarray(buckets_np)
    bias = _bias_expand(rel_bias_table, buckets, jnp.asarray(valid_np)).reshape(3, NH, QBLK, KBLK)
    bias2 = bias.reshape(3, NH // 2, 2 * QBLK, KBLK)
    bias_t = bias.reshape(3, NH // 2, 2, QBLK, KBLK).transpose(0, 1, 4, 2, 3).reshape(3, NH // 2, KBLK, 2 * QBLK)
    wdw32 = _pad_rows(w_dw[0], 32)
    wdw_full = _gather_small_cols(wdw32, chip)

    zrest, h, h_r4, h_r16, *gathered = _in_proj_rest(xs, g_pre_mix, w_inN[:, 3 * ATTN_COLS:], _gather_rider(shards[1:]))
    w_fiT, w_aoT, w_co, w_mx, w_fo = (t.reshape(4 * s.shape[0], s.shape[1]) for t, s in zip(gathered, shards[1:]))
    w_fiN, w_aoN = w_fiT.T, w_aoT.T
    w_coT, w_mxT, w_foT = w_co.T, w_mx.T, w_fo.T
    zq = _in_proj_qkv(h, w_inN[:, :3 * ATTN_COLS])
    og, lg = [], []
    for gi in range(3):
        o_g, l_g = _attn_fwd(zq[gi], bias2, gi)
        og.append(o_g)
        lg.append(l_g)
    cv, a = _conv_fwd(zrest, b_glu, wdw_full, b_dw, g_conv_ln, b_conv_ln)
    o, o_r4, o_r16, lse, lse_r4, lse_r16, ya, yc, mg, mm, x1 = _mix_fwd(
        og, lg, a, zrest, xs, w_aoN, w_co, b_conv_out, w_mx, g_post_mix)
    h2, gu, df, dx2, loss8, dg_post_ffn = _ffn_fwd(x1, tgt, g_pre_ffn, g_post_ffn, w_fiN, w_fo)

    ident = lambda n: [list(range(n))]

    def pair_sums(partials, names, tag):
        views = [g.reshape(4, 2, g.shape[0] // 8, g.shape[1]) for g in partials]
        got = _pair_exchange(views, f"grad_pair_exchange_{tag}")
        return [_pair_sum(v, r, f"pair_sum_{n}") for v, r, n in zip(views, got, names)]

    def chip_sums(pair, got, names):
        return [_chip_sum(p, r, f"chip_sum_{n}") for p, r, n in zip(pair, got, names)]

    dff, act = _ffn_bwd_act(df, gu, w_foT)
    g_fiT = _mm_tn(dff, h2, 512, [[2 * t if t < NFT else 2 * (t - NFT) + 1 for t in range(0, 22, 2)],
                                  [2 * t if t < NFT else 2 * (t - NFT) + 1 for t in range(1, 22, 2)]], "dw_ffn_in")
    g_fo = _mm_tn(act, df, FFN_H // 2, ident(2), "dw_ffn_out")
    names_a = ("w_ffn_in", "w_ffn_out")
    pair_a = pair_sums([g_fiT, g_fo], names_a, "ffn")
    dx1, dg_pre_ffn, *got_a = _ffn_bwd_in(dff, x1, dx2, g_pre_ffn, w_fiT, _chip_exchange_rider(pair_a))
    halves_a = chip_sums(pair_a, got_a, names_a)
    dmm, dya, dyc, do, do_r4, do_r16, da, dzg, dg_post_mix, db_conv_out = _mix_bwd(
        dx1, mm, ya, yc, zrest, g_post_mix, w_mxT, w_aoT, w_coT)
    dglu, db_glu_u, db_glu_g, dw_dw, dg_conv_ln, db_conv_ln, db_dw = _conv_bwd(da, cv, zrest, b_glu, wdw_full, g_conv_ln, b_conv_ln)
    first = lambda t: t.reshape(1, S, GW)
    dqkv, dbias = [], []
    for gi, (do_g, o_g, lse_g) in enumerate(((first(do), first(o), first(lse)), (do_r4, o_r4, lse_r4),
                                            (do_r16, o_r16, lse_r16))):
        d_g, db_g = _attn_bwd(zq[gi], do_g, o_g, lse_g, bias_t, gi)
        dqkv.append(d_g)
        dbias.append(db_g.reshape(NH // 2, KBLK, 2, QBLK).transpose(0, 2, 3, 1).reshape(NH, QBLK, KBLK))
    dtab = _bias_reduce(jnp.concatenate(dbias, axis=0), buckets)

    g_inT = _dw_in(dqkv, dglu, dzg, (h, h_r4, h_r16))
    g_aoT = _mm_tn(dya, o, 512, ident(2), "dw_attn_out")
    g_co = _mm_tn(a, dyc, 512, ident(2), "dw_conv_out")
    g_mx = _mm_tn(mg, dmm, 512, ident(2), "dw_mix_out")
    names_b = ("w_in", "w_attn_out", "w_conv_out", "w_mix_out")
    pair_b = pair_sums([g_inT, g_aoT, g_co, g_mx], names_b, "rest")
    grad_x, dg_pre_mix, *got_b = _in_bwd(dqkv, dglu, dzg, w_inT, xs, dx1, g_pre_mix, _chip_exchange_rider(pair_b))
    halves_b = chip_sums(pair_b, got_b, names_b)

    red = [t.reshape(t.shape[0] * t.shape[1], t.shape[2]) for t in _half_swap(halves_a + halves_b)]
    gw_ffn_in, gw_ffn_out, gw_in, gw_attn_out, gw_conv_out, gw_mix_out = (
        red[0].T, red[1], red[2].T, red[3].T, red[4], red[5])

    small = jnp.concatenate([loss8, dg_pre_mix, db_glu_u, db_glu_g, db_dw, dg_conv_ln, db_conv_ln, db_conv_out,
                             dg_post_mix, dg_pre_ffn, dg_post_ffn, dtab, dw_dw], axis=0)
    tot = _all_sum_small(small, "small_all_sum")
    row = lambda i: tot[LANE_ROWS * i:LANE_ROWS * i + 1]
    loss = tot[0, 0]
    g_g_pre_mix, g_b_glu = row(1), jnp.concatenate([row(2), row(3)], axis=1)
    g_b_dw, g_g_conv_ln, g_b_conv_ln, g_b_conv_out = row(4), row(5), row(6), row(7)
    g_g_post_mix, g_g_pre_ffn, g_g_post_ffn = row(8), row(9), row(10)
    g_tab = tot[88:112, 0:32].T
    g_w_dw = lax.dynamic_slice(tot[112:112 + CONV_W], (0, 256 * chip), (CONV_W, 256))

    vec_names = ["g_pre_mix", "b_dw", "g_conv_ln", "b_conv_ln", "b_conv_out", "g_post_mix", "g_pre_ffn", "g_post_ffn"]
    vec_w = [g_pre_mix, b_dw, g_conv_ln, b_conv_ln, b_conv_out, g_post_mix, g_pre_ffn, g_post_ffn]
    vec_m = [m_g_pre_mix, m_b_dw, m_g_conv_ln, m_b_conv_ln, m_b_conv_out, m_g_post_mix, m_g_pre_ffn, m_g_post_ffn]
    vec_v = [v_g_pre_mix, v_b_dw, v_g_conv_ln, v_b_conv_ln, v_b_conv_out, v_g_post_mix, v_g_pre_ffn, v_g_post_ffn]
    vec_g = [g_g_pre_mix, g_b_dw, g_g_conv_ln, g_b_conv_ln, g_b_conv_out, g_g_post_mix, g_g_pre_ffn, g_g_post_ffn]

    def pack(vs, glu, tab, dw):
        return jnp.concatenate([_vec_pack(vs), _pad_rows(glu.reshape(2, D), LANE_ROWS),
                                _pad_rows(jnp.pad(tab.T, ((0, 0), (0, D - 32))), 24),
                                _pad_rows(jnp.pad(dw, ((0, 0), (0, D - 256))), 32)], axis=0)

    sw = pack(vec_w, b_glu, rel_bias_table, w_dw[0])
    sg = pack(vec_g, g_b_glu, g_tab, g_w_dw)
    sm = pack(vec_m, m_b_glu, m_rel_bias_table, m_w_dw[0])
    sv = pack(vec_v, v_b_glu, v_rel_bias_table, v_w_dw[0])
    s_out = _adamw(sw, sg, sm, sv, "adamw_small")

    def unpack(t):
        vecs = {n: t[LANE_ROWS * i:LANE_ROWS * i + 1] for i, n in enumerate(vec_names)}
        vecs["b_glu"] = t[64:66].reshape(1, 2 * D)
        vecs["rel_bias_table"] = t[72:96, 0:32].T
        vecs["w_dw"] = t[96:96 + CONV_W, 0:256][None]
        return vecs

    small_out = [unpack(t) for t in s_out]
    big = {}
    for n, w, g, m, v in (("w_in", w_in, gw_in, m_w_in, v_w_in),
                          ("w_conv_out", w_conv_out, gw_conv_out, m_w_conv_out, v_w_conv_out),
                          ("w_attn_out", w_attn_out, gw_attn_out, m_w_attn_out, v_w_attn_out),
                          ("w_mix_out", w_mix_out, gw_mix_out, m_w_mix_out, v_w_mix_out),
                          ("w_ffn_in", w_ffn_in, gw_ffn_in, m_w_ffn_in, v_w_ffn_in),
                          ("w_ffn_out", w_ffn_out, gw_ffn_out, m_w_ffn_out, v_w_ffn_out)):
        big[n] = [t[None] for t in _adamw(w[0], g, m[0], v[0], f"adamw_{n}")]

    order = ["rel_bias_table", "g_pre_mix", "w_in", "b_glu", "w_dw", "b_dw", "g_conv_ln", "b_conv_ln", "w_conv_out",
             "b_conv_out", "w_attn_out", "w_mix_out", "g_post_mix", "g_pre_ffn", "w_ffn_in", "w_ffn_out", "g_post_ffn"]
    grads = {"rel_bias_table": g_tab, "g_pre_mix": g_g_pre_mix, "w_in": gw_in[None], "b_glu": g_b_glu,
             "w_dw": g_w_dw[None], "b_dw": g_b_dw, "g_conv_ln": g_g_conv_ln, "b_conv_ln": g_b_conv_ln,
             "w_conv_out": gw_conv_out[None], "b_conv_out": g_b_conv_out, "w_attn_out": gw_attn_out[None],
             "w_mix_out": gw_mix_out[None], "g_post_mix": g_g_post_mix, "g_pre_ffn": g_g_pre_ffn,
             "w_ffn_in": gw_ffn_in[None], "w_ffn_out": gw_ffn_out[None], "g_post_ffn": g_g_post_ffn}
    outs = [loss, grad_x.reshape(1, S, D)] + [grads[n] for n in order]
    for slot in range(3):
        outs += [big[n][slot] if n in big else small_out[slot][n] for n in order]
    return tuple(outs)


def _gather_small_cols(wdw32, chip):
    placed = lax.dynamic_update_slice(jnp.zeros((32, D), F32), wdw32, (0, 256 * chip))
    return _all_sum_small(placed, "conv_taps_gather") * 0.5
```

```python
import functools
import math

import numpy as np
import jax
import jax.numpy as jnp
from jax import lax
from jax.experimental import pallas as pl
from jax.experimental.pallas import tpu as pltpu

F32 = jnp.float32
BF16 = jnp.bfloat16
SDS = jax.ShapeDtypeStruct
MESH = pl.DeviceIdType.MESH
ANY = pl.BlockSpec(memory_space=pl.ANY)

D = 1024
HD = 64
NH = 8
GW = NH * HD
ATTN_COLS = 3 * GW
DILATIONS = (1, 4, 16)
SPAN = 128
QBLK = 128
KBLK = 2 * QBLK
CONV_W = 31
FFN_H = 2816
FFN_T = 256
NFT = FFN_H // FFN_T
RMS_EPS = 1e-6
LN_EPS = 1e-5
NEG_INF = -1e30
SCALE = HD ** -0.5
LANE_ROWS = 8
LANES = 128

ADAM_LR, ADAM_B1, ADAM_B2, ADAM_EPS, ADAM_WD, ADAM_STEP = 0.001, 0.9, 0.999, 1e-08, 0.01, 10

VMEM_LIMIT = 56 * 1024 * 1024


def _cp(sem):
    return pltpu.CompilerParams(dimension_semantics=sem, vmem_limit_bytes=VMEM_LIMIT)


def _dot(a, b):
    return jnp.dot(a, b, preferred_element_type=F32)


def _dot_nt(a, b):
    return lax.dot_general(a, b, (((1,), (1,)), ((), ())), preferred_element_type=F32)


def _dot_tn(a, b):
    return lax.dot_general(a, b, (((0,), (0,)), ((), ())), preferred_element_type=F32)


def _sigmoid(v):
    return 0.5 * jnp.tanh(0.5 * v) + 0.5


def _colsum8(v):
    s = jnp.sum(v, axis=0, keepdims=True)
    row = lax.broadcasted_iota(jnp.int32, (LANE_ROWS, v.shape[1]), 0)
    return jnp.where(row == 0, jnp.broadcast_to(s, (LANE_ROWS, v.shape[1])), 0.0)


def _col_scratch(n, width):
    return pltpu.VMEM((width // LANES, n, LANES), F32)


def _store_cols(scr, v):
    for lb in range(scr.shape[0]):
        scr[lb] = v[:, lb * LANES:(lb + 1) * LANES]


def _load_cols(scr):
    return jnp.concatenate([scr[lb] for lb in range(scr.shape[0])], axis=1)


def _split_residues(scr, dil, put):
    nb, n, _ = scr.shape
    for c in range(dil):
        put(c, jnp.concatenate([scr[lb, pl.ds(c, n // dil, stride=dil), :] for lb in range(nb)], axis=1))


def _merge_residues(scr, dil, get):
    nb, n, _ = scr.shape
    for c in range(dil):
        v = get(c)
        for lb in range(nb):
            scr[lb, pl.ds(c, n // dil, stride=dil), :] = v[:, lb * LANES:(lb + 1) * LANES]


def _residue_shape(S, dil, width):
    return (dil, S // dil, width)


def _residue_spec(TM, dil, width):
    return pl.BlockSpec((dil, TM // dil, width), lambda i: (0, i, 0))


def _in_proj_rest(x, g, w, rider):
    S = x.shape[0]
    N = w.shape[1]
    TM, TN = 512, 512

    def body(x_ref, g_ref, w_ref, zr_ref, h0_ref, h1_ref, h2_ref, hf_scr):
        xf = x_ref[...]
        r = lax.rsqrt(jnp.mean(xf * xf, axis=-1, keepdims=True) + RMS_EPS)
        hf = xf * r * g_ref[...]
        h0_ref[...] = hf.astype(BF16)
        _store_cols(hf_scr, hf)
        for dil, ref in ((DILATIONS[1], h1_ref), (DILATIONS[2], h2_ref)):
            def put(c, v, ref=ref):
                ref[c] = v.astype(BF16)
            _split_residues(hf_scr, dil, put)
        for j in range(N // TN):
            zr_ref[:, j * TN:(j + 1) * TN] = _dot(h0_ref[...], w_ref[:, j * TN:(j + 1) * TN]).astype(BF16)

    body, r_in, r_out, r_shape, r_scr = _ride(body, 3, 4, 1, rider, S // TM)
    return pl.pallas_call(
        body, name="in_proj_rest", grid=(S // TM,),
        in_specs=[pl.BlockSpec((TM, D), lambda i: (i, 0)),
                  pl.BlockSpec((1, D), lambda i: (0, 0)),
                  pl.BlockSpec((D, N), lambda i: (0, 0), pipeline_mode=pl.Buffered(1))] + r_in,
        out_specs=[pl.BlockSpec((TM, N), lambda i: (i, 0)), pl.BlockSpec((TM, D), lambda i: (i, 0)),
                   _residue_spec(TM, DILATIONS[1], D), _residue_spec(TM, DILATIONS[2], D)] + r_out,
        out_shape=[SDS((S, N), BF16), SDS((S, D), BF16),
                   SDS(_residue_shape(S, DILATIONS[1], D), BF16),
                   SDS(_residue_shape(S, DILATIONS[2], D), BF16)] + r_shape,
        scratch_shapes=[_col_scratch(TM, D)] + r_scr,
        compiler_params=_cp(("arbitrary",)),
    )(x, g, w, *rider.ins)


def _in_proj_qkv(h, w):
    S = h.shape[0]
    TM = 512

    def body(h_ref, w_ref, z0_ref, z1_ref, z2_ref, scr):
        outs = (z0_ref, z1_ref, z2_ref)
        for j in range(9):
            t, gi = j // 3, j % 3
            cols = slice(t * GW, (t + 1) * GW)
            zt = _dot(h_ref[...], w_ref[:, j * GW:(j + 1) * GW])
            if gi == 0:
                z0_ref[0, :, cols] = zt.astype(BF16)
            else:
                slot = scr.at[2 * t + gi - 1]
                _store_cols(slot, zt)

                def put(c, v, ref=outs[gi], cols=cols):
                    ref[c, :, cols] = v.astype(BF16)
                _split_residues(slot, DILATIONS[gi], put)

    return pl.pallas_call(
        body, name="in_proj_qkv", grid=(S // TM,),
        in_specs=[pl.BlockSpec((TM, D), lambda i: (i, 0)),
                  pl.BlockSpec(w.shape, lambda i: (0, 0), pipeline_mode=pl.Buffered(1))],
        out_specs=[_residue_spec(TM, d, 3 * GW) for d in DILATIONS],
        out_shape=[SDS(_residue_shape(S, d, 3 * GW), BF16) for d in DILATIONS],
        scratch_shapes=[pltpu.VMEM((6, GW // LANES, TM, LANES), F32)],
        compiler_params=_cp(("arbitrary",)),
    )(h, w)


def _bucket_tables():
    a = np.arange(QBLK, dtype=np.int32)[:, None]
    c = np.arange(KBLK, dtype=np.int32)[None, :]
    off = a - c + QBLK
    valid = ((off >= 0) & (off <= SPAN)).astype(np.float32)
    tabs = []
    for dil in DILATIONS:
        dist = np.maximum(off * dil, 0)
        df = np.maximum(dist, 1).astype(np.float32)
        large = 16 + (np.log(df / np.float32(16)) / np.float32(math.log(2048 / 16)) * np.float32(16)).astype(np.int32)
        large = np.minimum(large, 31)
        tabs.append(np.where(dist < 16, dist, large).astype(np.int32))
    return np.stack(tabs), valid


def _bias_expand(tab, buckets, valid):
    def body(tab_ref, b_ref, v_ref, o_ref):
        for gi in range(3):
            bk = b_ref[gi]
            for h in range(NH):
                acc = jnp.zeros((QBLK, KBLK), F32)
                for b in range(32):
                    acc = jnp.where(bk == b, tab_ref[b, gi * NH + h], acc)
                o_ref[gi * NH + h] = jnp.where(v_ref[...] > 0.5, acc, NEG_INF)

    return pl.pallas_call(
        body, name="bias_expand",
        in_specs=[pl.BlockSpec(memory_space=pltpu.SMEM),
                  pl.BlockSpec(memory_space=pltpu.VMEM), pl.BlockSpec(memory_space=pltpu.VMEM)],
        out_specs=pl.BlockSpec(memory_space=pltpu.VMEM),
        out_shape=SDS((3 * NH, QBLK, KBLK), F32),
    )(tab, buckets, valid)


def _bias_reduce(dbias, buckets):
    def body(d_ref, b_ref, o_ref):
        lane = lax.broadcasted_iota(jnp.int32, (1, D), 1)
        for gi in range(3):
            bk = b_ref[gi]
            for h in range(NH):
                dv = d_ref[gi * NH + h]
                row = jnp.zeros((1, D), F32)
                for b in range(32):
                    m = jnp.where(bk == b, dv, 0.0)
                    val = jnp.sum(jnp.sum(m, axis=0, keepdims=True), axis=1, keepdims=True)
                    row = jnp.where(lane == b, val, row)
                o_ref[gi * NH + h:gi * NH + h + 1, :] = row

    return pl.pallas_call(
        body, name="bias_reduce",
        in_specs=[pl.BlockSpec(memory_space=pltpu.VMEM), pl.BlockSpec(memory_space=pltpu.VMEM)],
        out_specs=pl.BlockSpec(memory_space=pltpu.VMEM),
        out_shape=SDS((3 * NH, D), F32),
    )(dbias, buckets)


def _attn_tile(S, dil):
    L = S // dil
    tq = min(512, L)
    return L, tq, tq // QBLK, L // tq


def _pair_stack(ref, rows, lanes, scale=None):
    blk = ref[rows, lanes]
    if scale is not None:
        blk = blk * scale
    lane = lax.broadcasted_iota(jnp.int32, blk.shape, 1)
    zero = jnp.zeros_like(blk)
    return jnp.concatenate([jnp.where(lane < HD, blk, zero), jnp.where(lane >= HD, blk, zero)], axis=0)


def _attn_fwd(zq, bias2, gi):
    dil, L, _ = zq.shape
    _, TQ, QB, ns = _attn_tile(L * dil, dil)
    NP = NH // 2

    def body(q_ref, kc_ref, kp_ref, vc_ref, vp_ref, b_ref, o_ref, l_ref, kext, vext):
        n = pl.program_id(1)
        kext[0:QBLK, :] = kp_ref[0]
        kext[QBLK:, :] = kc_ref[0]
        vext[0:QBLK, :] = vp_ref[0]
        vext[QBLK:, :] = vc_ref[0]
        col = lax.broadcasted_iota(jnp.int32, (2 * QBLK, KBLK), 1)
        no_prev = jnp.logical_and(n == 0, col < QBLK)
        lane = lax.broadcasted_iota(jnp.int32, (QBLK, LANES), 1)
        lanes_of = [slice(hp * LANES, (hp + 1) * LANES) for hp in range(NP)]
        for b in range(QB):
            rows = slice(b * QBLK, (b + 1) * QBLK)
            win = slice(b * QBLK, b * QBLK + KBLK)
            s = [_dot_nt(_pair_stack(q_ref.at[0], rows, pl_, SCALE), kext[win, pl_]) + b_ref[0, hp]
                 for hp, pl_ in enumerate(lanes_of)]
            if b == 0:
                s = [jnp.where(no_prev, NEG_INF, v) for v in s]
            m = [jnp.max(v, axis=-1, keepdims=True) for v in s]
            p = [jnp.exp(v - mv) for v, mv in zip(s, m)]
            l = [jnp.sum(v, axis=-1, keepdims=True) for v in p]
            o2 = [_dot(v.astype(BF16), vext[win, pl_]) / lv for v, lv, pl_ in zip(p, l, lanes_of)]
            for hp, pl_ in enumerate(lanes_of):
                lse2 = jnp.broadcast_to(m[hp] + jnp.log(l[hp]), (2 * QBLK, LANES))
                o_ref[0, rows, pl_] = jnp.where(lane < HD, o2[hp][0:QBLK], o2[hp][QBLK:]).astype(BF16)
                l_ref[0, rows, pl_] = jnp.where(lane < HD, lse2[0:QBLK], lse2[QBLK:])

    def prev(n):
        return jnp.maximum(n * QB - 1, 0)

    return pl.pallas_call(
        body, name=f"attn_fwd_g{gi}", grid=(dil, ns),
        in_specs=[pl.BlockSpec((1, TQ, GW), lambda c, n: (c, n, 0)),
                  pl.BlockSpec((1, TQ, GW), lambda c, n: (c, n, 1)),
                  pl.BlockSpec((1, QBLK, GW), lambda c, n: (c, prev(n), 1)),
                  pl.BlockSpec((1, TQ, GW), lambda c, n: (c, n, 2)),
                  pl.BlockSpec((1, QBLK, GW), lambda c, n: (c, prev(n), 2)),
                  pl.BlockSpec((1, NP, 2 * QBLK, KBLK), lambda c, n: (gi, 0, 0, 0))],
        out_specs=[pl.BlockSpec((1, TQ, GW), lambda c, n: (c, n, 0)),
                   pl.BlockSpec((1, TQ, GW), lambda c, n: (c, n, 0))],
        out_shape=[SDS((dil, L, GW), BF16), SDS((dil, L, GW), F32)],
        scratch_shapes=[pltpu.VMEM((QBLK + TQ, GW), BF16), pltpu.VMEM((QBLK + TQ, GW), BF16)],
        compiler_params=_cp(("arbitrary", "arbitrary")),
    )(zq, zq, zq, zq, zq, bias2)


CONV_TM = 256
SHIFT_PAD = 24


def _make_shifts(src, sh, n):
    for b in range(1, 8):
        sh[b - 1] = src[b:b + n + SHIFT_PAD, :]


def _shifted(src, sh, off, r0, n, lanes):
    a, b = divmod(off, 8)
    if b == 0:
        return src[8 * a + r0:8 * a + r0 + n, lanes]
    return sh[b - 1, 8 * a + r0:8 * a + r0 + n, lanes]


CONV_RC = 32


def _tap_blocks(TM):
    return [(r0, slice(l0, l0 + LANES)) for l0 in range(0, D, LANES) for r0 in range(0, TM, CONV_RC)]


def _conv_fwd(zrest, b_glu, wdw, b_dw, g_ln, b_ln):
    S = zrest.shape[0]
    TM = CONV_TM
    HALO = 32
    hb = TM // HALO

    def body(u_ref, g_ref, uh_ref, gh_ref, bg_ref, w_ref, bd_ref, gl_ref, bl_ref, cv_ref, a_ref, ext, sh):
        i = pl.program_id(0)
        bu = bg_ref[:, 0:D]
        bgt = bg_ref[:, D:2 * D]
        uh = (uh_ref[...].astype(F32) + bu) * _sigmoid(gh_ref[...].astype(F32) + bgt)
        ext[0:HALO, :] = jnp.where(i == 0, 0.0, uh)
        ext[HALO:, :] = (u_ref[...].astype(F32) + bu) * _sigmoid(g_ref[...].astype(F32) + bgt)
        _make_shifts(ext, sh, TM)
        acc = jnp.zeros((TM, D), F32)
        for j in range(CONV_W):
            acc = acc + _shifted(ext, sh, HALO - (CONV_W - 1) + j, 0, TM, slice(None)) * w_ref[j:j + 1, :]
        cv = (acc + bd_ref[...]).astype(BF16)
        cv_ref[...] = cv
        cf = cv.astype(F32)
        mu = jnp.mean(cf, axis=-1, keepdims=True)
        xc = cf - mu
        y = xc * lax.rsqrt(jnp.mean(xc * xc, axis=-1, keepdims=True) + LN_EPS) * gl_ref[...] + bl_ref[...]
        a_ref[...] = (y * _sigmoid(y)).astype(BF16)

    vec = pl.BlockSpec((1, D), lambda i: (0, 0))
    return pl.pallas_call(
        body, name="conv_fwd", grid=(S // TM,),
        in_specs=[pl.BlockSpec((TM, D), lambda i: (i, 0)), pl.BlockSpec((TM, D), lambda i: (i, 1)),
                  pl.BlockSpec((HALO, D), lambda i: (jnp.maximum(i * hb - 1, 0), 0)),
                  pl.BlockSpec((HALO, D), lambda i: (jnp.maximum(i * hb - 1, 0), 1)),
                  pl.BlockSpec((1, 2 * D), lambda i: (0, 0)),
                  pl.BlockSpec((32, D), lambda i: (0, 0)), vec, vec, vec],
        out_specs=[pl.BlockSpec((TM, D), lambda i: (i, 0)), pl.BlockSpec((TM, D), lambda i: (i, 0))],
        out_shape=[SDS((S, D), BF16), SDS((S, D), BF16)],
        scratch_shapes=[pltpu.VMEM((HALO + TM, D), F32), pltpu.VMEM((7, TM + SHIFT_PAD, D), F32)],
        compiler_params=_cp(("arbitrary",)),
    )(zrest, zrest, zrest, zrest, b_glu, wdw, b_dw, g_ln, b_ln)


def _mix_fwd(og, lg, a, zrest, x, w_ao, w_co, b_co, w_mx, g_pm):
    S = x.shape[0]
    TM = 512

    def body(o0, o1, o2, l0, l1, l2, a_ref, ga_ref, gc_ref, x_ref, wa_ref, wc_ref, bc_ref, wm_ref, g_ref,
             o_ref, oa_ref, ob_ref, lse_ref, lsea_ref, lseb_ref, ya_ref, yc_ref, mg_ref, mm_ref, x1_ref,
             so1, so2, sl1, sl2, so, sl):
        for dil, src, dst, cast in ((DILATIONS[1], o1, so1, True), (DILATIONS[2], o2, so2, True),
                                    (DILATIONS[1], l1, sl1, False), (DILATIONS[2], l2, sl2, False)):
            _merge_residues(dst, dil, (lambda c, src=src: src[c].astype(F32)) if cast else (lambda c, src=src: src[c]))
        la, lb, lc = l0[0], _load_cols(sl1), _load_cols(sl2)
        m = jnp.maximum(jnp.maximum(la, lb), lc)
        e0 = jnp.exp(la - m)
        e1 = jnp.exp(lb - m)
        e2 = jnp.exp(lc - m)
        den = e0 + e1 + e2
        of = (e0 * o0[0].astype(F32) + e1 * _load_cols(so1) + e2 * _load_cols(so2)) / den
        o = of.astype(BF16)
        o_ref[...] = o
        lse = m + jnp.log(den)
        lse_ref[...] = lse
        _store_cols(so, of)
        _store_cols(sl, lse)
        for dil, oref, lref in ((DILATIONS[1], oa_ref, lsea_ref), (DILATIONS[2], ob_ref, lseb_ref)):
            def put_o(c, v, oref=oref):
                oref[c] = v.astype(BF16)

            def put_l(c, v, lref=lref):
                lref[c] = v
            _split_residues(so, dil, put_o)
            _split_residues(sl, dil, put_l)
        ya = _dot(o, wa_ref[...]).astype(BF16)
        yc = (_dot(a_ref[...], wc_ref[...]) + bc_ref[...]).astype(BF16)
        ya_ref[...] = ya
        yc_ref[...] = yc
        mg = (_sigmoid(ga_ref[...].astype(F32)) * ya.astype(F32)
              + _sigmoid(gc_ref[...].astype(F32)) * yc.astype(F32)).astype(BF16)
        mg_ref[...] = mg
        mm = _dot(mg, wm_ref[...]).astype(BF16)
        mm_ref[...] = mm
        mf = mm.astype(F32)
        r = lax.rsqrt(jnp.mean(mf * mf, axis=-1, keepdims=True) + RMS_EPS)
        x1_ref[...] = x_ref[...] + mf * r * g_ref[...]

    row512 = pl.BlockSpec((TM, GW), lambda i: (i, 0))
    rowd = pl.BlockSpec((TM, D), lambda i: (i, 0))
    vec = pl.BlockSpec((1, D), lambda i: (0, 0))
    full = lambda r, c: pl.BlockSpec((r, c), lambda i: (0, 0))
    res = [_residue_spec(TM, d, GW) for d in DILATIONS]
    rshape = lambda d, t: SDS(_residue_shape(S, d, GW), t)
    scr = _col_scratch(TM, GW)
    return pl.pallas_call(
        body, name="mix_fwd", grid=(S // TM,),
        in_specs=res + res + [rowd, pl.BlockSpec((TM, D), lambda i: (i, 2)), pl.BlockSpec((TM, D), lambda i: (i, 3)),
                              rowd, full(GW, D), full(D, D), vec, full(D, D), vec],
        out_specs=[row512, res[1], res[2], row512, res[1], res[2], rowd, rowd, rowd, rowd, rowd],
        out_shape=[SDS((S, GW), BF16), rshape(DILATIONS[1], BF16), rshape(DILATIONS[2], BF16),
                   SDS((S, GW), F32), rshape(DILATIONS[1], F32), rshape(DILATIONS[2], F32),
                   SDS((S, D), BF16), SDS((S, D), BF16), SDS((S, D), BF16), SDS((S, D), BF16), SDS((S, D), F32)],
        scratch_shapes=[scr] * 6,
        compiler_params=_cp(("arbitrary",)),
    )(og[0], og[1], og[2], lg[0], lg[1], lg[2], a, zrest, zrest, x, w_ao, w_co, b_co, w_mx, g_pm)


def _ffn_fwd(x1, tgt, g_pre, g_post, w_fi, w_fo):
    S = x1.shape[0]
    TM = 512

    def body(x1_ref, t_ref, gp_ref, go_ref, wi_ref, wo_ref,
             h2_ref, gu_ref, df_ref, dx2_ref, loss_ref, dgo_ref):
        i = pl.program_id(0)

        @pl.when(i == 0)
        def _():
            loss_ref[...] = jnp.zeros_like(loss_ref)
            dgo_ref[...] = jnp.zeros_like(dgo_ref)

        xf = x1_ref[...]
        r = lax.rsqrt(jnp.mean(xf * xf, axis=-1, keepdims=True) + RMS_EPS)
        h2_ref[...] = (xf * r * gp_ref[...]).astype(BF16)
        for k in range(NFT):
            gu_ref[:, 2 * k * FFN_T:(2 * k + 1) * FFN_T] = _dot(
                h2_ref[...], wi_ref[:, k * FFN_T:(k + 1) * FFN_T]).astype(BF16)
            gu_ref[:, (2 * k + 1) * FFN_T:(2 * k + 2) * FFN_T] = _dot(
                h2_ref[...], wi_ref[:, FFN_H + k * FFN_T:FFN_H + (k + 1) * FFN_T]).astype(BF16)
        f = jnp.zeros((TM, D), F32)
        for k in range(NFT):
            gf = gu_ref[:, 2 * k * FFN_T:(2 * k + 1) * FFN_T].astype(F32)
            uf = gu_ref[:, (2 * k + 1) * FFN_T:(2 * k + 2) * FFN_T].astype(F32)
            act = (gf * _sigmoid(gf) * uf).astype(BF16)
            f = f + _dot(act, wo_ref[k * FFN_T:(k + 1) * FFN_T, :])
        r = lax.rsqrt(jnp.mean(f * f, axis=-1, keepdims=True) + RMS_EPS)
        nrm = f * r
        e = x1_ref[...] + nrm * go_ref[...] - t_ref[...]
        tot = jnp.sum(jnp.sum(e * e, axis=-1, keepdims=True), axis=0, keepdims=True) * (0.5 / D)
        corner = jnp.logical_and(lax.broadcasted_iota(jnp.int32, (LANE_ROWS, D), 0) == 0,
                                 lax.broadcasted_iota(jnp.int32, (LANE_ROWS, D), 1) == 0)
        loss_ref[...] += jnp.where(corner, tot, 0.0)
        dx2 = e * (1.0 / D)
        dx2_ref[...] = dx2
        dgo_ref[...] += _colsum8(dx2 * nrm)
        dn = dx2 * go_ref[...]
        df_ref[...] = (r * (dn - nrm * jnp.mean(dn * nrm, axis=-1, keepdims=True))).astype(BF16)

    rowd = pl.BlockSpec((TM, D), lambda i: (i, 0))
    vec = pl.BlockSpec((1, D), lambda i: (0, 0))
    acc8 = pl.BlockSpec((LANE_ROWS, D), lambda i: (0, 0))
    return pl.pallas_call(
        body, name="ffn_fwd", grid=(S // TM,),
        in_specs=[rowd, rowd, vec, vec,
                  pl.BlockSpec((D, 2 * FFN_H), lambda i: (0, 0), pipeline_mode=pl.Buffered(1)),
                  pl.BlockSpec((FFN_H, D), lambda i: (0, 0), pipeline_mode=pl.Buffered(1))],
        out_specs=[rowd, pl.BlockSpec((TM, 2 * FFN_H), lambda i: (i, 0)), rowd, rowd, acc8, acc8],
        out_shape=[SDS((S, D), BF16), SDS((S, 2 * FFN_H), BF16), SDS((S, D), BF16), SDS((S, D), F32),
                   SDS((LANE_ROWS, D), F32), SDS((LANE_ROWS, D), F32)],
        compiler_params=_cp(("arbitrary",)),
    )(x1, tgt, g_pre, g_post, w_fi, w_fo)


def _ffn_bwd_act(df, gu, w_foT):
    S = df.shape[0]
    TM = 512

    def body_act(df_ref, gu_ref, wo_ref, dff_ref, act_ref):
        dacts = [_dot(df_ref[...], wo_ref[:, k * FFN_T:(k + 1) * FFN_T]) for k in range(NFT)]
        for k in range(NFT):
            dact = dacts[k]
            g = gu_ref[:, 2 * k * FFN_T:(2 * k + 1) * FFN_T].astype(F32)
            u = gu_ref[:, (2 * k + 1) * FFN_T:(2 * k + 2) * FFN_T].astype(F32)
            sg = _sigmoid(g)
            sl = g * sg
            act_ref[:, k * FFN_T:(k + 1) * FFN_T] = (sl * u).astype(BF16)
            dff_ref[:, 2 * k * FFN_T:(2 * k + 1) * FFN_T] = (dact * u * (sg * (1.0 + g * (1.0 - sg)))).astype(BF16)
            dff_ref[:, (2 * k + 1) * FFN_T:(2 * k + 2) * FFN_T] = (dact * sl).astype(BF16)

    rowd = pl.BlockSpec((TM, D), lambda i: (i, 0))
    wide = pl.BlockSpec((TM, 2 * FFN_H), lambda i: (i, 0))
    return pl.pallas_call(
        body_act, name="ffn_bwd_act", grid=(S // TM,),
        in_specs=[rowd, wide, pl.BlockSpec((D, FFN_H), lambda i: (0, 0), pipeline_mode=pl.Buffered(1))],
        out_specs=[wide, pl.BlockSpec((TM, FFN_H), lambda i: (i, 0))],
        out_shape=[SDS((S, 2 * FFN_H), BF16), SDS((S, FFN_H), BF16)],
        compiler_params=_cp(("arbitrary",)),
    )(df, gu, w_foT)


def _ffn_bwd_in(dff, x1, dx2, g_pre, w_fiT, rider):
    S = x1.shape[0]
    TM = 512
    rowd = pl.BlockSpec((TM, D), lambda i: (i, 0))
    wide = pl.BlockSpec((TM, 2 * FFN_H), lambda i: (i, 0))
    KC = 512
    nkc = 2 * FFN_H // KC

    def body_in(dff_ref, x1_ref, dx2_ref, gp_ref, wi_ref, dx1_ref, dgp_ref):
        i = pl.program_id(0)

        @pl.when(i == 0)
        def _():
            dgp_ref[...] = jnp.zeros_like(dgp_ref)

        dh = jnp.zeros((TM, D), F32)
        for k in range(nkc):
            dh = dh + _dot(dff_ref[:, k * KC:k * KC + FFN_T], wi_ref[k * FFN_T:(k + 1) * FFN_T, :]) \
                + _dot(dff_ref[:, k * KC + FFN_T:(k + 1) * KC], wi_ref[FFN_H + k * FFN_T:FFN_H + (k + 1) * FFN_T, :])
        xf = x1_ref[...]
        r = lax.rsqrt(jnp.mean(xf * xf, axis=-1, keepdims=True) + RMS_EPS)
        nrm = xf * r
        dgp_ref[...] += _colsum8(dh * nrm)
        dn = dh * gp_ref[...]
        dx1_ref[...] = dx2_ref[...] + r * (dn - nrm * jnp.mean(dn * nrm, axis=-1, keepdims=True))

    body_in, r_in, r_out, r_shape, r_scr = _ride(body_in, 5, 2, 0, rider, S // TM)
    return pl.pallas_call(
        body_in, name="ffn_bwd_in", grid=(S // TM,),
        in_specs=[wide, rowd, rowd, pl.BlockSpec((1, D), lambda i: (0, 0)),
                  pl.BlockSpec((2 * FFN_H, D), lambda i: (0, 0), pipeline_mode=pl.Buffered(1))] + r_in,
        out_specs=[rowd, pl.BlockSpec((LANE_ROWS, D), lambda i: (0, 0))] + r_out,
        out_shape=[SDS((S, D), F32), SDS((LANE_ROWS, D), F32)] + r_shape,
        scratch_shapes=r_scr,
        compiler_params=_cp(("arbitrary",)),
    )(dff, x1, dx2, g_pre, w_fiT, *rider.ins)


def _mix_bwd(dx1, mm, ya, yc, zrest, g_pm, w_mxT, w_aoT, w_coT):
    S = dx1.shape[0]
    TM = 512

    def body(dx_ref, mm_ref, ya_ref, yc_ref, ga_ref, gc_ref, g_ref, wm_ref, wa_ref, wc_ref,
             dmm_ref, dya_ref, dyc_ref, do_ref, doa_ref, dob_ref, da_ref, dzg_ref, dgpm_ref, dbco_ref, sdo):
        i = pl.program_id(0)

        @pl.when(i == 0)
        def _():
            dgpm_ref[...] = jnp.zeros_like(dgpm_ref)
            dbco_ref[...] = jnp.zeros_like(dbco_ref)

        mf = mm_ref[...].astype(F32)
        r = lax.rsqrt(jnp.mean(mf * mf, axis=-1, keepdims=True) + RMS_EPS)
        nrm = mf * r
        dx = dx_ref[...]
        dgpm_ref[...] += _colsum8(dx * nrm)
        dn = dx * g_ref[...]
        dmm = (r * (dn - nrm * jnp.mean(dn * nrm, axis=-1, keepdims=True))).astype(BF16)
        dmm_ref[...] = dmm
        dmg = _dot(dmm, wm_ref[...])
        sa = _sigmoid(ga_ref[...].astype(F32))
        sc = _sigmoid(gc_ref[...].astype(F32))
        dya = (dmg * sa).astype(BF16)
        dyc = (dmg * sc).astype(BF16)
        dya_ref[...] = dya
        dyc_ref[...] = dyc
        dbco_ref[...] += _colsum8(dyc.astype(F32))
        dzg_ref[:, 0:D] = (dmg * ya_ref[...].astype(F32) * (sa * (1.0 - sa))).astype(BF16)
        dzg_ref[:, D:] = (dmg * yc_ref[...].astype(F32) * (sc * (1.0 - sc))).astype(BF16)
        dof = _dot(dya, wa_ref[...])
        do_ref[...] = dof.astype(BF16)
        _store_cols(sdo, dof)
        for dil, ref in ((DILATIONS[1], doa_ref), (DILATIONS[2], dob_ref)):
            def put(c, v, ref=ref):
                ref[c] = v.astype(BF16)
            _split_residues(sdo, dil, put)
        da_ref[...] = _dot(dyc, wc_ref[...]).astype(BF16)

    rowd = pl.BlockSpec((TM, D), lambda i: (i, 0))
    full = lambda r, c: pl.BlockSpec((r, c), lambda i: (0, 0))
    acc8 = pl.BlockSpec((LANE_ROWS, D), lambda i: (0, 0))
    return pl.pallas_call(
        body, name="mix_bwd", grid=(S // TM,),
        in_specs=[rowd, rowd, rowd, rowd, pl.BlockSpec((TM, D), lambda i: (i, 2)),
                  pl.BlockSpec((TM, D), lambda i: (i, 3)), full(1, D), full(D, D), full(D, GW), full(D, D)],
        out_specs=[rowd, rowd, rowd, pl.BlockSpec((TM, GW), lambda i: (i, 0)),
                   _residue_spec(TM, DILATIONS[1], GW), _residue_spec(TM, DILATIONS[2], GW), rowd,
                   pl.BlockSpec((TM, 2 * D), lambda i: (i, 0)), acc8, acc8],
        out_shape=[SDS((S, D), BF16), SDS((S, D), BF16), SDS((S, D), BF16), SDS((S, GW), BF16),
                   SDS(_residue_shape(S, DILATIONS[1], GW), BF16), SDS(_residue_shape(S, DILATIONS[2], GW), BF16),
                   SDS((S, D), BF16), SDS((S, 2 * D), BF16), SDS((LANE_ROWS, D), F32), SDS((LANE_ROWS, D), F32)],
        scratch_shapes=[_col_scratch(TM, GW)],
        compiler_params=_cp(("arbitrary",)),
    )(dx1, mm, ya, yc, zrest, zrest, g_pm, w_mxT, w_aoT, w_coT)


def _conv_bwd(da, cv, zrest, b_glu, wdw, g_ln, b_ln):
    S = da.shape[0]
    TM = CONV_TM
    HALO = 32
    hb = TM // HALO
    nh = S // HALO

    def body(da_ref, dan_ref, cv_ref, cvn_ref, u_ref, g_ref, uh_ref, gh_ref, bg_ref, w_ref, gl_ref, bl_ref,
             dglu_ref, dbu_ref, dbg_ref, dw_ref, dgl_ref, dbl_ref, dbd_ref, dext, uext, dsh, ush, du_scr, dw8):
        i = pl.program_id(0)
        last = i == pl.num_programs(0) - 1

        @pl.when(i == 0)
        def _():
            for ref in (dbu_ref, dbg_ref, dw8, dgl_ref, dbl_ref, dbd_ref):
                ref[...] = jnp.zeros_like(ref)

        def ln_bwd(da_v, cv_v):
            cf = cv_v.astype(F32)
            mu = jnp.mean(cf, axis=-1, keepdims=True)
            xc = cf - mu
            rstd = lax.rsqrt(jnp.mean(xc * xc, axis=-1, keepdims=True) + LN_EPS)
            xh = xc * rstd
            y = xh * gl_ref[...] + bl_ref[...]
            sy = _sigmoid(y)
            dy = da_v.astype(F32) * (sy * (1.0 + y * (1.0 - sy)))
            dxh = dy * gl_ref[...]
            dcv = rstd * (dxh - jnp.mean(dxh, axis=-1, keepdims=True)
                          - xh * jnp.mean(dxh * xh, axis=-1, keepdims=True))
            return dcv, dy, xh

        dcv, dy, xh = ln_bwd(da_ref[...], cv_ref[...])
        dgl_ref[...] += _colsum8(dy * xh)
        dbl_ref[...] += _colsum8(dy)
        dbd_ref[...] += _colsum8(dcv)
        dcvn, _, _ = ln_bwd(dan_ref[...], cvn_ref[...])
        dext[0:TM, :] = dcv
        dext[TM:, :] = jnp.where(last, 0.0, dcvn)

        bu = bg_ref[:, 0:D]
        bgt = bg_ref[:, D:2 * D]
        upre = u_ref[...].astype(F32) + bu
        sg = _sigmoid(g_ref[...].astype(F32) + bgt)
        uh = (uh_ref[...].astype(F32) + bu) * _sigmoid(gh_ref[...].astype(F32) + bgt)
        uext[0:HALO, :] = jnp.where(i == 0, 0.0, uh)
        uext[HALO:, :] = upre * sg

        _make_shifts(dext, dsh, TM)
        _make_shifts(uext, ush, TM)
        for r0, lanes in _tap_blocks(TM):
            acc = jnp.zeros((CONV_RC, LANES), F32)
            for j in range(CONV_W):
                acc = acc + _shifted(dext, dsh, CONV_W - 1 - j, r0, CONV_RC, lanes) * w_ref[j:j + 1, lanes]
            du_scr[r0:r0 + CONV_RC, lanes] = acc
        for l0 in range(0, D, LANES):
            lanes = slice(l0, l0 + LANES)
            accs = [jnp.zeros((LANE_ROWS, LANES), F32)] * CONV_W
            for r0 in range(0, TM, CONV_RC):
                dc = dext[r0:r0 + CONV_RC, lanes]
                for j in range(CONV_W):
                    prod = dc * _shifted(uext, ush, HALO - (CONV_W - 1) + j, r0, CONV_RC, lanes)
                    accs[j] = accs[j] + jnp.sum(prod.reshape(CONV_RC // LANE_ROWS, LANE_ROWS, LANES), axis=0)
            for j in range(CONV_W):
                dw8[j, :, lanes] += accs[j]

        @pl.when(last)
        def _():
            for j in range(CONV_W):
                dw_ref[j:j + 1, :] = jnp.sum(dw8[j], axis=0, keepdims=True)
            dw_ref[CONV_W:, :] = jnp.zeros((32 - CONV_W, D), F32)

        du = du_scr[...]
        dup = du * sg
        dgp = du * upre * (sg * (1.0 - sg))
        dglu_ref[:, 0:D] = dup.astype(BF16)
        dglu_ref[:, D:] = dgp.astype(BF16)
        dbu_ref[...] += _colsum8(dup.astype(BF16).astype(F32))
        dbg_ref[...] += _colsum8(dgp.astype(BF16).astype(F32))

    rowd = pl.BlockSpec((TM, D), lambda i: (i, 0))
    nxt = pl.BlockSpec((HALO, D), lambda i: (jnp.minimum((i + 1) * hb, nh - 1), 0))
    vec = pl.BlockSpec((1, D), lambda i: (0, 0))
    acc8 = pl.BlockSpec((LANE_ROWS, D), lambda i: (0, 0))
    return pl.pallas_call(
        body, name="conv_bwd", grid=(S // TM,),
        in_specs=[rowd, nxt, rowd, nxt,
                  pl.BlockSpec((TM, D), lambda i: (i, 0)), pl.BlockSpec((TM, D), lambda i: (i, 1)),
                  pl.BlockSpec((HALO, D), lambda i: (jnp.maximum(i * hb - 1, 0), 0)),
                  pl.BlockSpec((HALO, D), lambda i: (jnp.maximum(i * hb - 1, 0), 1)),
                  pl.BlockSpec((1, 2 * D), lambda i: (0, 0)), pl.BlockSpec((32, D), lambda i: (0, 0)), vec, vec],
        out_specs=[pl.BlockSpec((TM, 2 * D), lambda i: (i, 0)), acc8, acc8,
                   pl.BlockSpec((32, D), lambda i: (0, 0)), acc8, acc8, acc8],
        out_shape=[SDS((S, 2 * D), BF16), SDS((LANE_ROWS, D), F32), SDS((LANE_ROWS, D), F32), SDS((32, D), F32),
                   SDS((LANE_ROWS, D), F32), SDS((LANE_ROWS, D), F32), SDS((LANE_ROWS, D), F32)],
        scratch_shapes=[pltpu.VMEM((TM + HALO, D), F32), pltpu.VMEM((HALO + TM, D), F32),
                        pltpu.VMEM((7, TM + SHIFT_PAD, D), F32), pltpu.VMEM((7, TM + SHIFT_PAD, D), F32),
                        pltpu.VMEM((TM, D), F32), pltpu.VMEM((32, LANE_ROWS, D), F32)],
        compiler_params=_cp(("arbitrary",)),
    )(da, da, cv, cv, zrest, zrest, zrest, zrest, b_glu, wdw, g_ln, b_ln)


def _attn_bwd(zq, do, o, lse, bias_t, gi):
    dil, L, _ = zq.shape
    _, TQ, QB, ns = _attn_tile(L * dil, dil)
    NP = NH // 2

    def body(q3, kc3, kp3, vc3, vp3, do3, o3, l3, b_ref,
             out3, db_ref, kext, vext, dkx, dvx, dqn, dqc, dkc, dvc):
        q_ref, kc_ref, kp_ref, vc_ref, vp_ref, do_ref, o_ref, l_ref, out_ref = (
            r.at[0] for r in (q3, kc3, kp3, vc3, vp3, do3, o3, l3, out3))
        t = pl.program_id(0)
        n = lax.rem(cur(t), ns)

        @pl.when(t == 0)
        def _():
            db_ref[...] = jnp.zeros_like(db_ref)

        @pl.when(t < T - 1)
        def _():
            kext[0:QBLK, :] = kp_ref[...]
            kext[QBLK:, :] = kc_ref[...]
            vext[0:QBLK, :] = vp_ref[...]
            vext[QBLK:, :] = vc_ref[...]
            krow = lax.broadcasted_iota(jnp.int32, (KBLK, 2 * QBLK), 0)
            no_prev = jnp.logical_and(n == 0, krow < QBLK)
            lane = lax.broadcasted_iota(jnp.int32, (QBLK, LANES), 1)

            def overlap_add(parts):
                segs = [parts[0][0:QBLK]]
                for b in range(1, QB):
                    segs.append(parts[b - 1][QBLK:] + parts[b][0:QBLK])
                segs.append(parts[QB - 1][QBLK:])
                return jnp.concatenate(segs, axis=0)

            lanes_of = [slice(hp * LANES, (hp + 1) * LANES) for hp in range(NP)]
            dv_parts = [[] for _ in range(NP)]
            dk_parts = [[] for _ in range(NP)]
            dbsum = [None] * NP
            for b in range(QB):
                rows = slice(b * QBLK, (b + 1) * QBLK)
                win = slice(b * QBLK, b * QBLK + KBLK)
                q2 = [_pair_stack(q_ref, rows, pl_, SCALE) for pl_ in lanes_of]
                do2 = [_pair_stack(do_ref, rows, pl_) for pl_ in lanes_of]
                st = [_dot_nt(kext[win, pl_], q2[hp]) + b_ref[0, hp] for hp, pl_ in enumerate(lanes_of)]
                dpt = [_dot_nt(vext[win, pl_], do2[hp]) for hp, pl_ in enumerate(lanes_of)]
                lse_t = l_ref[rows, :].T
                prod_t = (do_ref[rows, :].astype(F32) * o_ref[rows, :].astype(F32)).T
                dst = []
                for hp in range(NP):
                    lo = hp * LANES
                    lse_row = jnp.concatenate([lse_t[lo:lo + 1], lse_t[lo + HD:lo + HD + 1]], axis=1)
                    delta_row = jnp.concatenate([jnp.sum(prod_t[lo:lo + HD], axis=0, keepdims=True),
                                                 jnp.sum(prod_t[lo + HD:lo + LANES], axis=0, keepdims=True)], axis=1)
                    s_hp = jnp.where(no_prev, NEG_INF, st[hp]) if b == 0 else st[hp]
                    pt = jnp.exp(s_hp - lse_row)
                    d = pt * (dpt[hp] - delta_row)
                    dbsum[hp] = d if dbsum[hp] is None else dbsum[hp] + d
                    dst.append(d.astype(BF16))
                    dv_parts[hp].append(_dot(pt.astype(BF16), do2[hp]))
                for hp, pl_ in enumerate(lanes_of):
                    dk_parts[hp].append(_dot(dst[hp], q2[hp]))
                    dq2 = _dot_tn(dst[hp], kext[win, pl_])
                    dqn[rows, pl_] = jnp.where(lane < HD, dq2[0:QBLK], dq2[QBLK:]) * SCALE
            for hp, pl_ in enumerate(lanes_of):
                db_ref[hp] += dbsum[hp]
                dvx[:, pl_] = overlap_add(dv_parts[hp])
                dkx[:, pl_] = overlap_add(dk_parts[hp])

        @pl.when(t > 0)
        def _():
            out_ref[:, 0:GW] = dqc[...].astype(BF16)
            out_ref[:, GW:2 * GW] = dkc[...].astype(BF16)
            out_ref[:, 2 * GW:] = dvc[...].astype(BF16)

        @pl.when(jnp.logical_and(t > 0, t < T - 1))
        def _():
            out_ref[TQ - QBLK:, GW:2 * GW] = (dkc[TQ - QBLK:, :] + dkx[0:QBLK, :]).astype(BF16)
            out_ref[TQ - QBLK:, 2 * GW:] = (dvc[TQ - QBLK:, :] + dvx[0:QBLK, :]).astype(BF16)

        @pl.when(t < T - 1)
        def _():
            dqc[...] = dqn[...]
            dkc[...] = dkx[QBLK:, :]
            dvc[...] = dvx[QBLK:, :]

    T = dil * ns + 1

    def cur(t):
        return jnp.minimum(t, T - 2)

    def blk(t, col):
        return (lax.div(cur(t), ns), lax.rem(cur(t), ns), col)

    def prev(t, col):
        return (lax.div(cur(t), ns), jnp.maximum(lax.rem(cur(t), ns) * QB - 1, 0), col)

    def late(t):
        tp = jnp.maximum(t - 1, 0)
        return (lax.div(tp, ns), lax.rem(tp, ns), 0)

    rows = lambda t: blk(t, 0)
    return pl.pallas_call(
        body, name=f"attn_bwd_g{gi}", grid=(T,),
        in_specs=[pl.BlockSpec((1, TQ, GW), lambda t: blk(t, 0)),
                  pl.BlockSpec((1, TQ, GW), lambda t: blk(t, 1)),
                  pl.BlockSpec((1, QBLK, GW), lambda t: prev(t, 1)),
                  pl.BlockSpec((1, TQ, GW), lambda t: blk(t, 2)),
                  pl.BlockSpec((1, QBLK, GW), lambda t: prev(t, 2)),
                  pl.BlockSpec((1, TQ, GW), rows), pl.BlockSpec((1, TQ, GW), rows), pl.BlockSpec((1, TQ, GW), rows),
                  pl.BlockSpec((1, NP, KBLK, 2 * QBLK), lambda t: (gi, 0, 0, 0))],
        out_specs=[pl.BlockSpec((1, TQ, 3 * GW), late),
                   pl.BlockSpec((NP, KBLK, 2 * QBLK), lambda t: (0, 0, 0))],
        out_shape=[SDS((dil, L, 3 * GW), BF16), SDS((NP, KBLK, 2 * QBLK), F32)],
        scratch_shapes=[pltpu.VMEM((QBLK + TQ, GW), BF16), pltpu.VMEM((QBLK + TQ, GW), BF16),
                        pltpu.VMEM((QBLK + TQ, GW), F32), pltpu.VMEM((QBLK + TQ, GW), F32),
                        pltpu.VMEM((TQ, GW), F32), pltpu.VMEM((TQ, GW), F32),
                        pltpu.VMEM((TQ, GW), F32), pltpu.VMEM((TQ, GW), F32)],
        compiler_params=_cp(("arbitrary",)),
    )(zq, zq, zq, zq, zq, do, o, lse, bias_t)


def _dz_block(k):
    if k < 9:
        return k % 3, k // 3
    if k < 13:
        return 3, k - 9
    return 4, k - 13


_DZ_SRC = np.array([_dz_block(k)[0] for k in range(17)], np.int32)


def _dz_hold(s):
    uses = [(k, _dz_block(k)[1]) for k in range(17) if _dz_block(k)[0] == s]
    hold = []
    for k in range(17):
        nxt = [b for kk, b in uses if kk >= k]
        hold.append(nxt[0] if nxt else uses[-1][1])
    return np.array(hold, np.int32)


def _table(tab, k):
    out = jnp.int32(int(tab[0]))
    for idx in range(1, len(tab)):
        out = jnp.where(k == idx, jnp.int32(int(tab[idx])), out)
    return out


def _w_in_tile(s, blk):
    return blk * 3 + s if s < 3 else (9 if s == 3 else 13) + blk


def _in_bwd(dqkv, dglu, dzg, w_inT, x, dx1, g, rider):
    S = x.shape[0]
    TM = 512

    def body(d0, d1, d2, d3, d4, w_ref, x_ref, dx1_ref, g_ref, gx_ref, dg_ref, scr1, scr2):
        i = pl.program_id(0)

        @pl.when(i == 0)
        def _():
            dg_ref[...] = jnp.zeros_like(dg_ref)

        def rows(s, blk):
            k = _w_in_tile(s, blk)
            return w_ref[k * GW:(k + 1) * GW, :]

        for s, ref, scr in ((1, d1, scr1), (2, d2, scr2)):
            dil = DILATIONS[s]
            part = jnp.zeros((TM, D), F32)
            for blk in range(3):
                part = part + _dot(ref[:, :, blk * GW:(blk + 1) * GW].reshape(TM, GW), rows(s, blk))
            _merge_residues(scr, dil, lambda c, part=part, dil=dil: part[c * (TM // dil):(c + 1) * (TM // dil)])
        dh = jnp.zeros((TM, D), F32)
        for blk in range(3):
            dh = dh + _dot(d0[0, :, blk * GW:(blk + 1) * GW], rows(0, blk))
        for s, ref in ((3, d3), (4, d4)):
            for blk in range(4):
                dh = dh + _dot(ref[:, blk * GW:(blk + 1) * GW], rows(s, blk))
        dh = dh + _load_cols(scr1) + _load_cols(scr2)
        xf = x_ref[...]
        r = lax.rsqrt(jnp.mean(xf * xf, axis=-1, keepdims=True) + RMS_EPS)
        nrm = xf * r
        dg_ref[...] += _colsum8(dh * nrm)
        dn = dh * g_ref[...]
        gx_ref[...] = dx1_ref[...] + r * (dn - nrm * jnp.mean(dn * nrm, axis=-1, keepdims=True))

    rowd = pl.BlockSpec((TM, D), lambda i: (i, 0))
    wide = pl.BlockSpec((TM, 2 * D), lambda i: (i, 0))
    body, r_in, r_out, r_shape, r_scr = _ride(body, 9, 2, 2, rider, S // TM)
    return pl.pallas_call(
        body, name="in_bwd", grid=(S // TM,),
        in_specs=[_residue_spec(TM, d, 3 * GW) for d in DILATIONS] + [wide, wide]
        + [pl.BlockSpec(w_inT.shape, lambda i: (0, 0), pipeline_mode=pl.Buffered(1)), rowd, rowd,
           pl.BlockSpec((1, D), lambda i: (0, 0))] + r_in,
        out_specs=[rowd, pl.BlockSpec((LANE_ROWS, D), lambda i: (0, 0))] + r_out,
        out_shape=[SDS((S, D), F32), SDS((LANE_ROWS, D), F32)] + r_shape,
        scratch_shapes=[_col_scratch(TM, D), _col_scratch(TM, D)] + r_scr,
        compiler_params=_cp(("arbitrary",)),
    )(*dqkv, dglu, dzg, w_inT, x, dx1, g, *rider.ins)


def _dw_in(dqkv, dglu, dzg, hs):
    S = hs[0].shape[0]
    TS = min(2048, S)
    nk = 17
    holds = [_dz_hold(s) for s in range(5)]
    h_of = (0, 1, 2, 0, 0)

    def body(d0, d1, d2, d3, d4, h0, h1, h2, o_ref, acc):
        m = pl.program_id(0)
        s_ = pl.program_id(1)

        @pl.when(s_ == 0)
        def _():
            acc[...] = jnp.zeros_like(acc)

        src = _table(_DZ_SRC, m)
        pairs = ((d0, h0), (d1, h1), (d2, h2), (d3, h0), (d4, h0))
        for s, (dref, href) in enumerate(pairs):
            @pl.when(src == s)
            def _(dref=dref, href=href):
                acc[...] += _dot_tn(dref[...].reshape(TS, GW), href[...].reshape(TS, D))

        @pl.when(s_ == pl.num_programs(1) - 1)
        def _():
            o_ref[...] = acc[...].astype(BF16)

    def row(s, m, s_):
        return jnp.where(_table(_DZ_SRC, m) == s, s_, 0)

    def dspec(s):
        if s < 3:
            dil = DILATIONS[s]
            return pl.BlockSpec((dil, TS // dil, GW), lambda m, s_: (0, row(s, m, s_), _table(holds[s], m)))
        return pl.BlockSpec((TS, GW), lambda m, s_: (row(s, m, s_), _table(holds[s], m)))

    def hrow(j, m, s_):
        used = _table(np.array([int(h_of[_dz_block(k)[0]] == j) for k in range(nk)], np.int32), m)
        return jnp.where(used == 1, s_, 0)

    hspecs = [pl.BlockSpec((TS, D), lambda m, s_: (hrow(0, m, s_), 0))] + [
        pl.BlockSpec((DILATIONS[j], TS // DILATIONS[j], D), lambda m, s_, j=j: (0, hrow(j, m, s_), 0)) for j in (1, 2)]
    return pl.pallas_call(
        body, name="dw_in", grid=(nk, S // TS),
        in_specs=[dspec(s) for s in range(5)] + hspecs,
        out_specs=pl.BlockSpec((GW, D), lambda m, s_: (m, 0)),
        out_shape=SDS((nk * GW, D), BF16),
        scratch_shapes=[pltpu.VMEM((GW, D), F32)],
        compiler_params=_cp(("arbitrary", "arbitrary")),
    )(*dqkv, dglu, dzg, *hs)


def _mm_tn(a, b, tm, a_maps, name):
    S, N = b.shape
    parts = len(a_maps)
    tp = tm // parts
    nm = len(a_maps[0])
    TS = min(4096 if tm <= 512 else 2048, S)
    tabs = [np.array(t, np.int32) for t in a_maps]

    def body(*refs):
        a_refs = refs[:parts]
        b_ref, o_ref, acc = refs[parts:]
        s_ = pl.program_id(1)

        @pl.when(s_ == 0)
        def _():
            acc[...] = jnp.zeros_like(acc)

        for p, ar in enumerate(a_refs):
            acc[p * tp:(p + 1) * tp, :] += _dot_tn(ar[...], b_ref[...])

        @pl.when(s_ == pl.num_programs(1) - 1)
        def _():
            o_ref[...] = acc[...].astype(BF16)

    return pl.pallas_call(
        body, name=name, grid=(nm, S // TS),
        in_specs=[pl.BlockSpec((TS, tp), lambda m, s_, t=t: (s_, _table(t, m))) for t in tabs]
        + [pl.BlockSpec((TS, N), lambda m, s_: (s_, 0))],
        out_specs=pl.BlockSpec((tm, N), lambda m, s_: (m, 0)),
        out_shape=SDS((nm * tm, N), BF16),
        scratch_shapes=[pltpu.VMEM((tm, N), F32)],
        compiler_params=_cp(("arbitrary", "arbitrary")),
    )(*([a] * parts), b)


def _row_tile(rows, cols, limit=1 << 20):
    if rows * cols * 4 <= limit:
        return rows
    best = None
    for t in range(8, rows, 8):
        if rows % t == 0 and t * cols * 4 <= limit:
            best = t
    return best


def _adamw(w, g, m, v, name):
    R, C = w.shape
    tr = _row_tile(R, C)

    def body(w_ref, g_ref, m_ref, v_ref, d_ref, nm_ref, nv_ref):
        gg = g_ref[...]
        nm = ADAM_B1 * m_ref[...] + (1.0 - ADAM_B1) * gg
        nv = ADAM_B2 * v_ref[...] + (1.0 - ADAM_B2) * (gg * gg)
        m_hat = nm / (1.0 - ADAM_B1 ** ADAM_STEP)
        v_hat = nv / (1.0 - ADAM_B2 ** ADAM_STEP)
        d_ref[...] = -ADAM_LR * (m_hat / (jnp.sqrt(v_hat) + ADAM_EPS) + ADAM_WD * w_ref[...])
        nm_ref[...] = nm
        nv_ref[...] = nv

    spec = pl.BlockSpec((tr, C), lambda i: (i, 0))
    return pl.pallas_call(
        body, name=name, grid=(R // tr,), in_specs=[spec] * 4, out_specs=[spec] * 3,
        out_shape=[SDS((R, C), F32)] * 3, compiler_params=_cp(("arbitrary",)),
    )(w, g, m, v)


_FLIPS = ((1, 0), (0, 1), (1, 1))


def _place():
    x, y, c = lax.axis_index("x"), lax.axis_index("y"), lax.axis_index("c")
    return x, y, c


def _peer_chips(x, y):
    return [((x + fx) % 2, (y + fy) % 2) for fx, fy in _FLIPS]


def _gather_weights(shards):
    nw = len(shards)
    views = [s.reshape(2, s.shape[0] // 2, s.shape[1]) for s in shards]

    def body(*refs):
        ins = refs[:nw]
        outs = refs[nw:2 * nw]
        ici_send, ici_recv, d2d_send, d2d_recv, loc = refs[2 * nw:]
        x, y, c = _place()
        j = 2 * x + y
        chips = _peer_chips(x, y)
        copies = []
        for w in range(nw):
            cp = pltpu.make_async_copy(ins[w], outs[w].at[j], loc.at[w])
            cp.start()
            copies.append(cp)
        sends = []
        for w in range(nw):
            for k, (px, py) in enumerate(chips):
                cp = pltpu.make_async_remote_copy(
                    src_ref=ins[w].at[c], dst_ref=outs[w].at[j, c], send_sem=ici_send.at[w, k],
                    recv_sem=ici_recv.at[w, k], device_id=(px, py, c), device_id_type=MESH)
                cp.start()
                sends.append(cp)
        for w in range(nw):
            for k, (px, py) in enumerate(chips):
                jk = 2 * px + py
                land = outs[w].at[jk, c]
                pltpu.make_async_remote_copy(
                    src_ref=ins[w].at[c], dst_ref=land, send_sem=ici_send.at[w, k],
                    recv_sem=ici_recv.at[w, k], device_id=(px, py, c), device_id_type=MESH).wait_recv()
                cp = pltpu.make_async_remote_copy(
                    src_ref=land, dst_ref=land, send_sem=d2d_send.at[w, k],
                    recv_sem=d2d_recv.at[w, k], device_id=(x, y, 1 - c), device_id_type=MESH)
                cp.start()
                sends.append(cp)
        for w in range(nw):
            for k, (px, py) in enumerate(chips):
                jk = 2 * px + py
                land = outs[w].at[jk, 1 - c]
                pltpu.make_async_remote_copy(
                    src_ref=land, dst_ref=land, send_sem=d2d_send.at[w, k],
                    recv_sem=d2d_recv.at[w, k], device_id=(x, y, 1 - c), device_id_type=MESH).wait_recv()
        for cp in sends:
            cp.wait_send()
        for cp in copies:
            cp.wait()

    outs = pl.pallas_call(
        body, name="gather_weights",
        in_specs=[ANY] * nw, out_specs=[ANY] * nw,
        out_shape=[SDS((4,) + v.shape, BF16) for v in views],
        scratch_shapes=[pltpu.SemaphoreType.DMA((nw, 3)), pltpu.SemaphoreType.DMA((nw, 3)),
                        pltpu.SemaphoreType.DMA((nw, 3)), pltpu.SemaphoreType.DMA((nw, 3)),
                        pltpu.SemaphoreType.DMA((nw,))],
    )(*views)
    return [o.reshape(4 * s.shape[0], s.shape[1]) for o, s in zip(outs, shards)]


class _Rider:
    def __init__(self, ins, out_shape, scratch, start, finish, mid=None):
        self.ins, self.out_shape, self.scratch = list(ins), list(out_shape), list(scratch)
        self.start, self.finish, self.mid = start, finish, mid


def _ride(body, n_in, n_out, n_scr, rider, steps):
    if rider is None:
        return body, [], [], [], []
    ri, ro = len(rider.ins), len(rider.out_shape)

    def wrapped(*refs):
        ins, r_ins = refs[:n_in], refs[n_in:n_in + ri]
        o0 = n_in + ri
        outs, r_outs = refs[o0:o0 + n_out], refs[o0 + n_out:o0 + n_out + ro]
        s0 = o0 + n_out + ro
        scr, r_scr = refs[s0:s0 + n_scr], refs[s0 + n_scr:]
        i = pl.program_id(0)

        @pl.when(i == 0)
        def _():
            rider.start(r_ins, r_outs, r_scr)

        if rider.mid is not None:
            @pl.when(i == (3 * steps) // 4)
            def _():
                rider.mid(r_ins, r_outs, r_scr)

        body(*ins, *outs, *scr)

        @pl.when(i == steps - 1)
        def _():
            rider.finish(r_ins, r_outs, r_scr)

    return wrapped, [ANY] * ri, [ANY] * ro, rider.out_shape, rider.scratch


def _gather_rider(shards):
    nw = len(shards)
    views = [s.reshape(2, s.shape[0] // 2, s.shape[1]) for s in shards]

    def parts(ins, outs, sems):
        ici_send, ici_recv, d2d_send, d2d_recv, loc = sems
        x, y, c = _place()
        j = 2 * x + y
        local, ici, land_ici, fwd, land_fwd = [], [], [], [], []
        for w in range(nw):
            local.append(pltpu.make_async_copy(ins[w], outs[w].at[j], loc.at[w]))
            for k, (px, py) in enumerate(_peer_chips(x, y)):
                jk = 2 * px + py
                ici.append(pltpu.make_async_remote_copy(
                    src_ref=ins[w].at[c], dst_ref=outs[w].at[j, c], send_sem=ici_send.at[w, k],
                    recv_sem=ici_recv.at[w, k], device_id=(px, py, c), device_id_type=MESH))
                mine = outs[w].at[jk, c]
                land_ici.append(pltpu.make_async_remote_copy(
                    src_ref=ins[w].at[c], dst_ref=mine, send_sem=ici_send.at[w, k],
                    recv_sem=ici_recv.at[w, k], device_id=(px, py, c), device_id_type=MESH))
                fwd.append(pltpu.make_async_remote_copy(
                    src_ref=mine, dst_ref=mine, send_sem=d2d_send.at[w, k],
                    recv_sem=d2d_recv.at[w, k], device_id=(x, y, 1 - c), device_id_type=MESH))
                theirs = outs[w].at[jk, 1 - c]
                land_fwd.append(pltpu.make_async_remote_copy(
                    src_ref=theirs, dst_ref=theirs, send_sem=d2d_send.at[w, k],
                    recv_sem=d2d_recv.at[w, k], device_id=(x, y, 1 - c), device_id_type=MESH))
        return local, ici, land_ici, fwd, land_fwd

    def start(ins, outs, sems):
        local, ici, _, _, _ = parts(ins, outs, sems)
        for cp in local + ici:
            cp.start()

    def mid(ins, outs, sems):
        _, _, land_ici, fwd, _ = parts(ins, outs, sems)
        for landed, cp in zip(land_ici, fwd):
            landed.wait_recv()
            cp.start()

    def finish(ins, outs, sems):
        local, ici, _, fwd, land_fwd = parts(ins, outs, sems)
        for cp in land_fwd:
            cp.wait_recv()
        for cp in ici + fwd:
            cp.wait_send()
        for cp in local:
            cp.wait()

    sem = pltpu.SemaphoreType.DMA
    return _Rider(views, [SDS((4,) + v.shape, BF16) for v in views],
                  [sem((nw, 3)), sem((nw, 3)), sem((nw, 3)), sem((nw, 3)), sem((nw,))], start, finish, mid)


def _chip_exchange_rider(parts):
    nw = len(parts)

    def copies(ins, outs, sems):
        send, recv = sems
        x, y, c = _place()
        return [pltpu.make_async_remote_copy(
            src_ref=ins[w].at[2 * px + py], dst_ref=outs[w].at[k], send_sem=send.at[w, k],
            recv_sem=recv.at[w, k], device_id=(px, py, c), device_id_type=MESH)
            for w in range(nw) for k, (px, py) in enumerate(_peer_chips(x, y))]

    def start(ins, outs, sems):
        for cp in copies(ins, outs, sems):
            cp.start()

    def finish(ins, outs, sems):
        for cp in copies(ins, outs, sems):
            cp.wait()

    sem = pltpu.SemaphoreType.DMA
    return _Rider(parts, [SDS((3,) + p.shape[1:], BF16) for p in parts], [sem((nw, 3)), sem((nw, 3))], start, finish)


def _pair_exchange(grads, name):
    nw = len(grads)

    def body(*refs):
        ins = refs[:nw]
        outs = refs[nw:2 * nw]
        send, recv = refs[2 * nw:]
        x, y, c = _place()
        cps = []
        for w in range(nw):
            cp = pltpu.make_async_remote_copy(
                src_ref=ins[w].at[:, pl.ds(1 - c, 1)], dst_ref=outs[w], send_sem=send.at[w], recv_sem=recv.at[w],
                device_id=(x, y, 1 - c), device_id_type=MESH)
            cp.start()
            cps.append(cp)
        for cp in cps:
            cp.wait()

    return pl.pallas_call(
        body, name=name, in_specs=[ANY] * nw, out_specs=[ANY] * nw,
        out_shape=[SDS((4, 1) + g.shape[2:], BF16) for g in grads],
        scratch_shapes=[pltpu.SemaphoreType.DMA((nw,)), pltpu.SemaphoreType.DMA((nw,))],
    )(*grads)


def _half_tile(rh):
    best = 16
    for t in range(16, 545, 16):
        if rh % t == 0:
            best = t
    return best


def _pair_sum(g, got, name):
    _, _, rh, n = g.shape
    tr = _half_tile(rh)

    def body(a_ref, b_ref, o_ref):
        o_ref[...] = (a_ref[...].astype(F32) + b_ref[...].astype(F32)).astype(BF16)

    return pl.pallas_call(
        body, name=name, grid=(4, rh // tr),
        in_specs=[pl.BlockSpec((1, 1, tr, n), lambda s, i: (s, lax.axis_index("c"), i, 0)),
                  pl.BlockSpec((1, 1, tr, n), lambda s, i: (s, 0, i, 0))],
        out_specs=pl.BlockSpec((1, 1, tr, n), lambda s, i: (s, 0, i, 0)),
        out_shape=SDS((4, 1, rh, n), BF16),
        compiler_params=_cp(("arbitrary", "arbitrary")),
    )(g, got)


def _chip_sum(part, got, name):
    _, _, rh, n = part.shape
    tr = _half_tile(rh)

    def body(a_ref, b_ref, o_ref):
        acc = a_ref[0, 0].astype(F32)
        for k in range(3):
            acc = acc + b_ref[k, 0].astype(F32)
        o_ref[0] = acc

    return pl.pallas_call(
        body, name=name, grid=(rh // tr,),
        in_specs=[pl.BlockSpec((1, 1, tr, n), lambda i: (2 * lax.axis_index("x") + lax.axis_index("y"), 0, i, 0)),
                  pl.BlockSpec((3, 1, tr, n), lambda i: (0, 0, i, 0))],
        out_specs=pl.BlockSpec((1, tr, n), lambda i: (lax.axis_index("c"), i, 0)),
        out_shape=SDS((2, rh, n), F32),
        compiler_params=_cp(("arbitrary",)),
    )(part, got)


def _half_swap(halves):
    nw = len(halves)

    def body(*refs):
        ins = refs[:nw]
        outs = refs[nw:2 * nw]
        send, recv = refs[2 * nw:]
        x, y, c = _place()
        cps = []
        for w in range(nw):
            cp = pltpu.make_async_remote_copy(
                src_ref=ins[w].at[c], dst_ref=outs[w].at[c], send_sem=send.at[w], recv_sem=recv.at[w],
                device_id=(x, y, 1 - c), device_id_type=MESH)
            cp.start()
            cps.append(cp)
        for cp in cps:
            cp.wait()

    return pl.pallas_call(
        body, name="grad_half_swap", in_specs=[ANY] * nw, out_specs=[ANY] * nw,
        out_shape=[SDS(h.shape, F32) for h in halves],
        input_output_aliases={w: w for w in range(nw)},
        scratch_shapes=[pltpu.SemaphoreType.DMA((nw,)), pltpu.SemaphoreType.DMA((nw,))],
    )(*halves)


def _all_sum_small(part, name):
    R = part.shape[0]

    def body(p_ref, o_ref, land, send, recv):
        x, y, c = _place()
        me = 4 * x + 2 * y + c
        cps = []
        for d in range(1, 8):
            t = (me + d) % 8
            cp = pltpu.make_async_remote_copy(
                src_ref=p_ref, dst_ref=land.at[me], send_sem=send.at[d - 1], recv_sem=recv.at[d - 1],
                device_id=(t // 4, (t // 2) % 2, t % 2), device_id_type=MESH)
            cp.start()
            cps.append(cp)
        land[me] = p_ref[...]
        for cp in cps:
            cp.wait()
        acc = land[0]
        for d in range(1, 8):
            acc = acc + land[d]
        o_ref[...] = acc

    return pl.pallas_call(
        body, name=name,
        in_specs=[pl.BlockSpec(memory_space=pltpu.VMEM)], out_specs=pl.BlockSpec(memory_space=pltpu.VMEM),
        out_shape=SDS((R, D), F32),
        scratch_shapes=[pltpu.VMEM((8, R, D), F32), pltpu.SemaphoreType.DMA((7,)), pltpu.SemaphoreType.DMA((7,))],
        compiler_params=pltpu.CompilerParams(vmem_limit_bytes=VMEM_LIMIT),
    )(part)


def _pad_rows(a, rows):
    return jnp.pad(a, ((0, rows - a.shape[0]), (0, 0)))


def _vec_pack(vs):
    return jnp.concatenate([_pad_rows(v, LANE_ROWS) for v in vs], axis=0)


def kernel(x, rel_bias_table, g_pre_mix, w_in, b_glu, w_dw, b_dw, g_conv_ln, b_conv_ln, w_conv_out, b_conv_out, w_attn_out, w_mix_out, g_post_mix, g_pre_ffn, w_ffn_in, w_ffn_out, g_post_ffn, loss_target, m_rel_bias_table, m_g_pre_mix, m_w_in, m_b_glu, m_w_dw, m_b_dw, m_g_conv_ln, m_b_conv_ln, m_w_conv_out, m_b_conv_out, m_w_attn_out, m_w_mix_out, m_g_post_mix, m_g_pre_ffn, m_w_ffn_in, m_w_ffn_out, m_g_post_ffn, v_rel_bias_table, v_g_pre_mix, v_w_in, v_b_glu, v_w_dw, v_b_dw, v_g_conv_ln, v_b_conv_ln, v_w_conv_out, v_b_conv_out, v_w_attn_out, v_w_mix_out, v_g_post_mix, v_g_pre_ffn, v_w_ffn_in, v_w_ffn_out, v_g_post_ffn):
    S = x.shape[1]
    xs = x.reshape(S, D)
    tgt = loss_target.reshape(S, D)
    cx, cy, cc = _place()
    chip = 2 * cx + cy

    shards = [w_in[0].T.astype(BF16),
              w_ffn_in[0].T.astype(BF16),
              w_attn_out[0].T.astype(BF16),
              w_conv_out[0].astype(BF16),
              w_mix_out[0].astype(BF16),
              w_ffn_out[0].astype(BF16)]
    (w_inT,) = _gather_weights(shards[:1])
    w_inN = w_inT.T

    buckets_np, valid_np = _bucket_tables()
    buckets = jnp.asarray(buckets_np)
    bias = _bias_expand(rel_bias_table, buckets, jnp.asarray(valid_np)).reshape(3, NH, QBLK, KBLK)
    bias2 = bias.reshape(3, NH // 2, 2 * QBLK, KBLK)
    bias_t = bias.reshape(3, NH // 2, 2, QBLK, KBLK).transpose(0, 1, 4, 2, 3).reshape(3, NH // 2, KBLK, 2 * QBLK)
    wdw32 = _pad_rows(w_dw[0], 32)
    wdw_full = _gather_small_cols(wdw32, chip)

    zrest, h, h_r4, h_r16, *gathered = _in_proj_rest(xs, g_pre_mix, w_inN[:, 3 * ATTN_COLS:], _gather_rider(shards[1:]))
    w_fiT, w_aoT, w_co, w_mx, w_fo = (t.reshape(4 * s.shape[0], s.shape[1]) for t, s in zip(gathered, shards[1:]))
    w_fiN, w_aoN = w_fiT.T, w_aoT.T
    w_coT, w_mxT, w_foT = w_co.T, w_mx.T, w_fo.T
    zq = _in_proj_qkv(h, w_inN[:, :3 * ATTN_COLS])
    og, lg = [], []
    for gi in range(3):
        o_g, l_g = _attn_fwd(zq[gi], bias2, gi)
        og.append(o_g)
        lg.append(l_g)
    cv, a = _conv_fwd(zrest, b_glu, wdw_full, b_dw, g_conv_ln, b_conv_ln)
    o, o_r4, o_r16, lse, lse_r4, lse_r16, ya, yc, mg, mm, x1 = _mix_fwd(
        og, lg, a, zrest, xs, w_aoN, w_co, b_conv_out, w_mx, g_post_mix)
    h2, gu, df, dx2, loss8, dg_post_ffn = _ffn_fwd(x1, tgt, g_pre_ffn, g_post_ffn, w_fiN, w_fo)

    ident = lambda n: [list(range(n))]

    def pair_sums(partials, names, tag):
        views = [g.reshape(4, 2, g.shape[0] // 8, g.shape[1]) for g in partials]
        got = _pair_exchange(views, f"grad_pair_exchange_{tag}")
        return [_pair_sum(v, r, f"pair_sum_{n}") for v, r, n in zip(views, got, names)]

    def chip_sums(pair, got, names):
        return [_chip_sum(p, r, f"chip_sum_{n}") for p, r, n in zip(pair, got, names)]

    dff, act = _ffn_bwd_act(df, gu, w_foT)
    g_fiT = _mm_tn(dff, h2, 512, [[2 * t if t < NFT else 2 * (t - NFT) + 1 for t in range(0, 22, 2)],
                                  [2 * t if t < NFT else 2 * (t - NFT) + 1 for t in range(1, 22, 2)]], "dw_ffn_in")
    g_fo = _mm_tn(act, df, FFN_H // 2, ident(2), "dw_ffn_out")
    names_a = ("w_ffn_in", "w_ffn_out")
    pair_a = pair_sums([g_fiT, g_fo], names_a, "ffn")
    dx1, dg_pre_ffn, *got_a = _ffn_bwd_in(dff, x1, dx2, g_pre_ffn, w_fiT, _chip_exchange_rider(pair_a))
    halves_a = chip_sums(pair_a, got_a, names_a)
    dmm, dya, dyc, do, do_r4, do_r16, da, dzg, dg_post_mix, db_conv_out = _mix_bwd(
        dx1, mm, ya, yc, zrest, g_post_mix, w_mxT, w_aoT, w_coT)
    dglu, db_glu_u, db_glu_g, dw_dw, dg_conv_ln, db_conv_ln, db_dw = _conv_bwd(da, cv, zrest, b_glu, wdw_full, g_conv_ln, b_conv_ln)
    first = lambda t: t.reshape(1, S, GW)
    dqkv, dbias = [], []
    for gi, (do_g, o_g, lse_g) in enumerate(((first(do), first(o), first(lse)), (do_r4, o_r4, lse_r4),
                                            (do_r16, o_r16, lse_r16))):
        d_g, db_g = _attn_bwd(zq[gi], do_g, o_g, lse_g, bias_t, gi)
        dqkv.append(d_g)
        dbias.append(db_g.reshape(NH // 2, KBLK, 2, QBLK).transpose(0, 2, 3, 1).reshape(NH, QBLK, KBLK))
    dtab = _bias_reduce(jnp.concatenate(dbias, axis=0), buckets)

    g_inT = _dw_in(dqkv, dglu, dzg, (h, h_r4, h_r16))
    g_aoT = _mm_tn(dya, o, 512, ident(2), "dw_attn_out")
    g_co = _mm_tn(a, dyc, 512, ident(2), "dw_conv_out")
    g_mx = _mm_tn(mg, dmm, 512, ident(2), "dw_mix_out")
    names_b = ("w_in", "w_attn_out", "w_conv_out", "w_mix_out")
    pair_b = pair_sums([g_inT, g_aoT, g_co, g_mx], names_b, "rest")
    grad_x, dg_pre_mix, *got_b = _in_bwd(dqkv, dglu, dzg, w_inT, xs, dx1, g_pre_mix, _chip_exchange_rider(pair_b))
    halves_b = chip_sums(pair_b, got_b, names_b)

    red = [t.reshape(t.shape[0] * t.shape[1], t.shape[2]) for t in _half_swap(halves_a + halves_b)]
    gw_ffn_in, gw_ffn_out, gw_in, gw_attn_out, gw_conv_out, gw_mix_out = (
        red[0].T, red[1], red[2].T, red[3].T, red[4], red[5])

    small = jnp.concatenate([loss8, dg_pre_mix, db_glu_u, db_glu_g, db_dw, dg_conv_ln, db_conv_ln, db_conv_out,
                             dg_post_mix, dg_pre_ffn, dg_post_ffn, dtab, dw_dw], axis=0)
    tot = _all_sum_small(small, "small_all_sum")
    row = lambda i: tot[LANE_ROWS * i:LANE_ROWS * i + 1]
    loss = tot[0, 0]
    g_g_pre_mix, g_b_glu = row(1), jnp.concatenate([row(2), row(3)], axis=1)
    g_b_dw, g_g_conv_ln, g_b_conv_ln, g_b_conv_out = row(4), row(5), row(6), row(7)
    g_g_post_mix, g_g_pre_ffn, g_g_post_ffn = row(8), row(9), row(10)
    g_tab = tot[88:112, 0:32].T
    g_w_dw = lax.dynamic_slice(tot[112:112 + CONV_W], (0, 256 * chip), (CONV_W, 256))

    vec_names = ["g_pre_mix", "b_dw", "g_conv_ln", "b_conv_ln", "b_conv_out", "g_post_mix", "g_pre_ffn", "g_post_ffn"]
    vec_w = [g_pre_mix, b_dw, g_conv_ln, b_conv_ln, b_conv_out, g_post_mix, g_pre_ffn, g_post_ffn]
    vec_m = [m_g_pre_mix, m_b_dw, m_g_conv_ln, m_b_conv_ln, m_b_conv_out, m_g_post_mix, m_g_pre_ffn, m_g_post_ffn]
    vec_v = [v_g_pre_mix, v_b_dw, v_g_conv_ln, v_b_conv_ln, v_b_conv_out, v_g_post_mix, v_g_pre_ffn, v_g_post_ffn]
    vec_g = [g_g_pre_mix, g_b_dw, g_g_conv_ln, g_b_conv_ln, g_b_conv_out, g_g_post_mix, g_g_pre_ffn, g_g_post_ffn]

    def pack(vs, glu, tab, dw):
        return jnp.concatenate([_vec_pack(vs), _pad_rows(glu.reshape(2, D), LANE_ROWS),
                                _pad_rows(jnp.pad(tab.T, ((0, 0), (0, D - 32))), 24),
                                _pad_rows(jnp.pad(dw, ((0, 0), (0, D - 256))), 32)], axis=0)

    sw = pack(vec_w, b_glu, rel_bias_table, w_dw[0])
    sg = pack(vec_g, g_b_glu, g_tab, g_w_dw)
    sm = pack(vec_m, m_b_glu, m_rel_bias_table, m_w_dw[0])
    sv = pack(vec_v, v_b_glu, v_rel_bias_table, v_w_dw[0])
    s_out = _adamw(sw, sg, sm, sv, "adamw_small")

    def unpack(t):
        vecs = {n: t[LANE_ROWS * i:LANE_ROWS * i + 1] for i, n in enumerate(vec_names)}
        vecs["b_glu"] = t[64:66].reshape(1, 2 * D)
        vecs["rel_bias_table"] = t[72:96, 0:32].T
        vecs["w_dw"] = t[96:96 + CONV_W, 0:256][None]
        return vecs

    small_out = [unpack(t) for t in s_out]
    big = {}
    for n, w, g, m, v in (("w_in", w_in, gw_in, m_w_in, v_w_in),
                          ("w_conv_out", w_conv_out, gw_conv_out, m_w_conv_out, v_w_conv_out),
                          ("w_attn_out", w_attn_out, gw_attn_out, m_w_attn_out, v_w_attn_out),
                          ("w_mix_out", w_mix_out, gw_mix_out, m_w_mix_out, v_w_mix_out),
                          ("w_ffn_in", w_ffn_in, gw_ffn_in, m_w_ffn_in, v_w_ffn_in),
                          ("w_ffn_out", w_ffn_out, gw_ffn_out, m_w_ffn_out, v_w_ffn_out)):
        big[n] = [t[None] for t in _adamw(w[0], g, m[0], v[0], f"adamw_{n}")]

    order = ["rel_bias_table", "g_pre_mix", "w_in", "b_glu", "w_dw", "b_dw", "g_conv_ln", "b_conv_ln", "w_conv_out",
             "b_conv_out", "w_attn_out", "w_mix_out", "g_post_mix", "g_pre_ffn", "w_ffn_in", "w_ffn_out", "g_post_ffn"]
    grads = {"rel_bias_table": g_tab, "g_pre_mix": g_g_pre_mix, "w_in": gw_in[None], "b_glu": g_b_glu,
             "w_dw": g_w_dw[None], "b_dw": g_b_dw, "g_conv_ln": g_g_conv_ln, "b_conv_ln": g_b_conv_ln,
             "w_conv_out": gw_conv_out[None], "b_conv_out": g_b_conv_out, "w_attn_out": gw_attn_out[None],
             "w_mix_out": gw_mix_out[None], "g_post_mix": g_g_post_mix, "g_pre_ffn": g_g_pre_ffn,
             "w_ffn_in": gw_ffn_in[None], "w_ffn_out": gw_ffn_out[None], "g_post_ffn": g_g_post_ffn}
    outs = [loss, grad_x.reshape(1, S, D)] + [grads[n] for n in order]
    for slot in range(3):
        outs += [big[n][slot] if n in big else small_out[slot][n] for n in order]
    return tuple(outs)


def _gather_small_cols(wdw32, chip):
    placed = lax.dynamic_update_slice(jnp.zeros((32, D), F32), wdw32, (0, 256 * chip))
    return _all_sum_small(placed, "conv_taps_gather") * 0.5
```
